```python
import math
import jax, jax.numpy as jnp
from jax import lax
import numpy as np

D_MODEL = 1024
BATCH = 2
SEQ = 8192
DEPTH = 1
DEC_BATCH = 128
DEC_SEQ = 1
PAST_LEN = 2048
PAGE_SIZE = 128

SSM_INNER = 2 * D_MODEL
SSM_HEAD_DIM = 64
SSM_HEADS = SSM_INNER // SSM_HEAD_DIM
SSM_GROUPS = 4
SSM_STATE = 128
SSM_CONV = 4
SSM_CHUNK = 128
SSM_GN = SSM_GROUPS * SSM_STATE
SSM_CONV_DIM = SSM_INNER + 2 * SSM_GN
ATTN_HEAD_DIM = 64
ATTN_HEADS = D_MODEL // ATTN_HEAD_DIM
ATTN_KV_HEADS = 4
ATTN_WIDTH = ATTN_HEADS * ATTN_HEAD_DIM
IDX_HEADS = 4
IDX_DIM = 64
IDX_WEIGHT_SCALE = IDX_HEADS ** -0.5 * IDX_DIM ** -0.5
TOPK_MAX = 256
Q_BLOCK = 128
ROPE_THETA = 10000.0
FFN_HIDDEN = ((8 * D_MODEL + 3 * 256 - 1) // (3 * 256)) * 256
NORM_EPS = 1e-6
IN_SPLITS = (SSM_INNER, SSM_INNER, SSM_GN, SSM_GN, SSM_HEADS,
             ATTN_WIDTH, ATTN_KV_HEADS * ATTN_HEAD_DIM, ATTN_KV_HEADS * ATTN_HEAD_DIM,
             IDX_HEADS * IDX_DIM, IDX_DIM, IDX_HEADS, D_MODEL, D_MODEL)
IN_DIM = sum(IN_SPLITS)

kernel_name = 'ssd_dsa_gated_hybrid_step'


def split_points(widths):
    pts, acc = [], 0
    for w in widths[:-1]:
        acc += w
        pts.append(acc)
    return pts


def rmsnorm(x, g):
    xf = x.astype(jnp.float32)
    y = xf * lax.rsqrt(jnp.mean(xf * xf, axis=-1, keepdims=True) + NORM_EPS)
    return (y * g.astype(jnp.float32)).astype(x.dtype)


def rope(x, pos):
    half = x.shape[-1] // 2
    inv = ROPE_THETA ** (-jnp.arange(half, dtype=jnp.float32) / half)
    ang = pos.astype(jnp.float32)[:, None] * inv[None, :]
    shape = (1, pos.shape[0]) + (1,) * (x.ndim - 3) + (half,)
    cos = jnp.cos(ang).reshape(shape).astype(x.dtype)
    sin = jnp.sin(ang).reshape(shape).astype(x.dtype)
    x1, x2 = x[..., :half], x[..., half:]
    return jnp.concatenate([x1 * cos - x2 * sin, x2 * cos + x1 * sin], axis=-1)


def causal_conv(xbc, conv_state, w, b):
    L = xbc.shape[1]
    full = jnp.concatenate([conv_state.astype(xbc.dtype), xbc], axis=1)
    out = sum(full[:, i:i + L] * w[i] for i in range(SSM_CONV)) + b
    return jax.nn.silu(out), full[:, -(SSM_CONV - 1):]


def ssd_scan(x, dt, A, Bm, Cm, init):
    b, L = x.shape[:2]
    Q = min(SSM_CHUNK, L)
    nc = -(-L // Q)
    pad = nc * Q - L
    if pad:
        pw = ((0, 0), (0, pad), (0, 0), (0, 0))
        x, Bm, Cm = jnp.pad(x, pw), jnp.pad(Bm, pw), jnp.pad(Cm, pw)
        dt = jnp.pad(dt, ((0, 0), (0, pad), (0, 0)))
    G, R, P, N = SSM_GROUPS, SSM_HEADS // SSM_GROUPS, SSM_HEAD_DIM, SSM_STATE
    x = x.reshape(b, nc, Q, G, R, P)
    dt = dt.reshape(b, nc, Q, G, R)
    Bm = Bm.reshape(b, nc, Q, G, N)
    Cm = Cm.reshape(b, nc, Q, G, N)
    Acs = jnp.cumsum(dt * A.reshape(G, R), axis=2)
    tri = jnp.tril(jnp.ones((Q, Q), dtype=bool))
    seg = Acs[:, :, :, None] - Acs[:, :, None, :]
    decay = jnp.exp(jnp.where(tri[None, None, :, :, None, None], seg, -jnp.inf))
    cb = jnp.einsum('bclgn,bcsgn->bclsg', Cm, Bm)
    m = cb[..., None] * decay * dt[:, :, None]
    y_diag = jnp.einsum('bclsgr,bcsgrp->bclgrp', m, x)
    decay_end = jnp.exp(Acs[:, :, -1:] - Acs)
    states = jnp.einsum('bclgn,bclgrp->bcgrpn', Bm, x * (decay_end * dt)[..., None])
    chunk_decay = jnp.exp(Acs[:, :, -1])

    def step(s, inp):
        st, dec = inp
        return s * dec[..., None, None] + st, s

    final, s_in = lax.scan(step, init.reshape(b, G, R, P, N),
                           (jnp.moveaxis(states, 1, 0), jnp.moveaxis(chunk_decay, 1, 0)))
    s_in = jnp.moveaxis(s_in, 0, 1)
    y_off = jnp.einsum('bclgn,bcgrpn->bclgrp', Cm, s_in) * jnp.exp(Acs)[..., None]
    y = (y_diag + y_off).reshape(b, nc * Q, SSM_HEADS, P)[:, :L]
    return y, final.reshape(b, SSM_HEADS, P, N)


def ssm_branch(z, xs, bm, cm, dtr, ssm_state, dt_bias, a_log, d_skip, norm_g):
    B, L, _ = xs.shape
    f32 = jnp.float32
    xh = xs.astype(f32).reshape(B, L, SSM_HEADS, SSM_HEAD_DIM)
    dt = jax.nn.softplus(dtr.astype(f32) + dt_bias.astype(f32))
    A = -jnp.exp(a_log.astype(f32))
    y, new_state = ssd_scan(xh, dt, A,
                            bm.astype(f32).reshape(B, L, SSM_GROUPS, SSM_STATE),
                            cm.astype(f32).reshape(B, L, SSM_GROUPS, SSM_STATE),
                            ssm_state.astype(f32))
    y = y + d_skip.astype(f32)[:, None] * xh
    y = y.reshape(B, L, SSM_INNER) * jax.nn.silu(z.astype(f32))
    yg = y.reshape(B, L, SSM_GROUPS, SSM_INNER // SSM_GROUPS)
    yg = yg * lax.rsqrt(jnp.mean(yg * yg, axis=-1, keepdims=True) + NORM_EPS)
    y = yg.reshape(B, L, SSM_INNER) * norm_g.astype(f32)
    return y.astype(xs.dtype), new_state.astype(xs.dtype)


def dsa_select(qi, wts, ki, qpos, topk):
    L = ki.shape[1]
    s = jax.nn.relu(jnp.einsum('bqhd,bsd->bqhs', qi.astype(jnp.float32), ki.astype(jnp.float32)))
    score = jnp.einsum('bqhs,bqh->bqs', s, wts.astype(jnp.float32))
    kpos = jnp.arange(L, dtype=jnp.int32)
    score = jnp.where(kpos[None, None, :] <= qpos[None, :, None], score, -jnp.inf)
    _, idx = lax.top_k(score, topk)
    valid = idx <= qpos[None, :, None]
    return idx, valid


def sparse_attend(q, k_sel, v_sel, valid):
    B, Tq = q.shape[:2]
    logits = jnp.einsum('bqgrd,bqkgd->bqgrk', q.astype(jnp.float32),
                        k_sel.astype(jnp.float32)) * (ATTN_HEAD_DIM ** -0.5)
    logits = jnp.where(valid[:, :, None, None, :], logits, -jnp.inf)
    p = jax.nn.softmax(logits, axis=-1)
    out = jnp.einsum('bqgrk,bqkgd->bqgrd', p, v_sel.astype(jnp.float32))
    return out.reshape(B, Tq, ATTN_WIDTH).astype(q.dtype)


def dsa_prompt(q, k, v, qi, ki, wts, pos):
    B, T = q.shape[:2]
    nblk = T // Q_BLOCK
    topk = min(TOPK_MAX, T // 4)
    bidx = jnp.arange(B)[:, None, None]

    def blk(a):
        return jnp.swapaxes(a.reshape((B, nblk, Q_BLOCK) + a.shape[2:]), 0, 1)

    def one_block(args):
        qb, qib, wb, pb = args
        idx, valid = dsa_select(qib, wb, ki, pb, topk)
        return sparse_attend(qb, k[bidx, idx], v[bidx, idx], valid)

    out = lax.map(one_block, (blk(q), blk(qi), blk(wts), pos.reshape(nblk, Q_BLOCK)))
    return jnp.swapaxes(out, 0, 1).reshape(B, T, ATTN_WIDTH)


def dsa_sample(q, k_new, v_new, qi, ki_new, wts, qpos, cache_k, cache_v, cache_kidx, page_table):
    DB, DS = q.shape[:2]
    past = page_table.shape[1] * PAGE_SIZE
    ki_past = cache_kidx[page_table].reshape(DB, past, IDX_DIM).astype(ki_new.dtype)
    ki_all = jnp.concatenate([ki_past, ki_new], axis=1)
    topk = min(TOPK_MAX, (past + DS) // 4)
    idx, valid = dsa_select(qi, wts, ki_all, qpos, topk)
    bidx = jnp.arange(DB)[:, None, None]
    in_past = idx < past
    pidx = jnp.minimum(idx, past - 1)
    phys = page_table[bidx, pidx // PAGE_SIZE]
    off = pidx % PAGE_SIZE
    nidx = jnp.clip(idx - past, 0, DS - 1)

    def gather(cache, new):
        return jnp.where(in_past[..., None, None], cache[phys, off].astype(new.dtype), new[bidx, nidx])

    return sparse_attend(q, gather(cache_k, k_new), gather(cache_v, v_new), valid)


def layer(x, c, pos, conv_state, ssm_state, attend, p):
    B, L, _ = x.shape
    mod = (jax.nn.silu(c) @ p['w_ada'] + p['b_ada'])[:, None, :]
    sh1, sc1, g1, sh2, sc2, g2 = jnp.split(mod, 6, axis=-1)
    h = rmsnorm(x, p['norm1_g']) * (1 + sc1) + sh1
    (z, xs, bm, cm, dtr, q, k, v, qi, ki, wi, gs, ga) = jnp.split(
        h @ p['w_in'], split_points(IN_SPLITS), axis=-1)
    xbc, new_conv = causal_conv(jnp.concatenate([xs, bm, cm], axis=-1), conv_state,
                                p['conv_w'], p['conv_b'])
    xs, bm, cm = jnp.split(xbc, split_points((SSM_INNER, SSM_GN, SSM_GN)), axis=-1)
    y_ssm, new_ssm = ssm_branch(z, xs, bm, cm, dtr, ssm_state, p['dt_bias'], p['a_log'],
                                p['d_skip'], p['ssm_norm_g'])
    R = ATTN_HEADS // ATTN_KV_HEADS
    q = rope(q.reshape(B, L, ATTN_KV_HEADS, R, ATTN_HEAD_DIM), pos)
    k = rope(k.reshape(B, L, ATTN_KV_HEADS, ATTN_HEAD_DIM), pos)
    v = v.reshape(B, L, ATTN_KV_HEADS, ATTN_HEAD_DIM)
    qi = rope(qi.reshape(B, L, IDX_HEADS, IDX_DIM), pos)
    ki = rope(ki, pos)
    y_attn = attend(q, k, v, qi, ki, wi * IDX_WEIGHT_SCALE, pos)
    merged = (jax.nn.sigmoid(gs) * (y_ssm @ p['w_proj_ssm'])
              + jax.nn.sigmoid(ga) * (y_attn @ p['w_proj_attn']))
    x = x + g1 * (merged @ p['w_out'])
    h2 = rmsnorm(x, p['norm2_g']) * (1 + sc2) + sh2
    gate, up = jnp.split(h2 @ p['w_ffn_in'], 2, axis=-1)
    x = x + g2 * ((jax.nn.silu(gate) * up) @ p['w_ffn_out'])
    return x, (k, v, ki, new_conv, new_ssm)


def setup_inputs(seed: int = 0) -> dict:
    key = jax.random.key(seed)
    it = iter(jax.random.split(key, 32))
    nrm = lambda shape, s: jax.random.normal(next(it), shape, jnp.float32) * s
    n_pages = PAST_LEN // PAGE_SIZE
    n_used = DEC_BATCH * n_pages
    n_pool = n_used + max(1, n_used // 4)
    D = D_MODEL
    x_prompt = nrm((BATCH, SEQ, D), 1.0)
    x_sample = nrm((DEC_BATCH, DEC_SEQ, D), 1.0)
    cache_k = nrm((DEPTH, n_pool, PAGE_SIZE, ATTN_KV_HEADS, ATTN_HEAD_DIM), 1.0)
    cache_v = nrm((DEPTH, n_pool, PAGE_SIZE, ATTN_KV_HEADS, ATTN_HEAD_DIM), 1.0)
    cache_kidx = nrm((DEPTH, n_pool, PAGE_SIZE, IDX_DIM), 1.0)
    page_table = jax.random.permutation(next(it), n_pool)[:n_used].reshape(
        DEC_BATCH, n_pages).astype(jnp.int32)
    c_prompt = nrm((BATCH, D), 1.0)
    c_sample = nrm((DEC_BATCH, D), 1.0)
    state_conv = nrm((DEPTH, DEC_BATCH, SSM_CONV - 1, SSM_CONV_DIM), 1.0)
    state_ssm = nrm((DEPTH, DEC_BATCH, SSM_HEADS, SSM_HEAD_DIM, SSM_STATE), 0.3)
    w_ada = nrm((DEPTH, D, 6 * D), 0.5 * D ** -0.5)
    b_ada = nrm((DEPTH, 6 * D), 0.02)
    norm1_g = 1.0 + nrm((DEPTH, D), 0.05)
    w_in = nrm((DEPTH, D, IN_DIM), D ** -0.5)
    conv_w = nrm((DEPTH, SSM_CONV, SSM_CONV_DIM), SSM_CONV ** -0.5)
    conv_b = nrm((DEPTH, SSM_CONV_DIM), 0.02)
    u = jax.random.uniform(next(it), (DEPTH, SSM_HEADS), jnp.float32)
    dt0 = jnp.exp(u * (math.log(0.1) - math.log(0.001)) + math.log(0.001))
    dt_bias = dt0 + jnp.log(-jnp.expm1(-dt0))
    a_log = jnp.log(jax.random.uniform(next(it), (DEPTH, SSM_HEADS), jnp.float32, 1.0, 16.0))
    d_skip = 1.0 + nrm((DEPTH, SSM_HEADS), 0.1)
    ssm_norm_g = 1.0 + nrm((DEPTH, SSM_INNER), 0.05)
    w_proj_ssm = nrm((DEPTH, SSM_INNER, D), SSM_INNER ** -0.5)
    w_proj_attn = nrm((DEPTH, ATTN_WIDTH, D), ATTN_WIDTH ** -0.5)
    w_out = nrm((DEPTH, D, D), D ** -0.5)
    norm2_g = 1.0 + nrm((DEPTH, D), 0.05)
    w_ffn_in = nrm((DEPTH, D, 2 * FFN_HIDDEN), D ** -0.5)
    w_ffn_out = nrm((DEPTH, FFN_HIDDEN, D), FFN_HIDDEN ** -0.5)
    final_g = 1.0 + nrm((D,), 0.05)
    return {'x_prompt': x_prompt, 'x_sample': x_sample, 'cache_k': cache_k, 'cache_v': cache_v,
            'cache_kidx': cache_kidx, 'state_conv': state_conv, 'state_ssm': state_ssm,
            'page_table': page_table, 'c_prompt': c_prompt, 'c_sample': c_sample,
            'w_ada': w_ada, 'b_ada': b_ada, 'norm1_g': norm1_g, 'w_in': w_in,
            'conv_w': conv_w, 'conv_b': conv_b, 'dt_bias': dt_bias, 'a_log': a_log,
            'd_skip': d_skip, 'ssm_norm_g': ssm_norm_g, 'w_proj_ssm': w_proj_ssm,
            'w_proj_attn': w_proj_attn, 'w_out': w_out, 'norm2_g': norm2_g,
            'w_ffn_in': w_ffn_in, 'w_ffn_out': w_ffn_out, 'final_g': final_g}


def reference(x_prompt, x_sample, cache_k, cache_v, cache_kidx, state_conv, state_ssm,
              page_table, c_prompt, c_sample, w_ada, b_ada, norm1_g, w_in, conv_w, conv_b,
              dt_bias, a_log, d_skip, ssm_norm_g, w_proj_ssm, w_proj_attn, w_out, norm2_g,
              w_ffn_in, w_ffn_out, final_g):
    B, T, _ = x_prompt.shape
    DB, DS, _ = x_sample.shape
    past = page_table.shape[1] * PAGE_SIZE
    pos_p = jnp.arange(T, dtype=jnp.int32)
    pos_s = past + jnp.arange(DS, dtype=jnp.int32)
    yp, ys = x_prompt, x_sample
    outs_p, outs_s = [], []
    for l in range(DEPTH):
        p = dict(w_ada=w_ada[l], b_ada=b_ada[l], norm1_g=norm1_g[l], w_in=w_in[l],
                 conv_w=conv_w[l], conv_b=conv_b[l], dt_bias=dt_bias[l], a_log=a_log[l],
                 d_skip=d_skip[l], ssm_norm_g=ssm_norm_g[l], w_proj_ssm=w_proj_ssm[l],
                 w_proj_attn=w_proj_attn[l], w_out=w_out[l], norm2_g=norm2_g[l],
                 w_ffn_in=w_ffn_in[l], w_ffn_out=w_ffn_out[l])
        conv0 = jnp.zeros((B, SSM_CONV - 1, SSM_CONV_DIM), x_prompt.dtype)
        ssm0 = jnp.zeros((B, SSM_HEADS, SSM_HEAD_DIM, SSM_STATE), jnp.float32)
        yp, st_p = layer(yp, c_prompt, pos_p, conv0, ssm0, dsa_prompt, p)

        def attend_s(q, k, v, qi, ki, wts, qpos, l=l):
            return dsa_sample(q, k, v, qi, ki, wts, qpos, cache_k[l], cache_v[l],
                              cache_kidx[l], page_table)

        ys, st_s = layer(ys, c_sample, pos_s, state_conv[l], state_ssm[l], attend_s, p)
        outs_p.append(st_p)
        outs_s.append(st_s)

    def stack(outs, i):
        return jnp.stack([o[i] for o in outs], axis=0)

    y_prompt = rmsnorm(yp, final_g)
    y_sample = rmsnorm(ys, final_g)
    k_prompt, v_prompt, kidx_prompt = stack(outs_p, 0), stack(outs_p, 1), stack(outs_p, 2)
    conv_prompt, ssm_prompt = stack(outs_p, 3), stack(outs_p, 4)
    k_sample, v_sample, kidx_sample = stack(outs_s, 0), stack(outs_s, 1), stack(outs_s, 2)
    conv_sample, ssm_sample = stack(outs_s, 3), stack(outs_s, 4)
    return (y_prompt, y_sample, k_prompt, v_prompt, kidx_prompt, conv_prompt, ssm_prompt,
            k_sample, v_sample, kidx_sample, conv_sample, ssm_sample)
```

```python
import functools

import jax
import jax.numpy as jnp
import numpy as np
from jax import lax
from jax.experimental import pallas as pl
from jax.experimental.pallas import tpu as pltpu

F32, BF16, I32 = jnp.float32, jnp.bfloat16, jnp.int32
HIGHEST = lax.Precision.HIGHEST

LANES = 128
HEAD = 64
SSM_STATE = 128
SSM_GROUPS = 4
SSM_CONV = 4
SSM_CHUNK = 128
ATTN_KV_HEADS = 4
IDX_HEADS = 4
TOPK_MAX = 256
Q_BLOCK = 128
PAGE_SIZE = 128
ROPE_THETA = 10000.0
NORM_EPS = 1e-6
KEY_BLOCK = 512
PAGES_PER_STEP = 4
INT_MIN = -(2 ** 31)
NEG_BIG = -1e30


def _cparams(sem, vmem_mb):
    return pltpu.CompilerParams(dimension_semantics=sem, vmem_limit_bytes=vmem_mb << 20)


def _bdot(a, b):
    return jnp.dot(a.astype(BF16), b.astype(BF16), preferred_element_type=F32)


def _bdot_nt(a, b):
    return lax.dot_general(a.astype(BF16), b.astype(BF16), (((1,), (1,)), ((), ())),
                           preferred_element_type=F32)


def _silu(x):
    return x * jax.nn.sigmoid(x)


def _softplus(x):
    return jnp.maximum(x, 0.0) + jnp.log1p(jnp.exp(-jnp.abs(x)))


def _norm_mod(x, g, scale, shift):
    ms = jnp.mean(x * x, axis=-1, keepdims=True)
    return (x * lax.rsqrt(ms + NORM_EPS) * g) * (1.0 + scale) + shift


def _rope128(x, cos, sin_lo, sin_hi):
    return x * cos + pltpu.roll(x, 96, 1) * sin_lo + pltpu.roll(x, 32, 1) * sin_hi


def _rope_wide(x, cos, sin_lo, sin_hi):
    parts = [_rope128(x[:, j:j + LANES], cos, sin_lo, sin_hi) for j in range(0, x.shape[1], LANES)]
    return parts[0] if len(parts) == 1 else jnp.concatenate(parts, axis=1)


def _ada_body(c_ref, w_ref, b_ref, o_ref):
    o_ref[...] = _bdot(_silu(c_ref[...]), w_ref[...]) + b_ref[...]


def _ada(c_all, w_bf, b):
    mp, d = c_all.shape
    n = w_bf.shape[1]
    tn = n // 4
    return pl.pallas_call(
        _ada_body, grid=(n // tn,),
        in_specs=[pl.BlockSpec((mp, d), lambda j: (0, 0)),
                  pl.BlockSpec((d, tn), lambda j: (0, j)),
                  pl.BlockSpec((1, tn), lambda j: (0, j))],
        out_specs=pl.BlockSpec((mp, tn), lambda j: (0, j)),
        out_shape=jax.ShapeDtypeStruct((mp, n), F32),
        compiler_params=_cparams(("arbitrary",), 32), name="ada")(c_all, w_bf, b)


def _inproj_ssm_body(inner, cdim, x_ref, mod_ref, g_ref, w_ref, z_ref, xbc_ref, dt_ref):
    d = x_ref.shape[1]
    h = _norm_mod(x_ref[...], g_ref[...], mod_ref[:, d:2 * d], mod_ref[:, 0:d]).astype(BF16)
    z_ref[...] = jnp.dot(h, w_ref[:, 0:inner], preferred_element_type=F32)
    xbc_ref[...] = jnp.dot(h, w_ref[:, inner:inner + cdim], preferred_element_type=F32)
    dt_ref[...] = jnp.dot(h, w_ref[:, inner + cdim:inner + cdim + LANES], preferred_element_type=F32)


def _row_specs(tm, tpb, d, mod_rows):
    x_spec = pl.BlockSpec((tm, d), lambda m: (m, 0))
    mod_spec = pl.BlockSpec((None, mod_rows, 6 * d), lambda m: (m // tpb, 0, 0))
    return x_spec, mod_spec


def _const_spec(shape):
    return pl.BlockSpec(shape, lambda m: (0,) * len(shape))


def _inproj_ssm(x, mod3, g, w_bf, tm, tpb, inner, cdim):
    m, d = x.shape
    x_spec, mod_spec = _row_specs(tm, tpb, d, mod3.shape[1])
    row = lambda n: pl.BlockSpec((tm, n), lambda i: (i, 0))
    return pl.pallas_call(
        functools.partial(_inproj_ssm_body, inner, cdim), grid=(m // tm,),
        in_specs=[x_spec, mod_spec, _const_spec((1, d)), _const_spec(w_bf.shape)],
        out_specs=[row(inner), row(cdim), row(LANES)],
        out_shape=[jax.ShapeDtypeStruct((m, inner), F32), jax.ShapeDtypeStruct((m, cdim), F32),
                   jax.ShapeDtypeStruct((m, LANES), F32)],
        compiler_params=_cparams(("parallel",), 52), name="inproj_ssm")(x, mod3, g, w_bf)


def _inproj_attn_body(aw, kvw, iw, x_ref, mod_ref, g_ref, w_ref, cos_ref, slo_ref, shi_ref,
                      q_ref, k_ref, kb_ref, v_ref, vb_ref, qi_ref, gs_ref, ga_ref, sm_ref, ki2_ref):
    d = x_ref.shape[1]
    h = _norm_mod(x_ref[...], g_ref[...], mod_ref[:, d:2 * d], mod_ref[:, 0:d]).astype(BF16)
    cos, slo, shi = cos_ref[...], slo_ref[...], shi_ref[...]

    def proj(a, b):
        return jnp.dot(h, w_ref[:, a:b], preferred_element_type=F32)

    o = 0
    q_ref[...] = _rope_wide(proj(o, o + aw), cos, slo, shi).astype(BF16)
    o += aw
    k = _rope_wide(proj(o, o + kvw), cos, slo, shi)
    k_ref[...] = k
    kb_ref[...] = k.astype(BF16)
    o += kvw
    v = proj(o, o + kvw)
    v_ref[...] = v
    vb_ref[...] = v.astype(BF16)
    o += kvw
    qi_ref[...] = _rope_wide(proj(o, o + iw), cos, slo, shi).astype(BF16)
    o += iw
    gs_ref[...] = proj(o, o + d)
    o += d
    ga_ref[...] = proj(o, o + d)
    o += d
    s = proj(o, o + LANES)
    lane = lax.broadcasted_iota(I32, s.shape, 1)
    sm = jnp.where(lane < HEAD, _rope128(s, cos, slo, shi), s)
    sm_ref[...] = sm
    ki2_ref[...] = jnp.where(lane < HEAD, sm, pltpu.roll(sm, HEAD, 1)).astype(BF16)


def _inproj_attn(x, mod3, g, w_bf, tabs, tm, tpb, aw, kvw, iw):
    m, d = x.shape
    x_spec, mod_spec = _row_specs(tm, tpb, d, mod3.shape[1])
    ntab = tabs[0].shape[0] // tm
    tab_spec = pl.BlockSpec((tm, LANES), lambda i: (i % ntab, 0))
    row = lambda n: pl.BlockSpec((tm, n), lambda i: (i, 0))
    widths = [(aw, BF16), (kvw, F32), (kvw, BF16), (kvw, F32), (kvw, BF16), (iw, BF16),
              (d, F32), (d, F32), (LANES, F32), (LANES, BF16)]
    return pl.pallas_call(
        functools.partial(_inproj_attn_body, aw, kvw, iw), grid=(m // tm,),
        in_specs=[x_spec, mod_spec, _const_spec((1, d)), _const_spec(w_bf.shape),
                  tab_spec, tab_spec, tab_spec],
        out_specs=[row(n) for n, _ in widths],
        out_shape=[jax.ShapeDtypeStruct((m, n), dt) for n, dt in widths],
        compiler_params=_cparams(("parallel",), 52), name="inproj_attn")(x, mod3, g, w_bf, *tabs)


def _ssd_body(inner, z_ref, xbc_ref, dtr_ref, cw_ref, cb_ref, dtb_ref, a_ref, dsk_ref, ng_ref,
              y_ref, st_ref, full_s, act_s, st_s, y_s):
    c = pl.program_id(1)
    Q, N = SSM_CHUNK, SSM_STATE
    cdim = xbc_ref.shape[1]
    heads_per_group = inner // HEAD // SSM_GROUPS
    gw = inner // SSM_GROUPS

    @pl.when(c == 0)
    def _():
        full_s[0:8, :] = jnp.zeros((8, cdim), F32)
        st_s[...] = jnp.zeros(st_s.shape, F32)

    full_s[8:8 + Q, :] = xbc_ref[...]
    for j in range(0, cdim, 512):
        acc = cb_ref[:, j:j + 512] + full_s[8:8 + Q, j:j + 512] * cw_ref[3:4, j:j + 512]
        for i in range(SSM_CONV - 1):
            acc = acc + full_s[5 + i:5 + i + Q, j:j + 512] * cw_ref[i:i + 1, j:j + 512]
        act_s[:, j:j + 512] = _silu(acc)
    full_s[0:8, :] = full_s[Q:Q + 8, :]

    dt = _softplus(dtr_ref[...] + dtb_ref[...])
    row = lax.broadcasted_iota(I32, (Q, Q), 0)
    col = lax.broadcasted_iota(I32, (Q, Q), 1)
    tri = row >= col
    acs = jnp.dot(tri.astype(F32), dt * a_ref[...], precision=HIGHEST, preferred_element_type=F32)
    acs_t, dt_t = acs.T, dt.T
    last = acs[Q - 1:Q, :]
    wdt = jnp.exp(last - acs) * dt
    eacs = jnp.exp(acs)
    cdec = jnp.exp(last)
    low = lax.broadcasted_iota(I32, (Q, LANES), 1) < HEAD
    low1 = low[0:1, :]

    for g in range(SSM_GROUPS):
        bg = act_s[:, inner + g * N:inner + (g + 1) * N]
        cg = act_s[:, inner + SSM_GROUPS * N + g * N:inner + SSM_GROUPS * N + (g + 1) * N]
        cb = _bdot_nt(cg, bg)
        bg_t = bg.T.astype(BF16)
        for p in range(heads_per_group // 2):
            h0 = g * heads_per_group + 2 * p
            js = slice(h0 * HEAD, h0 * HEAD + LANES)
            xp = act_s[:, js]
            xp_bf = xp.astype(BF16)
            stp = st_s[:, js]
            stp_bf = stp.astype(BF16)
            ys = []
            for h in (h0, h0 + 1):
                seg = acs[:, h:h + 1] - acs_t[h:h + 1, :]
                decay = jnp.exp(jnp.where(tri, seg, -jnp.inf))
                m = (cb * decay) * dt_t[h:h + 1, :]
                ce = cg * eacs[:, h:h + 1]
                ys.append(_bdot(m, xp_bf) + _bdot(ce, stp_bf))
            y_s[:, js] = jnp.where(low, ys[0], ys[1])
            wcol = jnp.where(low, wdt[:, h0:h0 + 1], wdt[:, h0 + 1:h0 + 2])
            dst = jnp.dot(bg_t, (xp * wcol).astype(BF16), preferred_element_type=F32)
            cd = jnp.where(low1, cdec[:, h0:h0 + 1], cdec[:, h0 + 1:h0 + 2])
            st_s[:, js] = stp * cd + dst

    for g in range(SSM_GROUPS):
        gs = slice(g * gw, (g + 1) * gw)
        y = y_s[:, gs] + dsk_ref[:, gs] * act_s[:, gs]
        y = y * _silu(z_ref[:, gs])
        ms = jnp.mean(y * y, axis=-1, keepdims=True)
        y_ref[:, gs] = (y * lax.rsqrt(ms + NORM_EPS) * ng_ref[:, gs]).astype(BF16)

    @pl.when(c == pl.num_programs(1) - 1)
    def _():
        st_ref[...] = st_s[...].T


def _ssd_prompt(z, xbc, dtr, cw, cb, dtb, a, dsk, ng, nb):
    m, inner = z.shape
    cdim = xbc.shape[1]
    nc = m // nb // SSM_CHUNK
    row = lambda n: pl.BlockSpec((SSM_CHUNK, n), lambda b, c: (b * nc + c, 0))
    const = lambda shape: pl.BlockSpec(shape, lambda b, c: (0,) * len(shape))
    return pl.pallas_call(
        functools.partial(_ssd_body, inner), grid=(nb, nc),
        in_specs=[row(inner), row(cdim), row(LANES), const(cw.shape), const(cb.shape),
                  const(dtb.shape), const(a.shape), const(dsk.shape), const(ng.shape)],
        out_specs=[row(inner), pl.BlockSpec((None, inner, SSM_STATE), lambda b, c: (b, 0, 0))],
        out_shape=[jax.ShapeDtypeStruct((m, inner), BF16),
                   jax.ShapeDtypeStruct((nb, inner, SSM_STATE), F32)],
        scratch_shapes=[pltpu.VMEM((SSM_CHUNK + 8, cdim), F32), pltpu.VMEM((SSM_CHUNK, cdim), F32),
                        pltpu.VMEM((SSM_STATE, inner), F32), pltpu.VMEM((SSM_CHUNK, inner), F32)],
        compiler_params=_cparams(("parallel", "arbitrary"), 40), name="ssd_prompt",
    )(z, xbc, dtr, cw, cb, dtb, a, dsk, ng)


def _ssd_step_body(z_ref, xs_ref, bc_ref, cxs_ref, cbc_ref, dtr_ref, st_ref,
                   wxs_ref, wbc_ref, bxs_ref, bbc_ref, dtb_ref, a_ref, dsk_ref, ng_ref,
                   y_ref, sto_ref):
    G = SSM_GROUPS
    last = SSM_CONV - 1
    xs = bxs_ref[...] + xs_ref[...] * wxs_ref[last]
    bc = bbc_ref[...] + bc_ref[...] * wbc_ref[last]
    for i in range(last):
        xs = xs + cxs_ref[i] * wxs_ref[i]
        bc = bc + cbc_ref[i] * wbc_ref[i]
    xs, bc = _silu(xs), _silu(bc)
    dt = _softplus(dtr_ref[...] + dtb_ref[...])
    dec = jnp.exp(dt * a_ref[...])
    xdt = xs * dt
    npair = xs.shape[0]
    pairs_per_group = npair // G
    r = lax.broadcasted_iota(I32, (LANES, LANES), 0)
    cidx = lax.broadcasted_iota(I32, (LANES, LANES), 1)
    eye = (r == cidx).astype(F32)
    nt = (((1,), (1,)), ((), ()))
    dec_t = lax.dot_general(eye, dec, nt, precision=HIGHEST, preferred_element_type=F32)
    xdt_t = lax.dot_general(eye, xdt, nt, precision=HIGHEST, preferred_element_type=F32)
    rows = lax.broadcasted_iota(I32, (npair, 1), 0)
    cbv = jnp.sum(bc[0:G, :] * bc[G:2 * G, :], axis=-1, keepdims=True)
    cbx = jnp.zeros((npair, 1), F32)
    for g in range(G):
        cbx = cbx + jnp.where(rows // pairs_per_group == g, cbv[g:g + 1, :], 0.0)
    c_bf = bc.astype(BF16)
    yoff = jnp.zeros(xs.shape, F32)
    for j in range(npair):
        g = j // pairs_per_group
        s = st_ref[j * LANES:(j + 1) * LANES, :]
        sto_ref[j * LANES:(j + 1) * LANES, :] = s * dec_t[:, j:j + 1] + xdt_t[:, j:j + 1] * bc[g:g + 1, :]
        rj = _bdot_nt(c_bf, s)
        yoff = yoff + jnp.where(rows == j, rj[G + g:G + g + 1, :], 0.0)
    y = yoff * dec + cbx * dt * xs + dsk_ref[...] * xs
    y = y * _silu(z_ref[...])
    ssq = jnp.sum(y * y, axis=-1, keepdims=True)
    msx = jnp.zeros((npair, 1), F32)
    for g in range(G):
        ing = rows // pairs_per_group == g
        tot = jnp.sum(jnp.where(ing, ssq, 0.0), axis=0, keepdims=True)
        msx = msx + jnp.where(ing, tot, 0.0)
    msx = msx / (pairs_per_group * LANES)
    y_ref[...] = y * lax.rsqrt(msx + NORM_EPS) * ng_ref[...]


def _ssd_sample(z, xs, bc, cxs, cbc, dtr, st, wxs, wbc, bxs, bbc, dtb, a, dsk, ng):
    db, npair, _ = z.shape
    per_b = lambda shape: pl.BlockSpec((None,) + shape, lambda b: (b,) + (0,) * len(shape))
    const = lambda arr: pl.BlockSpec(arr.shape, lambda b: (0,) * arr.ndim)
    return pl.pallas_call(
        _ssd_step_body, grid=(db,),
        in_specs=[per_b(z.shape[1:]), per_b(xs.shape[1:]), per_b(bc.shape[1:]), per_b(cxs.shape[1:]),
                  per_b(cbc.shape[1:]), per_b(dtr.shape[1:]), per_b(st.shape[1:]),
                  const(wxs), const(wbc), const(bxs), const(bbc), const(dtb), const(a), const(dsk), const(ng)],
        out_specs=[per_b(z.shape[1:]), per_b(st.shape[1:])],
        out_shape=[jax.ShapeDtypeStruct(z.shape, F32), jax.ShapeDtypeStruct(st.shape, F32)],
        compiler_params=_cparams(("parallel",), 32), name="ssd_sample",
    )(z, xs, bc, cxs, cbc, dtr, st, wxs, wbc, bxs, bbc, dtb, a, dsk, ng)


def _score_key(score):
    bits = pltpu.bitcast(score, I32)
    return jnp.where(bits < 0, bits ^ jnp.int32(0x7FFFFFFF), bits)


def _lane_fold(x):
    acc = x[:, 0:LANES]
    for j in range(LANES, x.shape[1], LANES):
        acc = acc + x[:, j:j + LANES]
    return acc


def _kth_key(key_s, rows, nchunks, cw, topk):
    def count_ge(cand):
        def body(s, acc):
            key = key_s[:, pl.ds(pl.multiple_of(s * cw, cw), cw)]
            return acc + _lane_fold(jnp.where(key >= cand, 1.0, 0.0))
        acc = lax.fori_loop(0, nchunks, body, jnp.zeros((rows, LANES), F32))
        return jnp.sum(acc, axis=-1, keepdims=True)

    def bit_step(t, kth):
        cand = kth ^ lax.shift_left(jnp.int32(1), 31 - t)
        return jnp.where(count_ge(cand) >= topk, cand, kth)

    return lax.fori_loop(0, 32, bit_step, jnp.full((rows, 1), INT_MIN, I32))


def _selection_bias(key_s, bias_s, kth, rows, nchunks, cw, topk, qpos):
    def gt_body(s, acc):
        key = key_s[:, pl.ds(pl.multiple_of(s * cw, cw), cw)]
        return acc + _lane_fold(jnp.where(key > kth, 1.0, 0.0))
    cnt_gt = jnp.sum(lax.fori_loop(0, nchunks, gt_body, jnp.zeros((rows, LANES), F32)),
                     axis=-1, keepdims=True)
    need = topk - cnt_gt
    r = lax.broadcasted_iota(I32, (LANES, LANES), 0)
    c = lax.broadcasted_iota(I32, (LANES, LANES), 1)
    upper = jnp.where(r <= c, 1.0, 0.0).astype(BF16)
    lane = lax.broadcasted_iota(I32, (1, LANES), 1)

    def body(s, carry):
        for j in range(0, cw, LANES):
            off = pl.multiple_of(s * cw + j, LANES)
            key = key_s[:, pl.ds(off, LANES)]
            eq = key == kth
            eqf = jnp.where(eq, 1.0, 0.0)
            incl = jnp.dot(eqf.astype(BF16), upper, preferred_element_type=F32)
            rank = carry + incl - eqf
            sel = (key > kth) | (eq & (rank < need))
            sel = sel & ((off + lane) <= qpos)
            bias_s[:, pl.ds(off, LANES)] = jnp.where(sel, 0.0, -jnp.inf)
            carry = carry + incl[:, LANES - 1:LANES]
        return carry

    lax.fori_loop(0, nchunks, body, jnp.zeros((rows, 1), F32))


def _dsa_body(topk, idx_scale, q_ref, qi_ref, sm_ref, ki2_ref, k_ref, v_ref, o_ref, bias_s, key_s):
    i = pl.program_id(1)
    QB, KB = Q_BLOCK, KEY_BLOCK
    nkb = (i * QB + QB + KB - 1) // KB
    qpos = i * QB + lax.broadcasted_iota(I32, (QB, 1), 0)
    low = lax.broadcasted_iota(I32, (QB, LANES), 1) < HEAD
    zero_bf = jnp.zeros((QB, LANES), BF16)

    qi = qi_ref[...]
    sm = sm_ref[...]
    qh, wh = [], []
    for h in range(IDX_HEADS):
        chunk = qi[:, (h // 2) * LANES:(h // 2 + 1) * LANES]
        qh.append(jnp.where(low if h % 2 == 0 else ~low, chunk, zero_bf))
        wh.append(sm[:, HEAD + h:HEAD + h + 1] * idx_scale)

    def score_body(s, carry):
        off = pl.multiple_of(s * KB, KB)
        kib = ki2_ref[pl.ds(off, KB), :]
        acc = jnp.zeros((QB, KB), F32)
        for h in range(IDX_HEADS):
            acc = acc + wh[h] * jnp.maximum(_bdot_nt(qh[h], kib), 0.0)
        kpos = off + lax.broadcasted_iota(I32, (1, KB), 1)
        acc = jnp.where(kpos <= qpos, acc, -jnp.inf)
        key_s[:, pl.ds(off, KB)] = _score_key(acc)
        return carry

    lax.fori_loop(0, nkb, score_body, 0)
    kth = _kth_key(key_s, QB, nkb, KB, topk)
    _selection_bias(key_s, bias_s, kth, QB, nkb, KB, topk, qpos)

    q = q_ref[...]
    nchunk = q.shape[1] // LANES
    per_kv_chunk = nchunk // (ATTN_KV_HEADS // 2)
    scale = jnp.asarray(HEAD ** -0.5, BF16)
    for cj in range(nchunk):
        kvc = cj // per_kv_chunk
        qc = q[:, cj * LANES:(cj + 1) * LANES] * scale
        outs = []
        for half in range(2):
            qm = jnp.where(low if half == 0 else ~low, qc, zero_bf)

            def att_body(s, carry, qm=qm, kvc=kvc):
                m, l, acc = carry
                off = pl.multiple_of(s * KB, KB)
                kb = k_ref[pl.ds(off, KB), kvc * LANES:(kvc + 1) * LANES]
                vb = v_ref[pl.ds(off, KB), kvc * LANES:(kvc + 1) * LANES]
                lg = _bdot_nt(qm, kb) + bias_s[:, pl.ds(off, KB)]
                mn = jnp.maximum(m, jnp.max(lg, axis=-1, keepdims=True))
                p = jnp.exp(lg - mn)
                alpha = jnp.exp(m - mn)
                l = alpha * l + jnp.sum(p, axis=-1, keepdims=True)
                acc = alpha * acc + jnp.dot(p.astype(BF16), vb, preferred_element_type=F32)
                return mn, l, acc

            init = (jnp.full((QB, 1), NEG_BIG, F32), jnp.zeros((QB, 1), F32), jnp.zeros((QB, LANES), F32))
            _, l, acc = lax.fori_loop(0, nkb, att_body, init)
            outs.append(acc / l)
        o_ref[:, cj * LANES:(cj + 1) * LANES] = jnp.where(low, outs[0], outs[1]).astype(BF16)


def _dsa_prompt(q, qi, sm, ki2, k, v, nb, topk, idx_scale):
    m, aw = q.shape
    t = m // nb
    nq = t // Q_BLOCK
    tpad = -(-t // KEY_BLOCK) * KEY_BLOCK
    row = lambda n: pl.BlockSpec((Q_BLOCK, n), lambda b, i: (b * nq + i, 0))
    per_b = lambda n: pl.BlockSpec((t, n), lambda b, i: (b, 0))
    return pl.pallas_call(
        functools.partial(_dsa_body, topk, idx_scale), grid=(nb, nq),
        in_specs=[row(aw), row(qi.shape[1]), row(LANES), per_b(LANES), per_b(k.shape[1]), per_b(v.shape[1])],
        out_specs=row(aw),
        out_shape=jax.ShapeDtypeStruct((m, aw), BF16),
        scratch_shapes=[pltpu.VMEM((Q_BLOCK, tpad), F32), pltpu.VMEM((Q_BLOCK, tpad), I32)],
        compiler_params=_cparams(("parallel", "arbitrary"), 48), name="dsa_prompt",
    )(q, qi, sm, ki2, k, v)


def _page_specs(block, n):
    def make(u):
        return pl.BlockSpec((None,) + block, lambda b, s, pt: (pt[b, s * n + u],) + (0,) * len(block))
    return [make(u) for u in range(n)]


def _idx_score_body(pt_ref, qi_ref, w_ref, *refs):
    pages, o_ref = refs[:-1], refs[-1]
    qi = qi_ref[...]
    w = w_ref[...]
    for u, page in enumerate(pages):
        s = jnp.maximum(_bdot_nt(qi, page[...]), 0.0)
        o_ref[:, u * PAGE_SIZE:(u + 1) * PAGE_SIZE] = jnp.sum(w * s, axis=0, keepdims=True)


def _idx_scores_sample(page_table, qi8, w8, cache_kidx):
    db, npages = page_table.shape
    n = PAGES_PER_STEP
    grid_spec = pltpu.PrefetchScalarGridSpec(
        num_scalar_prefetch=1, grid=(db, npages // n),
        in_specs=[pl.BlockSpec((None,) + qi8.shape[1:], lambda b, s, pt: (b, 0, 0)),
                  pl.BlockSpec((None,) + w8.shape[1:], lambda b, s, pt: (b, 0, 0))]
                 + _page_specs(cache_kidx.shape[1:], n),
        out_specs=pl.BlockSpec((None, 1, n * PAGE_SIZE), lambda b, s, pt: (b, 0, s)))
    return pl.pallas_call(
        _idx_score_body, grid_spec=grid_spec,
        out_shape=jax.ShapeDtypeStruct((db, 1, npages * PAGE_SIZE), F32),
        compiler_params=_cparams(("parallel", "arbitrary"), 32), name="idx_scores_sample",
    )(page_table, qi8, w8, *([cache_kidx] * n))


def _select_sample_body(topk, idx_scale, past, sc_ref, qi_ref, sm_ref, bias_ref, key_s):
    rows = sc_ref.shape[0]
    sm = sm_ref[...]
    qi = qi_ref[...]
    ki = sm[:, 0:HEAD]
    new = jnp.zeros((rows, 1), F32)
    for h in range(IDX_HEADS):
        d = jnp.sum(qi[:, h * HEAD:(h + 1) * HEAD] * ki, axis=-1, keepdims=True)
        new = new + (sm[:, HEAD + h:HEAD + h + 1] * idx_scale) * jnp.maximum(d, 0.0)
    lane = lax.broadcasted_iota(I32, (rows, LANES), 1)
    key_s[:, 0:past] = _score_key(sc_ref[...])
    key_s[:, past:past + LANES] = _score_key(jnp.where(lane == 0, new, -jnp.inf))
    nchunks = (past + LANES) // LANES
    kth = _kth_key(key_s, rows, nchunks, LANES, topk)
    qpos = jnp.full((rows, 1), past, I32)
    _selection_bias(key_s, bias_ref, kth, rows, nchunks, LANES, topk, qpos)


def _select_sample(scores, qi, sm, topk, idx_scale):
    db, past = scores.shape
    full = lambda a: pl.BlockSpec(a.shape, lambda i: (0,) * a.ndim)
    return pl.pallas_call(
        functools.partial(_select_sample_body, topk, idx_scale, past), grid=(1,),
        in_specs=[full(scores), full(qi), full(sm)],
        out_specs=pl.BlockSpec((db, past + LANES), lambda i: (0, 0)),
        out_shape=jax.ShapeDtypeStruct((db, past + LANES), F32),
        scratch_shapes=[pltpu.VMEM((db, past + LANES), I32)],
        compiler_params=_cparams(("arbitrary",), 32), name="select_sample")(scores, qi, sm)


def _attend_sample_body(pt_ref, q_ref, bias_ref, bnew_ref, knew_ref, vnew_ref, *refs):
    n = PAGES_PER_STEP
    kpages, vpages = refs[:n], refs[n:2 * n]
    o_ref, m_s, l_s, acc_s = refs[2 * n:]
    s = pl.program_id(1)

    @pl.when(s == 0)
    def _():
        m_s[...] = jnp.full(m_s.shape, NEG_BIG, F32)
        l_s[...] = jnp.zeros(l_s.shape, F32)
        acc_s[...] = jnp.zeros(acc_s.shape, F32)

    q = q_ref[...] * (HEAD ** -0.5)

    def update(lg, pv):
        m = m_s[...]
        mn = jnp.maximum(m, jnp.max(lg, axis=-1, keepdims=True))
        p = jnp.exp(lg - mn)
        alpha = jnp.exp(m - mn)
        l_s[...] = alpha * l_s[...] + jnp.sum(p, axis=-1, keepdims=True)
        acc_s[...] = alpha * acc_s[...] + pv(p)
        m_s[...] = mn

    for u in range(n):
        lg = _bdot_nt(q, kpages[u][...]) + bias_ref[:, u * PAGE_SIZE:(u + 1) * PAGE_SIZE]
        update(lg, lambda p, u=u: _bdot(p, vpages[u][...]))

    @pl.when(s == pl.num_programs(1) - 1)
    def _():
        lg = jnp.sum(q * knew_ref[...], axis=-1, keepdims=True) + bnew_ref[:, 0:1]
        update(lg, lambda p: p * vnew_ref[...])
        o_ref[...] = acc_s[...] / l_s[...]


def _attend_sample(page_table, qexp, bias_past, bias_new, knew, vnew, cache_k, cache_v):
    db, npages = page_table.shape
    n = PAGES_PER_STEP
    nh, kvw = qexp.shape[1:]
    per_b = lambda a: pl.BlockSpec((None,) + a.shape[1:], lambda b, s, pt: (b,) + (0,) * (a.ndim - 1))
    grid_spec = pltpu.PrefetchScalarGridSpec(
        num_scalar_prefetch=1, grid=(db, npages // n),
        in_specs=[per_b(qexp),
                  pl.BlockSpec((None, 1, n * PAGE_SIZE), lambda b, s, pt: (b, 0, s)),
                  per_b(bias_new), per_b(knew), per_b(vnew)]
                 + _page_specs(cache_k.shape[1:], n) + _page_specs(cache_v.shape[1:], n),
        out_specs=per_b(qexp),
        scratch_shapes=[pltpu.VMEM((nh, 1), F32), pltpu.VMEM((nh, 1), F32), pltpu.VMEM((nh, kvw), F32)])
    return pl.pallas_call(
        _attend_sample_body, grid_spec=grid_spec,
        out_shape=jax.ShapeDtypeStruct(qexp.shape, F32),
        compiler_params=_cparams(("parallel", "arbitrary"), 32), name="attend_sample",
    )(page_table, qexp, bias_past, bias_new, knew, vnew, *([cache_k] * n), *([cache_v] * n))


def _merge_body(x_ref, mod_ref, ys_ref, ya_ref, gs_ref, ga_ref, wps_ref, wpa_ref, wo_ref, o_ref):
    d = x_ref.shape[1]
    merged = (jax.nn.sigmoid(gs_ref[...]) * _bdot(ys_ref[...], wps_ref[...])
              + jax.nn.sigmoid(ga_ref[...]) * _bdot(ya_ref[...], wpa_ref[...]))
    o_ref[...] = x_ref[...] + mod_ref[:, 2 * d:3 * d] * _bdot(merged, wo_ref[...])


def _merge(x, mod3, ys, ya, gs, ga, wps, wpa, wo, tm, tpb):
    m, d = x.shape
    x_spec, mod_spec = _row_specs(tm, tpb, d, mod3.shape[1])
    row = lambda n: pl.BlockSpec((tm, n), lambda i: (i, 0))
    return pl.pallas_call(
        _merge_body, grid=(m // tm,),
        in_specs=[x_spec, mod_spec, row(ys.shape[1]), row(ya.shape[1]), row(d), row(d),
                  _const_spec(wps.shape), _const_spec(wpa.shape), _const_spec(wo.shape)],
        out_specs=row(d), out_shape=jax.ShapeDtypeStruct((m, d), F32),
        compiler_params=_cparams(("parallel",), 48), name="merge")(x, mod3, ys, ya, gs, ga, wps, wpa, wo)


def _ffn_body(last_layer, x_ref, mod_ref, g_ref, fg_ref, wg_ref, wu_ref, wo_ref, o_ref):
    d = x_ref.shape[1]
    x = x_ref[...]
    h = _norm_mod(x, g_ref[...], mod_ref[:, 4 * d:5 * d], mod_ref[:, 3 * d:4 * d]).astype(BF16)
    gate = jnp.dot(h, wg_ref[...], preferred_element_type=F32)
    up = jnp.dot(h, wu_ref[...], preferred_element_type=F32)
    x2 = x + mod_ref[:, 5 * d:6 * d] * _bdot(_silu(gate) * up, wo_ref[...])
    if last_layer:
        ms = jnp.mean(x2 * x2, axis=-1, keepdims=True)
        x2 = x2 * lax.rsqrt(ms + NORM_EPS) * fg_ref[...]
    o_ref[...] = x2


def _ffn(x, mod3, g, fg, wg, wu, wo, tm, tpb, last_layer):
    m, d = x.shape
    x_spec, mod_spec = _row_specs(tm, tpb, d, mod3.shape[1])
    return pl.pallas_call(
        functools.partial(_ffn_body, last_layer), grid=(m // tm,),
        in_specs=[x_spec, mod_spec, _const_spec((1, d)), _const_spec((1, d)),
                  _const_spec(wg.shape), _const_spec(wu.shape), _const_spec(wo.shape)],
        out_specs=pl.BlockSpec((tm, d), lambda i: (i, 0)), out_shape=jax.ShapeDtypeStruct((m, d), F32),
        compiler_params=_cparams(("parallel",), 56), name="ffn")(x, mod3, g, fg, wg, wu, wo)


def _rope_tables(pos):
    half = HEAD // 2
    inv = ROPE_THETA ** (-jnp.arange(half, dtype=F32) / half)
    ang = pos.astype(F32)[:, None] * inv[None, :]
    cos = jnp.tile(jnp.cos(ang), (1, LANES // half))
    sin = jnp.tile(jnp.sin(ang), (1, LANES // half))
    first = (jnp.arange(LANES) % HEAD) < half
    return cos, jnp.where(first, -sin, 0.0), jnp.where(first, 0.0, sin)


def _q_head_order(n_heads):
    rep = n_heads // ATTN_KV_HEADS
    order = []
    for c in range(ATTN_KV_HEADS // 2):
        for j in range(rep):
            order += [2 * c * rep + j, (2 * c + 1) * rep + j]
    return np.asarray(order)


def kernel(x_prompt, x_sample, cache_k, cache_v, cache_kidx, state_conv, state_ssm, page_table, c_prompt, c_sample, w_ada, b_ada, norm1_g, w_in, conv_w, conv_b, dt_bias, a_log, d_skip, ssm_norm_g, w_proj_ssm, w_proj_attn, w_out, norm2_g, w_ffn_in, w_ffn_out, final_g):
    nb, t, d = x_prompt.shape
    db, ds, _ = x_sample.shape
    depth = w_in.shape[0]
    assert ds == 1 and t % Q_BLOCK == 0 and t % SSM_CHUNK == 0
    n_heads_ssm = dt_bias.shape[1]
    inner = n_heads_ssm * HEAD
    gn = SSM_GROUPS * SSM_STATE
    cdim = inner + 2 * gn
    kvw = ATTN_KV_HEADS * HEAD
    aw = w_proj_attn.shape[1]
    n_heads = aw // HEAD
    iw = IDX_HEADS * HEAD
    ffn_hidden = w_ffn_out.shape[1]
    npages = page_table.shape[1]
    past = npages * PAGE_SIZE
    topk_p = min(TOPK_MAX, t // 4)
    topk_s = min(TOPK_MAX, (past + ds) // 4)
    assert past + ds >= topk_s and npages % PAGES_PER_STEP == 0
    idx_scale = IDX_HEADS ** -0.5 * HEAD ** -0.5
    tm_p = 256 if t % 256 == 0 else 128
    tpb_p = t // tm_p

    splits = np.cumsum([inner, inner, gn, gn, n_heads_ssm, aw, kvw, kvw, iw, HEAD, IDX_HEADS, d])
    order = _q_head_order(n_heads)
    inv_order = np.argsort(order)
    group_onehot = jnp.asarray(np.arange(n_heads)[:, None] // (n_heads // ATTN_KV_HEADS)
                               == np.arange(ATTN_KV_HEADS)[None, :], F32)

    cos_p, slo_p, shi_p = _rope_tables(jnp.arange(t, dtype=I32))
    tabs_p = (cos_p, slo_p, shi_p)
    tabs_s = tuple(jnp.broadcast_to(a, (db, LANES)) for a in _rope_tables(past + jnp.arange(ds, dtype=I32)))

    rows_c = nb + db
    c_all = jnp.concatenate([c_prompt, c_sample, jnp.zeros((-rows_c % 8, d), F32)], axis=0)

    yp = x_prompt.reshape(nb * t, d)
    ys = x_sample.reshape(db, d)
    outs_p, outs_s = [], []
    for l in range(depth):
        (wz, wxs, wbm, wcm, wdt, wq, wk, wv, wqi, wki, wwi, wgs, wga) = jnp.split(w_in[l], splits, axis=1)
        w_ssm = jnp.concatenate([wz, wxs, wbm, wcm, wdt, jnp.zeros((d, LANES - n_heads_ssm), F32)],
                                axis=1).astype(BF16)
        wq_perm = wq.reshape(d, n_heads, HEAD)[:, order].reshape(d, aw)
        w_attn = jnp.concatenate([wq_perm, wk, wv, wqi, wgs, wga, wki, wwi,
                                  jnp.zeros((d, LANES - HEAD - IDX_HEADS), F32)], axis=1).astype(BF16)
        wps = w_proj_ssm[l].astype(BF16)
        wpa = w_proj_attn[l].reshape(n_heads, HEAD, d)[order].reshape(aw, d).astype(BF16)
        wo = w_out[l].astype(BF16)
        wg = w_ffn_in[l][:, :ffn_hidden].astype(BF16)
        wu = w_ffn_in[l][:, ffn_hidden:].astype(BF16)
        wfo = w_ffn_out[l].astype(BF16)
        g1 = norm1_g[l][None, :]
        g2 = norm2_g[l][None, :]
        a_neg = -jnp.exp(a_log[l])
        pad_h = LANES - n_heads_ssm
        dtb_row = jnp.pad(dt_bias[l], (0, pad_h))[None, :]
        a_row = jnp.pad(a_neg, (0, pad_h))[None, :]
        dsk_row = jnp.repeat(d_skip[l], HEAD)[None, :]
        ng_row = ssm_norm_g[l][None, :]
        cw = conv_w[l]
        cb = conv_b[l][None, :]

        mod = _ada(c_all, w_ada[l].astype(BF16), b_ada[l][None, :])
        mod_p = mod[:nb][:, None, :]
        mod_s = mod[nb:nb + db][None]

        z, xbc, dtr = _inproj_ssm(yp, mod_p, g1, w_ssm, tm_p, tpb_p, inner, cdim)
        (q_bf, k, k_bf, v, v_bf, qi_bf, gs, ga, sm, ki2) = _inproj_attn(
            yp, mod_p, g1, w_attn, tabs_p, tm_p, tpb_p, aw, kvw, iw)
        y_ssm, st = _ssd_prompt(z, xbc, dtr, cw, cb, dtb_row, a_row, dsk_row, ng_row, nb)
        y_attn = _dsa_prompt(q_bf, qi_bf, sm, ki2, k_bf, v_bf, nb, topk_p, idx_scale)
        x1 = _merge(yp, mod_p, y_ssm, y_attn, gs, ga, wps, wpa, wo, tm_p, tpb_p)
        yp_next = _ffn(x1, mod_p, g2, final_g[None, :], wg, wu, wfo, tm_p, tpb_p, l == depth - 1)
        outs_p.append((k.reshape(nb, t, ATTN_KV_HEADS, HEAD), v.reshape(nb, t, ATTN_KV_HEADS, HEAD),
                       sm[:, :HEAD].reshape(nb, t, HEAD),
                       xbc.reshape(nb, t, cdim)[:, t - (SSM_CONV - 1):],
                       st.reshape(nb, n_heads_ssm, HEAD, SSM_STATE)))

        z_s, xbc_s, dtr_s = _inproj_ssm(ys, mod_s, g1, w_ssm, db, 1, inner, cdim)
        (q_s, k_s, _, v_s, _, qi_s, gs_s, ga_s, sm_s, _) = _inproj_attn(
            ys, mod_s, g1, w_attn, tabs_s, db, 1, aw, kvw, iw)
        npair = inner // LANES
        nbc = 2 * gn // LANES
        sc = state_conv[l]
        y_ssm_s, st_s = _ssd_sample(
            z_s.reshape(db, npair, LANES), xbc_s[:, :inner].reshape(db, npair, LANES),
            xbc_s[:, inner:].reshape(db, nbc, LANES),
            sc[:, :, :inner].reshape(db, SSM_CONV - 1, npair, LANES),
            sc[:, :, inner:].reshape(db, SSM_CONV - 1, nbc, LANES),
            jnp.repeat(dtr_s[:, :n_heads_ssm], HEAD, axis=1).reshape(db, npair, LANES),
            state_ssm[l].reshape(db, inner, SSM_STATE),
            cw[:, :inner].reshape(SSM_CONV, npair, LANES), cw[:, inner:].reshape(SSM_CONV, nbc, LANES),
            cb[:, :inner].reshape(npair, LANES), cb[:, inner:].reshape(nbc, LANES),
            jnp.repeat(dt_bias[l], HEAD).reshape(npair, LANES), jnp.repeat(a_neg, HEAD).reshape(npair, LANES),
            dsk_row.reshape(npair, LANES), ng_row.reshape(npair, LANES))

        qi_f = qi_s.astype(F32)
        qi8 = jnp.pad(qi_f.reshape(db, IDX_HEADS, HEAD), ((0, 0), (0, 8 - IDX_HEADS), (0, 0)))
        w8 = jnp.broadcast_to(jnp.pad(sm_s[:, HEAD:HEAD + IDX_HEADS] * idx_scale,
                                      ((0, 0), (0, 8 - IDX_HEADS)))[:, :, None], (db, 8, LANES))
        scores = _idx_scores_sample(page_table, qi8, w8, cache_kidx[l])
        bias = _select_sample(scores.reshape(db, past), qi_f, sm_s, topk_s, idx_scale)
        q_orig = q_s.astype(F32).reshape(db, n_heads, HEAD)[:, inv_order]
        qexp = (q_orig[:, :, None, :] * group_onehot[None, :, :, None]).reshape(db, n_heads, kvw)
        att = _attend_sample(page_table, qexp, bias[:, None, :past], bias[:, None, past:],
                             k_s[:, None, :], v_s[:, None, :],
                             cache_k[l].reshape(-1, PAGE_SIZE, kvw), cache_v[l].reshape(-1, PAGE_SIZE, kvw))
        att = jnp.sum(att.reshape(db, n_heads, ATTN_KV_HEADS, HEAD) * group_onehot[None, :, :, None], axis=2)
        y_attn_s = att[:, order].reshape(db, aw)
        x1_s = _merge(ys, mod_s, y_ssm_s.reshape(db, inner), y_attn_s, gs_s, ga_s, wps, wpa, wo, db, 1)
        ys_next = _ffn(x1_s, mod_s, g2, final_g[None, :], wg, wu, wfo, db, 1, l == depth - 1)
        outs_s.append((k_s.reshape(db, ds, ATTN_KV_HEADS, HEAD), v_s.reshape(db, ds, ATTN_KV_HEADS, HEAD),
                       sm_s[:, :HEAD].reshape(db, ds, HEAD),
                       jnp.concatenate([sc[:, 1:], xbc_s[:, None, :]], axis=1),
                       st_s.reshape(db, n_heads_ssm, HEAD, SSM_STATE)))
        yp, ys = yp_next, ys_next

    stack = lambda outs, i: jnp.stack([o[i] for o in outs], axis=0)
    return (yp.reshape(nb, t, d), ys.reshape(db, ds, d),
            stack(outs_p, 0), stack(outs_p, 1), stack(outs_p, 2), stack(outs_p, 3), stack(outs_p, 4),
            stack(outs_s, 0), stack(outs_s, 1), stack(outs_s, 2), stack(outs_s, 3), stack(outs_s, 4))
```

```python
import functools

import jax
import jax.numpy as jnp
import numpy as np
from jax import lax
from jax.experimental import pallas as pl
from jax.experimental.pallas import tpu as pltpu

F32, BF16, I32 = jnp.float32, jnp.bfloat16, jnp.int32
HIGHEST = lax.Precision.HIGHEST

LANES = 128
HEAD = 64
SSM_STATE = 128
SSM_GROUPS = 4
SSM_CONV = 4
SSM_CHUNK = 128
ATTN_KV_HEADS = 4
IDX_HEADS = 4
TOPK_MAX = 256
Q_BLOCK = 128
PAGE_SIZE = 128
ROPE_THETA = 10000.0
NORM_EPS = 1e-6
KEY_BLOCK = 512
PAGES_PER_STEP = 4
INT_MIN = -(2 ** 31)
NEG_BIG = -1e30


def _cparams(sem, vmem_mb):
    return pltpu.CompilerParams(dimension_semantics=sem, vmem_limit_bytes=vmem_mb << 20)


def _bdot(a, b):
    return jnp.dot(a.astype(BF16), b.astype(BF16), preferred_element_type=F32)


def _bdot_nt(a, b):
    return lax.dot_general(a.astype(BF16), b.astype(BF16), (((1,), (1,)), ((), ())),
                           preferred_element_type=F32)


def _silu(x):
    return x * jax.nn.sigmoid(x)


def _softplus(x):
    return jnp.maximum(x, 0.0) + jnp.log1p(jnp.exp(-jnp.abs(x)))


def _norm_mod(x, g, scale, shift):
    ms = jnp.mean(x * x, axis=-1, keepdims=True)
    return (x * lax.rsqrt(ms + NORM_EPS) * g) * (1.0 + scale) + shift


def _rope128(x, cos, sin_lo, sin_hi):
    return x * cos + pltpu.roll(x, 96, 1) * sin_lo + pltpu.roll(x, 32, 1) * sin_hi


def _rope_wide(x, cos, sin_lo, sin_hi):
    parts = [_rope128(x[:, j:j + LANES], cos, sin_lo, sin_hi) for j in range(0, x.shape[1], LANES)]
    return parts[0] if len(parts) == 1 else jnp.concatenate(parts, axis=1)


def _ada_body(c_ref, w_ref, b_ref, o_ref):
    o_ref[...] = _bdot(_silu(c_ref[...]), w_ref[...]) + b_ref[...]


def _ada(c_all, w_bf, b):
    mp, d = c_all.shape
    n = w_bf.shape[1]
    tn = n // 4
    return pl.pallas_call(
        _ada_body, grid=(n // tn,),
        in_specs=[pl.BlockSpec((mp, d), lambda j: (0, 0)),
                  pl.BlockSpec((d, tn), lambda j: (0, j)),
                  pl.BlockSpec((1, tn), lambda j: (0, j))],
        out_specs=pl.BlockSpec((mp, tn), lambda j: (0, j)),
        out_shape=jax.ShapeDtypeStruct((mp, n), F32),
        compiler_params=_cparams(("arbitrary",), 32), name="ada")(c_all, w_bf, b)


def _inproj_ssm_body(inner, cdim, x_ref, mod_ref, g_ref, w_ref, z_ref, xbc_ref, dt_ref):
    d = x_ref.shape[1]
    h = _norm_mod(x_ref[...], g_ref[...], mod_ref[:, d:2 * d], mod_ref[:, 0:d]).astype(BF16)
    z_ref[...] = jnp.dot(h, w_ref[:, 0:inner], preferred_element_type=F32)
    xbc_ref[...] = jnp.dot(h, w_ref[:, inner:inner + cdim], preferred_element_type=F32)
    dt_ref[...] = jnp.dot(h, w_ref[:, inner + cdim:inner + cdim + LANES], preferred_element_type=F32)


def _row_specs(tm, tpb, d, mod_rows):
    x_spec = pl.BlockSpec((tm, d), lambda m: (m, 0))
    mod_spec = pl.BlockSpec((None, mod_rows, 6 * d), lambda m: (m // tpb, 0, 0))
    return x_spec, mod_spec


def _const_spec(shape):
    return pl.BlockSpec(shape, lambda m: (0,) * len(shape))


def _inproj_ssm(x, mod3, g, w_bf, tm, tpb, inner, cdim):
    m, d = x.shape
    x_spec, mod_spec = _row_specs(tm, tpb, d, mod3.shape[1])
    row = lambda n: pl.BlockSpec((tm, n), lambda i: (i, 0))
    return pl.pallas_call(
        functools.partial(_inproj_ssm_body, inner, cdim), grid=(m // tm,),
        in_specs=[x_spec, mod_spec, _const_spec((1, d)), _const_spec(w_bf.shape)],
        out_specs=[row(inner), row(cdim), row(LANES)],
        out_shape=[jax.ShapeDtypeStruct((m, inner), F32), jax.ShapeDtypeStruct((m, cdim), F32),
                   jax.ShapeDtypeStruct((m, LANES), F32)],
        compiler_params=_cparams(("parallel",), 52), name="inproj_ssm")(x, mod3, g, w_bf)


def _inproj_attn_body(aw, kvw, iw, x_ref, mod_ref, g_ref, w_ref, cos_ref, slo_ref, shi_ref,
                      q_ref, k_ref, kb_ref, v_ref, vb_ref, qi_ref, gs_ref, ga_ref, sm_ref, ki2_ref):
    d = x_ref.shape[1]
    h = _norm_mod(x_ref[...], g_ref[...], mod_ref[:, d:2 * d], mod_ref[:, 0:d]).astype(BF16)
    cos, slo, shi = cos_ref[...], slo_ref[...], shi_ref[...]

    def proj(a, b):
        return jnp.dot(h, w_ref[:, a:b], preferred_element_type=F32)

    o = 0
    q_ref[...] = _rope_wide(proj(o, o + aw), cos, slo, shi).astype(BF16)
    o += aw
    k = _rope_wide(proj(o, o + kvw), cos, slo, shi)
    k_ref[...] = k
    kb_ref[...] = k.astype(BF16)
    o += kvw
    v = proj(o, o + kvw)
    v_ref[...] = v
    vb_ref[...] = v.astype(BF16)
    o += kvw
    qi_ref[...] = _rope_wide(proj(o, o + iw), cos, slo, shi).astype(BF16)
    o += iw
    gs_ref[...] = proj(o, o + d)
    o += d
    ga_ref[...] = proj(o, o + d)
    o += d
    s = proj(o, o + LANES)
    lane = lax.broadcasted_iota(I32, s.shape, 1)
    sm = jnp.where(lane < HEAD, _rope128(s, cos, slo, shi), s)
    sm_ref[...] = sm
    ki2_ref[...] = jnp.where(lane < HEAD, sm, pltpu.roll(sm, HEAD, 1)).astype(BF16)


def _inproj_attn(x, mod3, g, w_bf, tabs, tm, tpb, aw, kvw, iw):
    m, d = x.shape
    x_spec, mod_spec = _row_specs(tm, tpb, d, mod3.shape[1])
    ntab = tabs[0].shape[0] // tm
    tab_spec = pl.BlockSpec((tm, LANES), lambda i: (i % ntab, 0))
    row = lambda n: pl.BlockSpec((tm, n), lambda i: (i, 0))
    widths = [(aw, BF16), (kvw, F32), (kvw, BF16), (kvw, F32), (kvw, BF16), (iw, BF16),
              (d, F32), (d, F32), (LANES, F32), (LANES, BF16)]
    return pl.pallas_call(
        functools.partial(_inproj_attn_body, aw, kvw, iw), grid=(m // tm,),
        in_specs=[x_spec, mod_spec, _const_spec((1, d)), _const_spec(w_bf.shape),
                  tab_spec, tab_spec, tab_spec],
        out_specs=[row(n) for n, _ in widths],
        out_shape=[jax.ShapeDtypeStruct((m, n), dt) for n, dt in widths],
        compiler_params=_cparams(("parallel",), 52), name="inproj_attn")(x, mod3, g, w_bf, *tabs)


def _ssd_body(inner, z_ref, xbc_ref, dtr_ref, cw_ref, cb_ref, dtb_ref, a_ref, dsk_ref, ng_ref,
              y_ref, st_ref, full_s, act_s, st_s, y_s):
    c = pl.program_id(1)
    Q, N = SSM_CHUNK, SSM_STATE
    cdim = xbc_ref.shape[1]
    heads_per_group = inner // HEAD // SSM_GROUPS
    gw = inner // SSM_GROUPS

    @pl.when(c == 0)
    def _():
        full_s[0:8, :] = jnp.zeros((8, cdim), F32)
        st_s[...] = jnp.zeros(st_s.shape, F32)

    full_s[8:8 + Q, :] = xbc_ref[...]
    for j in range(0, cdim, 512):
        acc = cb_ref[:, j:j + 512] + full_s[8:8 + Q, j:j + 512] * cw_ref[3:4, j:j + 512]
        for i in range(SSM_CONV - 1):
            acc = acc + full_s[5 + i:5 + i + Q, j:j + 512] * cw_ref[i:i + 1, j:j + 512]
        act_s[:, j:j + 512] = _silu(acc)
    full_s[0:8, :] = full_s[Q:Q + 8, :]

    dt = _softplus(dtr_ref[...] + dtb_ref[...])
    row = lax.broadcasted_iota(I32, (Q, Q), 0)
    col = lax.broadcasted_iota(I32, (Q, Q), 1)
    tri = row >= col
    acs = jnp.dot(tri.astype(F32), dt * a_ref[...], precision=HIGHEST, preferred_element_type=F32)
    acs_t, dt_t = acs.T, dt.T
    last = acs[Q - 1:Q, :]
    wdt = jnp.exp(last - acs) * dt
    eacs = jnp.exp(acs)
    cdec = jnp.exp(last)
    low = lax.broadcasted_iota(I32, (Q, LANES), 1) < HEAD
    low1 = low[0:1, :]

    for g in range(SSM_GROUPS):
        bg = act_s[:, inner + g * N:inner + (g + 1) * N]
        cg = act_s[:, inner + SSM_GROUPS * N + g * N:inner + SSM_GROUPS * N + (g + 1) * N]
        cb = _bdot_nt(cg, bg)
        bg_t = bg.T.astype(BF16)
        for p in range(heads_per_group // 2):
            h0 = g * heads_per_group + 2 * p
            js = slice(h0 * HEAD, h0 * HEAD + LANES)
            xp = act_s[:, js]
            xp_bf = xp.astype(BF16)
            stp = st_s[:, js]
            stp_bf = stp.astype(BF16)
            ys = []
            for h in (h0, h0 + 1):
                seg = acs[:, h:h + 1] - acs_t[h:h + 1, :]
                decay = jnp.exp(jnp.where(tri, seg, -jnp.inf))
                m = (cb * decay) * dt_t[h:h + 1, :]
                ce = cg * eacs[:, h:h + 1]
                ys.append(_bdot(m, xp_bf) + _bdot(ce, stp_bf))
            y_s[:, js] = jnp.where(low, ys[0], ys[1])
            wcol = jnp.where(low, wdt[:, h0:h0 + 1], wdt[:, h0 + 1:h0 + 2])
            dst = jnp.dot(bg_t, (xp * wcol).astype(BF16), preferred_element_type=F32)
            cd = jnp.where(low1, cdec[:, h0:h0 + 1], cdec[:, h0 + 1:h0 + 2])
            st_s[:, js] = stp * cd + dst

    for g in range(SSM_GROUPS):
        gs = slice(g * gw, (g + 1) * gw)
        y = y_s[:, gs] + dsk_ref[:, gs] * act_s[:, gs]
        y = y * _silu(z_ref[:, gs])
        ms = jnp.mean(y * y, axis=-1, keepdims=True)
        y_ref[:, gs] = (y * lax.rsqrt(ms + NORM_EPS) * ng_ref[:, gs]).astype(BF16)

    @pl.when(c == pl.num_programs(1) - 1)
    def _():
        st_ref[...] = st_s[...].T


def _ssd_prompt(z, xbc, dtr, cw, cb, dtb, a, dsk, ng, nb):
    m, inner = z.shape
    cdim = xbc.shape[1]
    nc = m // nb // SSM_CHUNK
    row = lambda n: pl.BlockSpec((SSM_CHUNK, n), lambda b, c: (b * nc + c, 0))
    const = lambda shape: pl.BlockSpec(shape, lambda b, c: (0,) * len(shape))
    return pl.pallas_call(
        functools.partial(_ssd_body, inner), grid=(nb, nc),
        in_specs=[row(inner), row(cdim), row(LANES), const(cw.shape), const(cb.shape),
                  const(dtb.shape), const(a.shape), const(dsk.shape), const(ng.shape)],
        out_specs=[row(inner), pl.BlockSpec((None, inner, SSM_STATE), lambda b, c: (b, 0, 0))],
        out_shape=[jax.ShapeDtypeStruct((m, inner), BF16),
                   jax.ShapeDtypeStruct((nb, inner, SSM_STATE), F32)],
        scratch_shapes=[pltpu.VMEM((SSM_CHUNK + 8, cdim), F32), pltpu.VMEM((SSM_CHUNK, cdim), F32),
                        pltpu.VMEM((SSM_STATE, inner), F32), pltpu.VMEM((SSM_CHUNK, inner), F32)],
        compiler_params=_cparams(("parallel", "arbitrary"), 40), name="ssd_prompt",
    )(z, xbc, dtr, cw, cb, dtb, a, dsk, ng)


def _ssd_step_body(z_ref, xs_ref, bc_ref, cxs_ref, cbc_ref, dtr_ref, st_ref,
                   wxs_ref, wbc_ref, bxs_ref, bbc_ref, dtb_ref, a_ref, dsk_ref, ng_ref,
                   y_ref, sto_ref):
    G = SSM_GROUPS
    last = SSM_CONV - 1
    xs = bxs_ref[...] + xs_ref[...] * wxs_ref[last]
    bc = bbc_ref[...] + bc_ref[...] * wbc_ref[last]
    for i in range(last):
        xs = xs + cxs_ref[i] * wxs_ref[i]
        bc = bc + cbc_ref[i] * wbc_ref[i]
    xs, bc = _silu(xs), _silu(bc)
    dt = _softplus(dtr_ref[...] + dtb_ref[...])
    dec = jnp.exp(dt * a_ref[...])
    xdt = xs * dt
    npair = xs.shape[0]
    pairs_per_group = npair // G
    r = lax.broadcasted_iota(I32, (LANES, LANES), 0)
    cidx = lax.broadcasted_iota(I32, (LANES, LANES), 1)
    eye = (r == cidx).astype(F32)
    nt = (((1,), (1,)), ((), ()))
    dec_t = lax.dot_general(eye, dec, nt, precision=HIGHEST, preferred_element_type=F32)
    xdt_t = lax.dot_general(eye, xdt, nt, precision=HIGHEST, preferred_element_type=F32)
    rows = lax.broadcasted_iota(I32, (npair, 1), 0)
    cbv = jnp.sum(bc[0:G, :] * bc[G:2 * G, :], axis=-1, keepdims=True)
    cbx = jnp.zeros((npair, 1), F32)
    for g in range(G):
        cbx = cbx + jnp.where(rows // pairs_per_group == g, cbv[g:g + 1, :], 0.0)
    c_bf = bc.astype(BF16)
    yoff = jnp.zeros(xs.shape, F32)
    for j in range(npair):
        g = j // pairs_per_group
        s = st_ref[j * LANES:(j + 1) * LANES, :]
        sto_ref[j * LANES:(j + 1) * LANES, :] = s * dec_t[:, j:j + 1] + xdt_t[:, j:j + 1] * bc[g:g + 1, :]
        rj = _bdot_nt(c_bf, s)
        yoff = yoff + jnp.where(rows == j, rj[G + g:G + g + 1, :], 0.0)
    y = yoff * dec + cbx * dt * xs + dsk_ref[...] * xs
    y = y * _silu(z_ref[...])
    ssq = jnp.sum(y * y, axis=-1, keepdims=True)
    msx = jnp.zeros((npair, 1), F32)
    for g in range(G):
        ing = rows // pairs_per_group == g
        tot = jnp.sum(jnp.where(ing, ssq, 0.0), axis=0, keepdims=True)
        msx = msx + jnp.where(ing, tot, 0.0)
    msx = msx / (pairs_per_group * LANES)
    y_ref[...] = y * lax.rsqrt(msx + NORM_EPS) * ng_ref[...]


def _ssd_sample(z, xs, bc, cxs, cbc, dtr, st, wxs, wbc, bxs, bbc, dtb, a, dsk, ng):
    db, npair, _ = z.shape
    per_b = lambda shape: pl.BlockSpec((None,) + shape, lambda b: (b,) + (0,) * len(shape))
    const = lambda arr: pl.BlockSpec(arr.shape, lambda b: (0,) * arr.ndim)
    return pl.pallas_call(
        _ssd_step_body, grid=(db,),
        in_specs=[per_b(z.shape[1:]), per_b(xs.shape[1:]), per_b(bc.shape[1:]), per_b(cxs.shape[1:]),
                  per_b(cbc.shape[1:]), per_b(dtr.shape[1:]), per_b(st.shape[1:]),
                  const(wxs), const(wbc), const(bxs), const(bbc), const(dtb), const(a), const(dsk), const(ng)],
        out_specs=[per_b(z.shape[1:]), per_b(st.shape[1:])],
        out_shape=[jax.ShapeDtypeStruct(z.shape, F32), jax.ShapeDtypeStruct(st.shape, F32)],
        compiler_params=_cparams(("parallel",), 32), name="ssd_sample",
    )(z, xs, bc, cxs, cbc, dtr, st, wxs, wbc, bxs, bbc, dtb, a, dsk, ng)


def _score_key(score):
    bits = pltpu.bitcast(score, I32)
    return jnp.where(bits < 0, bits ^ jnp.int32(0x7FFFFFFF), bits)


def _lane_fold(x):
    acc = x[:, 0:LANES]
    for j in range(LANES, x.shape[1], LANES):
        acc = acc + x[:, j:j + LANES]
    return acc


def _kth_key(key_s, rows, nchunks, cw, topk):
    def count_ge(cand):
        def body(s, acc):
            key = key_s[:, pl.ds(pl.multiple_of(s * cw, cw), cw)]
            return acc + _lane_fold(jnp.where(key >= cand, 1.0, 0.0))
        acc = lax.fori_loop(0, nchunks, body, jnp.zeros((rows, LANES), F32))
        return jnp.sum(acc, axis=-1, keepdims=True)

    def bit_step(t, kth):
        cand = kth ^ lax.shift_left(jnp.int32(1), 31 - t)
        return jnp.where(count_ge(cand) >= topk, cand, kth)

    return lax.fori_loop(0, 32, bit_step, jnp.full((rows, 1), INT_MIN, I32))


def _selection_bias(key_s, bias_s, kth, rows, nchunks, cw, topk, qpos):
    def gt_body(s, acc):
        key = key_s[:, pl.ds(pl.multiple_of(s * cw, cw), cw)]
        return acc + _lane_fold(jnp.where(key > kth, 1.0, 0.0))
    cnt_gt = jnp.sum(lax.fori_loop(0, nchunks, gt_body, jnp.zeros((rows, LANES), F32)),
                     axis=-1, keepdims=True)
    need = topk - cnt_gt
    r = lax.broadcasted_iota(I32, (LANES, LANES), 0)
    c = lax.broadcasted_iota(I32, (LANES, LANES), 1)
    upper = jnp.where(r <= c, 1.0, 0.0).astype(BF16)
    lane = lax.broadcasted_iota(I32, (1, LANES), 1)

    def body(s, carry):
        for j in range(0, cw, LANES):
            off = pl.multiple_of(s * cw + j, LANES)
            key = key_s[:, pl.ds(off, LANES)]
            eq = key == kth
            eqf = jnp.where(eq, 1.0, 0.0)
            incl = jnp.dot(eqf.astype(BF16), upper, preferred_element_type=F32)
            rank = carry + incl - eqf
            sel = (key > kth) | (eq & (rank < need))
            sel = sel & ((off + lane) <= qpos)
            bias_s[:, pl.ds(off, LANES)] = jnp.where(sel, 0.0, -jnp.inf)
            carry = carry + incl[:, LANES - 1:LANES]
        return carry

    lax.fori_loop(0, nchunks, body, jnp.zeros((rows, 1), F32))


def _dsa_body(topk, idx_scale, q_ref, qi_ref, sm_ref, ki2_ref, k_ref, v_ref, o_ref,
              bias_s, key_s, qs_s, m_s, acc_s):
    i = pl.program_id(1)
    QB, KB = Q_BLOCK, KEY_BLOCK
    nkb = (i * QB + QB + KB - 1) // KB
    qpos = i * QB + lax.broadcasted_iota(I32, (QB, 1), 0)
    low = lax.broadcasted_iota(I32, (QB, LANES), 1) < HEAD
    zero_bf = jnp.zeros((QB, LANES), BF16)

    qi = qi_ref[...]
    sm = sm_ref[...]
    qh, wh = [], []
    for h in range(IDX_HEADS):
        chunk = qi[:, (h // 2) * LANES:(h // 2 + 1) * LANES]
        qh.append(jnp.where(low if h % 2 == 0 else ~low, chunk, zero_bf))
        wh.append(sm[:, HEAD + h:HEAD + h + 1] * idx_scale)

    def score_body(s, carry):
        off = pl.multiple_of(s * KB, KB)
        kib = ki2_ref[pl.ds(off, KB), :]
        acc = jnp.zeros((QB, KB), F32)
        for h in range(IDX_HEADS):
            acc = acc + wh[h] * jnp.maximum(_bdot_nt(qh[h], kib), 0.0)
        kpos = off + lax.broadcasted_iota(I32, (1, KB), 1)
        acc = jnp.where(kpos <= qpos, acc, -jnp.inf)
        key_s[:, pl.ds(off, KB)] = _score_key(acc)
        return carry

    lax.fori_loop(0, nkb, score_body, 0)
    kth = _kth_key(key_s, QB, nkb, KB, topk)
    _selection_bias(key_s, bias_s, kth, QB, nkb, KB, topk, qpos)

    q = q_ref[...]
    nchunk = q.shape[1] // LANES
    per_kv_chunk = nchunk // (ATTN_KV_HEADS // 2)
    nstack = 2 * per_kv_chunk
    scale = jnp.asarray(HEAD ** -0.5, BF16)
    for cj in range(nchunk):
        kvc, j = divmod(cj, per_kv_chunk)
        qc = q[:, cj * LANES:(cj + 1) * LANES] * scale
        for half in range(2):
            r0 = (kvc * nstack + half * per_kv_chunk + j) * QB
            qs_s[r0:r0 + QB, :] = jnp.where(low if half == 0 else ~low, qc, zero_bf)

    low_kb = lax.broadcasted_iota(I32, (KB, LANES), 1) < HEAD
    one_bf = jnp.ones((KB, LANES), BF16)
    for kvc in range(ATTN_KV_HEADS // 2):
        m_s[...] = jnp.full(m_s.shape, NEG_BIG, F32)
        acc_s[...] = jnp.zeros(acc_s.shape, F32)

        def att_body(s, carry, kvc=kvc):
            off = pl.multiple_of(s * KB, KB)
            kb = k_ref[pl.ds(off, KB), kvc * LANES:(kvc + 1) * LANES]
            vb = v_ref[pl.ds(off, KB), kvc * LANES:(kvc + 1) * LANES]
            lg = _bdot_nt(qs_s[kvc * nstack * QB:(kvc + 1) * nstack * QB, :], kb)
            lg = (lg.reshape(nstack, QB, KB) + bias_s[:, pl.ds(off, KB)][None]).reshape(nstack * QB, KB)
            m = m_s[...]
            mn = jnp.maximum(m, jnp.max(lg, axis=-1, keepdims=True))
            p = jnp.exp(lg - mn).astype(BF16)
            m_s[...] = mn
            hrows = per_kv_chunk * QB
            pv = jnp.concatenate(
                [jnp.dot(p[:hrows], jnp.where(low_kb, vb, one_bf), preferred_element_type=F32),
                 jnp.dot(p[hrows:], jnp.where(low_kb, one_bf, vb), preferred_element_type=F32)], axis=0)
            acc_s[...] = jnp.exp(m - mn) * acc_s[...] + pv
            return carry

        lax.fori_loop(0, nkb, att_body, 0)
        acc = acc_s[...]
        out = acc / pltpu.roll(acc, HEAD, 1)
        for j in range(per_kv_chunk):
            cj = kvc * per_kv_chunk + j
            lo_rows = out[j * QB:(j + 1) * QB, :]
            hi_rows = out[(per_kv_chunk + j) * QB:(per_kv_chunk + j + 1) * QB, :]
            o_ref[:, cj * LANES:(cj + 1) * LANES] = jnp.where(low, lo_rows, hi_rows).astype(BF16)


def _dsa_prompt(q, qi, sm, ki2, k, v, nb, topk, idx_scale):
    m, aw = q.shape
    t = m // nb
    nq = t // Q_BLOCK
    tpad = -(-t // KEY_BLOCK) * KEY_BLOCK
    stack_rows = aw // HEAD // (ATTN_KV_HEADS // 2) * Q_BLOCK
    row = lambda n: pl.BlockSpec((Q_BLOCK, n), lambda b, i: (b * nq + i, 0))
    per_b = lambda n: pl.BlockSpec((t, n), lambda b, i: (b, 0))
    return pl.pallas_call(
        functools.partial(_dsa_body, topk, idx_scale), grid=(nb, nq),
        in_specs=[row(aw), row(qi.shape[1]), row(LANES), per_b(LANES), per_b(k.shape[1]), per_b(v.shape[1])],
        out_specs=row(aw),
        out_shape=jax.ShapeDtypeStruct((m, aw), BF16),
        scratch_shapes=[pltpu.VMEM((Q_BLOCK, tpad), F32), pltpu.VMEM((Q_BLOCK, tpad), I32),
                        pltpu.VMEM((2 * stack_rows, LANES), BF16), pltpu.VMEM((stack_rows, 1), F32),
                        pltpu.VMEM((stack_rows, LANES), F32)],
        compiler_params=_cparams(("parallel", "arbitrary"), 48), name="dsa_prompt",
    )(q, qi, sm, ki2, k, v)


def _page_specs(block, n):
    def make(u):
        return pl.BlockSpec((None,) + block, lambda b, s, pt: (pt[b, s * n + u],) + (0,) * len(block))
    return [make(u) for u in range(n)]


def _idx_score_body(pt_ref, qi_ref, w_ref, *refs):
    pages, o_ref = refs[:-1], refs[-1]
    qi = qi_ref[...]
    w = w_ref[...]
    for u, page in enumerate(pages):
        s = jnp.maximum(_bdot_nt(qi, page[...]), 0.0)
        o_ref[:, u * PAGE_SIZE:(u + 1) * PAGE_SIZE] = jnp.sum(w * s, axis=0, keepdims=True)


def _idx_scores_sample(page_table, qi8, w8, cache_kidx):
    db, npages = page_table.shape
    n = PAGES_PER_STEP
    grid_spec = pltpu.PrefetchScalarGridSpec(
        num_scalar_prefetch=1, grid=(db, npages // n),
        in_specs=[pl.BlockSpec((None,) + qi8.shape[1:], lambda b, s, pt: (b, 0, 0)),
                  pl.BlockSpec((None,) + w8.shape[1:], lambda b, s, pt: (b, 0, 0))]
                 + _page_specs(cache_kidx.shape[1:], n),
        out_specs=pl.BlockSpec((None, 1, n * PAGE_SIZE), lambda b, s, pt: (b, 0, s)))
    return pl.pallas_call(
        _idx_score_body, grid_spec=grid_spec,
        out_shape=jax.ShapeDtypeStruct((db, 1, npages * PAGE_SIZE), F32),
        compiler_params=_cparams(("parallel", "arbitrary"), 32), name="idx_scores_sample",
    )(page_table, qi8, w8, *([cache_kidx] * n))


def _select_sample_body(topk, idx_scale, past, sc_ref, qi_ref, sm_ref, bias_ref, key_s):
    rows = sc_ref.shape[0]
    sm = sm_ref[...]
    qi = qi_ref[...]
    ki = sm[:, 0:HEAD]
    new = jnp.zeros((rows, 1), F32)
    for h in range(IDX_HEADS):
        d = jnp.sum(qi[:, h * HEAD:(h + 1) * HEAD] * ki, axis=-1, keepdims=True)
        new = new + (sm[:, HEAD + h:HEAD + h + 1] * idx_scale) * jnp.maximum(d, 0.0)
    lane = lax.broadcasted_iota(I32, (rows, LANES), 1)
    key_s[:, 0:past] = _score_key(sc_ref[...])
    key_s[:, past:past + LANES] = _score_key(jnp.where(lane == 0, new, -jnp.inf))
    nchunks = (past + LANES) // LANES
    kth = _kth_key(key_s, rows, nchunks, LANES, topk)
    qpos = jnp.full((rows, 1), past, I32)
    _selection_bias(key_s, bias_ref, kth, rows, nchunks, LANES, topk, qpos)


def _select_sample(scores, qi, sm, topk, idx_scale):
    db, past = scores.shape
    full = lambda a: pl.BlockSpec(a.shape, lambda i: (0,) * a.ndim)
    return pl.pallas_call(
        functools.partial(_select_sample_body, topk, idx_scale, past), grid=(1,),
        in_specs=[full(scores), full(qi), full(sm)],
        out_specs=pl.BlockSpec((db, past + LANES), lambda i: (0, 0)),
        out_shape=jax.ShapeDtypeStruct((db, past + LANES), F32),
        scratch_shapes=[pltpu.VMEM((db, past + LANES), I32)],
        compiler_params=_cparams(("arbitrary",), 32), name="select_sample")(scores, qi, sm)


def _attend_sample_body(pt_ref, q_ref, bias_ref, bnew_ref, knew_ref, vnew_ref, *refs):
    n = PAGES_PER_STEP
    kpages, vpages = refs[:n], refs[n:2 * n]
    o_ref, m_s, l_s, acc_s = refs[2 * n:]
    s = pl.program_id(1)

    @pl.when(s == 0)
    def _():
        m_s[...] = jnp.full(m_s.shape, NEG_BIG, F32)
        l_s[...] = jnp.zeros(l_s.shape, F32)
        acc_s[...] = jnp.zeros(acc_s.shape, F32)

    q = q_ref[...] * (HEAD ** -0.5)

    def update(lg, pv):
        m = m_s[...]
        mn = jnp.maximum(m, jnp.max(lg, axis=-1, keepdims=True))
        p = jnp.exp(lg - mn)
        alpha = jnp.exp(m - mn)
        l_s[...] = alpha * l_s[...] + jnp.sum(p, axis=-1, keepdims=True)
        acc_s[...] = alpha * acc_s[...] + pv(p)
        m_s[...] = mn

    for u in range(n):
        lg = _bdot_nt(q, kpages[u][...]) + bias_ref[:, u * PAGE_SIZE:(u + 1) * PAGE_SIZE]
        update(lg, lambda p, u=u: _bdot(p, vpages[u][...]))

    @pl.when(s == pl.num_programs(1) - 1)
    def _():
        lg = jnp.sum(q * knew_ref[...], axis=-1, keepdims=True) + bnew_ref[:, 0:1]
        update(lg, lambda p: p * vnew_ref[...])
        o_ref[...] = acc_s[...] / l_s[...]


def _attend_sample(page_table, qexp, bias_past, bias_new, knew, vnew, cache_k, cache_v):
    db, npages = page_table.shape
    n = PAGES_PER_STEP
    nh, kvw = qexp.shape[1:]
    per_b = lambda a: pl.BlockSpec((None,) + a.shape[1:], lambda b, s, pt: (b,) + (0,) * (a.ndim - 1))
    grid_spec = pltpu.PrefetchScalarGridSpec(
        num_scalar_prefetch=1, grid=(db, npages // n),
        in_specs=[per_b(qexp),
                  pl.BlockSpec((None, 1, n * PAGE_SIZE), lambda b, s, pt: (b, 0, s)),
                  per_b(bias_new), per_b(knew), per_b(vnew)]
                 + _page_specs(cache_k.shape[1:], n) + _page_specs(cache_v.shape[1:], n),
        out_specs=per_b(qexp),
        scratch_shapes=[pltpu.VMEM((nh, 1), F32), pltpu.VMEM((nh, 1), F32), pltpu.VMEM((nh, kvw), F32)])
    return pl.pallas_call(
        _attend_sample_body, grid_spec=grid_spec,
        out_shape=jax.ShapeDtypeStruct(qexp.shape, F32),
        compiler_params=_cparams(("parallel", "arbitrary"), 32), name="attend_sample",
    )(page_table, qexp, bias_past, bias_new, knew, vnew, *([cache_k] * n), *([cache_v] * n))


def _merge_body(x_ref, mod_ref, ys_ref, ya_ref, gs_ref, ga_ref, wps_ref, wpa_ref, wo_ref, o_ref):
    d = x_ref.shape[1]
    merged = (jax.nn.sigmoid(gs_ref[...]) * _bdot(ys_ref[...], wps_ref[...])
              + jax.nn.sigmoid(ga_ref[...]) * _bdot(ya_ref[...], wpa_ref[...]))
    o_ref[...] = x_ref[...] + mod_ref[:, 2 * d:3 * d] * _bdot(merged, wo_ref[...])


def _merge(x, mod3, ys, ya, gs, ga, wps, wpa, wo, tm, tpb):
    m, d = x.shape
    x_spec, mod_spec = _row_specs(tm, tpb, d, mod3.shape[1])
    row = lambda n: pl.BlockSpec((tm, n), lambda i: (i, 0))
    return pl.pallas_call(
        _merge_body, grid=(m // tm,),
        in_specs=[x_spec, mod_spec, row(ys.shape[1]), row(ya.shape[1]), row(d), row(d),
                  _const_spec(wps.shape), _const_spec(wpa.shape), _const_spec(wo.shape)],
        out_specs=row(d), out_shape=jax.ShapeDtypeStruct((m, d), F32),
        compiler_params=_cparams(("parallel",), 48), name="merge")(x, mod3, ys, ya, gs, ga, wps, wpa, wo)


def _ffn_body(last_layer, x_ref, mod_ref, g_ref, fg_ref, wg_ref, wu_ref, wo_ref, o_ref):
    d = x_ref.shape[1]
    x = x_ref[...]
    h = _norm_mod(x, g_ref[...], mod_ref[:, 4 * d:5 * d], mod_ref[:, 3 * d:4 * d]).astype(BF16)
    gate = jnp.dot(h, wg_ref[...], preferred_element_type=F32)
    up = jnp.dot(h, wu_ref[...], preferred_element_type=F32)
    x2 = x + mod_ref[:, 5 * d:6 * d] * _bdot(_silu(gate) * up, wo_ref[...])
    if last_layer:
        ms = jnp.mean(x2 * x2, axis=-1, keepdims=True)
        x2 = x2 * lax.rsqrt(ms + NORM_EPS) * fg_ref[...]
    o_ref[...] = x2


def _ffn(x, mod3, g, fg, wg, wu, wo, tm, tpb, last_layer):
    m, d = x.shape
    x_spec, mod_spec = _row_specs(tm, tpb, d, mod3.shape[1])
    return pl.pallas_call(
        functools.partial(_ffn_body, last_layer), grid=(m // tm,),
        in_specs=[x_spec, mod_spec, _const_spec((1, d)), _const_spec((1, d)),
                  _const_spec(wg.shape), _const_spec(wu.shape), _const_spec(wo.shape)],
        out_specs=pl.BlockSpec((tm, d), lambda i: (i, 0)), out_shape=jax.ShapeDtypeStruct((m, d), F32),
        compiler_params=_cparams(("parallel",), 56), name="ffn")(x, mod3, g, fg, wg, wu, wo)


def _rope_tables(pos):
    half = HEAD // 2
    inv = ROPE_THETA ** (-jnp.arange(half, dtype=F32) / half)
    ang = pos.astype(F32)[:, None] * inv[None, :]
    cos = jnp.tile(jnp.cos(ang), (1, LANES // half))
    sin = jnp.tile(jnp.sin(ang), (1, LANES // half))
    first = (jnp.arange(LANES) % HEAD) < half
    return cos, jnp.where(first, -sin, 0.0), jnp.where(first, 0.0, sin)


def _q_head_order(n_heads):
    rep = n_heads // ATTN_KV_HEADS
    order = []
    for c in range(ATTN_KV_HEADS // 2):
        for j in range(rep):
            order += [2 * c * rep + j, (2 * c + 1) * rep + j]
    return np.asarray(order)


def kernel(x_prompt, x_sample, cache_k, cache_v, cache_kidx, state_conv, state_ssm, page_table, c_prompt, c_sample, w_ada, b_ada, norm1_g, w_in, conv_w, conv_b, dt_bias, a_log, d_skip, ssm_norm_g, w_proj_ssm, w_proj_attn, w_out, norm2_g, w_ffn_in, w_ffn_out, final_g):
    nb, t, d = x_prompt.shape
    db, ds, _ = x_sample.shape
    depth = w_in.shape[0]
    assert ds == 1 and t % Q_BLOCK == 0 and t % SSM_CHUNK == 0
    n_heads_ssm = dt_bias.shape[1]
    inner = n_heads_ssm * HEAD
    gn = SSM_GROUPS * SSM_STATE
    cdim = inner + 2 * gn
    kvw = ATTN_KV_HEADS * HEAD
    aw = w_proj_attn.shape[1]
    n_heads = aw // HEAD
    iw = IDX_HEADS * HEAD
    ffn_hidden = w_ffn_out.shape[1]
    npages = page_table.shape[1]
    past = npages * PAGE_SIZE
    topk_p = min(TOPK_MAX, t // 4)
    topk_s = min(TOPK_MAX, (past + ds) // 4)
    assert past + ds >= topk_s and npages % PAGES_PER_STEP == 0
    idx_scale = IDX_HEADS ** -0.5 * HEAD ** -0.5
    tm_p = 256 if t % 256 == 0 else 128
    tpb_p = t // tm_p

    splits = np.cumsum([inner, inner, gn, gn, n_heads_ssm, aw, kvw, kvw, iw, HEAD, IDX_HEADS, d])
    order = _q_head_order(n_heads)
    inv_order = np.argsort(order)
    group_onehot = jnp.asarray(np.arange(n_heads)[:, None] // (n_heads // ATTN_KV_HEADS)
                               == np.arange(ATTN_KV_HEADS)[None, :], F32)

    cos_p, slo_p, shi_p = _rope_tables(jnp.arange(t, dtype=I32))
    tabs_p = (cos_p, slo_p, shi_p)
    tabs_s = tuple(jnp.broadcast_to(a, (db, LANES)) for a in _rope_tables(past + jnp.arange(ds, dtype=I32)))

    rows_c = nb + db
    c_all = jnp.concatenate([c_prompt, c_sample, jnp.zeros((-rows_c % 8, d), F32)], axis=0)

    yp = x_prompt.reshape(nb * t, d)
    ys = x_sample.reshape(db, d)
    outs_p, outs_s = [], []
    for l in range(depth):
        (wz, wxs, wbm, wcm, wdt, wq, wk, wv, wqi, wki, wwi, wgs, wga) = jnp.split(w_in[l], splits, axis=1)
        w_ssm = jnp.concatenate([wz, wxs, wbm, wcm, wdt, jnp.zeros((d, LANES - n_heads_ssm), F32)],
                                axis=1).astype(BF16)
        wq_perm = wq.reshape(d, n_heads, HEAD)[:, order].reshape(d, aw)
        w_attn = jnp.concatenate([wq_perm, wk, wv, wqi, wgs, wga, wki, wwi,
                                  jnp.zeros((d, LANES - HEAD - IDX_HEADS), F32)], axis=1).astype(BF16)
        wps = w_proj_ssm[l].astype(BF16)
        wpa = w_proj_attn[l].reshape(n_heads, HEAD, d)[order].reshape(aw, d).astype(BF16)
        wo = w_out[l].astype(BF16)
        wg = w_ffn_in[l][:, :ffn_hidden].astype(BF16)
        wu = w_ffn_in[l][:, ffn_hidden:].astype(BF16)
        wfo = w_ffn_out[l].astype(BF16)
        g1 = norm1_g[l][None, :]
        g2 = norm2_g[l][None, :]
        a_neg = -jnp.exp(a_log[l])
        pad_h = LANES - n_heads_ssm
        dtb_row = jnp.pad(dt_bias[l], (0, pad_h))[None, :]
        a_row = jnp.pad(a_neg, (0, pad_h))[None, :]
        dsk_row = jnp.repeat(d_skip[l], HEAD)[None, :]
        ng_row = ssm_norm_g[l][None, :]
        cw = conv_w[l]
        cb = conv_b[l][None, :]

        mod = _ada(c_all, w_ada[l].astype(BF16), b_ada[l][None, :])
        mod_p = mod[:nb][:, None, :]
        mod_s = mod[nb:nb + db][None]

        z, xbc, dtr = _inproj_ssm(yp, mod_p, g1, w_ssm, tm_p, tpb_p, inner, cdim)
        (q_bf, k, k_bf, v, v_bf, qi_bf, gs, ga, sm, ki2) = _inproj_attn(
            yp, mod_p, g1, w_attn, tabs_p, tm_p, tpb_p, aw, kvw, iw)
        y_ssm, st = _ssd_prompt(z, xbc, dtr, cw, cb, dtb_row, a_row, dsk_row, ng_row, nb)
        y_attn = _dsa_prompt(q_bf, qi_bf, sm, ki2, k_bf, v_bf, nb, topk_p, idx_scale)
        x1 = _merge(yp, mod_p, y_ssm, y_attn, gs, ga, wps, wpa, wo, tm_p, tpb_p)
        yp_next = _ffn(x1, mod_p, g2, final_g[None, :], wg, wu, wfo, tm_p, tpb_p, l == depth - 1)
        outs_p.append((k.reshape(nb, t, ATTN_KV_HEADS, HEAD), v.reshape(nb, t, ATTN_KV_HEADS, HEAD),
                       sm[:, :HEAD].reshape(nb, t, HEAD),
                       xbc.reshape(nb, t, cdim)[:, t - (SSM_CONV - 1):],
                       st.reshape(nb, n_heads_ssm, HEAD, SSM_STATE)))

        z_s, xbc_s, dtr_s = _inproj_ssm(ys, mod_s, g1, w_ssm, db, 1, inner, cdim)
        (q_s, k_s, _, v_s, _, qi_s, gs_s, ga_s, sm_s, _) = _inproj_attn(
            ys, mod_s, g1, w_attn, tabs_s, db, 1, aw, kvw, iw)
        npair = inner // LANES
        nbc = 2 * gn // LANES
        sc = state_conv[l]
        y_ssm_s, st_s = _ssd_sample(
            z_s.reshape(db, npair, LANES), xbc_s[:, :inner].reshape(db, npair, LANES),
            xbc_s[:, inner:].reshape(db, nbc, LANES),
            sc[:, :, :inner].reshape(db, SSM_CONV - 1, npair, LANES),
            sc[:, :, inner:].reshape(db, SSM_CONV - 1, nbc, LANES),
            jnp.repeat(dtr_s[:, :n_heads_ssm], HEAD, axis=1).reshape(db, npair, LANES),
            state_ssm[l].reshape(db, inner, SSM_STATE),
            cw[:, :inner].reshape(SSM_CONV, npair, LANES), cw[:, inner:].reshape(SSM_CONV, nbc, LANES),
            cb[:, :inner].reshape(npair, LANES), cb[:, inner:].reshape(nbc, LANES),
            jnp.repeat(dt_bias[l], HEAD).reshape(npair, LANES), jnp.repeat(a_neg, HEAD).reshape(npair, LANES),
            dsk_row.reshape(npair, LANES), ng_row.reshape(npair, LANES))

        qi_f = qi_s.astype(F32)
        qi8 = jnp.pad(qi_f.reshape(db, IDX_HEADS, HEAD), ((0, 0), (0, 8 - IDX_HEADS), (0, 0)))
        w8 = jnp.broadcast_to(jnp.pad(sm_s[:, HEAD:HEAD + IDX_HEADS] * idx_scale,
                                      ((0, 0), (0, 8 - IDX_HEADS)))[:, :, None], (db, 8, LANES))
        scores = _idx_scores_sample(page_table, qi8, w8, cache_kidx[l])
        bias = _select_sample(scores.reshape(db, past), qi_f, sm_s, topk_s, idx_scale)
        q_orig = q_s.astype(F32).reshape(db, n_heads, HEAD)[:, inv_order]
        qexp = (q_orig[:, :, None, :] * group_onehot[None, :, :, None]).reshape(db, n_heads, kvw)
        att = _attend_sample(page_table, qexp, bias[:, None, :past], bias[:, None, past:],
                             k_s[:, None, :], v_s[:, None, :],
                             cache_k[l].reshape(-1, PAGE_SIZE, kvw), cache_v[l].reshape(-1, PAGE_SIZE, kvw))
        att = jnp.sum(att.reshape(db, n_heads, ATTN_KV_HEADS, HEAD) * group_onehot[None, :, :, None], axis=2)
        y_attn_s = att[:, order].reshape(db, aw)
        x1_s = _merge(ys, mod_s, y_ssm_s.reshape(db, inner), y_attn_s, gs_s, ga_s, wps, wpa, wo, db, 1)
        ys_next = _ffn(x1_s, mod_s, g2, final_g[None, :], wg, wu, wfo, db, 1, l == depth - 1)
        outs_s.append((k_s.reshape(db, ds, ATTN_KV_HEADS, HEAD), v_s.reshape(db, ds, ATTN_KV_HEADS, HEAD),
                       sm_s[:, :HEAD].reshape(db, ds, HEAD),
                       jnp.concatenate([sc[:, 1:], xbc_s[:, None, :]], axis=1),
                       st_s.reshape(db, n_heads_ssm, HEAD, SSM_STATE)))
        yp, ys = yp_next, ys_next

    stack = lambda outs, i: jnp.stack([o[i] for o in outs], axis=0)
    return (yp.reshape(nb, t, d), ys.reshape(db, ds, d),
            stack(outs_p, 0), stack(outs_p, 1), stack(outs_p, 2), stack(outs_p, 3), stack(outs_p, 4),
            stack(outs_s, 0), stack(outs_s, 1), stack(outs_s, 2), stack(outs_s, 3), stack(outs_s, 4))
```

```python
import functools

import jax
import jax.numpy as jnp
import numpy as np
from jax import lax
from jax.experimental import pallas as pl
from jax.experimental.pallas import tpu as pltpu

F32, BF16, I32, I16 = jnp.float32, jnp.bfloat16, jnp.int32, jnp.int16
HIGHEST = lax.Precision.HIGHEST

LANES = 128
HEAD = 64
SSM_STATE = 128
SSM_GROUPS = 4
SSM_CONV = 4
SSM_CHUNK = 128
ATTN_KV_HEADS = 4
IDX_HEADS = 4
TOPK_MAX = 256
Q_BLOCK = 128
PAGE_SIZE = 128
ROPE_THETA = 10000.0
NORM_EPS = 1e-6
KEY_BLOCK = 512
Q_SCALE = HEAD ** -0.5 * 1.4426950408889634
INT_MIN = -(2 ** 31)
NEG_BIG = -1e30


def _cparams(sem, vmem_mb):
    return pltpu.CompilerParams(dimension_semantics=sem, vmem_limit_bytes=vmem_mb << 20)


def _bdot(a, b):
    return jnp.dot(a.astype(BF16), b.astype(BF16), preferred_element_type=F32)


def _bdot_nt(a, b):
    return lax.dot_general(a.astype(BF16), b.astype(BF16), (((1,), (1,)), ((), ())),
                           preferred_element_type=F32)


def _silu(x):
    return x * jax.nn.sigmoid(x)


def _softplus(x):
    return jnp.maximum(x, 0.0) + jnp.log1p(jnp.exp(-jnp.abs(x)))


def _norm_mod(x, g, scale, shift):
    ms = jnp.mean(x * x, axis=-1, keepdims=True)
    return (x * lax.rsqrt(ms + NORM_EPS) * g) * (1.0 + scale) + shift


def _rope128(x, cos, sin_lo, sin_hi):
    return x * cos + pltpu.roll(x, 96, 1) * sin_lo + pltpu.roll(x, 32, 1) * sin_hi


def _rope_wide(x, cos, sin_lo, sin_hi):
    parts = [_rope128(x[:, j:j + LANES], cos, sin_lo, sin_hi) for j in range(0, x.shape[1], LANES)]
    return parts[0] if len(parts) == 1 else jnp.concatenate(parts, axis=1)


def _ada_body(c_ref, w_ref, b_ref, o_ref):
    o_ref[...] = _bdot(_silu(c_ref[...]), w_ref[...]) + b_ref[...]


def _ada(c_all, w_bf, b):
    mp, d = c_all.shape
    n = w_bf.shape[1]
    tn = n // 4
    return pl.pallas_call(
        _ada_body, grid=(n // tn,),
        in_specs=[pl.BlockSpec((mp, d), lambda j: (0, 0)),
                  pl.BlockSpec((d, tn), lambda j: (0, j)),
                  pl.BlockSpec((1, tn), lambda j: (0, j))],
        out_specs=pl.BlockSpec((mp, tn), lambda j: (0, j)),
        out_shape=jax.ShapeDtypeStruct((mp, n), F32),
        compiler_params=_cparams(("arbitrary",), 32), name="ada")(c_all, w_bf, b)


def _inproj_ssm_body(inner, cdim, x_ref, mod_ref, g_ref, w_ref, z_ref, xbc_ref, dt_ref):
    d = x_ref.shape[1]
    h = _norm_mod(x_ref[...], g_ref[...], mod_ref[:, d:2 * d], mod_ref[:, 0:d]).astype(BF16)
    z_ref[...] = jnp.dot(h, w_ref[:, 0:inner], preferred_element_type=F32)
    xbc_ref[...] = jnp.dot(h, w_ref[:, inner:inner + cdim], preferred_element_type=F32)
    dt_ref[...] = jnp.dot(h, w_ref[:, inner + cdim:inner + cdim + LANES], preferred_element_type=F32)


def _row_specs(tm, tpb, d, mod_rows):
    x_spec = pl.BlockSpec((tm, d), lambda m: (m, 0))
    mod_spec = pl.BlockSpec((None, mod_rows, 6 * d), lambda m: (m // tpb, 0, 0))
    return x_spec, mod_spec


def _const_spec(shape):
    return pl.BlockSpec(shape, lambda m: (0,) * len(shape))


def _inproj_ssm(x, mod3, g, w_bf, tm, tpb, inner, cdim):
    m, d = x.shape
    x_spec, mod_spec = _row_specs(tm, tpb, d, mod3.shape[1])
    row = lambda n: pl.BlockSpec((tm, n), lambda i: (i, 0))
    return pl.pallas_call(
        functools.partial(_inproj_ssm_body, inner, cdim), grid=(m // tm,),
        in_specs=[x_spec, mod_spec, _const_spec((1, d)), _const_spec(w_bf.shape)],
        out_specs=[row(inner), row(cdim), row(LANES)],
        out_shape=[jax.ShapeDtypeStruct((m, inner), F32), jax.ShapeDtypeStruct((m, cdim), F32),
                   jax.ShapeDtypeStruct((m, LANES), F32)],
        compiler_params=_cparams(("parallel",), 52), name="inproj_ssm")(x, mod3, g, w_bf)


def _inproj_attn_body(aw, kvw, iw, x_ref, mod_ref, g_ref, w_ref, cos_ref, slo_ref, shi_ref,
                      q_ref, k_ref, kb_ref, v_ref, vt_ref, qi_ref, gs_ref, ga_ref, sm_ref, ki2_ref):
    d = x_ref.shape[1]
    h = _norm_mod(x_ref[...], g_ref[...], mod_ref[:, d:2 * d], mod_ref[:, 0:d]).astype(BF16)
    cos, slo, shi = cos_ref[...], slo_ref[...], shi_ref[...]

    def proj(a, b):
        return jnp.dot(h, w_ref[:, a:b], preferred_element_type=F32)

    o = 0
    q_ref[...] = (_rope_wide(proj(o, o + aw), cos, slo, shi) * Q_SCALE).astype(BF16)
    o += aw
    k = _rope_wide(proj(o, o + kvw), cos, slo, shi)
    k_ref[...] = k
    kb_ref[...] = k.astype(BF16)
    o += kvw
    v = proj(o, o + kvw)
    v_ref[...] = v
    vt_ref[...] = v.T.astype(BF16)
    o += kvw
    qi_ref[...] = _rope_wide(proj(o, o + iw), cos, slo, shi).astype(BF16)
    o += iw
    gs_ref[...] = proj(o, o + d)
    o += d
    ga_ref[...] = proj(o, o + d)
    o += d
    s = proj(o, o + LANES)
    lane = lax.broadcasted_iota(I32, s.shape, 1)
    sm = jnp.where(lane < HEAD, _rope128(s, cos, slo, shi), s)
    sm_ref[...] = sm
    ki2_ref[...] = jnp.where(lane < HEAD, sm, pltpu.roll(sm, HEAD, 1)).astype(BF16)


def _inproj_attn(x, mod3, g, w_bf, tabs, tm, tpb, aw, kvw, iw):
    m, d = x.shape
    x_spec, mod_spec = _row_specs(tm, tpb, d, mod3.shape[1])
    ntab = tabs[0].shape[0] // tm
    tab_spec = pl.BlockSpec((tm, LANES), lambda i: (i % ntab, 0))
    row = lambda n: pl.BlockSpec((tm, n), lambda i: (i, 0))
    widths = [(aw, BF16), (kvw, F32), (kvw, BF16), (kvw, F32), None, (iw, BF16),
              (d, F32), (d, F32), (LANES, F32), (LANES, BF16)]
    vt_spec = pl.BlockSpec((kvw, tm), lambda i: (0, i))
    return pl.pallas_call(
        functools.partial(_inproj_attn_body, aw, kvw, iw), grid=(m // tm,),
        in_specs=[x_spec, mod_spec, _const_spec((1, d)), _const_spec(w_bf.shape),
                  tab_spec, tab_spec, tab_spec],
        out_specs=[vt_spec if w is None else row(w[0]) for w in widths],
        out_shape=[jax.ShapeDtypeStruct((kvw, m), BF16) if w is None else jax.ShapeDtypeStruct((m, w[0]), w[1])
                   for w in widths],
        compiler_params=_cparams(("parallel",), 52), name="inproj_attn")(x, mod3, g, w_bf, *tabs)


def _ssd_body(inner, z_ref, xbc_ref, dtr_ref, cw_ref, cb_ref, dtb_ref, a_ref, dsk_ref, ng_ref,
              y_ref, st_ref, full_s, act_s, st_s, y_s):
    c = pl.program_id(1)
    Q, N = SSM_CHUNK, SSM_STATE
    cdim = xbc_ref.shape[1]
    heads_per_group = inner // HEAD // SSM_GROUPS
    gw = inner // SSM_GROUPS

    @pl.when(c == 0)
    def _():
        full_s[0:8, :] = jnp.zeros((8, cdim), F32)
        st_s[...] = jnp.zeros(st_s.shape, F32)

    full_s[8:8 + Q, :] = xbc_ref[...]
    for j in range(0, cdim, 512):
        acc = cb_ref[:, j:j + 512] + full_s[8:8 + Q, j:j + 512] * cw_ref[3:4, j:j + 512]
        for i in range(SSM_CONV - 1):
            acc = acc + full_s[5 + i:5 + i + Q, j:j + 512] * cw_ref[i:i + 1, j:j + 512]
        act_s[:, j:j + 512] = _silu(acc)
    full_s[0:8, :] = full_s[Q:Q + 8, :]

    dt = _softplus(dtr_ref[...] + dtb_ref[...])
    row = lax.broadcasted_iota(I32, (Q, Q), 0)
    col = lax.broadcasted_iota(I32, (Q, Q), 1)
    tri = row >= col
    acs = jnp.dot(tri.astype(F32), dt * a_ref[...], precision=HIGHEST, preferred_element_type=F32)
    acs_t, dt_t = acs.T, dt.T
    last = acs[Q - 1:Q, :]
    wdt = jnp.exp(last - acs) * dt
    eacs = jnp.exp(acs)
    cdec = jnp.exp(last)
    low = lax.broadcasted_iota(I32, (Q, LANES), 1) < HEAD
    low1 = low[0:1, :]

    for g in range(SSM_GROUPS):
        bg = act_s[:, inner + g * N:inner + (g + 1) * N]
        cg = act_s[:, inner + SSM_GROUPS * N + g * N:inner + SSM_GROUPS * N + (g + 1) * N]
        cb = _bdot_nt(cg, bg)
        bg_t = bg.T.astype(BF16)
        for p in range(heads_per_group // 2):
            h0 = g * heads_per_group + 2 * p
            js = slice(h0 * HEAD, h0 * HEAD + LANES)
            xp = act_s[:, js]
            xp_bf = xp.astype(BF16)
            stp = st_s[:, js]
            stp_bf = stp.astype(BF16)
            ys = []
            for h in (h0, h0 + 1):
                seg = acs[:, h:h + 1] - acs_t[h:h + 1, :]
                decay = jnp.exp(jnp.where(tri, seg, -jnp.inf))
                m = (cb * decay) * dt_t[h:h + 1, :]
                ce = cg * eacs[:, h:h + 1]
                ys.append(_bdot(m, xp_bf) + _bdot(ce, stp_bf))
            y_s[:, js] = jnp.where(low, ys[0], ys[1])
            wcol = jnp.where(low, wdt[:, h0:h0 + 1], wdt[:, h0 + 1:h0 + 2])
            dst = jnp.dot(bg_t, (xp * wcol).astype(BF16), preferred_element_type=F32)
            cd = jnp.where(low1, cdec[:, h0:h0 + 1], cdec[:, h0 + 1:h0 + 2])
            st_s[:, js] = stp * cd + dst

    for g in range(SSM_GROUPS):
        gs = slice(g * gw, (g + 1) * gw)
        y = y_s[:, gs] + dsk_ref[:, gs] * act_s[:, gs]
        y = y * _silu(z_ref[:, gs])
        ms = jnp.mean(y * y, axis=-1, keepdims=True)
        y_ref[:, gs] = (y * lax.rsqrt(ms + NORM_EPS) * ng_ref[:, gs]).astype(BF16)

    @pl.when(c == pl.num_programs(1) - 1)
    def _():
        st_ref[...] = st_s[...].T


def _ssd_prompt(z, xbc, dtr, cw, cb, dtb, a, dsk, ng, nb):
    m, inner = z.shape
    cdim = xbc.shape[1]
    nc = m // nb // SSM_CHUNK
    row = lambda n: pl.BlockSpec((SSM_CHUNK, n), lambda b, c: (b * nc + c, 0))
    const = lambda shape: pl.BlockSpec(shape, lambda b, c: (0,) * len(shape))
    return pl.pallas_call(
        functools.partial(_ssd_body, inner), grid=(nb, nc),
        in_specs=[row(inner), row(cdim), row(LANES), const(cw.shape), const(cb.shape),
                  const(dtb.shape), const(a.shape), const(dsk.shape), const(ng.shape)],
        out_specs=[row(inner), pl.BlockSpec((None, inner, SSM_STATE), lambda b, c: (b, 0, 0))],
        out_shape=[jax.ShapeDtypeStruct((m, inner), BF16),
                   jax.ShapeDtypeStruct((nb, inner, SSM_STATE), F32)],
        scratch_shapes=[pltpu.VMEM((SSM_CHUNK + 8, cdim), F32), pltpu.VMEM((SSM_CHUNK, cdim), F32),
                        pltpu.VMEM((SSM_STATE, inner), F32), pltpu.VMEM((SSM_CHUNK, inner), F32)],
        compiler_params=_cparams(("parallel", "arbitrary"), 40), name="ssd_prompt",
    )(z, xbc, dtr, cw, cb, dtb, a, dsk, ng)


def _ssd_step_body(z_ref, xs_ref, bc_ref, cxs_ref, cbc_ref, dtr_ref, st_ref,
                   wxs_ref, wbc_ref, bxs_ref, bbc_ref, dtb_ref, a_ref, dsk_ref, ng_ref,
                   y_ref, sto_ref):
    G = SSM_GROUPS
    last = SSM_CONV - 1
    xs = bxs_ref[...] + xs_ref[...] * wxs_ref[last]
    bc = bbc_ref[...] + bc_ref[...] * wbc_ref[last]
    for i in range(last):
        xs = xs + cxs_ref[i] * wxs_ref[i]
        bc = bc + cbc_ref[i] * wbc_ref[i]
    xs, bc = _silu(xs), _silu(bc)
    dt = _softplus(dtr_ref[...] + dtb_ref[...])
    dec = jnp.exp(dt * a_ref[...])
    xdt = xs * dt
    npair = xs.shape[0]
    pairs_per_group = npair // G
    r = lax.broadcasted_iota(I32, (LANES, LANES), 0)
    cidx = lax.broadcasted_iota(I32, (LANES, LANES), 1)
    eye = (r == cidx).astype(F32)
    nt = (((1,), (1,)), ((), ()))
    dec_t = lax.dot_general(eye, dec, nt, precision=HIGHEST, preferred_element_type=F32)
    xdt_t = lax.dot_general(eye, xdt, nt, precision=HIGHEST, preferred_element_type=F32)
    rows = lax.broadcasted_iota(I32, (npair, 1), 0)
    cbv = jnp.sum(bc[0:G, :] * bc[G:2 * G, :], axis=-1, keepdims=True)
    cbx = jnp.zeros((npair, 1), F32)
    for g in range(G):
        cbx = cbx + jnp.where(rows // pairs_per_group == g, cbv[g:g + 1, :], 0.0)
    c_bf = bc.astype(BF16)
    yoff = jnp.zeros(xs.shape, F32)
    for j in range(npair):
        g = j // pairs_per_group
        s = st_ref[j * LANES:(j + 1) * LANES, :]
        sto_ref[j * LANES:(j + 1) * LANES, :] = s * dec_t[:, j:j + 1] + xdt_t[:, j:j + 1] * bc[g:g + 1, :]
        rj = _bdot_nt(c_bf, s)
        yoff = yoff + jnp.where(rows == j, rj[G + g:G + g + 1, :], 0.0)
    y = yoff * dec + cbx * dt * xs + dsk_ref[...] * xs
    y = y * _silu(z_ref[...])
    ssq = jnp.sum(y * y, axis=-1, keepdims=True)
    msx = jnp.zeros((npair, 1), F32)
    for g in range(G):
        ing = rows // pairs_per_group == g
        tot = jnp.sum(jnp.where(ing, ssq, 0.0), axis=0, keepdims=True)
        msx = msx + jnp.where(ing, tot, 0.0)
    msx = msx / (pairs_per_group * LANES)
    y_ref[...] = y * lax.rsqrt(msx + NORM_EPS) * ng_ref[...]


def _ssd_sample(z, xs, bc, cxs, cbc, dtr, st, wxs, wbc, bxs, bbc, dtb, a, dsk, ng):
    db, npair, _ = z.shape
    per_b = lambda shape: pl.BlockSpec((None,) + shape, lambda b: (b,) + (0,) * len(shape))
    const = lambda arr: pl.BlockSpec(arr.shape, lambda b: (0,) * arr.ndim)
    return pl.pallas_call(
        _ssd_step_body, grid=(db,),
        in_specs=[per_b(z.shape[1:]), per_b(xs.shape[1:]), per_b(bc.shape[1:]), per_b(cxs.shape[1:]),
                  per_b(cbc.shape[1:]), per_b(dtr.shape[1:]), per_b(st.shape[1:]),
                  const(wxs), const(wbc), const(bxs), const(bbc), const(dtb), const(a), const(dsk), const(ng)],
        out_specs=[per_b(z.shape[1:]), per_b(st.shape[1:])],
        out_shape=[jax.ShapeDtypeStruct(z.shape, F32), jax.ShapeDtypeStruct(st.shape, F32)],
        compiler_params=_cparams(("parallel",), 32), name="ssd_sample",
    )(z, xs, bc, cxs, cbc, dtr, st, wxs, wbc, bxs, bbc, dtb, a, dsk, ng)


def _score_key(score):
    bits = pltpu.bitcast(score, I32)
    return jnp.where(bits < 0, bits ^ jnp.int32(0x7FFFFFFF), bits)


def _lane_fold(x):
    acc = x[:, 0:LANES]
    for j in range(LANES, x.shape[1], LANES):
        acc = acc + x[:, j:j + LANES]
    return acc


def _kth_key(key_s, rows, nchunks, cw, topk):
    def count_ge(cand):
        def body(s, acc):
            key = key_s[:, pl.ds(pl.multiple_of(s * cw, cw), cw)]
            return acc + _lane_fold(jnp.where(key >= cand, 1.0, 0.0))
        acc = lax.fori_loop(0, nchunks, body, jnp.zeros((rows, LANES), F32))
        return jnp.sum(acc, axis=-1, keepdims=True)

    def bit_step(t, kth):
        cand = kth ^ lax.shift_left(jnp.int32(1), 31 - t)
        return jnp.where(count_ge(cand) >= topk, cand, kth)

    return lax.fori_loop(0, 32, bit_step, jnp.full((rows, 1), INT_MIN, I32))


def _selection_bias(key_s, bias_s, kth, rows, nchunks, cw, topk, qpos):
    def gt_body(s, acc):
        key = key_s[:, pl.ds(pl.multiple_of(s * cw, cw), cw)]
        return acc + _lane_fold(jnp.where(key > kth, 1.0, 0.0))
    cnt_gt = jnp.sum(lax.fori_loop(0, nchunks, gt_body, jnp.zeros((rows, LANES), F32)),
                     axis=-1, keepdims=True)
    need = topk - cnt_gt
    r = lax.broadcasted_iota(I32, (LANES, LANES), 0)
    c = lax.broadcasted_iota(I32, (LANES, LANES), 1)
    upper = jnp.where(r <= c, 1.0, 0.0).astype(BF16)
    lane = lax.broadcasted_iota(I32, (1, LANES), 1)

    def body(s, carry):
        for j in range(0, cw, LANES):
            off = pl.multiple_of(s * cw + j, LANES)
            key = key_s[:, pl.ds(off, LANES)]
            eq = key == kth
            eqf = jnp.where(eq, 1.0, 0.0)
            incl = jnp.dot(eqf.astype(BF16), upper, preferred_element_type=F32)
            rank = carry + incl - eqf
            sel = (key > kth) | (eq & (rank < need))
            sel = sel & ((off + lane) <= qpos)
            bias_s[:, pl.ds(off, LANES)] = jnp.where(sel, 0.0, -jnp.inf)
            carry = carry + incl[:, LANES - 1:LANES]
        return carry

    lax.fori_loop(0, nchunks, body, jnp.zeros((rows, 1), F32))


def _row_fold(x, h):
    parts = [x[j:j + h, :] for j in range(0, x.shape[0], h)]
    while len(parts) > 1:
        parts = [a + b for a, b in zip(parts[0::2], parts[1::2])] + (parts[-1:] if len(parts) % 2 else [])
    return parts[0]


def _dsa_t_body(topk, idx_scale, q_ref, qi_ref, sm_ref, ki2_ref, k_ref, vt_ref, o_ref,
                key_s, hi_s, lo_s, bias_s, qs_s, lga_s, lgb_s, m_s, acc_s):
    i = pl.program_id(1)
    QB, KB = Q_BLOCK, KEY_BLOCK
    nkb = (i * QB + QB + KB - 1) // KB
    qpos = i * QB + lax.broadcasted_iota(I32, (1, QB), 1)
    low = lax.broadcasted_iota(I32, (QB, LANES), 1) < HEAD
    zero_bf = jnp.zeros((QB, LANES), BF16)
    one16, zero16, min16 = jnp.int16(1), jnp.int16(0), jnp.int16(-32768)
    blk = lambda s: pl.ds(pl.multiple_of(s * KB, KB), KB)

    qi = qi_ref[...]
    sm_t = sm_ref[...].T
    qh, wh = [], []
    for h in range(IDX_HEADS):
        chunk = qi[:, (h // 2) * LANES:(h // 2 + 1) * LANES]
        qh.append(jnp.where(low if h % 2 == 0 else ~low, chunk, zero_bf))
        wh.append(sm_t[HEAD + h:HEAD + h + 1, :] * idx_scale)

    def score_body(s, carry):
        kib = ki2_ref[blk(s), :]
        acc = jnp.zeros((KB, QB), F32)
        for h in range(IDX_HEADS):
            acc = acc + wh[h] * jnp.maximum(_bdot_nt(kib, qh[h]), 0.0)
        kpos = s * KB + lax.broadcasted_iota(I32, (KB, 1), 0)
        key = _score_key(jnp.where(kpos <= qpos, acc, -jnp.inf))
        key_s[blk(s), :] = key
        hi_s[blk(s), :] = (key >> 16).astype(I16)
        lo_s[blk(s), :] = ((key & 0xFFFF) - 32768).astype(I16)
        return carry

    lax.fori_loop(0, nkb, score_body, 0)

    def count16(ref, cmp):
        def body(s, acc):
            return acc + _row_fold(jnp.where(cmp(ref[blk(s), :]), one16, zero16), 16)
        acc = lax.fori_loop(0, nkb, body, jnp.zeros((16, QB), I16))
        return jnp.sum(acc.astype(F32), axis=0, keepdims=True)

    def search16(ref, base):
        def step(t, u):
            cand = u | lax.shift_left(jnp.int32(1), 15 - t)
            c16 = (cand - 32768).astype(I16)
            return jnp.where(base + count16(ref, lambda x: x >= c16) >= topk, cand, u)
        return lax.fori_loop(0, 16, step, jnp.zeros((1, QB), I32))

    u_hi = search16(hi_s, 0.0)
    k_hi = (u_hi - 32768).astype(I16)

    def tie_body(s, carry):
        lo_s[blk(s), :] = jnp.where(hi_s[blk(s), :] == k_hi, lo_s[blk(s), :], min16)
        return carry

    lax.fori_loop(0, nkb, tie_body, 0)
    n_hi_gt = count16(hi_s, lambda x: x > k_hi)
    u_lo = search16(lo_s, n_hi_gt)
    k_lo = (u_lo - 32768).astype(I16)
    kth = ((u_hi - 32768) << 16) | u_lo
    need = topk - (n_hi_gt + count16(lo_s, lambda x: x > k_lo))

    r_i = lax.broadcasted_iota(I32, (LANES, LANES), 0)
    c_i = lax.broadcasted_iota(I32, (LANES, LANES), 1)
    lower = jnp.where(r_i >= c_i, 1.0, 0.0).astype(BF16)
    sub_iota = lax.broadcasted_iota(I32, (LANES, 1), 0)

    def bias_body(s, carry):
        for j in range(0, KB, LANES):
            off = pl.multiple_of(s * KB + j, LANES)
            key = key_s[pl.ds(off, LANES), :]
            eq = key == kth
            eqf = jnp.where(eq, 1.0, 0.0)
            incl = jnp.dot(lower, eqf.astype(BF16), preferred_element_type=F32)
            sel = (key > kth) | (eq & (carry + incl - eqf < need))
            sel = sel & ((off + sub_iota) <= qpos)
            bias_s[pl.ds(off, LANES), :] = jnp.where(sel, 0.0, -jnp.inf)
            carry = carry + incl[LANES - 1:LANES, :]
        return carry

    lax.fori_loop(0, nkb, bias_body, jnp.zeros((1, QB), F32))

    q = q_ref[...]
    nchunk = q.shape[1] // LANES
    per_kv_chunk = nchunk // (ATTN_KV_HEADS // 2)
    nstack = 2 * per_kv_chunk
    srows = nstack * QB
    for cj in range(nchunk):
        kvc, j = divmod(cj, per_kv_chunk)
        qc = q[:, cj * LANES:(cj + 1) * LANES]
        for half in range(2):
            r0 = (kvc * nstack + half * per_kv_chunk + j) * QB
            qs_s[r0:r0 + QB, :] = jnp.where(low if half == 0 else ~low, qc, zero_bf)

    low_t = lax.broadcasted_iota(I32, (LANES, KB), 0) < HEAD
    one_bf = jnp.ones((LANES, KB), BF16)
    cols = lambda r: slice(r * QB, (r + 1) * QB)

    def logits_to(dst, s, kvc):
        s = jnp.minimum(s, nkb - 1)
        kb = k_ref[blk(s), kvc * LANES:(kvc + 1) * LANES]
        bias = bias_s[blk(s), :]
        lg = _bdot_nt(kb, qs_s[kvc * srows:(kvc + 1) * srows, :])
        for r in range(nstack):
            dst[:, cols(r)] = lg[:, cols(r)] + bias

    def consume(src, s, kvc):
        vt = vt_ref[kvc * LANES:(kvc + 1) * LANES, blk(s)]
        v_sel = (jnp.where(low_t, vt, one_bf), jnp.where(low_t, one_bf, vt))
        ps, alphas = [], []
        for r in range(nstack):
            lg = src[:, cols(r)]
            m = m_s[:, cols(r)]
            mn = jnp.maximum(m, jnp.max(lg, axis=0, keepdims=True))
            m_s[:, cols(r)] = mn
            ps.append(jnp.exp2(lg - mn).astype(BF16))
            alphas.append(jnp.exp2(m - mn))
        for half in range(2):
            hs = slice(half * per_kv_chunk, (half + 1) * per_kv_chunk)
            hc = slice(half * per_kv_chunk * QB, (half + 1) * per_kv_chunk * QB)
            pv = jnp.dot(v_sel[half], jnp.concatenate(ps[hs], axis=1), preferred_element_type=F32)
            acc_s[:, hc] = jnp.concatenate(alphas[hs], axis=1) * acc_s[:, hc] + pv

    low_rows = lax.broadcasted_iota(I32, (LANES, QB), 0) < HEAD
    for kvc in range(ATTN_KV_HEADS // 2):
        m_s[...] = jnp.full(m_s.shape, NEG_BIG, F32)
        acc_s[...] = jnp.zeros(acc_s.shape, F32)
        logits_to(lga_s, 0, kvc)

        def pair_body(i2, carry, kvc=kvc):
            logits_to(lgb_s, 2 * i2 + 1, kvc)
            consume(lga_s, 2 * i2, kvc)
            logits_to(lga_s, 2 * i2 + 2, kvc)
            consume(lgb_s, 2 * i2 + 1, kvc)
            return carry

        lax.fori_loop(0, nkb // 2, pair_body, 0)

        @pl.when(nkb % 2 == 1)
        def _(kvc=kvc):
            consume(lga_s, nkb - 1, kvc)

        acc = acc_s[...]
        for j in range(per_kv_chunk):
            a = acc[:, cols(j)]
            b = acc[:, cols(per_kv_chunk + j)]
            chunk_t = jnp.where(low_rows, a / a[HEAD:HEAD + 1, :], b / b[0:1, :])
            cj = kvc * per_kv_chunk + j
            o_ref[:, cj * LANES:(cj + 1) * LANES] = chunk_t.T.astype(BF16)


def _dsa_prompt_t(q, qi, sm, ki2, k, vt, nb, topk):
    m, aw = q.shape
    t = m // nb
    nq = t // Q_BLOCK
    tpad = -(-t // KEY_BLOCK) * KEY_BLOCK
    srows = aw // HEAD // (ATTN_KV_HEADS // 2) * Q_BLOCK
    idx_scale = IDX_HEADS ** -0.5 * HEAD ** -0.5
    row = lambda n: pl.BlockSpec((Q_BLOCK, n), lambda b, i: (b * nq + i, 0))
    per_b = lambda n: pl.BlockSpec((t, n), lambda b, i: (b, 0))
    return pl.pallas_call(
        functools.partial(_dsa_t_body, topk, idx_scale), grid=(nb, nq),
        in_specs=[row(aw), row(qi.shape[1]), row(LANES), per_b(LANES), per_b(k.shape[1]),
                  pl.BlockSpec((vt.shape[0], t), lambda b, i: (0, b))],
        out_specs=row(aw),
        out_shape=jax.ShapeDtypeStruct((m, aw), BF16),
        scratch_shapes=[pltpu.VMEM((tpad, Q_BLOCK), I32), pltpu.VMEM((tpad, Q_BLOCK), I16),
                        pltpu.VMEM((tpad, Q_BLOCK), I16), pltpu.VMEM((tpad, Q_BLOCK), F32),
                        pltpu.VMEM((2 * srows, LANES), BF16),
                        pltpu.VMEM((KEY_BLOCK, srows), F32), pltpu.VMEM((KEY_BLOCK, srows), F32),
                        pltpu.VMEM((1, srows), F32), pltpu.VMEM((LANES, srows), F32)],
        compiler_params=_cparams(("parallel", "arbitrary"), 52), name="dsa_prompt",
    )(q, qi, sm, ki2, k, vt)


def _page_specs(block, npages):
    def make(u):
        return pl.BlockSpec((None,) + block, lambda b, pt: (pt[b, u],) + (0,) * len(block))
    return [make(u) for u in range(npages)]


def _idx_score_body(pt_ref, qi_ref, w_ref, *refs):
    pages, o_ref = refs[:-1], refs[-1]
    qi = qi_ref[...]
    w = w_ref[...]
    for u, page in enumerate(pages):
        s = jnp.maximum(_bdot(qi, page[...]), 0.0)
        o_ref[:, u * PAGE_SIZE:(u + 1) * PAGE_SIZE] = jnp.sum(w * s, axis=0, keepdims=True)


def _idx_scores_sample(page_table, qi8, w8, kidx_t):
    db, npages = page_table.shape
    grid_spec = pltpu.PrefetchScalarGridSpec(
        num_scalar_prefetch=1, grid=(db,),
        in_specs=[pl.BlockSpec((None,) + qi8.shape[1:], lambda b, pt: (b, 0, 0)),
                  pl.BlockSpec((None,) + w8.shape[1:], lambda b, pt: (b, 0, 0))]
                 + _page_specs(kidx_t.shape[1:], npages),
        out_specs=pl.BlockSpec((None, 1, npages * PAGE_SIZE), lambda b, pt: (b, 0, 0)))
    return pl.pallas_call(
        _idx_score_body, grid_spec=grid_spec,
        out_shape=jax.ShapeDtypeStruct((db, 1, npages * PAGE_SIZE), F32),
        compiler_params=_cparams(("parallel",), 32), name="idx_scores_sample",
    )(page_table, qi8, w8, *([kidx_t] * npages))


def _select_sample_body(topk, idx_scale, past, sc_ref, qi_ref, sm_ref, bias_ref, key_s):
    rows = sc_ref.shape[0]
    sm = sm_ref[...]
    qi = qi_ref[...]
    ki = sm[:, 0:HEAD]
    new = jnp.zeros((rows, 1), F32)
    for h in range(IDX_HEADS):
        d = jnp.sum(qi[:, h * HEAD:(h + 1) * HEAD] * ki, axis=-1, keepdims=True)
        new = new + (sm[:, HEAD + h:HEAD + h + 1] * idx_scale) * jnp.maximum(d, 0.0)
    lane = lax.broadcasted_iota(I32, (rows, LANES), 1)
    key_s[:, 0:past] = _score_key(sc_ref[...])
    key_s[:, past:past + LANES] = _score_key(jnp.where(lane == 0, new, -jnp.inf))
    nchunks = (past + LANES) // LANES
    kth = _kth_key(key_s, rows, nchunks, LANES, topk)
    qpos = jnp.full((rows, 1), past, I32)
    _selection_bias(key_s, bias_ref, kth, rows, nchunks, LANES, topk, qpos)


def _select_sample(scores, qi, sm, topk, idx_scale):
    db, past = scores.shape
    full = lambda a: pl.BlockSpec(a.shape, lambda i: (0,) * a.ndim)
    return pl.pallas_call(
        functools.partial(_select_sample_body, topk, idx_scale, past), grid=(1,),
        in_specs=[full(scores), full(qi), full(sm)],
        out_specs=pl.BlockSpec((db, past + LANES), lambda i: (0, 0)),
        out_shape=jax.ShapeDtypeStruct((db, past + LANES), F32),
        scratch_shapes=[pltpu.VMEM((db, past + LANES), I32)],
        compiler_params=_cparams(("arbitrary",), 32), name="select_sample")(scores, qi, sm)


def _attend_sample_body(npages, pt_ref, q_ref, bias_ref, knew_ref, vnew_ref, *refs):
    kpages, vpages, o_ref = refs[:npages], refs[npages:2 * npages], refs[2 * npages]
    G = ATTN_KV_HEADS
    q = q_ref[...]
    q_bf = q.astype(BF16)
    nh = q.shape[0]
    past = npages * PAGE_SIZE
    group = lax.broadcasted_iota(I32, (nh, 1), 0) // (nh // G)

    def by_group(parts):
        out = parts[G - 1]
        for g in range(G - 2, -1, -1):
            out = jnp.where(group == g, parts[g], out)
        return out

    lg = jnp.concatenate(
        [by_group([_bdot(q_bf, kpages[u][g]) for g in range(G)]) for u in range(npages)], axis=1)
    lg = lg + bias_ref[:, 0:past]
    lg_new = by_group([jnp.sum(q * knew_ref[g:g + 1, :], axis=-1, keepdims=True) for g in range(G)])
    lg_new = lg_new + bias_ref[:, past:past + 1]
    m = jnp.maximum(jnp.max(lg, axis=-1, keepdims=True), lg_new)
    p = jnp.exp2(lg - m)
    p_new = jnp.exp2(lg_new - m)
    denom = jnp.sum(p, axis=-1, keepdims=True) + p_new
    p_bf = p.astype(BF16)
    accs = [p_new * vnew_ref[g:g + 1, :] for g in range(G)]
    for u in range(npages):
        pu = p_bf[:, u * PAGE_SIZE:(u + 1) * PAGE_SIZE]
        for g in range(G):
            accs[g] = accs[g] + _bdot_nt(pu, vpages[u][g])
    o_ref[...] = by_group(accs) / denom


def _attend_sample(page_table, q, bias, knew, vnew, k_t, v_t):
    db, npages = page_table.shape
    per_b = lambda a: pl.BlockSpec((None,) + a.shape[1:], lambda b, pt: (b,) + (0,) * (a.ndim - 1))
    grid_spec = pltpu.PrefetchScalarGridSpec(
        num_scalar_prefetch=1, grid=(db,),
        in_specs=[per_b(q), per_b(bias), per_b(knew), per_b(vnew)]
                 + _page_specs(k_t.shape[1:], npages) + _page_specs(v_t.shape[1:], npages),
        out_specs=per_b(q))
    return pl.pallas_call(
        functools.partial(_attend_sample_body, npages), grid_spec=grid_spec,
        out_shape=jax.ShapeDtypeStruct(q.shape, F32),
        compiler_params=_cparams(("parallel",), 48), name="attend_sample",
    )(page_table, q, bias, knew, vnew, *([k_t] * npages), *([v_t] * npages))


def _merge_body(x_ref, mod_ref, ys_ref, ya_ref, gs_ref, ga_ref, wps_ref, wpa_ref, wo_ref, o_ref):
    d = x_ref.shape[1]
    merged = (jax.nn.sigmoid(gs_ref[...]) * _bdot(ys_ref[...], wps_ref[...])
              + jax.nn.sigmoid(ga_ref[...]) * _bdot(ya_ref[...], wpa_ref[...]))
    o_ref[...] = x_ref[...] + mod_ref[:, 2 * d:3 * d] * _bdot(merged, wo_ref[...])


def _merge(x, mod3, ys, ya, gs, ga, wps, wpa, wo, tm, tpb):
    m, d = x.shape
    x_spec, mod_spec = _row_specs(tm, tpb, d, mod3.shape[1])
    row = lambda n: pl.BlockSpec((tm, n), lambda i: (i, 0))
    return pl.pallas_call(
        _merge_body, grid=(m // tm,),
        in_specs=[x_spec, mod_spec, row(ys.shape[1]), row(ya.shape[1]), row(d), row(d),
                  _const_spec(wps.shape), _const_spec(wpa.shape), _const_spec(wo.shape)],
        out_specs=row(d), out_shape=jax.ShapeDtypeStruct((m, d), F32),
        compiler_params=_cparams(("parallel",), 48), name="merge")(x, mod3, ys, ya, gs, ga, wps, wpa, wo)


def _ffn_body(last_layer, x_ref, mod_ref, g_ref, fg_ref, wg_ref, wu_ref, wo_ref, o_ref):
    d = x_ref.shape[1]
    x = x_ref[...]
    h = _norm_mod(x, g_ref[...], mod_ref[:, 4 * d:5 * d], mod_ref[:, 3 * d:4 * d]).astype(BF16)
    gate = jnp.dot(h, wg_ref[...], preferred_element_type=F32)
    up = jnp.dot(h, wu_ref[...], preferred_element_type=F32)
    x2 = x + mod_ref[:, 5 * d:6 * d] * _bdot(_silu(gate) * up, wo_ref[...])
    if last_layer:
        ms = jnp.mean(x2 * x2, axis=-1, keepdims=True)
        x2 = x2 * lax.rsqrt(ms + NORM_EPS) * fg_ref[...]
    o_ref[...] = x2


def _ffn(x, mod3, g, fg, wg, wu, wo, tm, tpb, last_layer):
    m, d = x.shape
    x_spec, mod_spec = _row_specs(tm, tpb, d, mod3.shape[1])
    return pl.pallas_call(
        functools.partial(_ffn_body, last_layer), grid=(m // tm,),
        in_specs=[x_spec, mod_spec, _const_spec((1, d)), _const_spec((1, d)),
                  _const_spec(wg.shape), _const_spec(wu.shape), _const_spec(wo.shape)],
        out_specs=pl.BlockSpec((tm, d), lambda i: (i, 0)), out_shape=jax.ShapeDtypeStruct((m, d), F32),
        compiler_params=_cparams(("parallel",), 56), name="ffn")(x, mod3, g, fg, wg, wu, wo)


def _rope_tables(pos):
    half = HEAD // 2
    inv = ROPE_THETA ** (-jnp.arange(half, dtype=F32) / half)
    ang = pos.astype(F32)[:, None] * inv[None, :]
    cos = jnp.tile(jnp.cos(ang), (1, LANES // half))
    sin = jnp.tile(jnp.sin(ang), (1, LANES // half))
    first = (jnp.arange(LANES) % HEAD) < half
    return cos, jnp.where(first, -sin, 0.0), jnp.where(first, 0.0, sin)


def _q_head_order(n_heads):
    rep = n_heads // ATTN_KV_HEADS
    order = []
    for c in range(ATTN_KV_HEADS // 2):
        for j in range(rep):
            order += [2 * c * rep + j, (2 * c + 1) * rep + j]
    return np.asarray(order)


def kernel(x_prompt, x_sample, cache_k, cache_v, cache_kidx, state_conv, state_ssm, page_table, c_prompt, c_sample, w_ada, b_ada, norm1_g, w_in, conv_w, conv_b, dt_bias, a_log, d_skip, ssm_norm_g, w_proj_ssm, w_proj_attn, w_out, norm2_g, w_ffn_in, w_ffn_out, final_g):
    nb, t, d = x_prompt.shape
    db, ds, _ = x_sample.shape
    depth = w_in.shape[0]
    assert ds == 1 and t % Q_BLOCK == 0 and t % SSM_CHUNK == 0
    n_heads_ssm = dt_bias.shape[1]
    inner = n_heads_ssm * HEAD
    gn = SSM_GROUPS * SSM_STATE
    cdim = inner + 2 * gn
    kvw = ATTN_KV_HEADS * HEAD
    aw = w_proj_attn.shape[1]
    n_heads = aw // HEAD
    iw = IDX_HEADS * HEAD
    ffn_hidden = w_ffn_out.shape[1]
    npages = page_table.shape[1]
    past = npages * PAGE_SIZE
    topk_p = min(TOPK_MAX, t // 4)
    topk_s = min(TOPK_MAX, (past + ds) // 4)
    assert past + ds >= topk_s
    idx_scale = IDX_HEADS ** -0.5 * HEAD ** -0.5
    tm_p = 256 if t % 256 == 0 else 128
    tpb_p = t // tm_p

    splits = np.cumsum([inner, inner, gn, gn, n_heads_ssm, aw, kvw, kvw, iw, HEAD, IDX_HEADS, d])
    order = _q_head_order(n_heads)
    inv_order = np.argsort(order)

    cos_p, slo_p, shi_p = _rope_tables(jnp.arange(t, dtype=I32))
    tabs_p = (cos_p, slo_p, shi_p)
    tabs_s = tuple(jnp.broadcast_to(a, (db, LANES)) for a in _rope_tables(past + jnp.arange(ds, dtype=I32)))

    rows_c = nb + db
    c_all = jnp.concatenate([c_prompt, c_sample, jnp.zeros((-rows_c % 8, d), F32)], axis=0)

    yp = x_prompt.reshape(nb * t, d)
    ys = x_sample.reshape(db, d)
    outs_p, outs_s = [], []
    for l in range(depth):
        (wz, wxs, wbm, wcm, wdt, wq, wk, wv, wqi, wki, wwi, wgs, wga) = jnp.split(w_in[l], splits, axis=1)
        w_ssm = jnp.concatenate([wz, wxs, wbm, wcm, wdt, jnp.zeros((d, LANES - n_heads_ssm), F32)],
                                axis=1).astype(BF16)
        wq_perm = wq.reshape(d, n_heads, HEAD)[:, order].reshape(d, aw)
        w_attn = jnp.concatenate([wq_perm, wk, wv, wqi, wgs, wga, wki, wwi,
                                  jnp.zeros((d, LANES - HEAD - IDX_HEADS), F32)], axis=1).astype(BF16)
        wps = w_proj_ssm[l].astype(BF16)
        wpa = w_proj_attn[l].reshape(n_heads, HEAD, d)[order].reshape(aw, d).astype(BF16)
        wo = w_out[l].astype(BF16)
        wg = w_ffn_in[l][:, :ffn_hidden].astype(BF16)
        wu = w_ffn_in[l][:, ffn_hidden:].astype(BF16)
        wfo = w_ffn_out[l].astype(BF16)
        g1 = norm1_g[l][None, :]
        g2 = norm2_g[l][None, :]
        a_neg = -jnp.exp(a_log[l])
        pad_h = LANES - n_heads_ssm
        dtb_row = jnp.pad(dt_bias[l], (0, pad_h))[None, :]
        a_row = jnp.pad(a_neg, (0, pad_h))[None, :]
        dsk_row = jnp.repeat(d_skip[l], HEAD)[None, :]
        ng_row = ssm_norm_g[l][None, :]
        cw = conv_w[l]
        cb = conv_b[l][None, :]

        mod = _ada(c_all, w_ada[l].astype(BF16), b_ada[l][None, :])
        mod_p = mod[:nb][:, None, :]
        mod_s = mod[nb:nb + db][None]

        z, xbc, dtr = _inproj_ssm(yp, mod_p, g1, w_ssm, tm_p, tpb_p, inner, cdim)
        (q_bf, k, k_bf, v, vt_bf, qi_bf, gs, ga, sm, ki2) = _inproj_attn(
            yp, mod_p, g1, w_attn, tabs_p, tm_p, tpb_p, aw, kvw, iw)
        y_ssm, st = _ssd_prompt(z, xbc, dtr, cw, cb, dtb_row, a_row, dsk_row, ng_row, nb)
        y_attn = _dsa_prompt_t(q_bf, qi_bf, sm, ki2, k_bf, vt_bf, nb, topk_p)
        x1 = _merge(yp, mod_p, y_ssm, y_attn, gs, ga, wps, wpa, wo, tm_p, tpb_p)
        yp_next = _ffn(x1, mod_p, g2, final_g[None, :], wg, wu, wfo, tm_p, tpb_p, l == depth - 1)
        outs_p.append((k.reshape(nb, t, ATTN_KV_HEADS, HEAD), v.reshape(nb, t, ATTN_KV_HEADS, HEAD),
                       sm[:, :HEAD].reshape(nb, t, HEAD),
                       xbc.reshape(nb, t, cdim)[:, t - (SSM_CONV - 1):],
                       st.reshape(nb, n_heads_ssm, HEAD, SSM_STATE)))

        z_s, xbc_s, dtr_s = _inproj_ssm(ys, mod_s, g1, w_ssm, db, 1, inner, cdim)
        (q_s, k_s, _, v_s, _, qi_s, gs_s, ga_s, sm_s, _) = _inproj_attn(
            ys, mod_s, g1, w_attn, tabs_s, db, 1, aw, kvw, iw)
        npair = inner // LANES
        nbc = 2 * gn // LANES
        sc = state_conv[l]
        y_ssm_s, st_s = _ssd_sample(
            z_s.reshape(db, npair, LANES), xbc_s[:, :inner].reshape(db, npair, LANES),
            xbc_s[:, inner:].reshape(db, nbc, LANES),
            sc[:, :, :inner].reshape(db, SSM_CONV - 1, npair, LANES),
            sc[:, :, inner:].reshape(db, SSM_CONV - 1, nbc, LANES),
            jnp.repeat(dtr_s[:, :n_heads_ssm], HEAD, axis=1).reshape(db, npair, LANES),
            state_ssm[l].reshape(db, inner, SSM_STATE),
            cw[:, :inner].reshape(SSM_CONV, npair, LANES), cw[:, inner:].reshape(SSM_CONV, nbc, LANES),
            cb[:, :inner].reshape(npair, LANES), cb[:, inner:].reshape(nbc, LANES),
            jnp.repeat(dt_bias[l], HEAD).reshape(npair, LANES), jnp.repeat(a_neg, HEAD).reshape(npair, LANES),
            dsk_row.reshape(npair, LANES), ng_row.reshape(npair, LANES))

        qi_f = qi_s.astype(F32)
        qi8 = jnp.pad(qi_f.reshape(db, IDX_HEADS, HEAD), ((0, 0), (0, 8 - IDX_HEADS), (0, 0)))
        w8 = jnp.broadcast_to(jnp.pad(sm_s[:, HEAD:HEAD + IDX_HEADS] * idx_scale,
                                      ((0, 0), (0, 8 - IDX_HEADS)))[:, :, None], (db, 8, LANES))
        scores = _idx_scores_sample(page_table, qi8, w8, jnp.transpose(cache_kidx[l], (0, 2, 1)))
        bias = _select_sample(scores.reshape(db, past), qi_f, sm_s, topk_s, idx_scale)
        q_orig = q_s.astype(F32).reshape(db, n_heads, HEAD)[:, inv_order]
        att = _attend_sample(page_table, q_orig, bias[:, None, :],
                             k_s.reshape(db, ATTN_KV_HEADS, HEAD), v_s.reshape(db, ATTN_KV_HEADS, HEAD),
                             jnp.transpose(cache_k[l], (0, 2, 3, 1)), jnp.transpose(cache_v[l], (0, 2, 3, 1)))
        y_attn_s = att[:, order].reshape(db, aw)
        x1_s = _merge(ys, mod_s, y_ssm_s.reshape(db, inner), y_attn_s, gs_s, ga_s, wps, wpa, wo, db, 1)
        ys_next = _ffn(x1_s, mod_s, g2, final_g[None, :], wg, wu, wfo, db, 1, l == depth - 1)
        outs_s.append((k_s.reshape(db, ds, ATTN_KV_HEADS, HEAD), v_s.reshape(db, ds, ATTN_KV_HEADS, HEAD),
                       sm_s[:, :HEAD].reshape(db, ds, HEAD),
                       jnp.concatenate([sc[:, 1:], xbc_s[:, None, :]], axis=1),
                       st_s.reshape(db, n_heads_ssm, HEAD, SSM_STATE)))
        yp, ys = yp_next, ys_next

    stack = lambda outs, i: jnp.stack([o[i] for o in outs], axis=0)
    return (yp.reshape(nb, t, d), ys.reshape(db, ds, d),
            stack(outs_p, 0), stack(outs_p, 1), stack(outs_p, 2), stack(outs_p, 3), stack(outs_p, 4),
            stack(outs_s, 0), stack(outs_s, 1), stack(outs_s, 2), stack(outs_s, 3), stack(outs_s, 4))
```

```python
import functools

import jax
import jax.numpy as jnp
import numpy as np
from jax import lax
from jax.experimental import pallas as pl
from jax.experimental.pallas import tpu as pltpu

F32, BF16, I32, I16 = jnp.float32, jnp.bfloat16, jnp.int32, jnp.int16
HIGHEST = lax.Precision.HIGHEST

LANES = 128
HEAD = 64
SSM_STATE = 128
SSM_GROUPS = 4
SSM_CONV = 4
SSM_CHUNK = 128
ATTN_KV_HEADS = 4
IDX_HEADS = 4
TOPK_MAX = 256
Q_BLOCK = 128
PAGE_SIZE = 128
ROPE_THETA = 10000.0
NORM_EPS = 1e-6
KEY_BLOCK = 512
ONES_ROWS = 16
Q_SCALE = HEAD ** -0.5 * 1.4426950408889634
MASKED_SCORE = -3.3895313892515355e38
NEG_BIG = -1e30


def _cparams(sem, vmem_mb):
    return pltpu.CompilerParams(dimension_semantics=sem, vmem_limit_bytes=vmem_mb << 20)


def _bdot(a, b):
    return jnp.dot(a.astype(BF16), b.astype(BF16), preferred_element_type=F32)


def _bdot_nt(a, b):
    return lax.dot_general(a.astype(BF16), b.astype(BF16), (((1,), (1,)), ((), ())),
                           preferred_element_type=F32)


def _silu(x):
    h = 0.5 * x
    return h + h * jnp.tanh(h)


def _softplus(x):
    return jnp.maximum(x, 0.0) + jnp.log1p(jnp.exp(-jnp.abs(x)))


def _norm_mod(x, g, scale, shift):
    ms = jnp.mean(x * x, axis=-1, keepdims=True)
    return (x * lax.rsqrt(ms + NORM_EPS) * g) * (1.0 + scale) + shift


def _rope128(x, cos, sin_lo, sin_hi):
    return x * cos + pltpu.roll(x, 96, 1) * sin_lo + pltpu.roll(x, 32, 1) * sin_hi


def _rope_wide(x, cos, sin_lo, sin_hi):
    parts = [_rope128(x[:, j:j + LANES], cos, sin_lo, sin_hi) for j in range(0, x.shape[1], LANES)]
    return parts[0] if len(parts) == 1 else jnp.concatenate(parts, axis=1)


def _ada_body(c_ref, w_ref, b_ref, o_ref):
    o_ref[...] = _bdot(_silu(c_ref[...]), w_ref[...]) + b_ref[...]


def _ada(c_all, w_bf, b):
    mp, d = c_all.shape
    n = w_bf.shape[1]
    tn = n // 4
    return pl.pallas_call(
        _ada_body, grid=(n // tn,),
        in_specs=[pl.BlockSpec((mp, d), lambda j: (0, 0)),
                  pl.BlockSpec((d, tn), lambda j: (0, j)),
                  pl.BlockSpec((1, tn), lambda j: (0, j))],
        out_specs=pl.BlockSpec((mp, tn), lambda j: (0, j)),
        out_shape=jax.ShapeDtypeStruct((mp, n), F32),
        compiler_params=_cparams(("arbitrary",), 32), name="ada")(c_all, w_bf, b)


def _inproj_ssm_body(inner, cdim, x_ref, mod_ref, g_ref, w_ref, z_ref, xbc_ref, dt_ref):
    d = x_ref.shape[1]
    h = _norm_mod(x_ref[...], g_ref[...], mod_ref[:, d:2 * d], mod_ref[:, 0:d]).astype(BF16)
    z_ref[...] = jnp.dot(h, w_ref[:, 0:inner], preferred_element_type=F32)
    xbc_ref[...] = jnp.dot(h, w_ref[:, inner:inner + cdim], preferred_element_type=F32)
    dt_ref[...] = jnp.dot(h, w_ref[:, inner + cdim:inner + cdim + LANES], preferred_element_type=F32)


def _row_specs(tm, tpb, d, mod_rows):
    x_spec = pl.BlockSpec((tm, d), lambda m: (m, 0))
    mod_spec = pl.BlockSpec((None, mod_rows, 6 * d), lambda m: (m // tpb, 0, 0))
    return x_spec, mod_spec


def _const_spec(shape):
    return pl.BlockSpec(shape, lambda m: (0,) * len(shape))


def _inproj_ssm(x, mod3, g, w_bf, tm, tpb, inner, cdim):
    m, d = x.shape
    x_spec, mod_spec = _row_specs(tm, tpb, d, mod3.shape[1])
    row = lambda n: pl.BlockSpec((tm, n), lambda i: (i, 0))
    return pl.pallas_call(
        functools.partial(_inproj_ssm_body, inner, cdim), grid=(m // tm,),
        in_specs=[x_spec, mod_spec, _const_spec((1, d)), _const_spec(w_bf.shape)],
        out_specs=[row(inner), row(cdim), row(LANES)],
        out_shape=[jax.ShapeDtypeStruct((m, inner), F32), jax.ShapeDtypeStruct((m, cdim), F32),
                   jax.ShapeDtypeStruct((m, LANES), F32)],
        compiler_params=_cparams(("parallel",), 52), name="inproj_ssm")(x, mod3, g, w_bf)


def _inproj_attn_body(aw, kvw, iw, x_ref, mod_ref, g_ref, w_ref, cos_ref, slo_ref, shi_ref,
                      q_ref, kb_ref, kt_ref, vt_ref, vtb_ref, qi_ref, gs_ref, ga_ref, sm_ref, ki2_ref, kit_ref):
    d = x_ref.shape[1]
    h = _norm_mod(x_ref[...], g_ref[...], mod_ref[:, d:2 * d], mod_ref[:, 0:d]).astype(BF16)
    cos, slo, shi = cos_ref[...], slo_ref[...], shi_ref[...]

    def proj(a, b):
        return jnp.dot(h, w_ref[:, a:b], preferred_element_type=F32)

    o = 0
    q_ref[...] = (_rope_wide(proj(o, o + aw), cos, slo, shi) * Q_SCALE).astype(BF16)
    o += aw
    k = _rope_wide(proj(o, o + kvw), cos, slo, shi)
    kb_ref[...] = k.astype(BF16)
    kt_ref[...] = k.T
    o += kvw
    vt = proj(o, o + kvw).T
    vt_ref[...] = vt
    vtb_ref[...] = vt.astype(BF16)
    o += kvw
    qi_ref[...] = _rope_wide(proj(o, o + iw), cos, slo, shi).astype(BF16)
    o += iw
    gs_ref[...] = proj(o, o + d)
    o += d
    ga_ref[...] = proj(o, o + d)
    o += d
    s = proj(o, o + LANES)
    lane = lax.broadcasted_iota(I32, s.shape, 1)
    sm = jnp.where(lane < HEAD, _rope128(s, cos, slo, shi), s)
    sm_ref[...] = sm
    ki2_ref[...] = jnp.where(lane < HEAD, sm, pltpu.roll(sm, HEAD, 1)).astype(BF16)
    kit_ref[...] = sm.T[0:HEAD, :]


def _inproj_attn(x, mod3, g, w_bf, tabs, tm, tpb, aw, kvw, iw):
    m, d = x.shape
    x_spec, mod_spec = _row_specs(tm, tpb, d, mod3.shape[1])
    ntab = tabs[0].shape[0] // tm
    nb = m // (tm * tpb)
    tab_spec = pl.BlockSpec((tm, LANES), lambda i: (i % ntab, 0))
    row = lambda n, dt: (pl.BlockSpec((tm, n), lambda i: (i, 0)), jax.ShapeDtypeStruct((m, n), dt))
    tmin = lambda n, dt: (pl.BlockSpec((None, n, tm), lambda i: (i // tpb, 0, i % tpb)),
                          jax.ShapeDtypeStruct((nb, n, tm * tpb), dt))
    outs = [row(aw, BF16), row(kvw, BF16), tmin(kvw, F32), tmin(kvw, F32), tmin(kvw, BF16), row(iw, BF16),
            row(d, F32), row(d, F32), row(LANES, F32), row(LANES, BF16), tmin(HEAD, F32)]
    return pl.pallas_call(
        functools.partial(_inproj_attn_body, aw, kvw, iw), grid=(m // tm,),
        in_specs=[x_spec, mod_spec, _const_spec((1, d)), _const_spec(w_bf.shape),
                  tab_spec, tab_spec, tab_spec],
        out_specs=[spec for spec, _ in outs],
        out_shape=[shape for _, shape in outs],
        compiler_params=_cparams(("parallel",), 52), name="inproj_attn")(x, mod3, g, w_bf, *tabs)


def _ssd_body(inner, z_ref, xbc_ref, dtr_ref, cw_ref, cb_ref, dtb_ref, a_ref, dsk_ref, ng_ref,
              y_ref, st_ref, full_s, act_s, st_s, y_s):
    c = pl.program_id(1)
    Q, N = SSM_CHUNK, SSM_STATE
    cdim = xbc_ref.shape[1]
    heads_per_group = inner // HEAD // SSM_GROUPS
    gw = inner // SSM_GROUPS

    @pl.when(c == 0)
    def _():
        full_s[0:8, :] = jnp.zeros((8, cdim), F32)
        st_s[...] = jnp.zeros(st_s.shape, F32)

    full_s[8:8 + Q, :] = xbc_ref[...]
    for j in range(0, cdim, 512):
        acc = cb_ref[:, j:j + 512] + full_s[8:8 + Q, j:j + 512] * cw_ref[3:4, j:j + 512]
        for i in range(SSM_CONV - 1):
            acc = acc + full_s[5 + i:5 + i + Q, j:j + 512] * cw_ref[i:i + 1, j:j + 512]
        act_s[:, j:j + 512] = _silu(acc)
    full_s[0:8, :] = full_s[Q:Q + 8, :]

    dt = _softplus(dtr_ref[...] + dtb_ref[...])
    row = lax.broadcasted_iota(I32, (Q, Q), 0)
    col = lax.broadcasted_iota(I32, (Q, Q), 1)
    tri = row >= col
    acs = jnp.dot(tri.astype(F32), dt * a_ref[...], precision=HIGHEST, preferred_element_type=F32)
    acs_t, dt_t = acs.T, dt.T
    last = acs[Q - 1:Q, :]
    wdt = jnp.exp(last - acs) * dt
    eacs = jnp.exp(acs)
    cdec = jnp.exp(last)
    low = lax.broadcasted_iota(I32, (Q, LANES), 1) < HEAD
    low1 = low[0:1, :]

    for g in range(SSM_GROUPS):
        bg = act_s[:, inner + g * N:inner + (g + 1) * N]
        cg = act_s[:, inner + SSM_GROUPS * N + g * N:inner + SSM_GROUPS * N + (g + 1) * N]
        cb = _bdot_nt(cg, bg)
        bg_t = bg.T.astype(BF16)
        for p in range(heads_per_group // 2):
            h0 = g * heads_per_group + 2 * p
            js = slice(h0 * HEAD, h0 * HEAD + LANES)
            xp = act_s[:, js]
            xp_bf = xp.astype(BF16)
            stp = st_s[:, js]
            stp_bf = stp.astype(BF16)
            ys = []
            for h in (h0, h0 + 1):
                seg = acs[:, h:h + 1] - acs_t[h:h + 1, :]
                decay = jnp.exp(jnp.where(tri, seg, -jnp.inf))
                m = (cb * decay) * dt_t[h:h + 1, :]
                ce = cg * eacs[:, h:h + 1]
                ys.append(_bdot(m, xp_bf) + _bdot(ce, stp_bf))
            y_s[:, js] = jnp.where(low, ys[0], ys[1])
            wcol = jnp.where(low, wdt[:, h0:h0 + 1], wdt[:, h0 + 1:h0 + 2])
            dst = jnp.dot(bg_t, (xp * wcol).astype(BF16), preferred_element_type=F32)
            cd = jnp.where(low1, cdec[:, h0:h0 + 1], cdec[:, h0 + 1:h0 + 2])
            st_s[:, js] = stp * cd + dst

    for g in range(SSM_GROUPS):
        gs = slice(g * gw, (g + 1) * gw)
        y = y_s[:, gs] + dsk_ref[:, gs] * act_s[:, gs]
        y = y * _silu(z_ref[:, gs])
        ms = jnp.mean(y * y, axis=-1, keepdims=True)
        y_ref[:, gs] = (y * lax.rsqrt(ms + NORM_EPS) * ng_ref[:, gs]).astype(BF16)

    @pl.when(c == pl.num_programs(1) - 1)
    def _():
        st_ref[...] = st_s[...].T


def _ssd_prompt(z, xbc, dtr, cw, cb, dtb, a, dsk, ng, nb):
    m, inner = z.shape
    cdim = xbc.shape[1]
    nc = m // nb // SSM_CHUNK
    row = lambda n: pl.BlockSpec((SSM_CHUNK, n), lambda b, c: (b * nc + c, 0))
    const = lambda shape: pl.BlockSpec(shape, lambda b, c: (0,) * len(shape))
    return pl.pallas_call(
        functools.partial(_ssd_body, inner), grid=(nb, nc),
        in_specs=[row(inner), row(cdim), row(LANES), const(cw.shape), const(cb.shape),
                  const(dtb.shape), const(a.shape), const(dsk.shape), const(ng.shape)],
        out_specs=[row(inner), pl.BlockSpec((None, inner, SSM_STATE), lambda b, c: (b, 0, 0))],
        out_shape=[jax.ShapeDtypeStruct((m, inner), BF16),
                   jax.ShapeDtypeStruct((nb, inner, SSM_STATE), F32)],
        scratch_shapes=[pltpu.VMEM((SSM_CHUNK + 8, cdim), F32), pltpu.VMEM((SSM_CHUNK, cdim), F32),
                        pltpu.VMEM((SSM_STATE, inner), F32), pltpu.VMEM((SSM_CHUNK, inner), F32)],
        compiler_params=_cparams(("parallel", "arbitrary"), 40), name="ssd_prompt",
    )(z, xbc, dtr, cw, cb, dtb, a, dsk, ng)


def _ssd_step_body(z_ref, xs_ref, bc_ref, cxs_ref, cbc_ref, dtr_ref, st_ref,
                   wxs_ref, wbc_ref, bxs_ref, bbc_ref, dtb_ref, a_ref, dsk_ref, ng_ref,
                   y_ref, sto_ref):
    G = SSM_GROUPS
    last = SSM_CONV - 1
    xs = bxs_ref[...] + xs_ref[...] * wxs_ref[last]
    bc = bbc_ref[...] + bc_ref[...] * wbc_ref[last]
    for i in range(last):
        xs = xs + cxs_ref[i] * wxs_ref[i]
        bc = bc + cbc_ref[i] * wbc_ref[i]
    xs, bc = _silu(xs), _silu(bc)
    dt = _softplus(dtr_ref[...] + dtb_ref[...])
    dec = jnp.exp(dt * a_ref[...])
    xdt = xs * dt
    npair = xs.shape[0]
    pairs_per_group = npair // G
    r = lax.broadcasted_iota(I32, (LANES, LANES), 0)
    cidx = lax.broadcasted_iota(I32, (LANES, LANES), 1)
    eye = (r == cidx).astype(F32)
    nt = (((1,), (1,)), ((), ()))
    dec_t = lax.dot_general(eye, dec, nt, precision=HIGHEST, preferred_element_type=F32)
    xdt_t = lax.dot_general(eye, xdt, nt, precision=HIGHEST, preferred_element_type=F32)
    rows = lax.broadcasted_iota(I32, (npair, 1), 0)
    cbv = jnp.sum(bc[0:G, :] * bc[G:2 * G, :], axis=-1, keepdims=True)
    cbx = jnp.zeros((npair, 1), F32)
    for g in range(G):
        cbx = cbx + jnp.where(rows // pairs_per_group == g, cbv[g:g + 1, :], 0.0)
    c_bf = bc.astype(BF16)
    yoff = jnp.zeros(xs.shape, F32)
    for j in range(npair):
        g = j // pairs_per_group
        s = st_ref[j * LANES:(j + 1) * LANES, :]
        sto_ref[j * LANES:(j + 1) * LANES, :] = s * dec_t[:, j:j + 1] + xdt_t[:, j:j + 1] * bc[g:g + 1, :]
        rj = _bdot_nt(c_bf, s)
        yoff = yoff + jnp.where(rows == j, rj[G + g:G + g + 1, :], 0.0)
    y = yoff * dec + cbx * dt * xs + dsk_ref[...] * xs
    y = y * _silu(z_ref[...])
    ssq = jnp.sum(y * y, axis=-1, keepdims=True)
    msx = jnp.zeros((npair, 1), F32)
    for g in range(G):
        ing = rows // pairs_per_group == g
        tot = jnp.sum(jnp.where(ing, ssq, 0.0), axis=0, keepdims=True)
        msx = msx + jnp.where(ing, tot, 0.0)
    msx = msx / (pairs_per_group * LANES)
    y_ref[...] = y * lax.rsqrt(msx + NORM_EPS) * ng_ref[...]


def _ssd_sample(z, xs, bc, cxs, cbc, dtr, st, wxs, wbc, bxs, bbc, dtb, a, dsk, ng):
    db, npair, _ = z.shape
    per_b = lambda shape: pl.BlockSpec((None,) + shape, lambda b: (b,) + (0,) * len(shape))
    const = lambda arr: pl.BlockSpec(arr.shape, lambda b: (0,) * arr.ndim)
    return pl.pallas_call(
        _ssd_step_body, grid=(db,),
        in_specs=[per_b(z.shape[1:]), per_b(xs.shape[1:]), per_b(bc.shape[1:]), per_b(cxs.shape[1:]),
                  per_b(cbc.shape[1:]), per_b(dtr.shape[1:]), per_b(st.shape[1:]),
                  const(wxs), const(wbc), const(bxs), const(bbc), const(dtb), const(a), const(dsk), const(ng)],
        out_specs=[per_b(z.shape[1:]), per_b(st.shape[1:])],
        out_shape=[jax.ShapeDtypeStruct(z.shape, F32), jax.ShapeDtypeStruct(st.shape, F32)],
        compiler_params=_cparams(("parallel",), 32), name="ssd_sample",
    )(z, xs, bc, cxs, cbc, dtr, st, wxs, wbc, bxs, bbc, dtb, a, dsk, ng)


def _row_fold(x, h, op=jnp.add):
    parts = [x[j:j + h, :] for j in range(0, x.shape[0], h)]
    while len(parts) > 1:
        parts = [op(a, b) for a, b in zip(parts[0::2], parts[1::2])] + (parts[-1:] if len(parts) % 2 else [])
    return parts[0]


def _topk_bias(score_s, img_s, bias_s, nblk, kb, topk, qpos):
    nq = score_s.shape[1]
    blk = lambda s: pl.ds(pl.multiple_of(s * kb, kb), kb)
    one16, zero16 = jnp.int16(1), jnp.int16(0)
    work_s = bias_s

    def count(cmp):
        def body(s, acc):
            return acc + _row_fold(jnp.where(cmp(img_s[blk(s), :]), one16, zero16), 16)
        acc = lax.fori_loop(0, nblk, body, jnp.zeros((16, nq), I16))
        return jnp.sum(acc.astype(F32), axis=0, keepdims=True)

    def search(nbits, value_of):
        def step(t, u):
            code = u | lax.shift_left(jnp.int32(1), (nbits - 1 - t).astype(I32))
            c = value_of(code).astype(BF16)
            return jnp.where(count(lambda a: a >= c) >= topk, code, u)
        return lax.fori_loop(0, nbits, step, jnp.zeros((1, nq), I32))

    def bf16_value(code):
        pattern = jnp.where(code >= 32768, code - 32768, 65535 - code)
        return pltpu.bitcast(pattern << 16, jnp.float32).astype(F32)

    def set_image(fn):
        def body(s, carry):
            img_s[blk(s), :] = fn(s).astype(BF16)
            return carry
        lax.fori_loop(0, nblk, body, 0)

    t1 = bf16_value(search(16, bf16_value))

    e1 = jnp.clip((pltpu.bitcast(t1.astype(jnp.float32), I32) >> 23) & 0xFF, 25, 254)
    unit = pltpu.bitcast((e1 - 24) << 23, jnp.float32).astype(F32)
    inv_unit = pltpu.bitcast((278 - e1) << 23, jnp.float32).astype(F32)
    B2, B1 = 65536.0, 256.0

    def digit2(s):
        y = jnp.clip((score_s[blk(s), :] - t1) * inv_unit, -4 * B2, 4 * B2)
        work_s[blk(s), :] = y
        return jnp.floor(y * (1.0 / B2))
    set_image(digit2)
    t2 = (search(2, lambda code: (code - 1).astype(F32)) - 1).astype(F32)
    set_image(lambda s: jnp.floor(jnp.clip(work_s[blk(s), :] - t2 * B2, -B1, B2) * (1.0 / B1)))
    t3 = search(8, lambda code: code.astype(F32)).astype(F32)
    set_image(lambda s: jnp.floor(jnp.clip(work_s[blk(s), :] - (t2 * B2 + t3 * B1), -1.0, B1)))
    t4 = search(8, lambda code: code.astype(F32)).astype(F32)
    v0 = t1 + (t2 * B2 + t3 * B1 + t4) * unit

    def smallest(keep):
        def body(s, acc):
            x = score_s[blk(s), :]
            return jnp.minimum(acc, _row_fold(jnp.where(keep(x), x, jnp.inf), 8, jnp.minimum))
        acc = lax.fori_loop(0, nblk, body, jnp.full((8, nq), jnp.inf, F32))
        return jnp.min(acc, axis=0, keepdims=True)

    def count_above(v):
        def body(s, acc):
            return acc + _row_fold(jnp.where(score_s[blk(s), :] > v, 1.0, 0.0), 8)
        return jnp.sum(lax.fori_loop(0, nblk, body, jnp.zeros((8, nq), F32)), axis=0, keepdims=True)

    def refine(carry):
        v, above = carry
        v = jnp.where(above >= topk, smallest(lambda x: x > v), v)
        return v, count_above(v)

    v = smallest(lambda x: x >= v0)
    v, above = lax.while_loop(lambda c: jnp.max(c[1]) >= topk, refine, (v, count_above(v)))
    need = topk - above

    r_i = lax.broadcasted_iota(I32, (LANES, LANES), 0)
    c_i = lax.broadcasted_iota(I32, (LANES, LANES), 1)
    lower = jnp.where(r_i >= c_i, 1.0, 0.0).astype(BF16)
    sub_iota = lax.broadcasted_iota(I32, (LANES, 1), 0)

    def bias_body(s, carry):
        for j in range(0, kb, LANES):
            off = pl.multiple_of(s * kb + j, LANES)
            x = score_s[pl.ds(off, LANES), :]
            eq = x == v
            eqf = jnp.where(eq, 1.0, 0.0)
            incl = jnp.dot(lower, eqf.astype(BF16), preferred_element_type=F32)
            tie = jnp.where(carry + incl - eqf < need, 0.0, -jnp.inf)
            b = jnp.where(x > v, 0.0, jnp.where(eq, tie, -jnp.inf))
            bias_s[pl.ds(off, LANES), :] = jnp.where((off + sub_iota) <= qpos, b, -jnp.inf)
            carry = carry + incl[LANES - 1:LANES, :]
        return carry

    lax.fori_loop(0, nblk, bias_body, jnp.zeros((1, nq), F32))


def _dsa_t_body(topk, idx_scale, q_ref, qi_ref, sm_ref, ki2_ref, k_ref, vt_ref, o_ref,
                score_s, img_s, bias_s, qs_s, lga_s, lgb_s, m_s, acc_s):
    i = pl.program_id(1)
    QB, KB = Q_BLOCK, KEY_BLOCK
    nkb = (i * QB + QB + KB - 1) // KB
    qpos = i * QB + lax.broadcasted_iota(I32, (1, QB), 1)
    low = lax.broadcasted_iota(I32, (QB, LANES), 1) < HEAD
    zero_bf = jnp.zeros((QB, LANES), BF16)
    blk = lambda s: pl.ds(pl.multiple_of(s * KB, KB), KB)

    qi = qi_ref[...]
    sm_t = sm_ref[...].T
    qh, wh = [], []
    for h in range(IDX_HEADS):
        chunk = qi[:, (h // 2) * LANES:(h // 2 + 1) * LANES]
        qh.append(jnp.where(low if h % 2 == 0 else ~low, chunk, zero_bf))
        wh.append(sm_t[HEAD + h:HEAD + h + 1, :] * idx_scale)

    q_stack = jnp.concatenate(qh, axis=0)

    def score_body(s, carry):
        sc = _bdot_nt(ki2_ref[blk(s), :], q_stack)
        acc = jnp.zeros((KB, QB), F32)
        for h in range(IDX_HEADS):
            acc = acc + wh[h] * jnp.maximum(sc[:, h * QB:(h + 1) * QB], 0.0)
        kpos = s * KB + lax.broadcasted_iota(I32, (KB, 1), 0)
        sc = jnp.where(kpos <= qpos, jnp.maximum(acc, MASKED_SCORE), MASKED_SCORE)
        score_s[blk(s), :] = sc
        img_s[blk(s), :] = sc.astype(BF16)
        return carry

    lax.fori_loop(0, nkb, score_body, 0)
    _topk_bias(score_s, img_s, bias_s, nkb, KB, topk, qpos)

    q = q_ref[...]
    nchunk = q.shape[1] // LANES
    per_kv_chunk = nchunk // (ATTN_KV_HEADS // 2)
    nstack = 2 * per_kv_chunk
    srows = nstack * QB
    for cj in range(nchunk):
        kvc, j = divmod(cj, per_kv_chunk)
        qc = q[:, cj * LANES:(cj + 1) * LANES]
        for half in range(2):
            r0 = (kvc * nstack + half * per_kv_chunk + j) * QB
            qs_s[r0:r0 + QB, :] = jnp.where(low if half == 0 else ~low, qc, zero_bf)

    ones_rows = jnp.ones((ONES_ROWS, KB), BF16)
    cols = lambda r: slice(r * QB, (r + 1) * QB)
    n_kvc = ATTN_KV_HEADS // 2
    nsteps = n_kvc * nkb

    def step_of(t):
        t = jnp.minimum(t, nsteps - 1)
        kvc = (t >= nkb).astype(I32)
        return kvc, t - kvc * nkb

    def logits_to(dst, t):
        kvc, s = step_of(t)
        kb = k_ref[blk(s), pl.ds(pl.multiple_of(kvc * LANES, LANES), LANES)]
        bias = bias_s[blk(s), :]
        lg = _bdot_nt(kb, qs_s[pl.ds(pl.multiple_of(kvc * srows, srows), srows), :])
        for r in range(nstack):
            dst[:, cols(r)] = lg[:, cols(r)] + bias

    def consume(src, t):
        kvc, s = step_of(t)
        vt = vt_ref[pl.ds(pl.multiple_of(kvc * LANES, LANES), LANES), blk(s)]
        ps, alphas = [], []
        for r in range(nstack):
            lg = src[:, cols(r)]
            m = m_s[kvc, :, cols(r)]
            mn = jnp.maximum(m, jnp.max(lg, axis=0, keepdims=True))
            m_s[kvc, :, cols(r)] = mn
            ps.append(jnp.exp2(lg - mn).astype(BF16))
            alphas.append(jnp.exp2(m - mn))
        for half in range(2):
            hs = slice(half * per_kv_chunk, (half + 1) * per_kv_chunk)
            hc = slice(half * per_kv_chunk * QB, (half + 1) * per_kv_chunk * QB)
            v_aug = jnp.concatenate([vt[half * HEAD:(half + 1) * HEAD, :], ones_rows], axis=0)
            pv = jnp.dot(v_aug, jnp.concatenate(ps[hs], axis=1), preferred_element_type=F32)
            acc_s[kvc, :, hc] = jnp.concatenate(alphas[hs], axis=1) * acc_s[kvc, :, hc] + pv

    m_s[...] = jnp.full(m_s.shape, NEG_BIG, F32)
    acc_s[...] = jnp.zeros(acc_s.shape, F32)
    logits_to(lga_s, 0)

    def run_pairs(t0, npairs):
        for p in range(npairs):
            logits_to(lgb_s, t0 + 2 * p + 1)
            consume(lga_s, t0 + 2 * p)
            logits_to(lga_s, t0 + 2 * p + 2)
            consume(lgb_s, t0 + 2 * p + 1)

    def quad_body(i4, carry):
        run_pairs(4 * i4, 2)
        return carry

    lax.fori_loop(0, nsteps // 4, quad_body, 0)

    @pl.when(nsteps % 4 == 2)
    def _():
        run_pairs(nsteps - 2, 1)

    for kvc in range(n_kvc):
        acc = acc_s[kvc]
        for j in range(per_kv_chunk):
            a = acc[:, cols(j)]
            b = acc[:, cols(per_kv_chunk + j)]
            chunk_t = jnp.concatenate([a[0:HEAD, :] / a[HEAD:HEAD + 1, :], b[0:HEAD, :] / b[HEAD:HEAD + 1, :]],
                                      axis=0)
            cj = kvc * per_kv_chunk + j
            o_ref[:, cj * LANES:(cj + 1) * LANES] = chunk_t.T.astype(BF16)


def _dsa_prompt_t(q, qi, sm, ki2, k, vt, nb, topk):
    m, aw = q.shape
    t = m // nb
    nq = t // Q_BLOCK
    tpad = -(-t // KEY_BLOCK) * KEY_BLOCK
    srows = aw // HEAD // (ATTN_KV_HEADS // 2) * Q_BLOCK
    idx_scale = IDX_HEADS ** -0.5 * HEAD ** -0.5
    row = lambda n: pl.BlockSpec((Q_BLOCK, n), lambda b, i: (b * nq + i, 0))
    per_b = lambda n: pl.BlockSpec((t, n), lambda b, i: (b, 0))
    return pl.pallas_call(
        functools.partial(_dsa_t_body, topk, idx_scale), grid=(nb, nq),
        in_specs=[row(aw), row(qi.shape[1]), row(LANES), per_b(LANES), per_b(k.shape[1]),
                  pl.BlockSpec((None,) + vt.shape[1:], lambda b, i: (b, 0, 0))],
        out_specs=row(aw),
        out_shape=jax.ShapeDtypeStruct((m, aw), BF16),
        scratch_shapes=[pltpu.VMEM((tpad, Q_BLOCK), F32), pltpu.VMEM((tpad, Q_BLOCK), BF16),
                        pltpu.VMEM((tpad, Q_BLOCK), F32),
                        pltpu.VMEM((2 * srows, LANES), BF16),
                        pltpu.VMEM((KEY_BLOCK, srows), F32), pltpu.VMEM((KEY_BLOCK, srows), F32),
                        pltpu.VMEM((ATTN_KV_HEADS // 2, 1, srows), F32),
                        pltpu.VMEM((ATTN_KV_HEADS // 2, HEAD + ONES_ROWS, srows), F32)],
        compiler_params=_cparams(("parallel", "arbitrary"), 52), name="dsa_prompt",
    )(q, qi, sm, ki2, k, vt)


def _page_specs(block, npages):
    def make(u):
        return pl.BlockSpec((None,) + block, lambda b, pt: (pt[b, u],) + (0,) * len(block))
    return [make(u) for u in range(npages)]


def _idx_score_body(pt_ref, qi_ref, w_ref, *refs):
    pages, o_ref = refs[:-1], refs[-1]
    qi = qi_ref[...]
    w = w_ref[...]
    for u, page in enumerate(pages):
        s = jnp.maximum(_bdot(qi, page[...]), 0.0)
        o_ref[:, u * PAGE_SIZE:(u + 1) * PAGE_SIZE] = jnp.sum(w * s, axis=0, keepdims=True)


def _idx_scores_sample(page_table, qi8, w8, kidx_t):
    db, npages = page_table.shape
    grid_spec = pltpu.PrefetchScalarGridSpec(
        num_scalar_prefetch=1, grid=(db,),
        in_specs=[pl.BlockSpec((None,) + qi8.shape[1:], lambda b, pt: (b, 0, 0)),
                  pl.BlockSpec((None,) + w8.shape[1:], lambda b, pt: (b, 0, 0))]
                 + _page_specs(kidx_t.shape[1:], npages),
        out_specs=pl.BlockSpec((None, 1, npages * PAGE_SIZE), lambda b, pt: (b, 0, 0)))
    return pl.pallas_call(
        _idx_score_body, grid_spec=grid_spec,
        out_shape=jax.ShapeDtypeStruct((db, 1, npages * PAGE_SIZE), F32),
        compiler_params=_cparams(("parallel",), 32), name="idx_scores_sample",
    )(page_table, qi8, w8, *([kidx_t] * npages))


def _select_sample_body(topk, idx_scale, past, sc_ref, qi_ref, sm_ref, bias_ref, score_s, img_s, bias_s):
    rows = sc_ref.shape[0]
    sm = sm_ref[...]
    qi = qi_ref[...]
    ki = sm[:, 0:HEAD]
    new = jnp.zeros((rows, 1), F32)
    for h in range(IDX_HEADS):
        d = jnp.sum(qi[:, h * HEAD:(h + 1) * HEAD] * ki, axis=-1, keepdims=True)
        new = new + (sm[:, HEAD + h:HEAD + h + 1] * idx_scale) * jnp.maximum(d, 0.0)
    nblk = (past + LANES) // LANES
    lane = lax.broadcasted_iota(I32, (rows, LANES), 1)
    for j in range(nblk):
        js = slice(j * LANES, (j + 1) * LANES)
        sc = sc_ref[:, js] if j < nblk - 1 else jnp.where(lane == 0, new, MASKED_SCORE)
        sc = jnp.maximum(sc, MASKED_SCORE).T
        score_s[js, :] = sc
        img_s[js, :] = sc.astype(BF16)
    qpos = jnp.full((1, rows), past, I32)
    _topk_bias(score_s, img_s, bias_s, nblk, LANES, topk, qpos)
    for j in range(nblk):
        js = slice(j * LANES, (j + 1) * LANES)
        bias_ref[:, js] = bias_s[js, :].T


def _select_sample(scores, qi, sm, topk, idx_scale):
    db, past = scores.shape
    full = lambda a: pl.BlockSpec(a.shape, lambda i: (0,) * a.ndim)
    keys = past + LANES
    return pl.pallas_call(
        functools.partial(_select_sample_body, topk, idx_scale, past), grid=(1,),
        in_specs=[full(scores), full(qi), full(sm)],
        out_specs=pl.BlockSpec((db, keys), lambda i: (0, 0)),
        out_shape=jax.ShapeDtypeStruct((db, keys), F32),
        scratch_shapes=[pltpu.VMEM((keys, db), F32), pltpu.VMEM((keys, db), BF16), pltpu.VMEM((keys, db), F32)],
        compiler_params=_cparams(("arbitrary",), 32), name="select_sample")(scores, qi, sm)


def _attend_sample_body(npages, pt_ref, q_ref, bias_ref, knew_ref, vnew_ref, *refs):
    kpages, vpages, o_ref = refs[:npages], refs[npages:2 * npages], refs[2 * npages]
    G = ATTN_KV_HEADS
    q = q_ref[...]
    q_bf = q.astype(BF16)
    nh = q.shape[0]
    past = npages * PAGE_SIZE
    group = lax.broadcasted_iota(I32, (nh, 1), 0) // (nh // G)

    def by_group(parts):
        out = parts[G - 1]
        for g in range(G - 2, -1, -1):
            out = jnp.where(group == g, parts[g], out)
        return out

    lg = jnp.concatenate(
        [by_group([_bdot(q_bf, kpages[u][g]) for g in range(G)]) for u in range(npages)], axis=1)
    lg = lg + bias_ref[:, 0:past]
    lg_new = by_group([jnp.sum(q * knew_ref[g:g + 1, :], axis=-1, keepdims=True) for g in range(G)])
    lg_new = lg_new + bias_ref[:, past:past + 1]
    m = jnp.maximum(jnp.max(lg, axis=-1, keepdims=True), lg_new)
    p = jnp.exp2(lg - m)
    p_new = jnp.exp2(lg_new - m)
    denom = jnp.sum(p, axis=-1, keepdims=True) + p_new
    p_bf = p.astype(BF16)
    accs = [p_new * vnew_ref[g:g + 1, :] for g in range(G)]
    for u in range(npages):
        pu = p_bf[:, u * PAGE_SIZE:(u + 1) * PAGE_SIZE]
        for g in range(G):
            accs[g] = accs[g] + _bdot_nt(pu, vpages[u][g])
    o_ref[...] = by_group(accs) / denom


def _attend_sample(page_table, q, bias, knew, vnew, k_t, v_t):
    db, npages = page_table.shape
    per_b = lambda a: pl.BlockSpec((None,) + a.shape[1:], lambda b, pt: (b,) + (0,) * (a.ndim - 1))
    grid_spec = pltpu.PrefetchScalarGridSpec(
        num_scalar_prefetch=1, grid=(db,),
        in_specs=[per_b(q), per_b(bias), per_b(knew), per_b(vnew)]
                 + _page_specs(k_t.shape[1:], npages) + _page_specs(v_t.shape[1:], npages),
        out_specs=per_b(q))
    return pl.pallas_call(
        functools.partial(_attend_sample_body, npages), grid_spec=grid_spec,
        out_shape=jax.ShapeDtypeStruct(q.shape, F32),
        compiler_params=_cparams(("parallel",), 48), name="attend_sample",
    )(page_table, q, bias, knew, vnew, *([k_t] * npages), *([v_t] * npages))


def _merge_body(x_ref, mod_ref, ys_ref, ya_ref, gs_ref, ga_ref, wps_ref, wpa_ref, wo_ref, o_ref):
    d = x_ref.shape[1]
    merged = (jax.nn.sigmoid(gs_ref[...]) * _bdot(ys_ref[...], wps_ref[...])
              + jax.nn.sigmoid(ga_ref[...]) * _bdot(ya_ref[...], wpa_ref[...]))
    o_ref[...] = x_ref[...] + mod_ref[:, 2 * d:3 * d] * _bdot(merged, wo_ref[...])


def _merge(x, mod3, ys, ya, gs, ga, wps, wpa, wo, tm, tpb):
    m, d = x.shape
    x_spec, mod_spec = _row_specs(tm, tpb, d, mod3.shape[1])
    row = lambda n: pl.BlockSpec((tm, n), lambda i: (i, 0))
    return pl.pallas_call(
        _merge_body, grid=(m // tm,),
        in_specs=[x_spec, mod_spec, row(ys.shape[1]), row(ya.shape[1]), row(d), row(d),
                  _const_spec(wps.shape), _const_spec(wpa.shape), _const_spec(wo.shape)],
        out_specs=row(d), out_shape=jax.ShapeDtypeStruct((m, d), F32),
        compiler_params=_cparams(("parallel",), 48), name="merge")(x, mod3, ys, ya, gs, ga, wps, wpa, wo)


def _ffn_body(last_layer, x_ref, mod_ref, g_ref, fg_ref, wg_ref, wu_ref, wo_ref, o_ref):
    d = x_ref.shape[1]
    x = x_ref[...]
    h = _norm_mod(x, g_ref[...], mod_ref[:, 4 * d:5 * d], mod_ref[:, 3 * d:4 * d]).astype(BF16)
    gate = jnp.dot(h, wg_ref[...], preferred_element_type=F32)
    up = jnp.dot(h, wu_ref[...], preferred_element_type=F32)
    x2 = x + mod_ref[:, 5 * d:6 * d] * _bdot(_silu(gate) * up, wo_ref[...])
    if last_layer:
        ms = jnp.mean(x2 * x2, axis=-1, keepdims=True)
        x2 = x2 * lax.rsqrt(ms + NORM_EPS) * fg_ref[...]
    o_ref[...] = x2


def _ffn(x, mod3, g, fg, wg, wu, wo, tm, tpb, last_layer):
    m, d = x.shape
    x_spec, mod_spec = _row_specs(tm, tpb, d, mod3.shape[1])
    return pl.pallas_call(
        functools.partial(_ffn_body, last_layer), grid=(m // tm,),
        in_specs=[x_spec, mod_spec, _const_spec((1, d)), _const_spec((1, d)),
                  _const_spec(wg.shape), _const_spec(wu.shape), _const_spec(wo.shape)],
        out_specs=pl.BlockSpec((tm, d), lambda i: (i, 0)), out_shape=jax.ShapeDtypeStruct((m, d), F32),
        compiler_params=_cparams(("parallel",), 56), name="ffn")(x, mod3, g, fg, wg, wu, wo)


def _rope_tables(pos):
    half = HEAD // 2
    inv = ROPE_THETA ** (-jnp.arange(half, dtype=F32) / half)
    ang = pos.astype(F32)[:, None] * inv[None, :]
    cos = jnp.tile(jnp.cos(ang), (1, LANES // half))
    sin = jnp.tile(jnp.sin(ang), (1, LANES // half))
    first = (jnp.arange(LANES) % HEAD) < half
    return cos, jnp.where(first, -sin, 0.0), jnp.where(first, 0.0, sin)


def _q_head_order(n_heads):
    rep = n_heads // ATTN_KV_HEADS
    order = []
    for c in range(ATTN_KV_HEADS // 2):
        for j in range(rep):
            order += [2 * c * rep + j, (2 * c + 1) * rep + j]
    return np.asarray(order)


def kernel(x_prompt, x_sample, cache_k, cache_v, cache_kidx, state_conv, state_ssm, page_table, c_prompt, c_sample, w_ada, b_ada, norm1_g, w_in, conv_w, conv_b, dt_bias, a_log, d_skip, ssm_norm_g, w_proj_ssm, w_proj_attn, w_out, norm2_g, w_ffn_in, w_ffn_out, final_g):
    nb, t, d = x_prompt.shape
    db, ds, _ = x_sample.shape
    depth = w_in.shape[0]
    assert ds == 1 and t % Q_BLOCK == 0 and t % SSM_CHUNK == 0
    n_heads_ssm = dt_bias.shape[1]
    inner = n_heads_ssm * HEAD
    gn = SSM_GROUPS * SSM_STATE
    cdim = inner + 2 * gn
    kvw = ATTN_KV_HEADS * HEAD
    aw = w_proj_attn.shape[1]
    n_heads = aw // HEAD
    iw = IDX_HEADS * HEAD
    ffn_hidden = w_ffn_out.shape[1]
    npages = page_table.shape[1]
    past = npages * PAGE_SIZE
    topk_p = min(TOPK_MAX, t // 4)
    topk_s = min(TOPK_MAX, (past + ds) // 4)
    assert past + ds >= topk_s
    idx_scale = IDX_HEADS ** -0.5 * HEAD ** -0.5
    tm_p = 256 if t % 256 == 0 else 128
    tpb_p = t // tm_p

    splits = np.cumsum([inner, inner, gn, gn, n_heads_ssm, aw, kvw, kvw, iw, HEAD, IDX_HEADS, d])
    order = _q_head_order(n_heads)
    inv_order = np.argsort(order)

    cos_p, slo_p, shi_p = _rope_tables(jnp.arange(t, dtype=I32))
    tabs_p = (cos_p, slo_p, shi_p)
    tabs_s = tuple(jnp.broadcast_to(a, (db, LANES)) for a in _rope_tables(past + jnp.arange(ds, dtype=I32)))

    rows_c = nb + db
    c_all = jnp.concatenate([c_prompt, c_sample, jnp.zeros((-rows_c % 8, d), F32)], axis=0)

    yp = x_prompt.reshape(nb * t, d)
    ys = x_sample.reshape(db, d)
    outs_p, outs_s = [], []
    for l in range(depth):
        (wz, wxs, wbm, wcm, wdt, wq, wk, wv, wqi, wki, wwi, wgs, wga) = jnp.split(w_in[l], splits, axis=1)
        w_ssm = jnp.concatenate([wz, wxs, wbm, wcm, wdt, jnp.zeros((d, LANES - n_heads_ssm), F32)],
                                axis=1).astype(BF16)
        wq_perm = wq.reshape(d, n_heads, HEAD)[:, order].reshape(d, aw)
        w_attn = jnp.concatenate([wq_perm, wk, wv, wqi, wgs, wga, wki, wwi,
                                  jnp.zeros((d, LANES - HEAD - IDX_HEADS), F32)], axis=1).astype(BF16)
        wps = w_proj_ssm[l].astype(BF16)
        wpa = w_proj_attn[l].reshape(n_heads, HEAD, d)[order].reshape(aw, d).astype(BF16)
        wo = w_out[l].astype(BF16)
        wg = w_ffn_in[l][:, :ffn_hidden].astype(BF16)
        wu = w_ffn_in[l][:, ffn_hidden:].astype(BF16)
        wfo = w_ffn_out[l].astype(BF16)
        g1 = norm1_g[l][None, :]
        g2 = norm2_g[l][None, :]
        a_neg = -jnp.exp(a_log[l])
        pad_h = LANES - n_heads_ssm
        dtb_row = jnp.pad(dt_bias[l], (0, pad_h))[None, :]
        a_row = jnp.pad(a_neg, (0, pad_h))[None, :]
        dsk_row = jnp.repeat(d_skip[l], HEAD)[None, :]
        ng_row = ssm_norm_g[l][None, :]
        cw = conv_w[l]
        cb = conv_b[l][None, :]

        mod = _ada(c_all, w_ada[l].astype(BF16), b_ada[l][None, :])
        mod_p = mod[:nb][:, None, :]
        mod_s = mod[nb:nb + db][None]

        z, xbc, dtr = _inproj_ssm(yp, mod_p, g1, w_ssm, tm_p, tpb_p, inner, cdim)
        (q_bf, k_bf, kt, vt, vt_bf, qi_bf, gs, ga, sm, ki2, kit) = _inproj_attn(
            yp, mod_p, g1, w_attn, tabs_p, tm_p, tpb_p, aw, kvw, iw)
        y_ssm, st = _ssd_prompt(z, xbc, dtr, cw, cb, dtb_row, a_row, dsk_row, ng_row, nb)
        y_attn = _dsa_prompt_t(q_bf, qi_bf, sm, ki2, k_bf, vt_bf, nb, topk_p)
        x1 = _merge(yp, mod_p, y_ssm, y_attn, gs, ga, wps, wpa, wo, tm_p, tpb_p)
        yp_next = _ffn(x1, mod_p, g2, final_g[None, :], wg, wu, wfo, tm_p, tpb_p, l == depth - 1)
        heads_last = lambda a: jnp.transpose(a.reshape(a.shape[0], ATTN_KV_HEADS, HEAD, a.shape[2]), (0, 3, 1, 2))
        outs_p.append((heads_last(kt), heads_last(vt), jnp.transpose(kit, (0, 2, 1)),
                       xbc.reshape(nb, t, cdim)[:, t - (SSM_CONV - 1):],
                       st.reshape(nb, n_heads_ssm, HEAD, SSM_STATE)))

        z_s, xbc_s, dtr_s = _inproj_ssm(ys, mod_s, g1, w_ssm, db, 1, inner, cdim)
        (q_s, _, kt_s, vt_s, _, qi_s, gs_s, ga_s, sm_s, _, kit_s) = _inproj_attn(
            ys, mod_s, g1, w_attn, tabs_s, db, 1, aw, kvw, iw)
        k_s, v_s = heads_last(kt_s)[0], heads_last(vt_s)[0]
        npair = inner // LANES
        nbc = 2 * gn // LANES
        sc = state_conv[l]
        y_ssm_s, st_s = _ssd_sample(
            z_s.reshape(db, npair, LANES), xbc_s[:, :inner].reshape(db, npair, LANES),
            xbc_s[:, inner:].reshape(db, nbc, LANES),
            sc[:, :, :inner].reshape(db, SSM_CONV - 1, npair, LANES),
            sc[:, :, inner:].reshape(db, SSM_CONV - 1, nbc, LANES),
            jnp.repeat(dtr_s[:, :n_heads_ssm], HEAD, axis=1).reshape(db, npair, LANES),
            state_ssm[l].reshape(db, inner, SSM_STATE),
            cw[:, :inner].reshape(SSM_CONV, npair, LANES), cw[:, inner:].reshape(SSM_CONV, nbc, LANES),
            cb[:, :inner].reshape(npair, LANES), cb[:, inner:].reshape(nbc, LANES),
            jnp.repeat(dt_bias[l], HEAD).reshape(npair, LANES), jnp.repeat(a_neg, HEAD).reshape(npair, LANES),
            dsk_row.reshape(npair, LANES), ng_row.reshape(npair, LANES))

        qi_f = qi_s.astype(F32)
        qi8 = jnp.pad(qi_f.reshape(db, IDX_HEADS, HEAD), ((0, 0), (0, 8 - IDX_HEADS), (0, 0)))
        w8 = jnp.broadcast_to(jnp.pad(sm_s[:, HEAD:HEAD + IDX_HEADS] * idx_scale,
                                      ((0, 0), (0, 8 - IDX_HEADS)))[:, :, None], (db, 8, LANES))
        scores = _idx_scores_sample(page_table, qi8, w8, jnp.transpose(cache_kidx[l], (0, 2, 1)))
        bias = _select_sample(scores.reshape(db, past), qi_f, sm_s, topk_s, idx_scale)
        q_orig = q_s.astype(F32).reshape(db, n_heads, HEAD)[:, inv_order]
        att = _attend_sample(page_table, q_orig, bias[:, None, :],
                             k_s, v_s,
                             jnp.transpose(cache_k[l], (0, 2, 3, 1)), jnp.transpose(cache_v[l], (0, 2, 3, 1)))
        y_attn_s = att[:, order].reshape(db, aw)
        x1_s = _merge(ys, mod_s, y_ssm_s.reshape(db, inner), y_attn_s, gs_s, ga_s, wps, wpa, wo, db, 1)
        ys_next = _ffn(x1_s, mod_s, g2, final_g[None, :], wg, wu, wfo, db, 1, l == depth - 1)
        outs_s.append((k_s[:, None], v_s[:, None], jnp.transpose(kit_s, (2, 0, 1)),
                       jnp.concatenate([sc[:, 1:], xbc_s[:, None, :]], axis=1),
                       st_s.reshape(db, n_heads_ssm, HEAD, SSM_STATE)))
        yp, ys = yp_next, ys_next

    stack = lambda outs, i: jnp.stack([o[i] for o in outs], axis=0)
    return (yp.reshape(nb, t, d), ys.reshape(db, ds, d),
            stack(outs_p, 0), stack(outs_p, 1), stack(outs_p, 2), stack(outs_p, 3), stack(outs_p, 4),
            stack(outs_s, 0), stack(outs_s, 1), stack(outs_s, 2), stack(outs_s, 3), stack(outs_s, 4))
```

```python
import functools

import jax
import jax.numpy as jnp
import numpy as np
from jax import lax
from jax.experimental import pallas as pl
from jax.experimental.pallas import tpu as pltpu

F32, BF16, I32, I16 = jnp.float32, jnp.bfloat16, jnp.int32, jnp.int16
HIGHEST = lax.Precision.HIGHEST

LANES = 128
HEAD = 64
SSM_STATE = 128
SSM_GROUPS = 4
SSM_CONV = 4
SSM_CHUNK = 128
ATTN_KV_HEADS = 4
IDX_HEADS = 4
TOPK_MAX = 256
Q_BLOCK = 128
PAGE_SIZE = 128
ROPE_THETA = 10000.0
NORM_EPS = 1e-6
KEY_BLOCK = 512
ONES_ROWS = 16
Q_SCALE = HEAD ** -0.5 * 1.4426950408889634
MASKED_SCORE = -3.3895313892515355e38
NEG_BIG = -1e30


def _cparams(sem, vmem_mb):
    return pltpu.CompilerParams(dimension_semantics=sem, vmem_limit_bytes=vmem_mb << 20)


def _bdot(a, b):
    return jnp.dot(a.astype(BF16), b.astype(BF16), preferred_element_type=F32)


def _bdot_nt(a, b):
    return lax.dot_general(a.astype(BF16), b.astype(BF16), (((1,), (1,)), ((), ())),
                           preferred_element_type=F32)


def _silu(x):
    h = 0.5 * x
    return h + h * jnp.tanh(h)


def _softplus(x):
    return jnp.maximum(x, 0.0) + jnp.log(1.0 + jnp.exp(-jnp.abs(x)))


def _norm_mod(x, g, scale, shift):
    ms = jnp.mean(x * x, axis=-1, keepdims=True)
    return (x * lax.rsqrt(ms + NORM_EPS) * g) * (1.0 + scale) + shift


def _rope128(x, cos, sin_lo, sin_hi):
    return x * cos + pltpu.roll(x, 96, 1) * sin_lo + pltpu.roll(x, 32, 1) * sin_hi


def _rope_wide(x, cos, sin_lo, sin_hi):
    parts = [_rope128(x[:, j:j + LANES], cos, sin_lo, sin_hi) for j in range(0, x.shape[1], LANES)]
    return parts[0] if len(parts) == 1 else jnp.concatenate(parts, axis=1)


def _ada_body(c_ref, w_ref, b_ref, o_ref):
    o_ref[...] = _bdot(_silu(c_ref[...]), w_ref[...]) + b_ref[...]


def _ada(c_all, w_bf, b):
    mp, d = c_all.shape
    n = w_bf.shape[1]
    tn = n // 4
    return pl.pallas_call(
        _ada_body, grid=(n // tn,),
        in_specs=[pl.BlockSpec((mp, d), lambda j: (0, 0)),
                  pl.BlockSpec((d, tn), lambda j: (0, j)),
                  pl.BlockSpec((1, tn), lambda j: (0, j))],
        out_specs=pl.BlockSpec((mp, tn), lambda j: (0, j)),
        out_shape=jax.ShapeDtypeStruct((mp, n), F32),
        compiler_params=_cparams(("arbitrary",), 32), name="ada")(c_all, w_bf, b)


def _inproj_ssm_body(inner, cdim, x_ref, mod_ref, g_ref, w_ref, z_ref, xbc_ref, dt_ref):
    d = x_ref.shape[1]
    h = _norm_mod(x_ref[...], g_ref[...], mod_ref[:, d:2 * d], mod_ref[:, 0:d]).astype(BF16)
    z_ref[...] = jnp.dot(h, w_ref[:, 0:inner], preferred_element_type=F32)
    xbc_ref[...] = jnp.dot(h, w_ref[:, inner:inner + cdim], preferred_element_type=F32)
    dt_ref[...] = jnp.dot(h, w_ref[:, inner + cdim:inner + cdim + LANES], preferred_element_type=F32)


def _row_specs(tm, tpb, d, mod_rows):
    x_spec = pl.BlockSpec((tm, d), lambda m: (m, 0))
    mod_spec = pl.BlockSpec((None, mod_rows, 6 * d), lambda m: (m // tpb, 0, 0))
    return x_spec, mod_spec


def _const_spec(shape):
    return pl.BlockSpec(shape, lambda m: (0,) * len(shape))


def _inproj_ssm(x, mod3, g, w_bf, tm, tpb, inner, cdim):
    m, d = x.shape
    x_spec, mod_spec = _row_specs(tm, tpb, d, mod3.shape[1])
    row = lambda n: pl.BlockSpec((tm, n), lambda i: (i, 0))
    return pl.pallas_call(
        functools.partial(_inproj_ssm_body, inner, cdim), grid=(m // tm,),
        in_specs=[x_spec, mod_spec, _const_spec((1, d)), _const_spec(w_bf.shape)],
        out_specs=[row(inner), row(cdim), row(LANES)],
        out_shape=[jax.ShapeDtypeStruct((m, inner), F32), jax.ShapeDtypeStruct((m, cdim), F32),
                   jax.ShapeDtypeStruct((m, LANES), F32)],
        compiler_params=_cparams(("parallel",), 52), name="inproj_ssm")(x, mod3, g, w_bf)


def _inproj_attn_body(aw, kvw, iw, x_ref, mod_ref, g_ref, w_ref, cos_ref, slo_ref, shi_ref,
                      q_ref, kb_ref, kt_ref, vt_ref, vtb_ref, qi_ref, gs_ref, ga_ref, sm_ref, ki2_ref, kit_ref):
    d = x_ref.shape[1]
    h = _norm_mod(x_ref[...], g_ref[...], mod_ref[:, d:2 * d], mod_ref[:, 0:d]).astype(BF16)
    cos, slo, shi = cos_ref[...], slo_ref[...], shi_ref[...]

    def proj(a, b):
        return jnp.dot(h, w_ref[:, a:b], preferred_element_type=F32)

    o = 0
    q_ref[...] = (_rope_wide(proj(o, o + aw), cos, slo, shi) * Q_SCALE).astype(BF16)
    o += aw
    k = _rope_wide(proj(o, o + kvw), cos, slo, shi)
    kb_ref[...] = k.astype(BF16)
    kt_ref[...] = k.T
    o += kvw
    vt = proj(o, o + kvw).T
    vt_ref[...] = vt
    vtb_ref[...] = vt.astype(BF16)
    o += kvw
    qi_ref[...] = _rope_wide(proj(o, o + iw), cos, slo, shi).astype(BF16)
    o += iw
    gs_ref[...] = proj(o, o + d)
    o += d
    ga_ref[...] = proj(o, o + d)
    o += d
    s = proj(o, o + LANES)
    lane = lax.broadcasted_iota(I32, s.shape, 1)
    sm = jnp.where(lane < HEAD, _rope128(s, cos, slo, shi), s)
    sm_ref[...] = sm
    ki2_ref[...] = jnp.where(lane < HEAD, sm, pltpu.roll(sm, HEAD, 1)).astype(BF16)
    kit_ref[...] = sm.T[0:HEAD, :]


def _inproj_attn(x, mod3, g, w_bf, tabs, tm, tpb, aw, kvw, iw):
    m, d = x.shape
    x_spec, mod_spec = _row_specs(tm, tpb, d, mod3.shape[1])
    ntab = tabs[0].shape[0] // tm
    nb = m // (tm * tpb)
    tab_spec = pl.BlockSpec((tm, LANES), lambda i: (i % ntab, 0))
    row = lambda n, dt: (pl.BlockSpec((tm, n), lambda i: (i, 0)), jax.ShapeDtypeStruct((m, n), dt))
    tmin = lambda n, dt: (pl.BlockSpec((None, n, tm), lambda i: (i // tpb, 0, i % tpb)),
                          jax.ShapeDtypeStruct((nb, n, tm * tpb), dt))
    outs = [row(aw, BF16), row(kvw, BF16), tmin(kvw, F32), tmin(kvw, F32), tmin(kvw, BF16), row(iw, BF16),
            row(d, F32), row(d, F32), row(LANES, F32), row(LANES, BF16), tmin(HEAD, F32)]
    return pl.pallas_call(
        functools.partial(_inproj_attn_body, aw, kvw, iw), grid=(m // tm,),
        in_specs=[x_spec, mod_spec, _const_spec((1, d)), _const_spec(w_bf.shape),
                  tab_spec, tab_spec, tab_spec],
        out_specs=[spec for spec, _ in outs],
        out_shape=[shape for _, shape in outs],
        compiler_params=_cparams(("parallel",), 52), name="inproj_attn")(x, mod3, g, w_bf, *tabs)


def _ssd_body(inner, z_ref, xbc_ref, dtr_ref, cw_ref, cb_ref, dtb_ref, a_ref, dsk_ref, ng_ref,
              y_ref, st_ref, full_s, act_s, st_s, y_s):
    c = pl.program_id(1)
    Q, N = SSM_CHUNK, SSM_STATE
    cdim = xbc_ref.shape[1]
    heads_per_group = inner // HEAD // SSM_GROUPS
    gw = inner // SSM_GROUPS

    @pl.when(c == 0)
    def _():
        full_s[0:8, :] = jnp.zeros((8, cdim), F32)
        st_s[...] = jnp.zeros(st_s.shape, F32)

    full_s[8:8 + Q, :] = xbc_ref[...]
    for j in range(0, cdim, 512):
        acc = cb_ref[:, j:j + 512] + full_s[8:8 + Q, j:j + 512] * cw_ref[3:4, j:j + 512]
        for i in range(SSM_CONV - 1):
            acc = acc + full_s[5 + i:5 + i + Q, j:j + 512] * cw_ref[i:i + 1, j:j + 512]
        act_s[:, j:j + 512] = _silu(acc)
    full_s[0:8, :] = full_s[Q:Q + 8, :]

    dt = _softplus(dtr_ref[...] + dtb_ref[...])
    row = lax.broadcasted_iota(I32, (Q, Q), 0)
    col = lax.broadcasted_iota(I32, (Q, Q), 1)
    tri = row >= col
    acs = jnp.dot(tri.astype(F32), dt * a_ref[...], precision=HIGHEST, preferred_element_type=F32)
    acs_t, dt_t = acs.T, dt.T
    last = acs[Q - 1:Q, :]
    wdt = jnp.exp(last - acs) * dt
    eacs = jnp.exp(acs)
    cdec = jnp.exp(last)
    low = lax.broadcasted_iota(I32, (Q, LANES), 1) < HEAD
    low1 = low[0:1, :]

    for g in range(SSM_GROUPS):
        bg = act_s[:, inner + g * N:inner + (g + 1) * N]
        cg = act_s[:, inner + SSM_GROUPS * N + g * N:inner + SSM_GROUPS * N + (g + 1) * N]
        cb = _bdot_nt(cg, bg)
        bg_t = bg.T.astype(BF16)
        for p in range(heads_per_group // 2):
            h0 = g * heads_per_group + 2 * p
            js = slice(h0 * HEAD, h0 * HEAD + LANES)
            xp = act_s[:, js]
            xp_bf = xp.astype(BF16)
            stp = st_s[:, js]
            stp_bf = stp.astype(BF16)
            ys = []
            for h in (h0, h0 + 1):
                seg = acs[:, h:h + 1] - acs_t[h:h + 1, :]
                decay = jnp.exp(jnp.where(tri, seg, -jnp.inf))
                m = (cb * decay) * dt_t[h:h + 1, :]
                ce = cg * eacs[:, h:h + 1]
                ys.append(_bdot(m, xp_bf) + _bdot(ce, stp_bf))
            y_s[:, js] = jnp.where(low, ys[0], ys[1])
            wcol = jnp.where(low, wdt[:, h0:h0 + 1], wdt[:, h0 + 1:h0 + 2])
            dst = jnp.dot(bg_t, (xp * wcol).astype(BF16), preferred_element_type=F32)
            cd = jnp.where(low1, cdec[:, h0:h0 + 1], cdec[:, h0 + 1:h0 + 2])
            st_s[:, js] = stp * cd + dst

    for g in range(SSM_GROUPS):
        gs = slice(g * gw, (g + 1) * gw)
        y = y_s[:, gs] + dsk_ref[:, gs] * act_s[:, gs]
        y = y * _silu(z_ref[:, gs])
        ms = jnp.mean(y * y, axis=-1, keepdims=True)
        y_ref[:, gs] = (y * lax.rsqrt(ms + NORM_EPS) * ng_ref[:, gs]).astype(BF16)

    @pl.when(c == pl.num_programs(1) - 1)
    def _():
        st_ref[...] = st_s[...].T


def _ssd_prompt(z, xbc, dtr, cw, cb, dtb, a, dsk, ng, nb):
    m, inner = z.shape
    cdim = xbc.shape[1]
    nc = m // nb // SSM_CHUNK
    row = lambda n: pl.BlockSpec((SSM_CHUNK, n), lambda b, c: (b * nc + c, 0))
    const = lambda shape: pl.BlockSpec(shape, lambda b, c: (0,) * len(shape))
    return pl.pallas_call(
        functools.partial(_ssd_body, inner), grid=(nb, nc),
        in_specs=[row(inner), row(cdim), row(LANES), const(cw.shape), const(cb.shape),
                  const(dtb.shape), const(a.shape), const(dsk.shape), const(ng.shape)],
        out_specs=[row(inner), pl.BlockSpec((None, inner, SSM_STATE), lambda b, c: (b, 0, 0))],
        out_shape=[jax.ShapeDtypeStruct((m, inner), BF16),
                   jax.ShapeDtypeStruct((nb, inner, SSM_STATE), F32)],
        scratch_shapes=[pltpu.VMEM((SSM_CHUNK + 8, cdim), F32), pltpu.VMEM((SSM_CHUNK, cdim), F32),
                        pltpu.VMEM((SSM_STATE, inner), F32), pltpu.VMEM((SSM_CHUNK, inner), F32)],
        compiler_params=_cparams(("parallel", "arbitrary"), 40), name="ssd_prompt",
    )(z, xbc, dtr, cw, cb, dtb, a, dsk, ng)


def _ssd_step_body(z_ref, xs_ref, bc_ref, cxs_ref, cbc_ref, dtr_ref, st_ref,
                   wxs_ref, wbc_ref, bxs_ref, bbc_ref, dtb_ref, a_ref, dsk_ref, ng_ref,
                   y_ref, sto_ref):
    G = SSM_GROUPS
    last = SSM_CONV - 1
    xs = bxs_ref[...] + xs_ref[...] * wxs_ref[last]
    bc = bbc_ref[...] + bc_ref[...] * wbc_ref[last]
    for i in range(last):
        xs = xs + cxs_ref[i] * wxs_ref[i]
        bc = bc + cbc_ref[i] * wbc_ref[i]
    xs, bc = _silu(xs), _silu(bc)
    dt = _softplus(dtr_ref[...] + dtb_ref[...])
    dec = jnp.exp(dt * a_ref[...])
    xdt = xs * dt
    npair = xs.shape[0]
    pairs_per_group = npair // G
    r = lax.broadcasted_iota(I32, (LANES, LANES), 0)
    cidx = lax.broadcasted_iota(I32, (LANES, LANES), 1)
    eye = (r == cidx).astype(F32)
    nt = (((1,), (1,)), ((), ()))
    dec_t = lax.dot_general(eye, dec, nt, precision=HIGHEST, preferred_element_type=F32)
    xdt_t = lax.dot_general(eye, xdt, nt, precision=HIGHEST, preferred_element_type=F32)
    rows = lax.broadcasted_iota(I32, (npair, 1), 0)
    cbv = jnp.sum(bc[0:G, :] * bc[G:2 * G, :], axis=-1, keepdims=True)
    cbx = jnp.zeros((npair, 1), F32)
    for g in range(G):
        cbx = cbx + jnp.where(rows // pairs_per_group == g, cbv[g:g + 1, :], 0.0)
    c_bf = bc.astype(BF16)
    yoff = jnp.zeros(xs.shape, F32)
    for j in range(npair):
        g = j // pairs_per_group
        s = st_ref[j * LANES:(j + 1) * LANES, :]
        sto_ref[j * LANES:(j + 1) * LANES, :] = s * dec_t[:, j:j + 1] + xdt_t[:, j:j + 1] * bc[g:g + 1, :]
        rj = _bdot_nt(c_bf, s)
        yoff = yoff + jnp.where(rows == j, rj[G + g:G + g + 1, :], 0.0)
    y = yoff * dec + cbx * dt * xs + dsk_ref[...] * xs
    y = y * _silu(z_ref[...])
    ssq = jnp.sum(y * y, axis=-1, keepdims=True)
    msx = jnp.zeros((npair, 1), F32)
    for g in range(G):
        ing = rows // pairs_per_group == g
        tot = jnp.sum(jnp.where(ing, ssq, 0.0), axis=0, keepdims=True)
        msx = msx + jnp.where(ing, tot, 0.0)
    msx = msx / (pairs_per_group * LANES)
    y_ref[...] = y * lax.rsqrt(msx + NORM_EPS) * ng_ref[...]


def _ssd_sample(z, xs, bc, cxs, cbc, dtr, st, wxs, wbc, bxs, bbc, dtb, a, dsk, ng):
    db, npair, _ = z.shape
    per_b = lambda shape: pl.BlockSpec((None,) + shape, lambda b: (b,) + (0,) * len(shape))
    const = lambda arr: pl.BlockSpec(arr.shape, lambda b: (0,) * arr.ndim)
    return pl.pallas_call(
        _ssd_step_body, grid=(db,),
        in_specs=[per_b(z.shape[1:]), per_b(xs.shape[1:]), per_b(bc.shape[1:]), per_b(cxs.shape[1:]),
                  per_b(cbc.shape[1:]), per_b(dtr.shape[1:]), per_b(st.shape[1:]),
                  const(wxs), const(wbc), const(bxs), const(bbc), const(dtb), const(a), const(dsk), const(ng)],
        out_specs=[per_b(z.shape[1:]), per_b(st.shape[1:])],
        out_shape=[jax.ShapeDtypeStruct(z.shape, F32), jax.ShapeDtypeStruct(st.shape, F32)],
        compiler_params=_cparams(("parallel",), 32), name="ssd_sample",
    )(z, xs, bc, cxs, cbc, dtr, st, wxs, wbc, bxs, bbc, dtb, a, dsk, ng)


def _row_fold(x, h, op=jnp.add):
    parts = [x[j:j + h, :] for j in range(0, x.shape[0], h)]
    while len(parts) > 1:
        parts = [op(a, b) for a, b in zip(parts[0::2], parts[1::2])] + (parts[-1:] if len(parts) % 2 else [])
    return parts[0]


def _topk_bias(score_s, img_s, bias_s, nblk, kb, topk, qpos, unroll=1):
    nq = score_s.shape[1]
    blk = lambda s: pl.ds(pl.multiple_of(s * kb, kb), kb)
    one16, zero16 = jnp.int16(1), jnp.int16(0)
    work_s = bias_s

    def over_blocks(body, init):
        def trip(s2, carry):
            for k in range(unroll):
                carry = body(s2 * unroll + k, carry)
            return carry
        return lax.fori_loop(0, nblk // unroll, trip, init)

    def count(cmp):
        def body(s, acc):
            return acc + _row_fold(jnp.where(cmp(img_s[blk(s), :]), one16, zero16), 16)
        acc = over_blocks(body, jnp.zeros((16, nq), I16))
        return jnp.sum(acc.astype(F32), axis=0, keepdims=True)

    def search(nbits, value_of):
        def step(t, u):
            code = u | lax.shift_left(jnp.int32(1), jnp.asarray(nbits - 1 - t, I32))
            c = value_of(code).astype(BF16)
            return jnp.where(count(lambda a: a >= c) >= topk, code, u)
        return lax.fori_loop(0, nbits, step, jnp.zeros((1, nq), I32))

    def bf16_value(code):
        pattern = jnp.where(code >= 32768, code - 32768, 65535 - code)
        return pltpu.bitcast(pattern << 16, jnp.float32).astype(F32)

    def set_image(fn):
        def body(s, carry):
            img_s[blk(s), :] = fn(s).astype(BF16)
            return carry
        over_blocks(body, 0)

    t1 = bf16_value(search(16, bf16_value))

    e1 = jnp.clip((pltpu.bitcast(t1.astype(jnp.float32), I32) >> 23) & 0xFF, 25, 254)
    unit = pltpu.bitcast((e1 - 24) << 23, jnp.float32).astype(F32)
    inv_unit = pltpu.bitcast((278 - e1) << 23, jnp.float32).astype(F32)
    B2, B1 = 65536.0, 256.0

    def digit2(s):
        y = jnp.clip((score_s[blk(s), :] - t1) * inv_unit, -4 * B2, 4 * B2)
        work_s[blk(s), :] = y
        return jnp.floor(y * (1.0 / B2))
    set_image(digit2)
    t2 = (search(2, lambda code: (code - 1).astype(F32)) - 1).astype(F32)
    set_image(lambda s: jnp.floor(jnp.clip(work_s[blk(s), :] - t2 * B2, -B1, B2) * (1.0 / B1)))
    t3 = search(8, lambda code: code.astype(F32)).astype(F32)
    set_image(lambda s: jnp.floor(jnp.clip(work_s[blk(s), :] - (t2 * B2 + t3 * B1), -1.0, B1)))
    t4 = search(8, lambda code: code.astype(F32)).astype(F32)
    v0 = t1 + (t2 * B2 + t3 * B1 + t4) * unit

    def smallest(keep):
        def body(s, acc):
            x = score_s[blk(s), :]
            return jnp.minimum(acc, _row_fold(jnp.where(keep(x), x, jnp.inf), 8, jnp.minimum))
        acc = over_blocks(body, jnp.full((8, nq), jnp.inf, F32))
        return jnp.min(acc, axis=0, keepdims=True)

    def count_above(v):
        def body(s, acc):
            return acc + _row_fold(jnp.where(score_s[blk(s), :] > v, 1.0, 0.0), 8)
        return jnp.sum(over_blocks(body, jnp.zeros((8, nq), F32)), axis=0, keepdims=True)

    def refine(carry):
        v, above = carry
        v = jnp.where(above >= topk, smallest(lambda x: x > v), v)
        return v, count_above(v)

    v = smallest(lambda x: x >= v0)
    v, above = lax.while_loop(lambda c: jnp.max(c[1]) >= topk, refine, (v, count_above(v)))
    need = topk - above

    r_i = lax.broadcasted_iota(I32, (LANES, LANES), 0)
    c_i = lax.broadcasted_iota(I32, (LANES, LANES), 1)
    lower = jnp.where(r_i >= c_i, 1.0, 0.0).astype(BF16)
    sub_iota = lax.broadcasted_iota(I32, (LANES, 1), 0)

    def bias_body(s, carry):
        for j in range(0, kb, LANES):
            off = pl.multiple_of(s * kb + j, LANES)
            x = score_s[pl.ds(off, LANES), :]
            eq = x == v
            eqf = jnp.where(eq, 1.0, 0.0)
            incl = jnp.dot(lower, eqf.astype(BF16), preferred_element_type=F32)
            tie = jnp.where(carry + incl - eqf < need, 0.0, -jnp.inf)
            b = jnp.where(x > v, 0.0, jnp.where(eq, tie, -jnp.inf))
            bias_s[pl.ds(off, LANES), :] = jnp.where((off + sub_iota) <= qpos, b, -jnp.inf)
            carry = carry + incl[LANES - 1:LANES, :]
        return carry

    over_blocks(bias_body, jnp.zeros((1, nq), F32))


def _dsa_t_body(topk, idx_scale, q_ref, qi_ref, sm_ref, ki2_ref, k_ref, vt_ref, o_ref,
                score_s, img_s, bias_s, qs_s, lga_s, lgb_s, m_s, acc_s):
    i = pl.program_id(1)
    QB, KB = Q_BLOCK, KEY_BLOCK
    nkb = (i * QB + QB + KB - 1) // KB
    qpos = i * QB + lax.broadcasted_iota(I32, (1, QB), 1)
    low = lax.broadcasted_iota(I32, (QB, LANES), 1) < HEAD
    zero_bf = jnp.zeros((QB, LANES), BF16)
    blk = lambda s: pl.ds(pl.multiple_of(s * KB, KB), KB)

    qi = qi_ref[...]
    sm_t = sm_ref[...].T
    qh, wh = [], []
    for h in range(IDX_HEADS):
        chunk = qi[:, (h // 2) * LANES:(h // 2 + 1) * LANES]
        qh.append(jnp.where(low if h % 2 == 0 else ~low, chunk, zero_bf))
        wh.append(sm_t[HEAD + h:HEAD + h + 1, :] * idx_scale)

    q_stack = jnp.concatenate(qh, axis=0)

    def score_body(s, carry):
        s_in = jnp.minimum(s, ki2_ref.shape[0] // KB - 1)
        sc = _bdot_nt(ki2_ref[blk(s_in), :], q_stack)
        acc = jnp.zeros((KB, QB), F32)
        for h in range(IDX_HEADS):
            acc = acc + wh[h] * jnp.maximum(sc[:, h * QB:(h + 1) * QB], 0.0)
        kpos = s * KB + lax.broadcasted_iota(I32, (KB, 1), 0)
        sc = jnp.where(kpos <= qpos, jnp.maximum(acc, MASKED_SCORE), MASKED_SCORE)
        score_s[blk(s), :] = sc
        img_s[blk(s), :] = sc.astype(BF16)
        return carry

    nkb_even = 2 * ((nkb + 1) // 2)
    lax.fori_loop(0, nkb_even, score_body, 0)
    _topk_bias(score_s, img_s, bias_s, nkb_even, KB, topk, qpos, unroll=2)

    q = q_ref[...]
    nchunk = q.shape[1] // LANES
    per_kv_chunk = nchunk // (ATTN_KV_HEADS // 2)
    nstack = 2 * per_kv_chunk
    srows = nstack * QB
    for cj in range(nchunk):
        kvc, j = divmod(cj, per_kv_chunk)
        qc = q[:, cj * LANES:(cj + 1) * LANES]
        for half in range(2):
            r0 = (kvc * nstack + half * per_kv_chunk + j) * QB
            qs_s[r0:r0 + QB, :] = jnp.where(low if half == 0 else ~low, qc, zero_bf)

    ones_rows = jnp.ones((ONES_ROWS, KB), BF16)
    cols = lambda r: slice(r * QB, (r + 1) * QB)
    n_kvc = ATTN_KV_HEADS // 2
    nsteps = n_kvc * nkb

    def step_of(t):
        t = jnp.minimum(t, nsteps - 1)
        kvc = (t >= nkb).astype(I32)
        return kvc, t - kvc * nkb

    def logits_to(dst, t):
        kvc, s = step_of(t)
        kb = k_ref[blk(s), pl.ds(pl.multiple_of(kvc * LANES, LANES), LANES)]
        bias = bias_s[blk(s), :]
        lg = _bdot_nt(kb, qs_s[pl.ds(pl.multiple_of(kvc * srows, srows), srows), :])
        for r in range(nstack):
            dst[:, cols(r)] = lg[:, cols(r)] + bias

    def consume(src, t):
        kvc, s = step_of(t)
        vt = vt_ref[pl.ds(pl.multiple_of(kvc * LANES, LANES), LANES), blk(s)]
        ps, alphas = [], []
        for r in range(nstack):
            lg = src[:, cols(r)]
            m = m_s[kvc, :, cols(r)]
            mn = jnp.maximum(m, jnp.max(lg, axis=0, keepdims=True))
            m_s[kvc, :, cols(r)] = mn
            ps.append(jnp.exp2(lg - mn).astype(BF16))
            alphas.append(jnp.exp2(m - mn))
        for half in range(2):
            hs = slice(half * per_kv_chunk, (half + 1) * per_kv_chunk)
            hc = slice(half * per_kv_chunk * QB, (half + 1) * per_kv_chunk * QB)
            v_aug = jnp.concatenate([vt[half * HEAD:(half + 1) * HEAD, :], ones_rows], axis=0)
            pv = jnp.dot(v_aug, jnp.concatenate(ps[hs], axis=1), preferred_element_type=F32)
            acc_s[kvc, :, hc] = jnp.concatenate(alphas[hs], axis=1) * acc_s[kvc, :, hc] + pv

    m_s[...] = jnp.full(m_s.shape, NEG_BIG, F32)
    acc_s[...] = jnp.zeros(acc_s.shape, F32)
    logits_to(lga_s, 0)

    def run_pairs(t0, npairs):
        for p in range(npairs):
            logits_to(lgb_s, t0 + 2 * p + 1)
            consume(lga_s, t0 + 2 * p)
            logits_to(lga_s, t0 + 2 * p + 2)
            consume(lgb_s, t0 + 2 * p + 1)

    def quad_body(i4, carry):
        run_pairs(4 * i4, 2)
        return carry

    lax.fori_loop(0, nsteps // 4, quad_body, 0)

    @pl.when(nsteps % 4 == 2)
    def _():
        run_pairs(nsteps - 2, 1)

    for kvc in range(n_kvc):
        acc = acc_s[kvc]
        for j in range(per_kv_chunk):
            a = acc[:, cols(j)]
            b = acc[:, cols(per_kv_chunk + j)]
            chunk_t = jnp.concatenate([a[0:HEAD, :] / a[HEAD:HEAD + 1, :], b[0:HEAD, :] / b[HEAD:HEAD + 1, :]],
                                      axis=0)
            cj = kvc * per_kv_chunk + j
            o_ref[:, cj * LANES:(cj + 1) * LANES] = chunk_t.T.astype(BF16)


def _dsa_prompt_t(q, qi, sm, ki2, k, vt, nb, topk):
    m, aw = q.shape
    t = m // nb
    nq = t // Q_BLOCK
    tpad = -(-t // (2 * KEY_BLOCK)) * 2 * KEY_BLOCK
    srows = aw // HEAD // (ATTN_KV_HEADS // 2) * Q_BLOCK
    idx_scale = IDX_HEADS ** -0.5 * HEAD ** -0.5
    row = lambda n: pl.BlockSpec((Q_BLOCK, n), lambda b, i: (b * nq + i, 0))
    per_b = lambda n: pl.BlockSpec((t, n), lambda b, i: (b, 0))
    return pl.pallas_call(
        functools.partial(_dsa_t_body, topk, idx_scale), grid=(nb, nq),
        in_specs=[row(aw), row(qi.shape[1]), row(LANES), per_b(LANES), per_b(k.shape[1]),
                  pl.BlockSpec((None,) + vt.shape[1:], lambda b, i: (b, 0, 0))],
        out_specs=row(aw),
        out_shape=jax.ShapeDtypeStruct((m, aw), BF16),
        scratch_shapes=[pltpu.VMEM((tpad, Q_BLOCK), F32), pltpu.VMEM((tpad, Q_BLOCK), BF16),
                        pltpu.VMEM((tpad, Q_BLOCK), F32),
                        pltpu.VMEM((2 * srows, LANES), BF16),
                        pltpu.VMEM((KEY_BLOCK, srows), F32), pltpu.VMEM((KEY_BLOCK, srows), F32),
                        pltpu.VMEM((ATTN_KV_HEADS // 2, 1, srows), F32),
                        pltpu.VMEM((ATTN_KV_HEADS // 2, HEAD + ONES_ROWS, srows), F32)],
        compiler_params=_cparams(("parallel", "arbitrary"), 52), name="dsa_prompt",
    )(q, qi, sm, ki2, k, vt)


def _page_specs(block, npages):
    def make(u):
        return pl.BlockSpec((None,) + block, lambda b, pt: (pt[b, u],) + (0,) * len(block))
    return [make(u) for u in range(npages)]


def _idx_score_body(pt_ref, qi_ref, w_ref, *refs):
    pages, o_ref = refs[:-1], refs[-1]
    qi = qi_ref[...]
    w = w_ref[...]
    for u, page in enumerate(pages):
        s = jnp.maximum(_bdot(qi, page[...]), 0.0)
        o_ref[:, u * PAGE_SIZE:(u + 1) * PAGE_SIZE] = jnp.sum(w * s, axis=0, keepdims=True)


def _idx_scores_sample(page_table, qi8, w8, kidx_t):
    db, npages = page_table.shape
    grid_spec = pltpu.PrefetchScalarGridSpec(
        num_scalar_prefetch=1, grid=(db,),
        in_specs=[pl.BlockSpec((None,) + qi8.shape[1:], lambda b, pt: (b, 0, 0)),
                  pl.BlockSpec((None,) + w8.shape[1:], lambda b, pt: (b, 0, 0))]
                 + _page_specs(kidx_t.shape[1:], npages),
        out_specs=pl.BlockSpec((None, 1, npages * PAGE_SIZE), lambda b, pt: (b, 0, 0)))
    return pl.pallas_call(
        _idx_score_body, grid_spec=grid_spec,
        out_shape=jax.ShapeDtypeStruct((db, 1, npages * PAGE_SIZE), F32),
        compiler_params=_cparams(("parallel",), 32), name="idx_scores_sample",
    )(page_table, qi8, w8, *([kidx_t] * npages))


def _select_sample_body(topk, idx_scale, past, sc_ref, qi_ref, sm_ref, bias_ref, score_s, img_s, bias_s):
    rows = sc_ref.shape[0]
    sm = sm_ref[...]
    qi = qi_ref[...]
    ki = sm[:, 0:HEAD]
    new = jnp.zeros((rows, 1), F32)
    for h in range(IDX_HEADS):
        d = jnp.sum(qi[:, h * HEAD:(h + 1) * HEAD] * ki, axis=-1, keepdims=True)
        new = new + (sm[:, HEAD + h:HEAD + h + 1] * idx_scale) * jnp.maximum(d, 0.0)
    nblk = (past + LANES) // LANES
    lane = lax.broadcasted_iota(I32, (rows, LANES), 1)
    for j in range(nblk):
        js = slice(j * LANES, (j + 1) * LANES)
        sc = sc_ref[:, js] if j < nblk - 1 else jnp.where(lane == 0, new, MASKED_SCORE)
        sc = jnp.maximum(sc, MASKED_SCORE).T
        score_s[js, :] = sc
        img_s[js, :] = sc.astype(BF16)
    qpos = jnp.full((1, rows), past, I32)
    _topk_bias(score_s, img_s, bias_s, nblk, LANES, topk, qpos)
    for j in range(nblk):
        js = slice(j * LANES, (j + 1) * LANES)
        bias_ref[:, js] = bias_s[js, :].T


def _select_sample(scores, qi, sm, topk, idx_scale):
    db, past = scores.shape
    full = lambda a: pl.BlockSpec(a.shape, lambda i: (0,) * a.ndim)
    keys = past + LANES
    return pl.pallas_call(
        functools.partial(_select_sample_body, topk, idx_scale, past), grid=(1,),
        in_specs=[full(scores), full(qi), full(sm)],
        out_specs=pl.BlockSpec((db, keys), lambda i: (0, 0)),
        out_shape=jax.ShapeDtypeStruct((db, keys), F32),
        scratch_shapes=[pltpu.VMEM((keys, db), F32), pltpu.VMEM((keys, db), BF16), pltpu.VMEM((keys, db), F32)],
        compiler_params=_cparams(("arbitrary",), 32), name="select_sample")(scores, qi, sm)


def _attend_sample_body(npages, pt_ref, q_ref, bias_ref, knew_ref, vnew_ref, *refs):
    kpages, vpages, o_ref = refs[:npages], refs[npages:2 * npages], refs[2 * npages]
    G = ATTN_KV_HEADS
    q = q_ref[...]
    q_bf = q.astype(BF16)
    nh = q.shape[0]
    past = npages * PAGE_SIZE
    group = lax.broadcasted_iota(I32, (nh, 1), 0) // (nh // G)

    def by_group(parts):
        out = parts[G - 1]
        for g in range(G - 2, -1, -1):
            out = jnp.where(group == g, parts[g], out)
        return out

    lg = jnp.concatenate(
        [by_group([_bdot(q_bf, kpages[u][g]) for g in range(G)]) for u in range(npages)], axis=1)
    lg = lg + bias_ref[:, 0:past]
    lg_new = by_group([jnp.sum(q * knew_ref[g:g + 1, :], axis=-1, keepdims=True) for g in range(G)])
    lg_new = lg_new + bias_ref[:, past:past + 1]
    m = jnp.maximum(jnp.max(lg, axis=-1, keepdims=True), lg_new)
    p = jnp.exp2(lg - m)
    p_new = jnp.exp2(lg_new - m)
    denom = jnp.sum(p, axis=-1, keepdims=True) + p_new
    p_bf = p.astype(BF16)
    accs = [p_new * vnew_ref[g:g + 1, :] for g in range(G)]
    for u in range(npages):
        pu = p_bf[:, u * PAGE_SIZE:(u + 1) * PAGE_SIZE]
        for g in range(G):
            accs[g] = accs[g] + _bdot_nt(pu, vpages[u][g])
    o_ref[...] = by_group(accs) / denom


def _attend_sample(page_table, q, bias, knew, vnew, k_t, v_t):
    db, npages = page_table.shape
    per_b = lambda a: pl.BlockSpec((None,) + a.shape[1:], lambda b, pt: (b,) + (0,) * (a.ndim - 1))
    grid_spec = pltpu.PrefetchScalarGridSpec(
        num_scalar_prefetch=1, grid=(db,),
        in_specs=[per_b(q), per_b(bias), per_b(knew), per_b(vnew)]
                 + _page_specs(k_t.shape[1:], npages) + _page_specs(v_t.shape[1:], npages),
        out_specs=per_b(q))
    return pl.pallas_call(
        functools.partial(_attend_sample_body, npages), grid_spec=grid_spec,
        out_shape=jax.ShapeDtypeStruct(q.shape, F32),
        compiler_params=_cparams(("parallel",), 48), name="attend_sample",
    )(page_table, q, bias, knew, vnew, *([k_t] * npages), *([v_t] * npages))


def _merge_body(x_ref, mod_ref, ys_ref, ya_ref, gs_ref, ga_ref, wps_ref, wpa_ref, wo_ref, o_ref):
    d = x_ref.shape[1]
    merged = (jax.nn.sigmoid(gs_ref[...]) * _bdot(ys_ref[...], wps_ref[...])
              + jax.nn.sigmoid(ga_ref[...]) * _bdot(ya_ref[...], wpa_ref[...]))
    o_ref[...] = x_ref[...] + mod_ref[:, 2 * d:3 * d] * _bdot(merged, wo_ref[...])


def _merge(x, mod3, ys, ya, gs, ga, wps, wpa, wo, tm, tpb):
    m, d = x.shape
    x_spec, mod_spec = _row_specs(tm, tpb, d, mod3.shape[1])
    row = lambda n: pl.BlockSpec((tm, n), lambda i: (i, 0))
    return pl.pallas_call(
        _merge_body, grid=(m // tm,),
        in_specs=[x_spec, mod_spec, row(ys.shape[1]), row(ya.shape[1]), row(d), row(d),
                  _const_spec(wps.shape), _const_spec(wpa.shape), _const_spec(wo.shape)],
        out_specs=row(d), out_shape=jax.ShapeDtypeStruct((m, d), F32),
        compiler_params=_cparams(("parallel",), 48), name="merge")(x, mod3, ys, ya, gs, ga, wps, wpa, wo)


def _ffn_body(last_layer, x_ref, mod_ref, g_ref, fg_ref, wg_ref, wu_ref, wo_ref, o_ref):
    d = x_ref.shape[1]
    x = x_ref[...]
    h = _norm_mod(x, g_ref[...], mod_ref[:, 4 * d:5 * d], mod_ref[:, 3 * d:4 * d]).astype(BF16)
    gate = jnp.dot(h, wg_ref[...], preferred_element_type=F32)
    up = jnp.dot(h, wu_ref[...], preferred_element_type=F32)
    x2 = x + mod_ref[:, 5 * d:6 * d] * _bdot(_silu(gate) * up, wo_ref[...])
    if last_layer:
        ms = jnp.mean(x2 * x2, axis=-1, keepdims=True)
        x2 = x2 * lax.rsqrt(ms + NORM_EPS) * fg_ref[...]
    o_ref[...] = x2


def _ffn(x, mod3, g, fg, wg, wu, wo, tm, tpb, last_layer):
    m, d = x.shape
    x_spec, mod_spec = _row_specs(tm, tpb, d, mod3.shape[1])
    return pl.pallas_call(
        functools.partial(_ffn_body, last_layer), grid=(m // tm,),
        in_specs=[x_spec, mod_spec, _const_spec((1, d)), _const_spec((1, d)),
                  _const_spec(wg.shape), _const_spec(wu.shape), _const_spec(wo.shape)],
        out_specs=pl.BlockSpec((tm, d), lambda i: (i, 0)), out_shape=jax.ShapeDtypeStruct((m, d), F32),
        compiler_params=_cparams(("parallel",), 56), name="ffn")(x, mod3, g, fg, wg, wu, wo)


def _rope_tables(pos):
    half = HEAD // 2
    inv = ROPE_THETA ** (-jnp.arange(half, dtype=F32) / half)
    ang = pos.astype(F32)[:, None] * inv[None, :]
    cos = jnp.tile(jnp.cos(ang), (1, LANES // half))
    sin = jnp.tile(jnp.sin(ang), (1, LANES // half))
    first = (jnp.arange(LANES) % HEAD) < half
    return cos, jnp.where(first, -sin, 0.0), jnp.where(first, 0.0, sin)


def _q_head_order(n_heads):
    rep = n_heads // ATTN_KV_HEADS
    order = []
    for c in range(ATTN_KV_HEADS // 2):
        for j in range(rep):
            order += [2 * c * rep + j, (2 * c + 1) * rep + j]
    return np.asarray(order)


def kernel(x_prompt, x_sample, cache_k, cache_v, cache_kidx, state_conv, state_ssm, page_table, c_prompt, c_sample, w_ada, b_ada, norm1_g, w_in, conv_w, conv_b, dt_bias, a_log, d_skip, ssm_norm_g, w_proj_ssm, w_proj_attn, w_out, norm2_g, w_ffn_in, w_ffn_out, final_g):
    nb, t, d = x_prompt.shape
    db, ds, _ = x_sample.shape
    depth = w_in.shape[0]
    assert ds == 1 and t % Q_BLOCK == 0 and t % SSM_CHUNK == 0
    n_heads_ssm = dt_bias.shape[1]
    inner = n_heads_ssm * HEAD
    gn = SSM_GROUPS * SSM_STATE
    cdim = inner + 2 * gn
    kvw = ATTN_KV_HEADS * HEAD
    aw = w_proj_attn.shape[1]
    n_heads = aw // HEAD
    iw = IDX_HEADS * HEAD
    ffn_hidden = w_ffn_out.shape[1]
    npages = page_table.shape[1]
    past = npages * PAGE_SIZE
    topk_p = min(TOPK_MAX, t // 4)
    topk_s = min(TOPK_MAX, (past + ds) // 4)
    assert past + ds >= topk_s
    idx_scale = IDX_HEADS ** -0.5 * HEAD ** -0.5
    tm_p = 256 if t % 256 == 0 else 128
    tpb_p = t // tm_p

    splits = np.cumsum([inner, inner, gn, gn, n_heads_ssm, aw, kvw, kvw, iw, HEAD, IDX_HEADS, d])
    order = _q_head_order(n_heads)
    inv_order = np.argsort(order)

    cos_p, slo_p, shi_p = _rope_tables(jnp.arange(t, dtype=I32))
    tabs_p = (cos_p, slo_p, shi_p)
    tabs_s = tuple(jnp.broadcast_to(a, (db, LANES)) for a in _rope_tables(past + jnp.arange(ds, dtype=I32)))

    rows_c = nb + db
    c_all = jnp.concatenate([c_prompt, c_sample, jnp.zeros((-rows_c % 8, d), F32)], axis=0)

    yp = x_prompt.reshape(nb * t, d)
    ys = x_sample.reshape(db, d)
    outs_p, outs_s = [], []
    for l in range(depth):
        (wz, wxs, wbm, wcm, wdt, wq, wk, wv, wqi, wki, wwi, wgs, wga) = jnp.split(w_in[l], splits, axis=1)
        w_ssm = jnp.concatenate([wz, wxs, wbm, wcm, wdt, jnp.zeros((d, LANES - n_heads_ssm), F32)],
                                axis=1).astype(BF16)
        wq_perm = wq.reshape(d, n_heads, HEAD)[:, order].reshape(d, aw)
        w_attn = jnp.concatenate([wq_perm, wk, wv, wqi, wgs, wga, wki, wwi,
                                  jnp.zeros((d, LANES - HEAD - IDX_HEADS), F32)], axis=1).astype(BF16)
        wps = w_proj_ssm[l].astype(BF16)
        wpa = w_proj_attn[l].reshape(n_heads, HEAD, d)[order].reshape(aw, d).astype(BF16)
        wo = w_out[l].astype(BF16)
        wg = w_ffn_in[l][:, :ffn_hidden].astype(BF16)
        wu = w_ffn_in[l][:, ffn_hidden:].astype(BF16)
        wfo = w_ffn_out[l].astype(BF16)
        g1 = norm1_g[l][None, :]
        g2 = norm2_g[l][None, :]
        a_neg = -jnp.exp(a_log[l])
        pad_h = LANES - n_heads_ssm
        dtb_row = jnp.pad(dt_bias[l], (0, pad_h))[None, :]
        a_row = jnp.pad(a_neg, (0, pad_h))[None, :]
        dsk_row = jnp.repeat(d_skip[l], HEAD)[None, :]
        ng_row = ssm_norm_g[l][None, :]
        cw = conv_w[l]
        cb = conv_b[l][None, :]

        mod = _ada(c_all, w_ada[l].astype(BF16), b_ada[l][None, :])
        mod_p = mod[:nb][:, None, :]
        mod_s = mod[nb:nb + db][None]

        z, xbc, dtr = _inproj_ssm(yp, mod_p, g1, w_ssm, tm_p, tpb_p, inner, cdim)
        (q_bf, k_bf, kt, vt, vt_bf, qi_bf, gs, ga, sm, ki2, kit) = _inproj_attn(
            yp, mod_p, g1, w_attn, tabs_p, tm_p, tpb_p, aw, kvw, iw)
        y_ssm, st = _ssd_prompt(z, xbc, dtr, cw, cb, dtb_row, a_row, dsk_row, ng_row, nb)
        y_attn = _dsa_prompt_t(q_bf, qi_bf, sm, ki2, k_bf, vt_bf, nb, topk_p)
        x1 = _merge(yp, mod_p, y_ssm, y_attn, gs, ga, wps, wpa, wo, tm_p, tpb_p)
        yp_next = _ffn(x1, mod_p, g2, final_g[None, :], wg, wu, wfo, tm_p, tpb_p, l == depth - 1)
        heads_last = lambda a: jnp.transpose(a.reshape(a.shape[0], ATTN_KV_HEADS, HEAD, a.shape[2]), (0, 3, 1, 2))
        outs_p.append((heads_last(kt), heads_last(vt), jnp.transpose(kit, (0, 2, 1)),
                       xbc.reshape(nb, t, cdim)[:, t - (SSM_CONV - 1):],
                       st.reshape(nb, n_heads_ssm, HEAD, SSM_STATE)))

        z_s, xbc_s, dtr_s = _inproj_ssm(ys, mod_s, g1, w_ssm, db, 1, inner, cdim)
        (q_s, _, kt_s, vt_s, _, qi_s, gs_s, ga_s, sm_s, _, kit_s) = _inproj_attn(
            ys, mod_s, g1, w_attn, tabs_s, db, 1, aw, kvw, iw)
        k_s, v_s = heads_last(kt_s)[0], heads_last(vt_s)[0]
        npair = inner // LANES
        nbc = 2 * gn // LANES
        sc = state_conv[l]
        y_ssm_s, st_s = _ssd_sample(
            z_s.reshape(db, npair, LANES), xbc_s[:, :inner].reshape(db, npair, LANES),
            xbc_s[:, inner:].reshape(db, nbc, LANES),
            sc[:, :, :inner].reshape(db, SSM_CONV - 1, npair, LANES),
            sc[:, :, inner:].reshape(db, SSM_CONV - 1, nbc, LANES),
            jnp.repeat(dtr_s[:, :n_heads_ssm], HEAD, axis=1).reshape(db, npair, LANES),
            state_ssm[l].reshape(db, inner, SSM_STATE),
            cw[:, :inner].reshape(SSM_CONV, npair, LANES), cw[:, inner:].reshape(SSM_CONV, nbc, LANES),
            cb[:, :inner].reshape(npair, LANES), cb[:, inner:].reshape(nbc, LANES),
            jnp.repeat(dt_bias[l], HEAD).reshape(npair, LANES), jnp.repeat(a_neg, HEAD).reshape(npair, LANES),
            dsk_row.reshape(npair, LANES), ng_row.reshape(npair, LANES))

        qi_f = qi_s.astype(F32)
        qi8 = jnp.pad(qi_f.reshape(db, IDX_HEADS, HEAD), ((0, 0), (0, 8 - IDX_HEADS), (0, 0)))
        w8 = jnp.broadcast_to(jnp.pad(sm_s[:, HEAD:HEAD + IDX_HEADS] * idx_scale,
                                      ((0, 0), (0, 8 - IDX_HEADS)))[:, :, None], (db, 8, LANES))
        scores = _idx_scores_sample(page_table, qi8, w8, jnp.transpose(cache_kidx[l], (0, 2, 1)))
        bias = _select_sample(scores.reshape(db, past), qi_f, sm_s, topk_s, idx_scale)
        q_orig = q_s.astype(F32).reshape(db, n_heads, HEAD)[:, inv_order]
        att = _attend_sample(page_table, q_orig, bias[:, None, :],
                             k_s, v_s,
                             jnp.transpose(cache_k[l], (0, 2, 3, 1)), jnp.transpose(cache_v[l], (0, 2, 3, 1)))
        y_attn_s = att[:, order].reshape(db, aw)
        x1_s = _merge(ys, mod_s, y_ssm_s.reshape(db, inner), y_attn_s, gs_s, ga_s, wps, wpa, wo, db, 1)
        ys_next = _ffn(x1_s, mod_s, g2, final_g[None, :], wg, wu, wfo, db, 1, l == depth - 1)
        outs_s.append((k_s[:, None], v_s[:, None], jnp.transpose(kit_s, (2, 0, 1)),
                       jnp.concatenate([sc[:, 1:], xbc_s[:, None, :]], axis=1),
                       st_s.reshape(db, n_heads_ssm, HEAD, SSM_STATE)))
        yp, ys = yp_next, ys_next

    stack = lambda outs, i: jnp.stack([o[i] for o in outs], axis=0)
    return (yp.reshape(nb, t, d), ys.reshape(db, ds, d),
            stack(outs_p, 0), stack(outs_p, 1), stack(outs_p, 2), stack(outs_p, 3), stack(outs_p, 4),
            stack(outs_s, 0), stack(outs_s, 1), stack(outs_s, 2), stack(outs_s, 3), stack(outs_s, 4))
```

```python
import functools

import jax
import jax.numpy as jnp
import numpy as np
from jax import lax
from jax.experimental import pallas as pl
from jax.experimental.pallas import tpu as pltpu

F32, BF16, I32, I16 = jnp.float32, jnp.bfloat16, jnp.int32, jnp.int16
HIGHEST = lax.Precision.HIGHEST

LANES = 128
HEAD = 64
SSM_STATE = 128
SSM_GROUPS = 4
SSM_CONV = 4
SSM_CHUNK = 128
ATTN_KV_HEADS = 4
IDX_HEADS = 4
TOPK_MAX = 256
Q_BLOCK = 128
PAGE_SIZE = 128
ROPE_THETA = 10000.0
NORM_EPS = 1e-6
KEY_BLOCK = 512
ONES_ROWS = 16
Q_SCALE = HEAD ** -0.5 * 1.4426950408889634
MASKED_SCORE = -3.3895313892515355e38
NEG_BIG = -1e30


def _cparams(sem, vmem_mb):
    return pltpu.CompilerParams(dimension_semantics=sem, vmem_limit_bytes=vmem_mb << 20)


def _bdot(a, b):
    return jnp.dot(a.astype(BF16), b.astype(BF16), preferred_element_type=F32)


def _bdot_nt(a, b):
    return lax.dot_general(a.astype(BF16), b.astype(BF16), (((1,), (1,)), ((), ())),
                           preferred_element_type=F32)


def _silu(x):
    h = 0.5 * x
    return h + h * jnp.tanh(h)


def _softplus(x):
    return jnp.maximum(x, 0.0) + jnp.log(1.0 + jnp.exp(-jnp.abs(x)))


def _norm_mod(x, g, scale, shift):
    ms = jnp.mean(x * x, axis=-1, keepdims=True)
    return (x * lax.rsqrt(ms + NORM_EPS) * g) * (1.0 + scale) + shift


def _rope128(x, cos, sin_lo, sin_hi):
    return x * cos + pltpu.roll(x, 96, 1) * sin_lo + pltpu.roll(x, 32, 1) * sin_hi


def _rope_wide(x, cos, sin_lo, sin_hi):
    parts = [_rope128(x[:, j:j + LANES], cos, sin_lo, sin_hi) for j in range(0, x.shape[1], LANES)]
    return parts[0] if len(parts) == 1 else jnp.concatenate(parts, axis=1)


def _ada_body(c_ref, w_ref, b_ref, o_ref):
    o_ref[...] = _bdot(_silu(c_ref[...]), w_ref[...]) + b_ref[...]


def _ada(c_all, w_bf, b):
    mp, d = c_all.shape
    n = w_bf.shape[1]
    tn = n // 4
    return pl.pallas_call(
        _ada_body, grid=(n // tn,),
        in_specs=[pl.BlockSpec((mp, d), lambda j: (0, 0)),
                  pl.BlockSpec((d, tn), lambda j: (0, j)),
                  pl.BlockSpec((1, tn), lambda j: (0, j))],
        out_specs=pl.BlockSpec((mp, tn), lambda j: (0, j)),
        out_shape=jax.ShapeDtypeStruct((mp, n), F32),
        compiler_params=_cparams(("arbitrary",), 32), name="ada")(c_all, w_bf, b)


def _inproj_ssm_body(inner, cdim, x_ref, mod_ref, g_ref, w_ref, z_ref, xbc_ref, dt_ref):
    d = x_ref.shape[1]
    h = _norm_mod(x_ref[...], g_ref[...], mod_ref[:, d:2 * d], mod_ref[:, 0:d]).astype(BF16)
    z_ref[...] = jnp.dot(h, w_ref[:, 0:inner], preferred_element_type=F32)
    xbc_ref[...] = jnp.dot(h, w_ref[:, inner:inner + cdim], preferred_element_type=F32)
    dt_ref[...] = jnp.dot(h, w_ref[:, inner + cdim:inner + cdim + LANES], preferred_element_type=F32)


def _row_specs(tm, tpb, d, mod_rows):
    x_spec = pl.BlockSpec((tm, d), lambda m: (m, 0))
    mod_spec = pl.BlockSpec((None, mod_rows, 6 * d), lambda m: (m // tpb, 0, 0))
    return x_spec, mod_spec


def _const_spec(shape):
    return pl.BlockSpec(shape, lambda m: (0,) * len(shape))


def _inproj_ssm(x, mod3, g, w_bf, tm, tpb, inner, cdim):
    m, d = x.shape
    x_spec, mod_spec = _row_specs(tm, tpb, d, mod3.shape[1])
    row = lambda n: pl.BlockSpec((tm, n), lambda i: (i, 0))
    return pl.pallas_call(
        functools.partial(_inproj_ssm_body, inner, cdim), grid=(m // tm,),
        in_specs=[x_spec, mod_spec, _const_spec((1, d)), _const_spec(w_bf.shape)],
        out_specs=[row(inner), row(cdim), row(LANES)],
        out_shape=[jax.ShapeDtypeStruct((m, inner), F32), jax.ShapeDtypeStruct((m, cdim), F32),
                   jax.ShapeDtypeStruct((m, LANES), F32)],
        compiler_params=_cparams(("parallel",), 52), name="inproj_ssm")(x, mod3, g, w_bf)


def _inproj_attn_body(aw, kvw, iw, x_ref, mod_ref, g_ref, w_ref, cos_ref, slo_ref, shi_ref,
                      q_ref, kb_ref, kt_ref, vt_ref, vtb_ref, qi_ref, gs_ref, ga_ref, sm_ref, ki2_ref, kit_ref):
    d = x_ref.shape[1]
    h = _norm_mod(x_ref[...], g_ref[...], mod_ref[:, d:2 * d], mod_ref[:, 0:d]).astype(BF16)
    cos, slo, shi = cos_ref[...], slo_ref[...], shi_ref[...]

    def proj(a, b):
        return jnp.dot(h, w_ref[:, a:b], preferred_element_type=F32)

    o = 0
    q_ref[...] = (_rope_wide(proj(o, o + aw), cos, slo, shi) * Q_SCALE).astype(BF16)
    o += aw
    k = _rope_wide(proj(o, o + kvw), cos, slo, shi)
    kb_ref[...] = k.astype(BF16)
    kt_ref[...] = k.T
    o += kvw
    vt = proj(o, o + kvw).T
    vt_ref[...] = vt
    vtb_ref[...] = vt.astype(BF16)
    o += kvw
    qi_ref[...] = _rope_wide(proj(o, o + iw), cos, slo, shi).astype(BF16)
    o += iw
    gs_ref[...] = proj(o, o + d)
    o += d
    ga_ref[...] = proj(o, o + d)
    o += d
    s = proj(o, o + LANES)
    lane = lax.broadcasted_iota(I32, s.shape, 1)
    sm = jnp.where(lane < HEAD, _rope128(s, cos, slo, shi), s)
    sm_ref[...] = sm
    ki2_ref[...] = jnp.where(lane < HEAD, sm, pltpu.roll(sm, HEAD, 1)).astype(BF16)
    kit_ref[...] = sm.T[0:HEAD, :]


def _inproj_attn(x, mod3, g, w_bf, tabs, tm, tpb, aw, kvw, iw):
    m, d = x.shape
    x_spec, mod_spec = _row_specs(tm, tpb, d, mod3.shape[1])
    ntab = tabs[0].shape[0] // tm
    nb = m // (tm * tpb)
    tab_spec = pl.BlockSpec((tm, LANES), lambda i: (i % ntab, 0))
    row = lambda n, dt: (pl.BlockSpec((tm, n), lambda i: (i, 0)), jax.ShapeDtypeStruct((m, n), dt))
    tmin = lambda n, dt: (pl.BlockSpec((None, n, tm), lambda i: (i // tpb, 0, i % tpb)),
                          jax.ShapeDtypeStruct((nb, n, tm * tpb), dt))
    outs = [row(aw, BF16), row(kvw, BF16), tmin(kvw, F32), tmin(kvw, F32), tmin(kvw, BF16), row(iw, BF16),
            row(d, F32), row(d, F32), row(LANES, F32), row(LANES, BF16), tmin(HEAD, F32)]
    return pl.pallas_call(
        functools.partial(_inproj_attn_body, aw, kvw, iw), grid=(m // tm,),
        in_specs=[x_spec, mod_spec, _const_spec((1, d)), _const_spec(w_bf.shape),
                  tab_spec, tab_spec, tab_spec],
        out_specs=[spec for spec, _ in outs],
        out_shape=[shape for _, shape in outs],
        compiler_params=_cparams(("parallel",), 52), name="inproj_attn")(x, mod3, g, w_bf, *tabs)


def _ssd_body(inner, z_ref, xbc_ref, dtr_ref, cw_ref, cb_ref, dtb_ref, a_ref, dsk_ref, ng_ref,
              y_ref, st_ref, full_s, act_s, st_s, y_s):
    c = pl.program_id(1)
    Q, N = SSM_CHUNK, SSM_STATE
    cdim = xbc_ref.shape[1]
    heads_per_group = inner // HEAD // SSM_GROUPS
    gw = inner // SSM_GROUPS

    @pl.when(c == 0)
    def _():
        full_s[0:8, :] = jnp.zeros((8, cdim), F32)
        st_s[...] = jnp.zeros(st_s.shape, F32)

    full_s[8:8 + Q, :] = xbc_ref[...]
    for j in range(0, cdim, 512):
        acc = cb_ref[:, j:j + 512] + full_s[8:8 + Q, j:j + 512] * cw_ref[3:4, j:j + 512]
        for i in range(SSM_CONV - 1):
            acc = acc + full_s[5 + i:5 + i + Q, j:j + 512] * cw_ref[i:i + 1, j:j + 512]
        act_s[:, j:j + 512] = _silu(acc)
    full_s[0:8, :] = full_s[Q:Q + 8, :]

    dt = _softplus(dtr_ref[...] + dtb_ref[...])
    row = lax.broadcasted_iota(I32, (Q, Q), 0)
    col = lax.broadcasted_iota(I32, (Q, Q), 1)
    tri = row >= col
    acs = jnp.dot(tri.astype(F32), dt * a_ref[...], precision=HIGHEST, preferred_element_type=F32)
    acs_t, dt_t = acs.T, dt.T
    last = acs[Q - 1:Q, :]
    wdt = jnp.exp(last - acs) * dt
    eacs = jnp.exp(acs)
    cdec = jnp.exp(last)
    low = lax.broadcasted_iota(I32, (Q, LANES), 1) < HEAD
    low1 = low[0:1, :]

    for g in range(SSM_GROUPS):
        bg = act_s[:, inner + g * N:inner + (g + 1) * N]
        cg = act_s[:, inner + SSM_GROUPS * N + g * N:inner + SSM_GROUPS * N + (g + 1) * N]
        cb = _bdot_nt(cg, bg)
        bg_t = bg.T.astype(BF16)
        for p in range(heads_per_group // 2):
            h0 = g * heads_per_group + 2 * p
            js = slice(h0 * HEAD, h0 * HEAD + LANES)
            xp = act_s[:, js]
            xp_bf = xp.astype(BF16)
            stp = st_s[:, js]
            stp_bf = stp.astype(BF16)
            ys = []
            for h in (h0, h0 + 1):
                seg = acs[:, h:h + 1] - acs_t[h:h + 1, :]
                decay = jnp.exp(jnp.where(tri, seg, -jnp.inf))
                m = (cb * decay) * dt_t[h:h + 1, :]
                ce = cg * eacs[:, h:h + 1]
                ys.append(_bdot(m, xp_bf) + _bdot(ce, stp_bf))
            y_s[:, js] = jnp.where(low, ys[0], ys[1])
            wcol = jnp.where(low, wdt[:, h0:h0 + 1], wdt[:, h0 + 1:h0 + 2])
            dst = jnp.dot(bg_t, (xp * wcol).astype(BF16), preferred_element_type=F32)
            cd = jnp.where(low1, cdec[:, h0:h0 + 1], cdec[:, h0 + 1:h0 + 2])
            st_s[:, js] = stp * cd + dst

    for g in range(SSM_GROUPS):
        gs = slice(g * gw, (g + 1) * gw)
        y = y_s[:, gs] + dsk_ref[:, gs] * act_s[:, gs]
        y = y * _silu(z_ref[:, gs])
        ms = jnp.mean(y * y, axis=-1, keepdims=True)
        y_ref[:, gs] = (y * lax.rsqrt(ms + NORM_EPS) * ng_ref[:, gs]).astype(BF16)

    @pl.when(c == pl.num_programs(1) - 1)
    def _():
        st_ref[...] = st_s[...].T


def _ssd_prompt(z, xbc, dtr, cw, cb, dtb, a, dsk, ng, nb):
    m, inner = z.shape
    cdim = xbc.shape[1]
    nc = m // nb // SSM_CHUNK
    row = lambda n: pl.BlockSpec((SSM_CHUNK, n), lambda b, c: (b * nc + c, 0))
    const = lambda shape: pl.BlockSpec(shape, lambda b, c: (0,) * len(shape))
    return pl.pallas_call(
        functools.partial(_ssd_body, inner), grid=(nb, nc),
        in_specs=[row(inner), row(cdim), row(LANES), const(cw.shape), const(cb.shape),
                  const(dtb.shape), const(a.shape), const(dsk.shape), const(ng.shape)],
        out_specs=[row(inner), pl.BlockSpec((None, inner, SSM_STATE), lambda b, c: (b, 0, 0))],
        out_shape=[jax.ShapeDtypeStruct((m, inner), BF16),
                   jax.ShapeDtypeStruct((nb, inner, SSM_STATE), F32)],
        scratch_shapes=[pltpu.VMEM((SSM_CHUNK + 8, cdim), F32), pltpu.VMEM((SSM_CHUNK, cdim), F32),
                        pltpu.VMEM((SSM_STATE, inner), F32), pltpu.VMEM((SSM_CHUNK, inner), F32)],
        compiler_params=_cparams(("parallel", "arbitrary"), 40), name="ssd_prompt",
    )(z, xbc, dtr, cw, cb, dtb, a, dsk, ng)


def _ssd_step_body(z_ref, xs_ref, bc_ref, cxs_ref, cbc_ref, dtr_ref, st_ref,
                   wxs_ref, wbc_ref, bxs_ref, bbc_ref, dtb_ref, a_ref, dsk_ref, ng_ref,
                   y_ref, sto_ref):
    G = SSM_GROUPS
    last = SSM_CONV - 1
    xs = bxs_ref[...] + xs_ref[...] * wxs_ref[last]
    bc = bbc_ref[...] + bc_ref[...] * wbc_ref[last]
    for i in range(last):
        xs = xs + cxs_ref[i] * wxs_ref[i]
        bc = bc + cbc_ref[i] * wbc_ref[i]
    xs, bc = _silu(xs), _silu(bc)
    dt = _softplus(dtr_ref[...] + dtb_ref[...])
    dec = jnp.exp(dt * a_ref[...])
    xdt = xs * dt
    npair = xs.shape[0]
    pairs_per_group = npair // G
    r = lax.broadcasted_iota(I32, (LANES, LANES), 0)
    cidx = lax.broadcasted_iota(I32, (LANES, LANES), 1)
    eye = (r == cidx).astype(F32)
    nt = (((1,), (1,)), ((), ()))
    dec_t = lax.dot_general(eye, dec, nt, precision=HIGHEST, preferred_element_type=F32)
    xdt_t = lax.dot_general(eye, xdt, nt, precision=HIGHEST, preferred_element_type=F32)
    rows = lax.broadcasted_iota(I32, (npair, 1), 0)
    cbv = jnp.sum(bc[0:G, :] * bc[G:2 * G, :], axis=-1, keepdims=True)
    cbx = jnp.zeros((npair, 1), F32)
    for g in range(G):
        cbx = cbx + jnp.where(rows // pairs_per_group == g, cbv[g:g + 1, :], 0.0)
    c_bf = bc.astype(BF16)
    yoff = jnp.zeros(xs.shape, F32)
    for j in range(npair):
        g = j // pairs_per_group
        s = st_ref[j * LANES:(j + 1) * LANES, :]
        sto_ref[j * LANES:(j + 1) * LANES, :] = s * dec_t[:, j:j + 1] + xdt_t[:, j:j + 1] * bc[g:g + 1, :]
        rj = _bdot_nt(c_bf, s)
        yoff = yoff + jnp.where(rows == j, rj[G + g:G + g + 1, :], 0.0)
    y = yoff * dec + cbx * dt * xs + dsk_ref[...] * xs
    y = y * _silu(z_ref[...])
    ssq = jnp.sum(y * y, axis=-1, keepdims=True)
    msx = jnp.zeros((npair, 1), F32)
    for g in range(G):
        ing = rows // pairs_per_group == g
        tot = jnp.sum(jnp.where(ing, ssq, 0.0), axis=0, keepdims=True)
        msx = msx + jnp.where(ing, tot, 0.0)
    msx = msx / (pairs_per_group * LANES)
    y_ref[...] = y * lax.rsqrt(msx + NORM_EPS) * ng_ref[...]


def _ssd_sample(z, xs, bc, cxs, cbc, dtr, st, wxs, wbc, bxs, bbc, dtb, a, dsk, ng):
    db, npair, _ = z.shape
    per_b = lambda shape: pl.BlockSpec((None,) + shape, lambda b: (b,) + (0,) * len(shape))
    const = lambda arr: pl.BlockSpec(arr.shape, lambda b: (0,) * arr.ndim)
    return pl.pallas_call(
        _ssd_step_body, grid=(db,),
        in_specs=[per_b(z.shape[1:]), per_b(xs.shape[1:]), per_b(bc.shape[1:]), per_b(cxs.shape[1:]),
                  per_b(cbc.shape[1:]), per_b(dtr.shape[1:]), per_b(st.shape[1:]),
                  const(wxs), const(wbc), const(bxs), const(bbc), const(dtb), const(a), const(dsk), const(ng)],
        out_specs=[per_b(z.shape[1:]), per_b(st.shape[1:])],
        out_shape=[jax.ShapeDtypeStruct(z.shape, F32), jax.ShapeDtypeStruct(st.shape, F32)],
        compiler_params=_cparams(("parallel",), 32), name="ssd_sample",
    )(z, xs, bc, cxs, cbc, dtr, st, wxs, wbc, bxs, bbc, dtb, a, dsk, ng)


def _row_fold(x, h, op=jnp.add):
    parts = [x[j:j + h, :] for j in range(0, x.shape[0], h)]
    while len(parts) > 1:
        parts = [op(a, b) for a, b in zip(parts[0::2], parts[1::2])] + (parts[-1:] if len(parts) % 2 else [])
    return parts[0]


def _topk_bias(score_s, img_s, bias_s, nblk, kb, topk, qpos, unroll=1):
    nq = score_s.shape[1]
    blk = lambda s: pl.ds(pl.multiple_of(s * kb, kb), kb)
    one16, zero16 = jnp.int16(1), jnp.int16(0)
    work_s = bias_s

    def over_blocks(body, init):
        def trip(s2, carry):
            for k in range(unroll):
                carry = body(s2 * unroll + k, carry)
            return carry
        return lax.fori_loop(0, nblk // unroll, trip, init)

    def count(cmp):
        def body(s, acc):
            return acc + _row_fold(jnp.where(cmp(img_s[blk(s), :]), one16, zero16), 16)
        acc = over_blocks(body, jnp.zeros((16, nq), I16))
        return jnp.sum(acc.astype(F32), axis=0, keepdims=True)

    def search(nbits, value_of):
        def step(t, u):
            code = u | lax.shift_left(jnp.int32(1), jnp.asarray(nbits - 1 - t, I32))
            c = value_of(code).astype(BF16)
            return jnp.where(count(lambda a: a >= c) >= topk, code, u)
        return lax.fori_loop(0, nbits, step, jnp.zeros((1, nq), I32))

    def bf16_value(code):
        pattern = jnp.where(code >= 32768, code - 32768, 65535 - code)
        return pltpu.bitcast(pattern << 16, jnp.float32).astype(F32)

    def set_image(fn):
        def body(s, carry):
            img_s[blk(s), :] = fn(s).astype(BF16)
            return carry
        over_blocks(body, 0)

    t1 = bf16_value(search(16, bf16_value))

    e1 = jnp.clip((pltpu.bitcast(t1.astype(jnp.float32), I32) >> 23) & 0xFF, 25, 254)
    unit = pltpu.bitcast((e1 - 24) << 23, jnp.float32).astype(F32)
    inv_unit = pltpu.bitcast((278 - e1) << 23, jnp.float32).astype(F32)
    B2, B1 = 65536.0, 256.0

    def digit2(s):
        y = (score_s[blk(s), :] - t1) * inv_unit
        work_s[blk(s), :] = y
        return jnp.floor(y * (1.0 / B2))
    set_image(digit2)
    t2 = (search(2, lambda code: (code - 1).astype(F32)) - 1).astype(F32)
    set_image(lambda s: jnp.floor((work_s[blk(s), :] - t2 * B2) * (1.0 / B1)))
    t3 = search(8, lambda code: code.astype(F32)).astype(F32)
    set_image(lambda s: jnp.floor(work_s[blk(s), :] - (t2 * B2 + t3 * B1)))
    t4 = search(8, lambda code: code.astype(F32)).astype(F32)
    v0 = t1 + (t2 * B2 + t3 * B1 + t4) * unit

    def smallest(keep):
        def body(s, acc):
            x = score_s[blk(s), :]
            return jnp.minimum(acc, _row_fold(jnp.where(keep(x), x, jnp.inf), 8, jnp.minimum))
        acc = over_blocks(body, jnp.full((8, nq), jnp.inf, F32))
        return jnp.min(acc, axis=0, keepdims=True)

    def count_above(v):
        def body(s, acc):
            return acc + _row_fold(jnp.where(score_s[blk(s), :] > v, 1.0, 0.0), 8)
        return jnp.sum(over_blocks(body, jnp.zeros((8, nq), F32)), axis=0, keepdims=True)

    def refine(carry):
        v, above = carry
        v = jnp.where(above >= topk, smallest(lambda x: x > v), v)
        return v, count_above(v)

    v = smallest(lambda x: x >= v0)
    v, above = lax.while_loop(lambda c: jnp.max(c[1]) >= topk, refine, (v, count_above(v)))
    need = topk - above

    r_i = lax.broadcasted_iota(I32, (LANES, LANES), 0)
    c_i = lax.broadcasted_iota(I32, (LANES, LANES), 1)
    lower = jnp.where(r_i >= c_i, 1.0, 0.0).astype(BF16)
    sub_iota = lax.broadcasted_iota(I32, (LANES, 1), 0)

    def bias_body(s, carry):
        for j in range(0, kb, LANES):
            off = pl.multiple_of(s * kb + j, LANES)
            x = score_s[pl.ds(off, LANES), :]
            eq = x == v
            eqf = jnp.where(eq, 1.0, 0.0)
            incl = jnp.dot(lower, eqf.astype(BF16), preferred_element_type=F32)
            tie = jnp.where(carry + incl - eqf < need, 0.0, -jnp.inf)
            b = jnp.where(x > v, 0.0, jnp.where(eq, tie, -jnp.inf))
            bias_s[pl.ds(off, LANES), :] = jnp.where((off + sub_iota) <= qpos, b, -jnp.inf)
            carry = carry + incl[LANES - 1:LANES, :]
        return carry

    over_blocks(bias_body, jnp.zeros((1, nq), F32))


def _dsa_t_body(topk, idx_scale, q_ref, qi_ref, sm_ref, ki2_ref, k_ref, vt_ref, o_ref,
                score_s, img_s, bias_s, qs_s, lga_s, lgb_s, m_s, acc_s):
    i = pl.program_id(1)
    QB, KB = Q_BLOCK, KEY_BLOCK
    nkb = (i * QB + QB + KB - 1) // KB
    qpos = i * QB + lax.broadcasted_iota(I32, (1, QB), 1)
    low = lax.broadcasted_iota(I32, (QB, LANES), 1) < HEAD
    zero_bf = jnp.zeros((QB, LANES), BF16)
    blk = lambda s: pl.ds(pl.multiple_of(s * KB, KB), KB)

    qi = qi_ref[...]
    sm_t = sm_ref[...].T
    qh, wh = [], []
    for h in range(IDX_HEADS):
        chunk = qi[:, (h // 2) * LANES:(h // 2 + 1) * LANES]
        qh.append(jnp.where(low if h % 2 == 0 else ~low, chunk, zero_bf))
        wh.append(sm_t[HEAD + h:HEAD + h + 1, :] * idx_scale)

    q_stack = jnp.concatenate(qh, axis=0)

    last_key_block = ki2_ref.shape[0] // KB - 1

    def score_pair(s2, carry):
        blocks = (2 * s2, 2 * s2 + 1)
        dots = [_bdot_nt(ki2_ref[blk(jnp.minimum(s, last_key_block)), :], q_stack) for s in blocks]
        for s, sc in zip(blocks, dots):
            acc = jnp.zeros((KB, QB), F32)
            for h in range(IDX_HEADS):
                acc = acc + wh[h] * jnp.maximum(sc[:, h * QB:(h + 1) * QB], 0.0)
            kpos = s * KB + lax.broadcasted_iota(I32, (KB, 1), 0)
            sc = jnp.where(kpos <= qpos, jnp.maximum(acc, MASKED_SCORE), MASKED_SCORE)
            score_s[blk(s), :] = sc
            img_s[blk(s), :] = sc.astype(BF16)
        return carry

    nkb_even = 2 * ((nkb + 1) // 2)
    lax.fori_loop(0, nkb_even // 2, score_pair, 0)
    _topk_bias(score_s, img_s, bias_s, nkb_even, KB, topk, qpos, unroll=2)

    q = q_ref[...]
    nchunk = q.shape[1] // LANES
    per_kv_chunk = nchunk // (ATTN_KV_HEADS // 2)
    nstack = 2 * per_kv_chunk
    srows = nstack * QB
    for cj in range(nchunk):
        kvc, j = divmod(cj, per_kv_chunk)
        qc = q[:, cj * LANES:(cj + 1) * LANES]
        for half in range(2):
            r0 = (kvc * nstack + half * per_kv_chunk + j) * QB
            qs_s[r0:r0 + QB, :] = jnp.where(low if half == 0 else ~low, qc, zero_bf)

    ones_rows = jnp.ones((ONES_ROWS, KB), BF16)
    cols = lambda r: slice(r * QB, (r + 1) * QB)
    n_kvc = ATTN_KV_HEADS // 2
    nsteps = n_kvc * nkb

    def step_of(t):
        t = jnp.minimum(t, nsteps - 1)
        kvc = (t >= nkb).astype(I32)
        return kvc, t - kvc * nkb

    def logits_to(dst, t):
        kvc, s = step_of(t)
        kb = k_ref[blk(s), pl.ds(pl.multiple_of(kvc * LANES, LANES), LANES)]
        bias = bias_s[blk(s), :]
        lg = _bdot_nt(kb, qs_s[pl.ds(pl.multiple_of(kvc * srows, srows), srows), :])
        for r in range(nstack):
            dst[:, cols(r)] = lg[:, cols(r)] + bias

    def consume(src, t):
        kvc, s = step_of(t)
        vt = vt_ref[pl.ds(pl.multiple_of(kvc * LANES, LANES), LANES), blk(s)]
        ps, alphas = [], []
        for r in range(nstack):
            lg = src[:, cols(r)]
            m = m_s[kvc, :, cols(r)]
            mn = jnp.maximum(m, jnp.max(lg, axis=0, keepdims=True))
            m_s[kvc, :, cols(r)] = mn
            ps.append(jnp.exp2(lg - mn).astype(BF16))
            alphas.append(jnp.exp2(m - mn))
        for half in range(2):
            hs = slice(half * per_kv_chunk, (half + 1) * per_kv_chunk)
            hc = slice(half * per_kv_chunk * QB, (half + 1) * per_kv_chunk * QB)
            v_aug = jnp.concatenate([vt[half * HEAD:(half + 1) * HEAD, :], ones_rows], axis=0)
            pv = jnp.dot(v_aug, jnp.concatenate(ps[hs], axis=1), preferred_element_type=F32)
            acc_s[kvc, :, hc] = jnp.concatenate(alphas[hs], axis=1) * acc_s[kvc, :, hc] + pv

    m_s[...] = jnp.full(m_s.shape, NEG_BIG, F32)
    acc_s[...] = jnp.zeros(acc_s.shape, F32)
    logits_to(lga_s, 0)

    def run_pairs(t0, npairs):
        for p in range(npairs):
            logits_to(lgb_s, t0 + 2 * p + 1)
            consume(lga_s, t0 + 2 * p)
            logits_to(lga_s, t0 + 2 * p + 2)
            consume(lgb_s, t0 + 2 * p + 1)

    def quad_body(i4, carry):
        run_pairs(4 * i4, 2)
        return carry

    lax.fori_loop(0, nsteps // 4, quad_body, 0)

    @pl.when(nsteps % 4 == 2)
    def _():
        run_pairs(nsteps - 2, 1)

    for kvc in range(n_kvc):
        acc = acc_s[kvc]
        for j in range(per_kv_chunk):
            a = acc[:, cols(j)]
            b = acc[:, cols(per_kv_chunk + j)]
            chunk_t = jnp.concatenate([a[0:HEAD, :] / a[HEAD:HEAD + 1, :], b[0:HEAD, :] / b[HEAD:HEAD + 1, :]],
                                      axis=0)
            cj = kvc * per_kv_chunk + j
            o_ref[:, cj * LANES:(cj + 1) * LANES] = chunk_t.T.astype(BF16)


def _dsa_prompt_t(q, qi, sm, ki2, k, vt, nb, topk):
    m, aw = q.shape
    t = m // nb
    nq = t // Q_BLOCK
    tpad = -(-t // (2 * KEY_BLOCK)) * 2 * KEY_BLOCK
    srows = aw // HEAD // (ATTN_KV_HEADS // 2) * Q_BLOCK
    idx_scale = IDX_HEADS ** -0.5 * HEAD ** -0.5
    row = lambda n: pl.BlockSpec((Q_BLOCK, n), lambda b, i: (b * nq + i, 0))
    per_b = lambda n: pl.BlockSpec((t, n), lambda b, i: (b, 0))
    return pl.pallas_call(
        functools.partial(_dsa_t_body, topk, idx_scale), grid=(nb, nq),
        in_specs=[row(aw), row(qi.shape[1]), row(LANES), per_b(LANES), per_b(k.shape[1]),
                  pl.BlockSpec((None,) + vt.shape[1:], lambda b, i: (b, 0, 0))],
        out_specs=row(aw),
        out_shape=jax.ShapeDtypeStruct((m, aw), BF16),
        scratch_shapes=[pltpu.VMEM((tpad, Q_BLOCK), F32), pltpu.VMEM((tpad, Q_BLOCK), BF16),
                        pltpu.VMEM((tpad, Q_BLOCK), F32),
                        pltpu.VMEM((2 * srows, LANES), BF16),
                        pltpu.VMEM((KEY_BLOCK, srows), F32), pltpu.VMEM((KEY_BLOCK, srows), F32),
                        pltpu.VMEM((ATTN_KV_HEADS // 2, 1, srows), F32),
                        pltpu.VMEM((ATTN_KV_HEADS // 2, HEAD + ONES_ROWS, srows), F32)],
        compiler_params=_cparams(("parallel", "arbitrary"), 52), name="dsa_prompt",
    )(q, qi, sm, ki2, k, vt)


def _page_specs(block, npages):
    def make(u):
        return pl.BlockSpec((None,) + block, lambda b, pt: (pt[b, u],) + (0,) * len(block))
    return [make(u) for u in range(npages)]


def _idx_score_body(pt_ref, qi_ref, w_ref, *refs):
    pages, o_ref = refs[:-1], refs[-1]
    qi = qi_ref[...]
    w = w_ref[...]
    for u, page in enumerate(pages):
        s = jnp.maximum(_bdot(qi, page[...]), 0.0)
        o_ref[:, u * PAGE_SIZE:(u + 1) * PAGE_SIZE] = jnp.sum(w * s, axis=0, keepdims=True)


def _idx_scores_sample(page_table, qi8, w8, kidx_t):
    db, npages = page_table.shape
    grid_spec = pltpu.PrefetchScalarGridSpec(
        num_scalar_prefetch=1, grid=(db,),
        in_specs=[pl.BlockSpec((None,) + qi8.shape[1:], lambda b, pt: (b, 0, 0)),
                  pl.BlockSpec((None,) + w8.shape[1:], lambda b, pt: (b, 0, 0))]
                 + _page_specs(kidx_t.shape[1:], npages),
        out_specs=pl.BlockSpec((None, 1, npages * PAGE_SIZE), lambda b, pt: (b, 0, 0)))
    return pl.pallas_call(
        _idx_score_body, grid_spec=grid_spec,
        out_shape=jax.ShapeDtypeStruct((db, 1, npages * PAGE_SIZE), F32),
        compiler_params=_cparams(("parallel",), 32), name="idx_scores_sample",
    )(page_table, qi8, w8, *([kidx_t] * npages))


def _select_sample_body(topk, idx_scale, past, sc_ref, qi_ref, sm_ref, bias_ref, score_s, img_s, bias_s):
    rows = sc_ref.shape[0]
    sm = sm_ref[...]
    qi = qi_ref[...]
    ki = sm[:, 0:HEAD]
    new = jnp.zeros((rows, 1), F32)
    for h in range(IDX_HEADS):
        d = jnp.sum(qi[:, h * HEAD:(h + 1) * HEAD] * ki, axis=-1, keepdims=True)
        new = new + (sm[:, HEAD + h:HEAD + h + 1] * idx_scale) * jnp.maximum(d, 0.0)
    nblk = (past + LANES) // LANES
    lane = lax.broadcasted_iota(I32, (rows, LANES), 1)
    for j in range(nblk):
        js = slice(j * LANES, (j + 1) * LANES)
        sc = sc_ref[:, js] if j < nblk - 1 else jnp.where(lane == 0, new, MASKED_SCORE)
        sc = jnp.maximum(sc, MASKED_SCORE).T
        score_s[js, :] = sc
        img_s[js, :] = sc.astype(BF16)
    qpos = jnp.full((1, rows), past, I32)
    _topk_bias(score_s, img_s, bias_s, nblk, LANES, topk, qpos)
    for j in range(nblk):
        js = slice(j * LANES, (j + 1) * LANES)
        bias_ref[:, js] = bias_s[js, :].T


def _select_sample(scores, qi, sm, topk, idx_scale):
    db, past = scores.shape
    full = lambda a: pl.BlockSpec(a.shape, lambda i: (0,) * a.ndim)
    keys = past + LANES
    return pl.pallas_call(
        functools.partial(_select_sample_body, topk, idx_scale, past), grid=(1,),
        in_specs=[full(scores), full(qi), full(sm)],
        out_specs=pl.BlockSpec((db, keys), lambda i: (0, 0)),
        out_shape=jax.ShapeDtypeStruct((db, keys), F32),
        scratch_shapes=[pltpu.VMEM((keys, db), F32), pltpu.VMEM((keys, db), BF16), pltpu.VMEM((keys, db), F32)],
        compiler_params=_cparams(("arbitrary",), 32), name="select_sample")(scores, qi, sm)


def _attend_sample_body(npages, pt_ref, q_ref, bias_ref, knew_ref, vnew_ref, *refs):
    kpages, vpages, o_ref = refs[:npages], refs[npages:2 * npages], refs[2 * npages]
    G = ATTN_KV_HEADS
    q = q_ref[...]
    q_bf = q.astype(BF16)
    nh = q.shape[0]
    past = npages * PAGE_SIZE
    group = lax.broadcasted_iota(I32, (nh, 1), 0) // (nh // G)

    def by_group(parts):
        out = parts[G - 1]
        for g in range(G - 2, -1, -1):
            out = jnp.where(group == g, parts[g], out)
        return out

    lg = jnp.concatenate(
        [by_group([_bdot(q_bf, kpages[u][g]) for g in range(G)]) for u in range(npages)], axis=1)
    lg = lg + bias_ref[:, 0:past]
    lg_new = by_group([jnp.sum(q * knew_ref[g:g + 1, :], axis=-1, keepdims=True) for g in range(G)])
    lg_new = lg_new + bias_ref[:, past:past + 1]
    m = jnp.maximum(jnp.max(lg, axis=-1, keepdims=True), lg_new)
    p = jnp.exp2(lg - m)
    p_new = jnp.exp2(lg_new - m)
    denom = jnp.sum(p, axis=-1, keepdims=True) + p_new
    p_bf = p.astype(BF16)
    accs = [p_new * vnew_ref[g:g + 1, :] for g in range(G)]
    for u in range(npages):
        pu = p_bf[:, u * PAGE_SIZE:(u + 1) * PAGE_SIZE]
        for g in range(G):
            accs[g] = accs[g] + _bdot_nt(pu, vpages[u][g])
    o_ref[...] = by_group(accs) / denom


def _attend_sample(page_table, q, bias, knew, vnew, k_t, v_t):
    db, npages = page_table.shape
    per_b = lambda a: pl.BlockSpec((None,) + a.shape[1:], lambda b, pt: (b,) + (0,) * (a.ndim - 1))
    grid_spec = pltpu.PrefetchScalarGridSpec(
        num_scalar_prefetch=1, grid=(db,),
        in_specs=[per_b(q), per_b(bias), per_b(knew), per_b(vnew)]
                 + _page_specs(k_t.shape[1:], npages) + _page_specs(v_t.shape[1:], npages),
        out_specs=per_b(q))
    return pl.pallas_call(
        functools.partial(_attend_sample_body, npages), grid_spec=grid_spec,
        out_shape=jax.ShapeDtypeStruct(q.shape, F32),
        compiler_params=_cparams(("parallel",), 48), name="attend_sample",
    )(page_table, q, bias, knew, vnew, *([k_t] * npages), *([v_t] * npages))


def _merge_body(x_ref, mod_ref, ys_ref, ya_ref, gs_ref, ga_ref, wps_ref, wpa_ref, wo_ref, o_ref):
    d = x_ref.shape[1]
    merged = (jax.nn.sigmoid(gs_ref[...]) * _bdot(ys_ref[...], wps_ref[...])
              + jax.nn.sigmoid(ga_ref[...]) * _bdot(ya_ref[...], wpa_ref[...]))
    o_ref[...] = x_ref[...] + mod_ref[:, 2 * d:3 * d] * _bdot(merged, wo_ref[...])


def _merge(x, mod3, ys, ya, gs, ga, wps, wpa, wo, tm, tpb):
    m, d = x.shape
    x_spec, mod_spec = _row_specs(tm, tpb, d, mod3.shape[1])
    row = lambda n: pl.BlockSpec((tm, n), lambda i: (i, 0))
    return pl.pallas_call(
        _merge_body, grid=(m // tm,),
        in_specs=[x_spec, mod_spec, row(ys.shape[1]), row(ya.shape[1]), row(d), row(d),
                  _const_spec(wps.shape), _const_spec(wpa.shape), _const_spec(wo.shape)],
        out_specs=row(d), out_shape=jax.ShapeDtypeStruct((m, d), F32),
        compiler_params=_cparams(("parallel",), 48), name="merge")(x, mod3, ys, ya, gs, ga, wps, wpa, wo)


def _ffn_body(last_layer, x_ref, mod_ref, g_ref, fg_ref, wg_ref, wu_ref, wo_ref, o_ref):
    d = x_ref.shape[1]
    x = x_ref[...]
    h = _norm_mod(x, g_ref[...], mod_ref[:, 4 * d:5 * d], mod_ref[:, 3 * d:4 * d]).astype(BF16)
    gate = jnp.dot(h, wg_ref[...], preferred_element_type=F32)
    up = jnp.dot(h, wu_ref[...], preferred_element_type=F32)
    x2 = x + mod_ref[:, 5 * d:6 * d] * _bdot(_silu(gate) * up, wo_ref[...])
    if last_layer:
        ms = jnp.mean(x2 * x2, axis=-1, keepdims=True)
        x2 = x2 * lax.rsqrt(ms + NORM_EPS) * fg_ref[...]
    o_ref[...] = x2


def _ffn(x, mod3, g, fg, wg, wu, wo, tm, tpb, last_layer):
    m, d = x.shape
    x_spec, mod_spec = _row_specs(tm, tpb, d, mod3.shape[1])
    return pl.pallas_call(
        functools.partial(_ffn_body, last_layer), grid=(m // tm,),
        in_specs=[x_spec, mod_spec, _const_spec((1, d)), _const_spec((1, d)),
                  _const_spec(wg.shape), _const_spec(wu.shape), _const_spec(wo.shape)],
        out_specs=pl.BlockSpec((tm, d), lambda i: (i, 0)), out_shape=jax.ShapeDtypeStruct((m, d), F32),
        compiler_params=_cparams(("parallel",), 56), name="ffn")(x, mod3, g, fg, wg, wu, wo)


def _rope_tables(pos):
    half = HEAD // 2
    inv = ROPE_THETA ** (-jnp.arange(half, dtype=F32) / half)
    ang = pos.astype(F32)[:, None] * inv[None, :]
    cos = jnp.tile(jnp.cos(ang), (1, LANES // half))
    sin = jnp.tile(jnp.sin(ang), (1, LANES // half))
    first = (jnp.arange(LANES) % HEAD) < half
    return cos, jnp.where(first, -sin, 0.0), jnp.where(first, 0.0, sin)


def _q_head_order(n_heads):
    rep = n_heads // ATTN_KV_HEADS
    order = []
    for c in range(ATTN_KV_HEADS // 2):
        for j in range(rep):
            order += [2 * c * rep + j, (2 * c + 1) * rep + j]
    return np.asarray(order)


def kernel(x_prompt, x_sample, cache_k, cache_v, cache_kidx, state_conv, state_ssm, page_table, c_prompt, c_sample, w_ada, b_ada, norm1_g, w_in, conv_w, conv_b, dt_bias, a_log, d_skip, ssm_norm_g, w_proj_ssm, w_proj_attn, w_out, norm2_g, w_ffn_in, w_ffn_out, final_g):
    nb, t, d = x_prompt.shape
    db, ds, _ = x_sample.shape
    depth = w_in.shape[0]
    assert ds == 1 and t % Q_BLOCK == 0 and t % SSM_CHUNK == 0
    n_heads_ssm = dt_bias.shape[1]
    inner = n_heads_ssm * HEAD
    gn = SSM_GROUPS * SSM_STATE
    cdim = inner + 2 * gn
    kvw = ATTN_KV_HEADS * HEAD
    aw = w_proj_attn.shape[1]
    n_heads = aw // HEAD
    iw = IDX_HEADS * HEAD
    ffn_hidden = w_ffn_out.shape[1]
    npages = page_table.shape[1]
    past = npages * PAGE_SIZE
    topk_p = min(TOPK_MAX, t // 4)
    topk_s = min(TOPK_MAX, (past + ds) // 4)
    assert past + ds >= topk_s
    idx_scale = IDX_HEADS ** -0.5 * HEAD ** -0.5
    tm_p = 256 if t % 256 == 0 else 128
    tpb_p = t // tm_p

    splits = np.cumsum([inner, inner, gn, gn, n_heads_ssm, aw, kvw, kvw, iw, HEAD, IDX_HEADS, d])
    order = _q_head_order(n_heads)
    inv_order = np.argsort(order)

    cos_p, slo_p, shi_p = _rope_tables(jnp.arange(t, dtype=I32))
    tabs_p = (cos_p, slo_p, shi_p)
    tabs_s = tuple(jnp.broadcast_to(a, (db, LANES)) for a in _rope_tables(past + jnp.arange(ds, dtype=I32)))

    rows_c = nb + db
    c_all = jnp.concatenate([c_prompt, c_sample, jnp.zeros((-rows_c % 8, d), F32)], axis=0)

    yp = x_prompt.reshape(nb * t, d)
    ys = x_sample.reshape(db, d)
    outs_p, outs_s = [], []
    for l in range(depth):
        (wz, wxs, wbm, wcm, wdt, wq, wk, wv, wqi, wki, wwi, wgs, wga) = jnp.split(w_in[l], splits, axis=1)
        w_ssm = jnp.concatenate([wz, wxs, wbm, wcm, wdt, jnp.zeros((d, LANES - n_heads_ssm), F32)],
                                axis=1).astype(BF16)
        wq_perm = wq.reshape(d, n_heads, HEAD)[:, order].reshape(d, aw)
        w_attn = jnp.concatenate([wq_perm, wk, wv, wqi, wgs, wga, wki, wwi,
                                  jnp.zeros((d, LANES - HEAD - IDX_HEADS), F32)], axis=1).astype(BF16)
        wps = w_proj_ssm[l].astype(BF16)
        wpa = w_proj_attn[l].reshape(n_heads, HEAD, d)[order].reshape(aw, d).astype(BF16)
        wo = w_out[l].astype(BF16)
        wg = w_ffn_in[l][:, :ffn_hidden].astype(BF16)
        wu = w_ffn_in[l][:, ffn_hidden:].astype(BF16)
        wfo = w_ffn_out[l].astype(BF16)
        g1 = norm1_g[l][None, :]
        g2 = norm2_g[l][None, :]
        a_neg = -jnp.exp(a_log[l])
        pad_h = LANES - n_heads_ssm
        dtb_row = jnp.pad(dt_bias[l], (0, pad_h))[None, :]
        a_row = jnp.pad(a_neg, (0, pad_h))[None, :]
        dsk_row = jnp.repeat(d_skip[l], HEAD)[None, :]
        ng_row = ssm_norm_g[l][None, :]
        cw = conv_w[l]
        cb = conv_b[l][None, :]

        mod = _ada(c_all, w_ada[l].astype(BF16), b_ada[l][None, :])
        mod_p = mod[:nb][:, None, :]
        mod_s = mod[nb:nb + db][None]

        z, xbc, dtr = _inproj_ssm(yp, mod_p, g1, w_ssm, tm_p, tpb_p, inner, cdim)
        (q_bf, k_bf, kt, vt, vt_bf, qi_bf, gs, ga, sm, ki2, kit) = _inproj_attn(
            yp, mod_p, g1, w_attn, tabs_p, tm_p, tpb_p, aw, kvw, iw)
        y_ssm, st = _ssd_prompt(z, xbc, dtr, cw, cb, dtb_row, a_row, dsk_row, ng_row, nb)
        y_attn = _dsa_prompt_t(q_bf, qi_bf, sm, ki2, k_bf, vt_bf, nb, topk_p)
        x1 = _merge(yp, mod_p, y_ssm, y_attn, gs, ga, wps, wpa, wo, tm_p, tpb_p)
        yp_next = _ffn(x1, mod_p, g2, final_g[None, :], wg, wu, wfo, tm_p, tpb_p, l == depth - 1)
        heads_last = lambda a: jnp.transpose(a.reshape(a.shape[0], ATTN_KV_HEADS, HEAD, a.shape[2]), (0, 3, 1, 2))
        outs_p.append((heads_last(kt), heads_last(vt), jnp.transpose(kit, (0, 2, 1)),
                       xbc.reshape(nb, t, cdim)[:, t - (SSM_CONV - 1):],
                       st.reshape(nb, n_heads_ssm, HEAD, SSM_STATE)))

        z_s, xbc_s, dtr_s = _inproj_ssm(ys, mod_s, g1, w_ssm, db, 1, inner, cdim)
        (q_s, _, kt_s, vt_s, _, qi_s, gs_s, ga_s, sm_s, _, kit_s) = _inproj_attn(
            ys, mod_s, g1, w_attn, tabs_s, db, 1, aw, kvw, iw)
        k_s, v_s = heads_last(kt_s)[0], heads_last(vt_s)[0]
        npair = inner // LANES
        nbc = 2 * gn // LANES
        sc = state_conv[l]
        y_ssm_s, st_s = _ssd_sample(
            z_s.reshape(db, npair, LANES), xbc_s[:, :inner].reshape(db, npair, LANES),
            xbc_s[:, inner:].reshape(db, nbc, LANES),
            sc[:, :, :inner].reshape(db, SSM_CONV - 1, npair, LANES),
            sc[:, :, inner:].reshape(db, SSM_CONV - 1, nbc, LANES),
            jnp.repeat(dtr_s[:, :n_heads_ssm], HEAD, axis=1).reshape(db, npair, LANES),
            state_ssm[l].reshape(db, inner, SSM_STATE),
            cw[:, :inner].reshape(SSM_CONV, npair, LANES), cw[:, inner:].reshape(SSM_CONV, nbc, LANES),
            cb[:, :inner].reshape(npair, LANES), cb[:, inner:].reshape(nbc, LANES),
            jnp.repeat(dt_bias[l], HEAD).reshape(npair, LANES), jnp.repeat(a_neg, HEAD).reshape(npair, LANES),
            dsk_row.reshape(npair, LANES), ng_row.reshape(npair, LANES))

        qi_f = qi_s.astype(F32)
        qi8 = jnp.pad(qi_f.reshape(db, IDX_HEADS, HEAD), ((0, 0), (0, 8 - IDX_HEADS), (0, 0)))
        w8 = jnp.broadcast_to(jnp.pad(sm_s[:, HEAD:HEAD + IDX_HEADS] * idx_scale,
                                      ((0, 0), (0, 8 - IDX_HEADS)))[:, :, None], (db, 8, LANES))
        scores = _idx_scores_sample(page_table, qi8, w8, jnp.transpose(cache_kidx[l], (0, 2, 1)))
        bias = _select_sample(scores.reshape(db, past), qi_f, sm_s, topk_s, idx_scale)
        q_orig = q_s.astype(F32).reshape(db, n_heads, HEAD)[:, inv_order]
        att = _attend_sample(page_table, q_orig, bias[:, None, :],
                             k_s, v_s,
                             jnp.transpose(cache_k[l], (0, 2, 3, 1)), jnp.transpose(cache_v[l], (0, 2, 3, 1)))
        y_attn_s = att[:, order].reshape(db, aw)
        x1_s = _merge(ys, mod_s, y_ssm_s.reshape(db, inner), y_attn_s, gs_s, ga_s, wps, wpa, wo, db, 1)
        ys_next = _ffn(x1_s, mod_s, g2, final_g[None, :], wg, wu, wfo, db, 1, l == depth - 1)
        outs_s.append((k_s[:, None], v_s[:, None], jnp.transpose(kit_s, (2, 0, 1)),
                       jnp.concatenate([sc[:, 1:], xbc_s[:, None, :]], axis=1),
                       st_s.reshape(db, n_heads_ssm, HEAD, SSM_STATE)))
        yp, ys = yp_next, ys_next

    stack = lambda outs, i: jnp.stack([o[i] for o in outs], axis=0)
    return (yp.reshape(nb, t, d), ys.reshape(db, ds, d),
            stack(outs_p, 0), stack(outs_p, 1), stack(outs_p, 2), stack(outs_p, 3), stack(outs_p, 4),
            stack(outs_s, 0), stack(outs_s, 1), stack(outs_s, 2), stack(outs_s, 3), stack(outs_s, 4))
```

```python
import functools

import jax
import jax.numpy as jnp
import numpy as np
from jax import lax
from jax.experimental import pallas as pl
from jax.experimental.pallas import tpu as pltpu

F32, BF16, I32, I16 = jnp.float32, jnp.bfloat16, jnp.int32, jnp.int16
HIGHEST = lax.Precision.HIGHEST

LANES = 128
HEAD = 64
SSM_STATE = 128
SSM_GROUPS = 4
SSM_CONV = 4
SSM_CHUNK = 128
ATTN_KV_HEADS = 4
IDX_HEADS = 4
TOPK_MAX = 256
Q_BLOCK = 128
PAGE_SIZE = 128
ROPE_THETA = 10000.0
NORM_EPS = 1e-6
KEY_BLOCK = 512
ONES_ROWS = 16
Q_SCALE = HEAD ** -0.5 * 1.4426950408889634
MASKED_SCORE = -3.3895313892515355e38
NEG_BIG = -1e30


def _cparams(sem, vmem_mb):
    return pltpu.CompilerParams(dimension_semantics=sem, vmem_limit_bytes=vmem_mb << 20)


def _bdot(a, b):
    return jnp.dot(a.astype(BF16), b.astype(BF16), preferred_element_type=F32)


def _bdot_nt(a, b):
    return lax.dot_general(a.astype(BF16), b.astype(BF16), (((1,), (1,)), ((), ())),
                           preferred_element_type=F32)


def _silu(x):
    h = 0.5 * x
    return h + h * jnp.tanh(h)


def _softplus(x):
    return jnp.maximum(x, 0.0) + jnp.log(1.0 + jnp.exp(-jnp.abs(x)))


def _norm_mod(x, g, scale, shift):
    ms = jnp.mean(x * x, axis=-1, keepdims=True)
    return (x * lax.rsqrt(ms + NORM_EPS) * g) * (1.0 + scale) + shift


def _rope128(x, cos, sin_lo, sin_hi):
    return x * cos + pltpu.roll(x, 96, 1) * sin_lo + pltpu.roll(x, 32, 1) * sin_hi


def _rope_wide(x, cos, sin_lo, sin_hi):
    parts = [_rope128(x[:, j:j + LANES], cos, sin_lo, sin_hi) for j in range(0, x.shape[1], LANES)]
    return parts[0] if len(parts) == 1 else jnp.concatenate(parts, axis=1)


def _ada_body(c_ref, w_ref, b_ref, o_ref):
    o_ref[...] = _bdot(_silu(c_ref[...]), w_ref[...]) + b_ref[...]


def _ada(c_all, w_bf, b):
    mp, d = c_all.shape
    n = w_bf.shape[1]
    tn = n // 4
    return pl.pallas_call(
        _ada_body, grid=(n // tn,),
        in_specs=[pl.BlockSpec((mp, d), lambda j: (0, 0)),
                  pl.BlockSpec((d, tn), lambda j: (0, j)),
                  pl.BlockSpec((1, tn), lambda j: (0, j))],
        out_specs=pl.BlockSpec((mp, tn), lambda j: (0, j)),
        out_shape=jax.ShapeDtypeStruct((mp, n), F32),
        compiler_params=_cparams(("arbitrary",), 32), name="ada")(c_all, w_bf, b)


def _inproj_ssm_body(inner, cdim, x_ref, mod_ref, g_ref, w_ref, z_ref, xbc_ref, dt_ref):
    d = x_ref.shape[1]
    h = _norm_mod(x_ref[...], g_ref[...], mod_ref[:, d:2 * d], mod_ref[:, 0:d]).astype(BF16)
    z_ref[...] = jnp.dot(h, w_ref[:, 0:inner], preferred_element_type=F32)
    xbc_ref[...] = jnp.dot(h, w_ref[:, inner:inner + cdim], preferred_element_type=F32)
    dt_ref[...] = jnp.dot(h, w_ref[:, inner + cdim:inner + cdim + LANES], preferred_element_type=F32)


def _row_specs(tm, tpb, d, mod_rows):
    x_spec = pl.BlockSpec((tm, d), lambda m: (m, 0))
    mod_spec = pl.BlockSpec((None, mod_rows, 6 * d), lambda m: (m // tpb, 0, 0))
    return x_spec, mod_spec


def _const_spec(shape):
    return pl.BlockSpec(shape, lambda m: (0,) * len(shape))


def _inproj_ssm(x, mod3, g, w_bf, tm, tpb, inner, cdim):
    m, d = x.shape
    x_spec, mod_spec = _row_specs(tm, tpb, d, mod3.shape[1])
    row = lambda n: pl.BlockSpec((tm, n), lambda i: (i, 0))
    return pl.pallas_call(
        functools.partial(_inproj_ssm_body, inner, cdim), grid=(m // tm,),
        in_specs=[x_spec, mod_spec, _const_spec((1, d)), _const_spec(w_bf.shape)],
        out_specs=[row(inner), row(cdim), row(LANES)],
        out_shape=[jax.ShapeDtypeStruct((m, inner), F32), jax.ShapeDtypeStruct((m, cdim), F32),
                   jax.ShapeDtypeStruct((m, LANES), F32)],
        compiler_params=_cparams(("parallel",), 52), name="inproj_ssm")(x, mod3, g, w_bf)


def _inproj_attn_body(aw, kvw, iw, x_ref, mod_ref, g_ref, w_ref, cos_ref, slo_ref, shi_ref,
                      q_ref, kb_ref, kt_ref, vt_ref, vtb_ref, qi_ref, gs_ref, ga_ref, sm_ref, ki2_ref, kit_ref):
    d = x_ref.shape[1]
    h = _norm_mod(x_ref[...], g_ref[...], mod_ref[:, d:2 * d], mod_ref[:, 0:d]).astype(BF16)
    cos, slo, shi = cos_ref[...], slo_ref[...], shi_ref[...]

    def proj(a, b):
        return jnp.dot(h, w_ref[:, a:b], preferred_element_type=F32)

    o = 0
    q_ref[...] = (_rope_wide(proj(o, o + aw), cos, slo, shi) * Q_SCALE).astype(BF16)
    o += aw
    k = _rope_wide(proj(o, o + kvw), cos, slo, shi)
    kb_ref[...] = k.astype(BF16)
    kt_ref[...] = k.T
    o += kvw
    vt = proj(o, o + kvw).T
    vt_ref[...] = vt
    vtb_ref[...] = vt.astype(BF16)
    o += kvw
    qi_ref[...] = _rope_wide(proj(o, o + iw), cos, slo, shi).astype(BF16)
    o += iw
    gs_ref[...] = proj(o, o + d)
    o += d
    ga_ref[...] = proj(o, o + d)
    o += d
    s = proj(o, o + LANES)
    lane = lax.broadcasted_iota(I32, s.shape, 1)
    sm = jnp.where(lane < HEAD, _rope128(s, cos, slo, shi), s)
    sm_ref[...] = sm
    ki2_ref[...] = jnp.where(lane < HEAD, sm, pltpu.roll(sm, HEAD, 1)).astype(BF16)
    kit_ref[...] = sm.T[0:HEAD, :]


def _inproj_attn(x, mod3, g, w_bf, tabs, tm, tpb, aw, kvw, iw):
    m, d = x.shape
    x_spec, mod_spec = _row_specs(tm, tpb, d, mod3.shape[1])
    ntab = tabs[0].shape[0] // tm
    nb = m // (tm * tpb)
    tab_spec = pl.BlockSpec((tm, LANES), lambda i: (i % ntab, 0))
    row = lambda n, dt: (pl.BlockSpec((tm, n), lambda i: (i, 0)), jax.ShapeDtypeStruct((m, n), dt))
    tmin = lambda n, dt: (pl.BlockSpec((None, n, tm), lambda i: (i // tpb, 0, i % tpb)),
                          jax.ShapeDtypeStruct((nb, n, tm * tpb), dt))
    outs = [row(aw, BF16), row(kvw, BF16), tmin(kvw, F32), tmin(kvw, F32), tmin(kvw, BF16), row(iw, BF16),
            row(d, F32), row(d, F32), row(LANES, F32), row(LANES, BF16), tmin(HEAD, F32)]
    return pl.pallas_call(
        functools.partial(_inproj_attn_body, aw, kvw, iw), grid=(m // tm,),
        in_specs=[x_spec, mod_spec, _const_spec((1, d)), _const_spec(w_bf.shape),
                  tab_spec, tab_spec, tab_spec],
        out_specs=[spec for spec, _ in outs],
        out_shape=[shape for _, shape in outs],
        compiler_params=_cparams(("parallel",), 52), name="inproj_attn")(x, mod3, g, w_bf, *tabs)


def _ssd_body(inner, z_ref, xbc_ref, dtr_ref, cw_ref, cb_ref, dtb_ref, a_ref, dsk_ref, ng_ref,
              y_ref, st_ref, full_s, act_s, st_s, y_s):
    c = pl.program_id(1)
    Q, N = SSM_CHUNK, SSM_STATE
    cdim = xbc_ref.shape[1]
    heads_per_group = inner // HEAD // SSM_GROUPS
    gw = inner // SSM_GROUPS

    @pl.when(c == 0)
    def _():
        full_s[0:8, :] = jnp.zeros((8, cdim), F32)
        st_s[...] = jnp.zeros(st_s.shape, F32)

    full_s[8:8 + Q, :] = xbc_ref[...]
    for j in range(0, cdim, 512):
        acc = cb_ref[:, j:j + 512] + full_s[8:8 + Q, j:j + 512] * cw_ref[3:4, j:j + 512]
        for i in range(SSM_CONV - 1):
            acc = acc + full_s[5 + i:5 + i + Q, j:j + 512] * cw_ref[i:i + 1, j:j + 512]
        act_s[:, j:j + 512] = _silu(acc)
    full_s[0:8, :] = full_s[Q:Q + 8, :]

    dt = _softplus(dtr_ref[...] + dtb_ref[...])
    row = lax.broadcasted_iota(I32, (Q, Q), 0)
    col = lax.broadcasted_iota(I32, (Q, Q), 1)
    tri = row >= col
    acs = jnp.dot(tri.astype(F32), dt * a_ref[...], precision=HIGHEST, preferred_element_type=F32)
    acs_t, dt_t = acs.T, dt.T
    last = acs[Q - 1:Q, :]
    wdt = jnp.exp(last - acs) * dt
    eacs = jnp.exp(acs)
    cdec = jnp.exp(last)
    low = lax.broadcasted_iota(I32, (Q, LANES), 1) < HEAD
    low1 = low[0:1, :]

    for g in range(SSM_GROUPS):
        bg = act_s[:, inner + g * N:inner + (g + 1) * N]
        cg = act_s[:, inner + SSM_GROUPS * N + g * N:inner + SSM_GROUPS * N + (g + 1) * N]
        cb = _bdot_nt(cg, bg)
        bg_t = bg.T.astype(BF16)
        for p in range(heads_per_group // 2):
            h0 = g * heads_per_group + 2 * p
            js = slice(h0 * HEAD, h0 * HEAD + LANES)
            xp = act_s[:, js]
            xp_bf = xp.astype(BF16)
            stp = st_s[:, js]
            stp_bf = stp.astype(BF16)
            ys = []
            for h in (h0, h0 + 1):
                seg = acs[:, h:h + 1] - acs_t[h:h + 1, :]
                decay = jnp.exp(jnp.where(tri, seg, -jnp.inf))
                m = (cb * decay) * dt_t[h:h + 1, :]
                ce = cg * eacs[:, h:h + 1]
                ys.append(_bdot(m, xp_bf) + _bdot(ce, stp_bf))
            y_s[:, js] = jnp.where(low, ys[0], ys[1])
            wcol = jnp.where(low, wdt[:, h0:h0 + 1], wdt[:, h0 + 1:h0 + 2])
            dst = jnp.dot(bg_t, (xp * wcol).astype(BF16), preferred_element_type=F32)
            cd = jnp.where(low1, cdec[:, h0:h0 + 1], cdec[:, h0 + 1:h0 + 2])
            st_s[:, js] = stp * cd + dst

    for g in range(SSM_GROUPS):
        gs = slice(g * gw, (g + 1) * gw)
        y = y_s[:, gs] + dsk_ref[:, gs] * act_s[:, gs]
        y = y * _silu(z_ref[:, gs])
        ms = jnp.mean(y * y, axis=-1, keepdims=True)
        y_ref[:, gs] = (y * lax.rsqrt(ms + NORM_EPS) * ng_ref[:, gs]).astype(BF16)

    @pl.when(c == pl.num_programs(1) - 1)
    def _():
        st_ref[...] = st_s[...].T


def _ssd_prompt(z, xbc, dtr, cw, cb, dtb, a, dsk, ng, nb):
    m, inner = z.shape
    cdim = xbc.shape[1]
    nc = m // nb // SSM_CHUNK
    row = lambda n: pl.BlockSpec((SSM_CHUNK, n), lambda b, c: (b * nc + c, 0))
    const = lambda shape: pl.BlockSpec(shape, lambda b, c: (0,) * len(shape))
    return pl.pallas_call(
        functools.partial(_ssd_body, inner), grid=(nb, nc),
        in_specs=[row(inner), row(cdim), row(LANES), const(cw.shape), const(cb.shape),
                  const(dtb.shape), const(a.shape), const(dsk.shape), const(ng.shape)],
        out_specs=[row(inner), pl.BlockSpec((None, inner, SSM_STATE), lambda b, c: (b, 0, 0))],
        out_shape=[jax.ShapeDtypeStruct((m, inner), BF16),
                   jax.ShapeDtypeStruct((nb, inner, SSM_STATE), F32)],
        scratch_shapes=[pltpu.VMEM((SSM_CHUNK + 8, cdim), F32), pltpu.VMEM((SSM_CHUNK, cdim), F32),
                        pltpu.VMEM((SSM_STATE, inner), F32), pltpu.VMEM((SSM_CHUNK, inner), F32)],
        compiler_params=_cparams(("parallel", "arbitrary"), 40), name="ssd_prompt",
    )(z, xbc, dtr, cw, cb, dtb, a, dsk, ng)


def _ssd_step_body(z_ref, xs_ref, bc_ref, cxs_ref, cbc_ref, dtr_ref, st_ref,
                   wxs_ref, wbc_ref, bxs_ref, bbc_ref, dtb_ref, a_ref, dsk_ref, ng_ref,
                   y_ref, sto_ref):
    G = SSM_GROUPS
    last = SSM_CONV - 1
    xs = bxs_ref[...] + xs_ref[...] * wxs_ref[last]
    bc = bbc_ref[...] + bc_ref[...] * wbc_ref[last]
    for i in range(last):
        xs = xs + cxs_ref[i] * wxs_ref[i]
        bc = bc + cbc_ref[i] * wbc_ref[i]
    xs, bc = _silu(xs), _silu(bc)
    dt = _softplus(dtr_ref[...] + dtb_ref[...])
    dec = jnp.exp(dt * a_ref[...])
    xdt = xs * dt
    npair = xs.shape[0]
    pairs_per_group = npair // G
    r = lax.broadcasted_iota(I32, (LANES, LANES), 0)
    cidx = lax.broadcasted_iota(I32, (LANES, LANES), 1)
    eye = (r == cidx).astype(F32)
    nt = (((1,), (1,)), ((), ()))
    dec_t = lax.dot_general(eye, dec, nt, precision=HIGHEST, preferred_element_type=F32)
    xdt_t = lax.dot_general(eye, xdt, nt, precision=HIGHEST, preferred_element_type=F32)
    rows = lax.broadcasted_iota(I32, (npair, 1), 0)
    cbv = jnp.sum(bc[0:G, :] * bc[G:2 * G, :], axis=-1, keepdims=True)
    cbx = jnp.zeros((npair, 1), F32)
    for g in range(G):
        cbx = cbx + jnp.where(rows // pairs_per_group == g, cbv[g:g + 1, :], 0.0)
    c_bf = bc.astype(BF16)
    yoff = jnp.zeros(xs.shape, F32)
    for j in range(npair):
        g = j // pairs_per_group
        s = st_ref[j * LANES:(j + 1) * LANES, :]
        sto_ref[j * LANES:(j + 1) * LANES, :] = s * dec_t[:, j:j + 1] + xdt_t[:, j:j + 1] * bc[g:g + 1, :]
        rj = _bdot_nt(c_bf, s)
        yoff = yoff + jnp.where(rows == j, rj[G + g:G + g + 1, :], 0.0)
    y = yoff * dec + cbx * dt * xs + dsk_ref[...] * xs
    y = y * _silu(z_ref[...])
    ssq = jnp.sum(y * y, axis=-1, keepdims=True)
    msx = jnp.zeros((npair, 1), F32)
    for g in range(G):
        ing = rows // pairs_per_group == g
        tot = jnp.sum(jnp.where(ing, ssq, 0.0), axis=0, keepdims=True)
        msx = msx + jnp.where(ing, tot, 0.0)
    msx = msx / (pairs_per_group * LANES)
    y_ref[...] = y * lax.rsqrt(msx + NORM_EPS) * ng_ref[...]


def _ssd_sample(z, xs, bc, cxs, cbc, dtr, st, wxs, wbc, bxs, bbc, dtb, a, dsk, ng):
    db, npair, _ = z.shape
    per_b = lambda shape: pl.BlockSpec((None,) + shape, lambda b: (b,) + (0,) * len(shape))
    const = lambda arr: pl.BlockSpec(arr.shape, lambda b: (0,) * arr.ndim)
    return pl.pallas_call(
        _ssd_step_body, grid=(db,),
        in_specs=[per_b(z.shape[1:]), per_b(xs.shape[1:]), per_b(bc.shape[1:]), per_b(cxs.shape[1:]),
                  per_b(cbc.shape[1:]), per_b(dtr.shape[1:]), per_b(st.shape[1:]),
                  const(wxs), const(wbc), const(bxs), const(bbc), const(dtb), const(a), const(dsk), const(ng)],
        out_specs=[per_b(z.shape[1:]), per_b(st.shape[1:])],
        out_shape=[jax.ShapeDtypeStruct(z.shape, F32), jax.ShapeDtypeStruct(st.shape, F32)],
        compiler_params=_cparams(("parallel",), 32), name="ssd_sample",
    )(z, xs, bc, cxs, cbc, dtr, st, wxs, wbc, bxs, bbc, dtb, a, dsk, ng)


def _row_fold(x, h, op=jnp.add):
    parts = [x[j:j + h, :] for j in range(0, x.shape[0], h)]
    while len(parts) > 1:
        parts = [op(a, b) for a, b in zip(parts[0::2], parts[1::2])] + (parts[-1:] if len(parts) % 2 else [])
    return parts[0]


def _topk_bias(score_s, img_s, bias_s, nblk, kb, topk, qpos, unroll=1, straight_counts=None):
    nq = score_s.shape[1]
    blk = lambda s: pl.ds(s * kb, kb) if isinstance(s, int) else pl.ds(pl.multiple_of(s * kb, kb), kb)
    one16, zero16 = jnp.int16(1), jnp.int16(0)
    work_s = bias_s

    def over_blocks(body, init):
        def trip(s2, carry):
            for k in range(unroll):
                carry = body(s2 * unroll + k, carry)
            return carry
        return lax.fori_loop(0, nblk // unroll, trip, init)

    def counted_blocks(body, init):
        if straight_counts is None:
            return over_blocks(body, init)

        def run(n):
            carry = init
            for s in range(n):
                carry = body(s, carry)
            return carry

        def pick(k):
            if k == len(straight_counts) - 1:
                return run(straight_counts[k])
            return lax.cond(nblk <= straight_counts[k], lambda: run(straight_counts[k]), lambda: pick(k + 1))
        return pick(0)

    def count(cmp):
        def body(s, acc):
            return acc + _row_fold(jnp.where(cmp(img_s[blk(s), :]), one16, zero16), 16)
        acc = counted_blocks(body, jnp.zeros((16, nq), I16))
        return jnp.sum(acc.astype(F32), axis=0, keepdims=True)

    def search(nbits, value_of):
        def step(t, u):
            code = u | lax.shift_left(jnp.int32(1), jnp.asarray(nbits - 1 - t, I32))
            c = value_of(code).astype(BF16)
            return jnp.where(count(lambda a: a >= c) >= topk, code, u)
        return lax.fori_loop(0, nbits, step, jnp.zeros((1, nq), I32))

    def bf16_value(code):
        pattern = jnp.where(code >= 32768, code - 32768, 65535 - code)
        return pltpu.bitcast(pattern << 16, jnp.float32).astype(F32)

    def set_image(fn):
        def body(s, carry):
            img_s[blk(s), :] = fn(s).astype(BF16)
            return carry
        over_blocks(body, 0)

    t1 = bf16_value(search(16, bf16_value))

    e1 = jnp.clip((pltpu.bitcast(t1.astype(jnp.float32), I32) >> 23) & 0xFF, 25, 254)
    unit = pltpu.bitcast((e1 - 24) << 23, jnp.float32).astype(F32)
    inv_unit = pltpu.bitcast((278 - e1) << 23, jnp.float32).astype(F32)
    B2, B1 = 65536.0, 256.0

    def digit2(s):
        y = (score_s[blk(s), :] - t1) * inv_unit
        work_s[blk(s), :] = y
        return jnp.floor(y * (1.0 / B2))
    set_image(digit2)
    t2 = (search(2, lambda code: (code - 1).astype(F32)) - 1).astype(F32)
    set_image(lambda s: jnp.floor((work_s[blk(s), :] - t2 * B2) * (1.0 / B1)))
    t3 = search(8, lambda code: code.astype(F32)).astype(F32)
    set_image(lambda s: jnp.floor(work_s[blk(s), :] - (t2 * B2 + t3 * B1)))
    t4 = search(8, lambda code: code.astype(F32)).astype(F32)
    v0 = t1 + (t2 * B2 + t3 * B1 + t4) * unit

    def smallest(keep):
        def body(s, acc):
            x = score_s[blk(s), :]
            return jnp.minimum(acc, _row_fold(jnp.where(keep(x), x, jnp.inf), 8, jnp.minimum))
        acc = over_blocks(body, jnp.full((8, nq), jnp.inf, F32))
        return jnp.min(acc, axis=0, keepdims=True)

    def count_above(v):
        def body(s, acc):
            return acc + _row_fold(jnp.where(score_s[blk(s), :] > v, 1.0, 0.0), 8)
        return jnp.sum(over_blocks(body, jnp.zeros((8, nq), F32)), axis=0, keepdims=True)

    def refine(carry):
        v, above = carry
        v = jnp.where(above >= topk, smallest(lambda x: x > v), v)
        return v, count_above(v)

    v = smallest(lambda x: x >= v0)
    v, above = lax.while_loop(lambda c: jnp.max(c[1]) >= topk, refine, (v, count_above(v)))
    need = topk - above

    r_i = lax.broadcasted_iota(I32, (LANES, LANES), 0)
    c_i = lax.broadcasted_iota(I32, (LANES, LANES), 1)
    lower = jnp.where(r_i >= c_i, 1.0, 0.0).astype(BF16)
    sub_iota = lax.broadcasted_iota(I32, (LANES, 1), 0)

    def bias_body(s, carry):
        for j in range(0, kb, LANES):
            off = pl.multiple_of(s * kb + j, LANES)
            x = score_s[pl.ds(off, LANES), :]
            eq = x == v
            eqf = jnp.where(eq, 1.0, 0.0)
            incl = jnp.dot(lower, eqf.astype(BF16), preferred_element_type=F32)
            tie = jnp.where(carry + incl - eqf < need, 0.0, -jnp.inf)
            b = jnp.where(x > v, 0.0, jnp.where(eq, tie, -jnp.inf))
            bias_s[pl.ds(off, LANES), :] = jnp.where((off + sub_iota) <= qpos, b, -jnp.inf)
            carry = carry + incl[LANES - 1:LANES, :]
        return carry

    over_blocks(bias_body, jnp.zeros((1, nq), F32))


def _dsa_t_body(topk, idx_scale, q_ref, qi_ref, sm_ref, ki2_ref, k_ref, vt_ref, o_ref,
                score_s, img_s, bias_s, qs_s, lga_s, lgb_s, m_s, acc_s):
    i = pl.program_id(1)
    QB, KB = Q_BLOCK, KEY_BLOCK
    nkb = (i * QB + QB + KB - 1) // KB
    qpos = i * QB + lax.broadcasted_iota(I32, (1, QB), 1)
    low = lax.broadcasted_iota(I32, (QB, LANES), 1) < HEAD
    zero_bf = jnp.zeros((QB, LANES), BF16)
    blk = lambda s: pl.ds(pl.multiple_of(s * KB, KB), KB)

    qi = qi_ref[...]
    sm_t = sm_ref[...].T
    qh, wh = [], []
    for h in range(IDX_HEADS):
        chunk = qi[:, (h // 2) * LANES:(h // 2 + 1) * LANES]
        qh.append(jnp.where(low if h % 2 == 0 else ~low, chunk, zero_bf))
        wh.append(sm_t[HEAD + h:HEAD + h + 1, :] * idx_scale)

    q_stack = jnp.concatenate(qh, axis=0)

    last_key_block = ki2_ref.shape[0] // KB - 1

    def score_pair(s2, carry):
        blocks = (2 * s2, 2 * s2 + 1)
        dots = [_bdot_nt(ki2_ref[blk(jnp.minimum(s, last_key_block)), :], q_stack) for s in blocks]
        for s, sc in zip(blocks, dots):
            acc = jnp.zeros((KB, QB), F32)
            for h in range(IDX_HEADS):
                acc = acc + wh[h] * jnp.maximum(sc[:, h * QB:(h + 1) * QB], 0.0)
            kpos = s * KB + lax.broadcasted_iota(I32, (KB, 1), 0)
            sc = jnp.where(kpos <= qpos, jnp.maximum(acc, MASKED_SCORE), MASKED_SCORE)
            score_s[blk(s), :] = sc
            img_s[blk(s), :] = sc.astype(BF16)
        return carry

    nkb_even = 2 * ((nkb + 1) // 2)
    lax.fori_loop(0, nkb_even // 2, score_pair, 0)

    @pl.when(nkb_even % 4 == 2)
    def _():
        for k in range(2):
            img_s[blk(nkb_even + k), :] = jnp.full((KB, QB), MASKED_SCORE, BF16)

    _topk_bias(score_s, img_s, bias_s, nkb_even, KB, topk, qpos, unroll=2,
               straight_counts=tuple(range(4, img_s.shape[0] // KB + 1, 4)))

    q = q_ref[...]
    nchunk = q.shape[1] // LANES
    per_kv_chunk = nchunk // (ATTN_KV_HEADS // 2)
    nstack = 2 * per_kv_chunk
    srows = nstack * QB
    for cj in range(nchunk):
        kvc, j = divmod(cj, per_kv_chunk)
        qc = q[:, cj * LANES:(cj + 1) * LANES]
        for half in range(2):
            r0 = (kvc * nstack + half * per_kv_chunk + j) * QB
            qs_s[r0:r0 + QB, :] = jnp.where(low if half == 0 else ~low, qc, zero_bf)

    ones_rows = jnp.ones((ONES_ROWS, KB), BF16)
    cols = lambda r: slice(r * QB, (r + 1) * QB)
    n_kvc = ATTN_KV_HEADS // 2
    nsteps = n_kvc * nkb

    def step_of(t):
        t = jnp.minimum(t, nsteps - 1)
        kvc = (t >= nkb).astype(I32)
        return kvc, t - kvc * nkb

    def logits_to(dst, t):
        kvc, s = step_of(t)
        kb = k_ref[blk(s), pl.ds(pl.multiple_of(kvc * LANES, LANES), LANES)]
        bias = bias_s[blk(s), :]
        lg = _bdot_nt(kb, qs_s[pl.ds(pl.multiple_of(kvc * srows, srows), srows), :])
        for r in range(nstack):
            dst[:, cols(r)] = lg[:, cols(r)] + bias

    def consume(src, t):
        kvc, s = step_of(t)
        vt = vt_ref[pl.ds(pl.multiple_of(kvc * LANES, LANES), LANES), blk(s)]
        ps, alphas = [], []
        for r in range(nstack):
            lg = src[:, cols(r)]
            m = m_s[kvc, :, cols(r)]
            mn = jnp.maximum(m, jnp.max(lg, axis=0, keepdims=True))
            m_s[kvc, :, cols(r)] = mn
            ps.append(jnp.exp2(lg - mn).astype(BF16))
            alphas.append(jnp.exp2(m - mn))
        for half in range(2):
            hs = slice(half * per_kv_chunk, (half + 1) * per_kv_chunk)
            hc = slice(half * per_kv_chunk * QB, (half + 1) * per_kv_chunk * QB)
            v_aug = jnp.concatenate([vt[half * HEAD:(half + 1) * HEAD, :], ones_rows], axis=0)
            pv = jnp.dot(v_aug, jnp.concatenate(ps[hs], axis=1), preferred_element_type=F32)
            acc_s[kvc, :, hc] = jnp.concatenate(alphas[hs], axis=1) * acc_s[kvc, :, hc] + pv

    m_s[...] = jnp.full(m_s.shape, NEG_BIG, F32)
    acc_s[...] = jnp.zeros(acc_s.shape, F32)
    logits_to(lga_s, 0)

    def run_pairs(t0, npairs):
        for p in range(npairs):
            logits_to(lgb_s, t0 + 2 * p + 1)
            consume(lga_s, t0 + 2 * p)
            logits_to(lga_s, t0 + 2 * p + 2)
            consume(lgb_s, t0 + 2 * p + 1)

    def quad_body(i4, carry):
        run_pairs(4 * i4, 2)
        return carry

    lax.fori_loop(0, nsteps // 4, quad_body, 0)

    @pl.when(nsteps % 4 == 2)
    def _():
        run_pairs(nsteps - 2, 1)

    for kvc in range(n_kvc):
        acc = acc_s[kvc]
        for j in range(per_kv_chunk):
            a = acc[:, cols(j)]
            b = acc[:, cols(per_kv_chunk + j)]
            chunk_t = jnp.concatenate([a[0:HEAD, :] / a[HEAD:HEAD + 1, :], b[0:HEAD, :] / b[HEAD:HEAD + 1, :]],
                                      axis=0)
            cj = kvc * per_kv_chunk + j
            o_ref[:, cj * LANES:(cj + 1) * LANES] = chunk_t.T.astype(BF16)


def _dsa_prompt_t(q, qi, sm, ki2, k, vt, nb, topk):
    m, aw = q.shape
    t = m // nb
    nq = t // Q_BLOCK
    tpad = -(-t // (4 * KEY_BLOCK)) * 4 * KEY_BLOCK
    srows = aw // HEAD // (ATTN_KV_HEADS // 2) * Q_BLOCK
    idx_scale = IDX_HEADS ** -0.5 * HEAD ** -0.5
    row = lambda n: pl.BlockSpec((Q_BLOCK, n), lambda b, i: (b * nq + i, 0))
    per_b = lambda n: pl.BlockSpec((t, n), lambda b, i: (b, 0))
    return pl.pallas_call(
        functools.partial(_dsa_t_body, topk, idx_scale), grid=(nb, nq),
        in_specs=[row(aw), row(qi.shape[1]), row(LANES), per_b(LANES), per_b(k.shape[1]),
                  pl.BlockSpec((None,) + vt.shape[1:], lambda b, i: (b, 0, 0))],
        out_specs=row(aw),
        out_shape=jax.ShapeDtypeStruct((m, aw), BF16),
        scratch_shapes=[pltpu.VMEM((tpad, Q_BLOCK), F32), pltpu.VMEM((tpad, Q_BLOCK), BF16),
                        pltpu.VMEM((tpad, Q_BLOCK), F32),
                        pltpu.VMEM((2 * srows, LANES), BF16),
                        pltpu.VMEM((KEY_BLOCK, srows), F32), pltpu.VMEM((KEY_BLOCK, srows), F32),
                        pltpu.VMEM((ATTN_KV_HEADS // 2, 1, srows), F32),
                        pltpu.VMEM((ATTN_KV_HEADS // 2, HEAD + ONES_ROWS, srows), F32)],
        compiler_params=_cparams(("parallel", "arbitrary"), 52), name="dsa_prompt",
    )(q, qi, sm, ki2, k, vt)


def _page_specs(block, npages):
    def make(u):
        return pl.BlockSpec((None,) + block, lambda b, pt: (pt[b, u],) + (0,) * len(block))
    return [make(u) for u in range(npages)]


def _idx_score_body(pt_ref, qi_ref, w_ref, *refs):
    pages, o_ref = refs[:-1], refs[-1]
    qi = qi_ref[...]
    w = w_ref[...]
    for u, page in enumerate(pages):
        s = jnp.maximum(_bdot(qi, page[...]), 0.0)
        o_ref[:, u * PAGE_SIZE:(u + 1) * PAGE_SIZE] = jnp.sum(w * s, axis=0, keepdims=True)


def _idx_scores_sample(page_table, qi8, w8, kidx_t):
    db, npages = page_table.shape
    grid_spec = pltpu.PrefetchScalarGridSpec(
        num_scalar_prefetch=1, grid=(db,),
        in_specs=[pl.BlockSpec((None,) + qi8.shape[1:], lambda b, pt: (b, 0, 0)),
                  pl.BlockSpec((None,) + w8.shape[1:], lambda b, pt: (b, 0, 0))]
                 + _page_specs(kidx_t.shape[1:], npages),
        out_specs=pl.BlockSpec((None, 1, npages * PAGE_SIZE), lambda b, pt: (b, 0, 0)))
    return pl.pallas_call(
        _idx_score_body, grid_spec=grid_spec,
        out_shape=jax.ShapeDtypeStruct((db, 1, npages * PAGE_SIZE), F32),
        compiler_params=_cparams(("parallel",), 32), name="idx_scores_sample",
    )(page_table, qi8, w8, *([kidx_t] * npages))


def _select_sample_body(topk, idx_scale, past, sc_ref, qi_ref, sm_ref, bias_ref, score_s, img_s, bias_s):
    rows = sc_ref.shape[0]
    sm = sm_ref[...]
    qi = qi_ref[...]
    ki = sm[:, 0:HEAD]
    new = jnp.zeros((rows, 1), F32)
    for h in range(IDX_HEADS):
        d = jnp.sum(qi[:, h * HEAD:(h + 1) * HEAD] * ki, axis=-1, keepdims=True)
        new = new + (sm[:, HEAD + h:HEAD + h + 1] * idx_scale) * jnp.maximum(d, 0.0)
    nblk = (past + LANES) // LANES
    lane = lax.broadcasted_iota(I32, (rows, LANES), 1)
    for j in range(nblk):
        js = slice(j * LANES, (j + 1) * LANES)
        sc = sc_ref[:, js] if j < nblk - 1 else jnp.where(lane == 0, new, MASKED_SCORE)
        sc = jnp.maximum(sc, MASKED_SCORE).T
        score_s[js, :] = sc
        img_s[js, :] = sc.astype(BF16)
    qpos = jnp.full((1, rows), past, I32)
    _topk_bias(score_s, img_s, bias_s, nblk, LANES, topk, qpos)
    for j in range(nblk):
        js = slice(j * LANES, (j + 1) * LANES)
        bias_ref[:, js] = bias_s[js, :].T


def _select_sample(scores, qi, sm, topk, idx_scale):
    db, past = scores.shape
    full = lambda a: pl.BlockSpec(a.shape, lambda i: (0,) * a.ndim)
    keys = past + LANES
    return pl.pallas_call(
        functools.partial(_select_sample_body, topk, idx_scale, past), grid=(1,),
        in_specs=[full(scores), full(qi), full(sm)],
        out_specs=pl.BlockSpec((db, keys), lambda i: (0, 0)),
        out_shape=jax.ShapeDtypeStruct((db, keys), F32),
        scratch_shapes=[pltpu.VMEM((keys, db), F32), pltpu.VMEM((keys, db), BF16), pltpu.VMEM((keys, db), F32)],
        compiler_params=_cparams(("arbitrary",), 32), name="select_sample")(scores, qi, sm)


def _attend_sample_body(npages, pt_ref, q_ref, bias_ref, knew_ref, vnew_ref, *refs):
    kpages, vpages, o_ref = refs[:npages], refs[npages:2 * npages], refs[2 * npages]
    G = ATTN_KV_HEADS
    q = q_ref[...]
    q_bf = q.astype(BF16)
    nh = q.shape[0]
    past = npages * PAGE_SIZE
    group = lax.broadcasted_iota(I32, (nh, 1), 0) // (nh // G)

    def by_group(parts):
        out = parts[G - 1]
        for g in range(G - 2, -1, -1):
            out = jnp.where(group == g, parts[g], out)
        return out

    lg = jnp.concatenate(
        [by_group([_bdot(q_bf, kpages[u][g]) for g in range(G)]) for u in range(npages)], axis=1)
    lg = lg + bias_ref[:, 0:past]
    lg_new = by_group([jnp.sum(q * knew_ref[g:g + 1, :], axis=-1, keepdims=True) for g in range(G)])
    lg_new = lg_new + bias_ref[:, past:past + 1]
    m = jnp.maximum(jnp.max(lg, axis=-1, keepdims=True), lg_new)
    p = jnp.exp2(lg - m)
    p_new = jnp.exp2(lg_new - m)
    denom = jnp.sum(p, axis=-1, keepdims=True) + p_new
    p_bf = p.astype(BF16)
    accs = [p_new * vnew_ref[g:g + 1, :] for g in range(G)]
    for u in range(npages):
        pu = p_bf[:, u * PAGE_SIZE:(u + 1) * PAGE_SIZE]
        for g in range(G):
            accs[g] = accs[g] + _bdot_nt(pu, vpages[u][g])
    o_ref[...] = by_group(accs) / denom


def _attend_sample(page_table, q, bias, knew, vnew, k_t, v_t):
    db, npages = page_table.shape
    per_b = lambda a: pl.BlockSpec((None,) + a.shape[1:], lambda b, pt: (b,) + (0,) * (a.ndim - 1))
    grid_spec = pltpu.PrefetchScalarGridSpec(
        num_scalar_prefetch=1, grid=(db,),
        in_specs=[per_b(q), per_b(bias), per_b(knew), per_b(vnew)]
                 + _page_specs(k_t.shape[1:], npages) + _page_specs(v_t.shape[1:], npages),
        out_specs=per_b(q))
    return pl.pallas_call(
        functools.partial(_attend_sample_body, npages), grid_spec=grid_spec,
        out_shape=jax.ShapeDtypeStruct(q.shape, F32),
        compiler_params=_cparams(("parallel",), 48), name="attend_sample",
    )(page_table, q, bias, knew, vnew, *([k_t] * npages), *([v_t] * npages))


def _merge_body(x_ref, mod_ref, ys_ref, ya_ref, gs_ref, ga_ref, wps_ref, wpa_ref, wo_ref, o_ref):
    d = x_ref.shape[1]
    merged = (jax.nn.sigmoid(gs_ref[...]) * _bdot(ys_ref[...], wps_ref[...])
              + jax.nn.sigmoid(ga_ref[...]) * _bdot(ya_ref[...], wpa_ref[...]))
    o_ref[...] = x_ref[...] + mod_ref[:, 2 * d:3 * d] * _bdot(merged, wo_ref[...])


def _merge(x, mod3, ys, ya, gs, ga, wps, wpa, wo, tm, tpb):
    m, d = x.shape
    x_spec, mod_spec = _row_specs(tm, tpb, d, mod3.shape[1])
    row = lambda n: pl.BlockSpec((tm, n), lambda i: (i, 0))
    return pl.pallas_call(
        _merge_body, grid=(m // tm,),
        in_specs=[x_spec, mod_spec, row(ys.shape[1]), row(ya.shape[1]), row(d), row(d),
                  _const_spec(wps.shape), _const_spec(wpa.shape), _const_spec(wo.shape)],
        out_specs=row(d), out_shape=jax.ShapeDtypeStruct((m, d), F32),
        compiler_params=_cparams(("parallel",), 48), name="merge")(x, mod3, ys, ya, gs, ga, wps, wpa, wo)


def _ffn_body(last_layer, x_ref, mod_ref, g_ref, fg_ref, wg_ref, wu_ref, wo_ref, o_ref):
    d = x_ref.shape[1]
    x = x_ref[...]
    h = _norm_mod(x, g_ref[...], mod_ref[:, 4 * d:5 * d], mod_ref[:, 3 * d:4 * d]).astype(BF16)
    gate = jnp.dot(h, wg_ref[...], preferred_element_type=F32)
    up = jnp.dot(h, wu_ref[...], preferred_element_type=F32)
    x2 = x + mod_ref[:, 5 * d:6 * d] * _bdot(_silu(gate) * up, wo_ref[...])
    if last_layer:
        ms = jnp.mean(x2 * x2, axis=-1, keepdims=True)
        x2 = x2 * lax.rsqrt(ms + NORM_EPS) * fg_ref[...]
    o_ref[...] = x2


def _ffn(x, mod3, g, fg, wg, wu, wo, tm, tpb, last_layer):
    m, d = x.shape
    x_spec, mod_spec = _row_specs(tm, tpb, d, mod3.shape[1])
    return pl.pallas_call(
        functools.partial(_ffn_body, last_layer), grid=(m // tm,),
        in_specs=[x_spec, mod_spec, _const_spec((1, d)), _const_spec((1, d)),
                  _const_spec(wg.shape), _const_spec(wu.shape), _const_spec(wo.shape)],
        out_specs=pl.BlockSpec((tm, d), lambda i: (i, 0)), out_shape=jax.ShapeDtypeStruct((m, d), F32),
        compiler_params=_cparams(("parallel",), 56), name="ffn")(x, mod3, g, fg, wg, wu, wo)


def _rope_tables(pos):
    half = HEAD // 2
    inv = ROPE_THETA ** (-jnp.arange(half, dtype=F32) / half)
    ang = pos.astype(F32)[:, None] * inv[None, :]
    cos = jnp.tile(jnp.cos(ang), (1, LANES // half))
    sin = jnp.tile(jnp.sin(ang), (1, LANES // half))
    first = (jnp.arange(LANES) % HEAD) < half
    return cos, jnp.where(first, -sin, 0.0), jnp.where(first, 0.0, sin)


def _q_head_order(n_heads):
    rep = n_heads // ATTN_KV_HEADS
    order = []
    for c in range(ATTN_KV_HEADS // 2):
        for j in range(rep):
            order += [2 * c * rep + j, (2 * c + 1) * rep + j]
    return np.asarray(order)


def kernel(x_prompt, x_sample, cache_k, cache_v, cache_kidx, state_conv, state_ssm, page_table, c_prompt, c_sample, w_ada, b_ada, norm1_g, w_in, conv_w, conv_b, dt_bias, a_log, d_skip, ssm_norm_g, w_proj_ssm, w_proj_attn, w_out, norm2_g, w_ffn_in, w_ffn_out, final_g):
    nb, t, d = x_prompt.shape
    db, ds, _ = x_sample.shape
    depth = w_in.shape[0]
    assert ds == 1 and t % Q_BLOCK == 0 and t % SSM_CHUNK == 0
    n_heads_ssm = dt_bias.shape[1]
    inner = n_heads_ssm * HEAD
    gn = SSM_GROUPS * SSM_STATE
    cdim = inner + 2 * gn
    kvw = ATTN_KV_HEADS * HEAD
    aw = w_proj_attn.shape[1]
    n_heads = aw // HEAD
    iw = IDX_HEADS * HEAD
    ffn_hidden = w_ffn_out.shape[1]
    npages = page_table.shape[1]
    past = npages * PAGE_SIZE
    topk_p = min(TOPK_MAX, t // 4)
    topk_s = min(TOPK_MAX, (past + ds) // 4)
    assert past + ds >= topk_s
    idx_scale = IDX_HEADS ** -0.5 * HEAD ** -0.5
    tm_p = 256 if t % 256 == 0 else 128
    tpb_p = t // tm_p

    splits = np.cumsum([inner, inner, gn, gn, n_heads_ssm, aw, kvw, kvw, iw, HEAD, IDX_HEADS, d])
    order = _q_head_order(n_heads)
    inv_order = np.argsort(order)

    cos_p, slo_p, shi_p = _rope_tables(jnp.arange(t, dtype=I32))
    tabs_p = (cos_p, slo_p, shi_p)
    tabs_s = tuple(jnp.broadcast_to(a, (db, LANES)) for a in _rope_tables(past + jnp.arange(ds, dtype=I32)))

    rows_c = nb + db
    c_all = jnp.concatenate([c_prompt, c_sample, jnp.zeros((-rows_c % 8, d), F32)], axis=0)

    yp = x_prompt.reshape(nb * t, d)
    ys = x_sample.reshape(db, d)
    outs_p, outs_s = [], []
    for l in range(depth):
        (wz, wxs, wbm, wcm, wdt, wq, wk, wv, wqi, wki, wwi, wgs, wga) = jnp.split(w_in[l], splits, axis=1)
        w_ssm = jnp.concatenate([wz, wxs, wbm, wcm, wdt, jnp.zeros((d, LANES - n_heads_ssm), F32)],
                                axis=1).astype(BF16)
        wq_perm = wq.reshape(d, n_heads, HEAD)[:, order].reshape(d, aw)
        w_attn = jnp.concatenate([wq_perm, wk, wv, wqi, wgs, wga, wki, wwi,
                                  jnp.zeros((d, LANES - HEAD - IDX_HEADS), F32)], axis=1).astype(BF16)
        wps = w_proj_ssm[l].astype(BF16)
        wpa = w_proj_attn[l].reshape(n_heads, HEAD, d)[order].reshape(aw, d).astype(BF16)
        wo = w_out[l].astype(BF16)
        wg = w_ffn_in[l][:, :ffn_hidden].astype(BF16)
        wu = w_ffn_in[l][:, ffn_hidden:].astype(BF16)
        wfo = w_ffn_out[l].astype(BF16)
        g1 = norm1_g[l][None, :]
        g2 = norm2_g[l][None, :]
        a_neg = -jnp.exp(a_log[l])
        pad_h = LANES - n_heads_ssm
        dtb_row = jnp.pad(dt_bias[l], (0, pad_h))[None, :]
        a_row = jnp.pad(a_neg, (0, pad_h))[None, :]
        dsk_row = jnp.repeat(d_skip[l], HEAD)[None, :]
        ng_row = ssm_norm_g[l][None, :]
        cw = conv_w[l]
        cb = conv_b[l][None, :]

        mod = _ada(c_all, w_ada[l].astype(BF16), b_ada[l][None, :])
        mod_p = mod[:nb][:, None, :]
        mod_s = mod[nb:nb + db][None]

        z, xbc, dtr = _inproj_ssm(yp, mod_p, g1, w_ssm, tm_p, tpb_p, inner, cdim)
        (q_bf, k_bf, kt, vt, vt_bf, qi_bf, gs, ga, sm, ki2, kit) = _inproj_attn(
            yp, mod_p, g1, w_attn, tabs_p, tm_p, tpb_p, aw, kvw, iw)
        y_ssm, st = _ssd_prompt(z, xbc, dtr, cw, cb, dtb_row, a_row, dsk_row, ng_row, nb)
        y_attn = _dsa_prompt_t(q_bf, qi_bf, sm, ki2, k_bf, vt_bf, nb, topk_p)
        x1 = _merge(yp, mod_p, y_ssm, y_attn, gs, ga, wps, wpa, wo, tm_p, tpb_p)
        yp_next = _ffn(x1, mod_p, g2, final_g[None, :], wg, wu, wfo, tm_p, tpb_p, l == depth - 1)
        heads_last = lambda a: jnp.transpose(a.reshape(a.shape[0], ATTN_KV_HEADS, HEAD, a.shape[2]), (0, 3, 1, 2))
        outs_p.append((heads_last(kt), heads_last(vt), jnp.transpose(kit, (0, 2, 1)),
                       xbc.reshape(nb, t, cdim)[:, t - (SSM_CONV - 1):],
                       st.reshape(nb, n_heads_ssm, HEAD, SSM_STATE)))

        z_s, xbc_s, dtr_s = _inproj_ssm(ys, mod_s, g1, w_ssm, db, 1, inner, cdim)
        (q_s, _, kt_s, vt_s, _, qi_s, gs_s, ga_s, sm_s, _, kit_s) = _inproj_attn(
            ys, mod_s, g1, w_attn, tabs_s, db, 1, aw, kvw, iw)
        k_s, v_s = heads_last(kt_s)[0], heads_last(vt_s)[0]
        npair = inner // LANES
        nbc = 2 * gn // LANES
        sc = state_conv[l]
        y_ssm_s, st_s = _ssd_sample(
            z_s.reshape(db, npair, LANES), xbc_s[:, :inner].reshape(db, npair, LANES),
            xbc_s[:, inner:].reshape(db, nbc, LANES),
            sc[:, :, :inner].reshape(db, SSM_CONV - 1, npair, LANES),
            sc[:, :, inner:].reshape(db, SSM_CONV - 1, nbc, LANES),
            jnp.repeat(dtr_s[:, :n_heads_ssm], HEAD, axis=1).reshape(db, npair, LANES),
            state_ssm[l].reshape(db, inner, SSM_STATE),
            cw[:, :inner].reshape(SSM_CONV, npair, LANES), cw[:, inner:].reshape(SSM_CONV, nbc, LANES),
            cb[:, :inner].reshape(npair, LANES), cb[:, inner:].reshape(nbc, LANES),
            jnp.repeat(dt_bias[l], HEAD).reshape(npair, LANES), jnp.repeat(a_neg, HEAD).reshape(npair, LANES),
            dsk_row.reshape(npair, LANES), ng_row.reshape(npair, LANES))

        qi_f = qi_s.astype(F32)
        qi8 = jnp.pad(qi_f.reshape(db, IDX_HEADS, HEAD), ((0, 0), (0, 8 - IDX_HEADS), (0, 0)))
        w8 = jnp.broadcast_to(jnp.pad(sm_s[:, HEAD:HEAD + IDX_HEADS] * idx_scale,
                                      ((0, 0), (0, 8 - IDX_HEADS)))[:, :, None], (db, 8, LANES))
        scores = _idx_scores_sample(page_table, qi8, w8, jnp.transpose(cache_kidx[l], (0, 2, 1)))
        bias = _select_sample(scores.reshape(db, past), qi_f, sm_s, topk_s, idx_scale)
        q_orig = q_s.astype(F32).reshape(db, n_heads, HEAD)[:, inv_order]
        att = _attend_sample(page_table, q_orig, bias[:, None, :],
                             k_s, v_s,
                             jnp.transpose(cache_k[l], (0, 2, 3, 1)), jnp.transpose(cache_v[l], (0, 2, 3, 1)))
        y_attn_s = att[:, order].reshape(db, aw)
        x1_s = _merge(ys, mod_s, y_ssm_s.reshape(db, inner), y_attn_s, gs_s, ga_s, wps, wpa, wo, db, 1)
        ys_next = _ffn(x1_s, mod_s, g2, final_g[None, :], wg, wu, wfo, db, 1, l == depth - 1)
        outs_s.append((k_s[:, None], v_s[:, None], jnp.transpose(kit_s, (2, 0, 1)),
                       jnp.concatenate([sc[:, 1:], xbc_s[:, None, :]], axis=1),
                       st_s.reshape(db, n_heads_ssm, HEAD, SSM_STATE)))
        yp, ys = yp_next, ys_next

    stack = lambda outs, i: jnp.stack([o[i] for o in outs], axis=0)
    return (yp.reshape(nb, t, d), ys.reshape(db, ds, d),
            stack(outs_p, 0), stack(outs_p, 1), stack(outs_p, 2), stack(outs_p, 3), stack(outs_p, 4),
            stack(outs_s, 0), stack(outs_s, 1), stack(outs_s, 2), stack(outs_s, 3), stack(outs_s, 4))
```

```python
import functools

import jax
import jax.numpy as jnp
import numpy as np
from jax import lax
from jax.experimental import pallas as pl
from jax.experimental.pallas import tpu as pltpu

F32, BF16, I32, I16 = jnp.float32, jnp.bfloat16, jnp.int32, jnp.int16
HIGHEST = lax.Precision.HIGHEST

LANES = 128
HEAD = 64
SSM_STATE = 128
SSM_GROUPS = 4
SSM_CONV = 4
SSM_CHUNK = 128
ATTN_KV_HEADS = 4
IDX_HEADS = 4
TOPK_MAX = 256
Q_BLOCK = 128
PAGE_SIZE = 128
ROPE_THETA = 10000.0
NORM_EPS = 1e-6
KEY_BLOCK = 512
ONES_ROWS = 16
Q_SCALE = HEAD ** -0.5 * 1.4426950408889634
MASKED_SCORE = -3.3895313892515355e38
NEG_BIG = -1e30


def _cparams(sem, vmem_mb):
    return pltpu.CompilerParams(dimension_semantics=sem, vmem_limit_bytes=vmem_mb << 20)


def _bdot(a, b):
    return jnp.dot(a.astype(BF16), b.astype(BF16), preferred_element_type=F32)


def _bdot_nt(a, b):
    return lax.dot_general(a.astype(BF16), b.astype(BF16), (((1,), (1,)), ((), ())),
                           preferred_element_type=F32)


def _silu(x):
    h = 0.5 * x
    return h + h * jnp.tanh(h)


def _softplus(x):
    return jnp.maximum(x, 0.0) + jnp.log(1.0 + jnp.exp(-jnp.abs(x)))


def _norm_mod(x, g, scale, shift):
    ms = jnp.mean(x * x, axis=-1, keepdims=True)
    return (x * lax.rsqrt(ms + NORM_EPS) * g) * (1.0 + scale) + shift


def _rope128(x, cos, sin_lo, sin_hi):
    return x * cos + pltpu.roll(x, 96, 1) * sin_lo + pltpu.roll(x, 32, 1) * sin_hi


def _rope_wide(x, cos, sin_lo, sin_hi):
    parts = [_rope128(x[:, j:j + LANES], cos, sin_lo, sin_hi) for j in range(0, x.shape[1], LANES)]
    return parts[0] if len(parts) == 1 else jnp.concatenate(parts, axis=1)


def _ada_body(c_ref, w_ref, b_ref, o_ref):
    o_ref[...] = _bdot(_silu(c_ref[...]), w_ref[...]) + b_ref[...]


def _ada(c_all, w_bf, b):
    mp, d = c_all.shape
    n = w_bf.shape[1]
    tn = n // 4
    return pl.pallas_call(
        _ada_body, grid=(n // tn,),
        in_specs=[pl.BlockSpec((mp, d), lambda j: (0, 0)),
                  pl.BlockSpec((d, tn), lambda j: (0, j)),
                  pl.BlockSpec((1, tn), lambda j: (0, j))],
        out_specs=pl.BlockSpec((mp, tn), lambda j: (0, j)),
        out_shape=jax.ShapeDtypeStruct((mp, n), F32),
        compiler_params=_cparams(("arbitrary",), 32), name="ada")(c_all, w_bf, b)


def _inproj_ssm_body(inner, cdim, x_ref, mod_ref, g_ref, w_ref, z_ref, xbc_ref, dt_ref):
    d = x_ref.shape[1]
    h = _norm_mod(x_ref[...], g_ref[...], mod_ref[:, d:2 * d], mod_ref[:, 0:d]).astype(BF16)
    z_ref[...] = jnp.dot(h, w_ref[:, 0:inner], preferred_element_type=F32)
    xbc_ref[...] = jnp.dot(h, w_ref[:, inner:inner + cdim], preferred_element_type=F32)
    dt_ref[...] = jnp.dot(h, w_ref[:, inner + cdim:inner + cdim + LANES], preferred_element_type=F32)


def _row_specs(tm, tpb, d, mod_rows):
    x_spec = pl.BlockSpec((tm, d), lambda m: (m, 0))
    mod_spec = pl.BlockSpec((None, mod_rows, 6 * d), lambda m: (m // tpb, 0, 0))
    return x_spec, mod_spec


def _const_spec(shape):
    return pl.BlockSpec(shape, lambda m: (0,) * len(shape))


def _inproj_ssm(x, mod3, g, w_bf, tm, tpb, inner, cdim):
    m, d = x.shape
    x_spec, mod_spec = _row_specs(tm, tpb, d, mod3.shape[1])
    row = lambda n: pl.BlockSpec((tm, n), lambda i: (i, 0))
    return pl.pallas_call(
        functools.partial(_inproj_ssm_body, inner, cdim), grid=(m // tm,),
        in_specs=[x_spec, mod_spec, _const_spec((1, d)), _const_spec(w_bf.shape)],
        out_specs=[row(inner), row(cdim), row(LANES)],
        out_shape=[jax.ShapeDtypeStruct((m, inner), F32), jax.ShapeDtypeStruct((m, cdim), F32),
                   jax.ShapeDtypeStruct((m, LANES), F32)],
        compiler_params=_cparams(("parallel",), 52), name="inproj_ssm")(x, mod3, g, w_bf)


def _inproj_attn_body(aw, kvw, iw, x_ref, mod_ref, g_ref, w_ref, cos_ref, slo_ref, shi_ref,
                      q_ref, kb_ref, kt_ref, vt_ref, vtb_ref, qi_ref, gs_ref, ga_ref, sm_ref, ki2_ref, kit_ref):
    d = x_ref.shape[1]
    h = _norm_mod(x_ref[...], g_ref[...], mod_ref[:, d:2 * d], mod_ref[:, 0:d]).astype(BF16)
    cos, slo, shi = cos_ref[...], slo_ref[...], shi_ref[...]

    def proj(a, b):
        return jnp.dot(h, w_ref[:, a:b], preferred_element_type=F32)

    o = 0
    q_ref[...] = (_rope_wide(proj(o, o + aw), cos, slo, shi) * Q_SCALE).astype(BF16)
    o += aw
    k = _rope_wide(proj(o, o + kvw), cos, slo, shi)
    kb_ref[...] = k.astype(BF16)
    kt_ref[...] = k.T
    o += kvw
    vt = proj(o, o + kvw).T
    vt_ref[...] = vt
    vtb_ref[...] = vt.astype(BF16)
    o += kvw
    qi_ref[...] = _rope_wide(proj(o, o + iw), cos, slo, shi).astype(BF16)
    o += iw
    gs_ref[...] = proj(o, o + d)
    o += d
    ga_ref[...] = proj(o, o + d)
    o += d
    s = proj(o, o + LANES)
    lane = lax.broadcasted_iota(I32, s.shape, 1)
    sm = jnp.where(lane < HEAD, _rope128(s, cos, slo, shi), s)
    sm_ref[...] = sm
    ki2_ref[...] = jnp.where(lane < HEAD, sm, pltpu.roll(sm, HEAD, 1)).astype(BF16)
    kit_ref[...] = sm.T[0:HEAD, :]


def _inproj_attn(x, mod3, g, w_bf, tabs, tm, tpb, aw, kvw, iw):
    m, d = x.shape
    x_spec, mod_spec = _row_specs(tm, tpb, d, mod3.shape[1])
    ntab = tabs[0].shape[0] // tm
    nb = m // (tm * tpb)
    tab_spec = pl.BlockSpec((tm, LANES), lambda i: (i % ntab, 0))
    row = lambda n, dt: (pl.BlockSpec((tm, n), lambda i: (i, 0)), jax.ShapeDtypeStruct((m, n), dt))
    tmin = lambda n, dt: (pl.BlockSpec((None, n, tm), lambda i: (i // tpb, 0, i % tpb)),
                          jax.ShapeDtypeStruct((nb, n, tm * tpb), dt))
    outs = [row(aw, BF16), row(kvw, BF16), tmin(kvw, F32), tmin(kvw, F32), tmin(kvw, BF16), row(iw, BF16),
            row(d, F32), row(d, F32), row(LANES, F32), row(LANES, BF16), tmin(HEAD, F32)]
    return pl.pallas_call(
        functools.partial(_inproj_attn_body, aw, kvw, iw), grid=(m // tm,),
        in_specs=[x_spec, mod_spec, _const_spec((1, d)), _const_spec(w_bf.shape),
                  tab_spec, tab_spec, tab_spec],
        out_specs=[spec for spec, _ in outs],
        out_shape=[shape for _, shape in outs],
        compiler_params=_cparams(("parallel",), 52), name="inproj_attn")(x, mod3, g, w_bf, *tabs)


def _ssd_body(inner, z_ref, xbc_ref, dtr_ref, cw_ref, cb_ref, dtb_ref, a_ref, dsk_ref, ng_ref,
              y_ref, st_ref, full_s, act_s, st_s, y_s):
    c = pl.program_id(1)
    Q, N = SSM_CHUNK, SSM_STATE
    cdim = xbc_ref.shape[1]
    heads_per_group = inner // HEAD // SSM_GROUPS
    gw = inner // SSM_GROUPS

    @pl.when(c == 0)
    def _():
        full_s[0:8, :] = jnp.zeros((8, cdim), F32)
        st_s[...] = jnp.zeros(st_s.shape, F32)

    full_s[8:8 + Q, :] = xbc_ref[...]
    for j in range(0, cdim, 512):
        acc = cb_ref[:, j:j + 512] + full_s[8:8 + Q, j:j + 512] * cw_ref[3:4, j:j + 512]
        for i in range(SSM_CONV - 1):
            acc = acc + full_s[5 + i:5 + i + Q, j:j + 512] * cw_ref[i:i + 1, j:j + 512]
        act_s[:, j:j + 512] = _silu(acc)
    full_s[0:8, :] = full_s[Q:Q + 8, :]

    dt = _softplus(dtr_ref[...] + dtb_ref[...])
    row = lax.broadcasted_iota(I32, (Q, Q), 0)
    col = lax.broadcasted_iota(I32, (Q, Q), 1)
    tri = row >= col
    acs = jnp.dot(tri.astype(F32), dt * a_ref[...], precision=HIGHEST, preferred_element_type=F32)
    acs_t, dt_t = acs.T, dt.T
    last = acs[Q - 1:Q, :]
    wdt = jnp.exp(last - acs) * dt
    eacs = jnp.exp(acs)
    cdec = jnp.exp(last)
    low = lax.broadcasted_iota(I32, (Q, LANES), 1) < HEAD
    low1 = low[0:1, :]

    for g in range(SSM_GROUPS):
        bg = act_s[:, inner + g * N:inner + (g + 1) * N]
        cg = act_s[:, inner + SSM_GROUPS * N + g * N:inner + SSM_GROUPS * N + (g + 1) * N]
        cb = _bdot_nt(cg, bg)
        bg_t = bg.T.astype(BF16)
        for p in range(heads_per_group // 2):
            h0 = g * heads_per_group + 2 * p
            js = slice(h0 * HEAD, h0 * HEAD + LANES)
            xp = act_s[:, js]
            xp_bf = xp.astype(BF16)
            stp = st_s[:, js]
            stp_bf = stp.astype(BF16)
            ys = []
            for h in (h0, h0 + 1):
                seg = acs[:, h:h + 1] - acs_t[h:h + 1, :]
                decay = jnp.exp(jnp.where(tri, seg, -jnp.inf))
                m = (cb * decay) * dt_t[h:h + 1, :]
                ce = cg * eacs[:, h:h + 1]
                ys.append(_bdot(m, xp_bf) + _bdot(ce, stp_bf))
            y_s[:, js] = jnp.where(low, ys[0], ys[1])
            wcol = jnp.where(low, wdt[:, h0:h0 + 1], wdt[:, h0 + 1:h0 + 2])
            dst = jnp.dot(bg_t, (xp * wcol).astype(BF16), preferred_element_type=F32)
            cd = jnp.where(low1, cdec[:, h0:h0 + 1], cdec[:, h0 + 1:h0 + 2])
            st_s[:, js] = stp * cd + dst

    for g in range(SSM_GROUPS):
        gs = slice(g * gw, (g + 1) * gw)
        y = y_s[:, gs] + dsk_ref[:, gs] * act_s[:, gs]
        y = y * _silu(z_ref[:, gs])
        ms = jnp.mean(y * y, axis=-1, keepdims=True)
        y_ref[:, gs] = (y * lax.rsqrt(ms + NORM_EPS) * ng_ref[:, gs]).astype(BF16)

    @pl.when(c == pl.num_programs(1) - 1)
    def _():
        st_ref[...] = st_s[...].T


def _ssd_prompt(z, xbc, dtr, cw, cb, dtb, a, dsk, ng, nb):
    m, inner = z.shape
    cdim = xbc.shape[1]
    nc = m // nb // SSM_CHUNK
    row = lambda n: pl.BlockSpec((SSM_CHUNK, n), lambda b, c: (b * nc + c, 0))
    const = lambda shape: pl.BlockSpec(shape, lambda b, c: (0,) * len(shape))
    return pl.pallas_call(
        functools.partial(_ssd_body, inner), grid=(nb, nc),
        in_specs=[row(inner), row(cdim), row(LANES), const(cw.shape), const(cb.shape),
                  const(dtb.shape), const(a.shape), const(dsk.shape), const(ng.shape)],
        out_specs=[row(inner), pl.BlockSpec((None, inner, SSM_STATE), lambda b, c: (b, 0, 0))],
        out_shape=[jax.ShapeDtypeStruct((m, inner), BF16),
                   jax.ShapeDtypeStruct((nb, inner, SSM_STATE), F32)],
        scratch_shapes=[pltpu.VMEM((SSM_CHUNK + 8, cdim), F32), pltpu.VMEM((SSM_CHUNK, cdim), F32),
                        pltpu.VMEM((SSM_STATE, inner), F32), pltpu.VMEM((SSM_CHUNK, inner), F32)],
        compiler_params=_cparams(("parallel", "arbitrary"), 40), name="ssd_prompt",
    )(z, xbc, dtr, cw, cb, dtb, a, dsk, ng)


def _ssd_step_body(z_ref, xs_ref, bc_ref, cxs_ref, cbc_ref, dtr_ref, st_ref,
                   wxs_ref, wbc_ref, bxs_ref, bbc_ref, dtb_ref, a_ref, dsk_ref, ng_ref,
                   y_ref, sto_ref):
    G = SSM_GROUPS
    last = SSM_CONV - 1
    xs = bxs_ref[...] + xs_ref[...] * wxs_ref[last]
    bc = bbc_ref[...] + bc_ref[...] * wbc_ref[last]
    for i in range(last):
        xs = xs + cxs_ref[i] * wxs_ref[i]
        bc = bc + cbc_ref[i] * wbc_ref[i]
    xs, bc = _silu(xs), _silu(bc)
    dt = _softplus(dtr_ref[...] + dtb_ref[...])
    dec = jnp.exp(dt * a_ref[...])
    xdt = xs * dt
    npair = xs.shape[0]
    pairs_per_group = npair // G
    r = lax.broadcasted_iota(I32, (LANES, LANES), 0)
    cidx = lax.broadcasted_iota(I32, (LANES, LANES), 1)
    eye = (r == cidx).astype(F32)
    nt = (((1,), (1,)), ((), ()))
    dec_t = lax.dot_general(eye, dec, nt, precision=HIGHEST, preferred_element_type=F32)
    xdt_t = lax.dot_general(eye, xdt, nt, precision=HIGHEST, preferred_element_type=F32)
    rows = lax.broadcasted_iota(I32, (npair, 1), 0)
    cbv = jnp.sum(bc[0:G, :] * bc[G:2 * G, :], axis=-1, keepdims=True)
    cbx = jnp.zeros((npair, 1), F32)
    for g in range(G):
        cbx = cbx + jnp.where(rows // pairs_per_group == g, cbv[g:g + 1, :], 0.0)
    c_bf = bc.astype(BF16)
    yoff = jnp.zeros(xs.shape, F32)
    for j in range(npair):
        g = j // pairs_per_group
        s = st_ref[j * LANES:(j + 1) * LANES, :]
        sto_ref[j * LANES:(j + 1) * LANES, :] = s * dec_t[:, j:j + 1] + xdt_t[:, j:j + 1] * bc[g:g + 1, :]
        rj = _bdot_nt(c_bf, s)
        yoff = yoff + jnp.where(rows == j, rj[G + g:G + g + 1, :], 0.0)
    y = yoff * dec + cbx * dt * xs + dsk_ref[...] * xs
    y = y * _silu(z_ref[...])
    ssq = jnp.sum(y * y, axis=-1, keepdims=True)
    msx = jnp.zeros((npair, 1), F32)
    for g in range(G):
        ing = rows // pairs_per_group == g
        tot = jnp.sum(jnp.where(ing, ssq, 0.0), axis=0, keepdims=True)
        msx = msx + jnp.where(ing, tot, 0.0)
    msx = msx / (pairs_per_group * LANES)
    y_ref[...] = y * lax.rsqrt(msx + NORM_EPS) * ng_ref[...]


def _ssd_sample(z, xs, bc, cxs, cbc, dtr, st, wxs, wbc, bxs, bbc, dtb, a, dsk, ng):
    db, npair, _ = z.shape
    per_b = lambda shape: pl.BlockSpec((None,) + shape, lambda b: (b,) + (0,) * len(shape))
    const = lambda arr: pl.BlockSpec(arr.shape, lambda b: (0,) * arr.ndim)
    return pl.pallas_call(
        _ssd_step_body, grid=(db,),
        in_specs=[per_b(z.shape[1:]), per_b(xs.shape[1:]), per_b(bc.shape[1:]), per_b(cxs.shape[1:]),
                  per_b(cbc.shape[1:]), per_b(dtr.shape[1:]), per_b(st.shape[1:]),
                  const(wxs), const(wbc), const(bxs), const(bbc), const(dtb), const(a), const(dsk), const(ng)],
        out_specs=[per_b(z.shape[1:]), per_b(st.shape[1:])],
        out_shape=[jax.ShapeDtypeStruct(z.shape, F32), jax.ShapeDtypeStruct(st.shape, F32)],
        compiler_params=_cparams(("parallel",), 32), name="ssd_sample",
    )(z, xs, bc, cxs, cbc, dtr, st, wxs, wbc, bxs, bbc, dtb, a, dsk, ng)


def _row_fold(x, h, op=jnp.add):
    parts = [x[j:j + h, :] for j in range(0, x.shape[0], h)]
    while len(parts) > 1:
        parts = [op(a, b) for a, b in zip(parts[0::2], parts[1::2])] + (parts[-1:] if len(parts) % 2 else [])
    return parts[0]


SORT8 = ((0, 1), (2, 3), (4, 5), (6, 7), (0, 2), (1, 3), (4, 6), (5, 7), (1, 2), (5, 6),
         (0, 4), (1, 5), (2, 6), (3, 7), (2, 4), (3, 5), (1, 2), (3, 4), (5, 6))
GROUP = 8
IMG_ROWS = 16


def _group_members(kb):
    n = kb // IMG_ROWS // GROUP
    return [[g + n * k for k in range(GROUP)] for g in range(n)]


def _sort_groups(x):
    kb = x.shape[0]
    half = IMG_ROWS // 2
    rows = [x[r:r + half, :] for r in range(0, kb, half)]
    for members in _group_members(kb):
        for b in range(2):
            idx = [2 * p + b for p in members]
            vals = [rows[i] for i in idx]
            for i, j in SORT8:
                vals[i], vals[j] = jnp.maximum(vals[i], vals[j]), jnp.minimum(vals[i], vals[j])
            for i, v in zip(idx, vals):
                rows[i] = v
    return jnp.concatenate(rows, axis=0)


def _topk_bias(score_s, sort_s, img_s, bias_s, nblk, kb, topk, qpos, unroll=1):
    nq = score_s.shape[1]
    blk = lambda s: pl.ds(pl.multiple_of(s * kb, kb), kb)
    one16, zero16 = jnp.int16(1), jnp.int16(0)
    work_s = bias_s
    groups = _group_members(kb)

    def over_blocks(body, init):
        def trip(s2, carry):
            for k in range(unroll):
                carry = body(s2 * unroll + k, carry)
            return carry
        return lax.fori_loop(0, nblk // unroll, trip, init)

    def count(c):
        two16, four16 = jnp.int16(2), jnp.int16(4)

        def body(s, acc):
            img = img_s[blk(s), :]
            tile = lambda p: img[p * IMG_ROWS:(p + 1) * IMG_ROWS, :]
            for members in groups:
                p = [tile(m) for m in members]
                m4 = p[3] >= c
                mb = jnp.where(m4, p[5], p[1]) >= c
                mc = jnp.where(m4, jnp.where(mb, p[6], p[4]), jnp.where(mb, p[2], p[0])) >= c
                m8 = p[7] >= c
                acc = acc + ((jnp.where(m4, four16, zero16) + jnp.where(mb, two16, zero16))
                             + (jnp.where(mc, one16, zero16) + jnp.where(m8, one16, zero16)))
            return acc
        acc = over_blocks(body, jnp.zeros((IMG_ROWS, nq), I16))
        return jnp.sum(acc.astype(F32), axis=0, keepdims=True)

    def search(nbits, value_of):
        def step(t, u):
            code = u | lax.shift_left(jnp.int32(1), jnp.asarray(nbits - 1 - t, I32))
            c = value_of(code).astype(BF16)
            return jnp.where(count(c) >= topk, code, u)
        return lax.fori_loop(0, nbits, step, jnp.zeros((1, nq), I32))

    def bf16_value(code):
        pattern = jnp.where(code >= 32768, code - 32768, 65535 - code)
        return pltpu.bitcast(pattern << 16, jnp.float32).astype(F32)

    def set_image(fn):
        def body(s, carry):
            img_s[blk(s), :] = fn(s).astype(BF16)
            return carry
        over_blocks(body, 0)

    t1 = bf16_value(search(16, bf16_value))

    e1 = jnp.clip((pltpu.bitcast(t1.astype(jnp.float32), I32) >> 23) & 0xFF, 25, 254)
    unit = pltpu.bitcast((e1 - 24) << 23, jnp.float32).astype(F32)
    inv_unit = pltpu.bitcast((278 - e1) << 23, jnp.float32).astype(F32)
    B2, B1 = 65536.0, 256.0

    def digit2(s):
        y = (sort_s[blk(s), :] - t1) * inv_unit
        work_s[blk(s), :] = y
        return jnp.floor(y * (1.0 / B2))
    set_image(digit2)
    t2 = (search(2, lambda code: (code - 1).astype(F32)) - 1).astype(F32)
    set_image(lambda s: jnp.floor((work_s[blk(s), :] - t2 * B2) * (1.0 / B1)))
    t3 = search(8, lambda code: code.astype(F32)).astype(F32)
    set_image(lambda s: jnp.floor(work_s[blk(s), :] - (t2 * B2 + t3 * B1)))
    t4 = search(8, lambda code: code.astype(F32)).astype(F32)
    v0 = t1 + (t2 * B2 + t3 * B1 + t4) * unit

    def smallest(keep):
        def body(s, acc):
            x = score_s[blk(s), :]
            return jnp.minimum(acc, _row_fold(jnp.where(keep(x), x, jnp.inf), 8, jnp.minimum))
        acc = over_blocks(body, jnp.full((8, nq), jnp.inf, F32))
        return jnp.min(acc, axis=0, keepdims=True)

    def count_above(v):
        def body(s, acc):
            return acc + _row_fold(jnp.where(score_s[blk(s), :] > v, 1.0, 0.0), 8)
        return jnp.sum(over_blocks(body, jnp.zeros((8, nq), F32)), axis=0, keepdims=True)

    def refine(carry):
        v, above = carry
        v = jnp.where(above >= topk, smallest(lambda x: x > v), v)
        return v, count_above(v)

    v = smallest(lambda x: x >= v0)
    v, above = lax.while_loop(lambda c: jnp.max(c[1]) >= topk, refine, (v, count_above(v)))
    need = topk - above

    r_i = lax.broadcasted_iota(I32, (LANES, LANES), 0)
    c_i = lax.broadcasted_iota(I32, (LANES, LANES), 1)
    lower = jnp.where(r_i >= c_i, 1.0, 0.0).astype(BF16)
    sub_iota = lax.broadcasted_iota(I32, (LANES, 1), 0)

    def bias_body(s, carry):
        for j in range(0, kb, LANES):
            off = pl.multiple_of(s * kb + j, LANES)
            x = score_s[pl.ds(off, LANES), :]
            eq = x == v
            eqf = jnp.where(eq, 1.0, 0.0)
            incl = jnp.dot(lower, eqf.astype(BF16), preferred_element_type=F32)
            tie = jnp.where(carry + incl - eqf < need, 0.0, -jnp.inf)
            b = jnp.where(x > v, 0.0, jnp.where(eq, tie, -jnp.inf))
            bias_s[pl.ds(off, LANES), :] = jnp.where((off + sub_iota) <= qpos, b, -jnp.inf)
            carry = carry + incl[LANES - 1:LANES, :]
        return carry

    over_blocks(bias_body, jnp.zeros((1, nq), F32))


def _dsa_t_body(topk, idx_scale, q_ref, qi_ref, sm_ref, ki2_ref, k_ref, vt_ref, o_ref,
                score_s, sort_s, img_s, bias_s, qs_s, lga_s, lgb_s, m_s, acc_s):
    i = pl.program_id(1)
    QB, KB = Q_BLOCK, KEY_BLOCK
    nkb = (i * QB + QB + KB - 1) // KB
    qpos = i * QB + lax.broadcasted_iota(I32, (1, QB), 1)
    low = lax.broadcasted_iota(I32, (QB, LANES), 1) < HEAD
    zero_bf = jnp.zeros((QB, LANES), BF16)
    blk = lambda s: pl.ds(pl.multiple_of(s * KB, KB), KB)

    qi = qi_ref[...]
    sm_t = sm_ref[...].T
    qh, wh = [], []
    for h in range(IDX_HEADS):
        chunk = qi[:, (h // 2) * LANES:(h // 2 + 1) * LANES]
        qh.append(jnp.where(low if h % 2 == 0 else ~low, chunk, zero_bf))
        wh.append(sm_t[HEAD + h:HEAD + h + 1, :] * idx_scale)

    q_stack = jnp.concatenate(qh, axis=0)

    last_key_block = ki2_ref.shape[0] // KB - 1

    def score_pair(s2, carry):
        blocks = (2 * s2, 2 * s2 + 1)
        dots = [_bdot_nt(ki2_ref[blk(jnp.minimum(s, last_key_block)), :], q_stack) for s in blocks]
        for s, sc in zip(blocks, dots):
            acc = jnp.zeros((KB, QB), F32)
            for h in range(IDX_HEADS):
                acc = acc + wh[h] * jnp.maximum(sc[:, h * QB:(h + 1) * QB], 0.0)
            kpos = s * KB + lax.broadcasted_iota(I32, (KB, 1), 0)
            sc = jnp.where(kpos <= qpos, jnp.maximum(acc, MASKED_SCORE), MASKED_SCORE)
            score_s[blk(s), :] = sc
            sc = _sort_groups(sc)
            sort_s[blk(s), :] = sc
            img_s[blk(s), :] = sc.astype(BF16)
        return carry

    nkb_even = 2 * ((nkb + 1) // 2)
    lax.fori_loop(0, nkb_even // 2, score_pair, 0)
    _topk_bias(score_s, sort_s, img_s, bias_s, nkb_even, KB, topk, qpos, unroll=2)

    q = q_ref[...]
    nchunk = q.shape[1] // LANES
    per_kv_chunk = nchunk // (ATTN_KV_HEADS // 2)
    nstack = 2 * per_kv_chunk
    srows = nstack * QB
    for cj in range(nchunk):
        kvc, j = divmod(cj, per_kv_chunk)
        qc = q[:, cj * LANES:(cj + 1) * LANES]
        for half in range(2):
            r0 = (kvc * nstack + half * per_kv_chunk + j) * QB
            qs_s[r0:r0 + QB, :] = jnp.where(low if half == 0 else ~low, qc, zero_bf)

    ones_rows = jnp.ones((ONES_ROWS, KB), BF16)
    cols = lambda r: slice(r * QB, (r + 1) * QB)
    n_kvc = ATTN_KV_HEADS // 2
    nsteps = n_kvc * nkb

    def step_of(t):
        t = jnp.minimum(t, nsteps - 1)
        kvc = (t >= nkb).astype(I32)
        return kvc, t - kvc * nkb

    def logits_to(dst, t):
        kvc, s = step_of(t)
        kb = k_ref[blk(s), pl.ds(pl.multiple_of(kvc * LANES, LANES), LANES)]
        bias = bias_s[blk(s), :]
        lg = _bdot_nt(kb, qs_s[pl.ds(pl.multiple_of(kvc * srows, srows), srows), :])
        for r in range(nstack):
            dst[:, cols(r)] = lg[:, cols(r)] + bias

    def consume(src, t):
        kvc, s = step_of(t)
        vt = vt_ref[pl.ds(pl.multiple_of(kvc * LANES, LANES), LANES), blk(s)]
        ps, alphas = [], []
        for r in range(nstack):
            lg = src[:, cols(r)]
            m = m_s[kvc, :, cols(r)]
            mn = jnp.maximum(m, jnp.max(lg, axis=0, keepdims=True))
            m_s[kvc, :, cols(r)] = mn
            ps.append(jnp.exp2(lg - mn).astype(BF16))
            alphas.append(jnp.exp2(m - mn))
        for half in range(2):
            hs = slice(half * per_kv_chunk, (half + 1) * per_kv_chunk)
            hc = slice(half * per_kv_chunk * QB, (half + 1) * per_kv_chunk * QB)
            v_aug = jnp.concatenate([vt[half * HEAD:(half + 1) * HEAD, :], ones_rows], axis=0)
            pv = jnp.dot(v_aug, jnp.concatenate(ps[hs], axis=1), preferred_element_type=F32)
            acc_s[kvc, :, hc] = jnp.concatenate(alphas[hs], axis=1) * acc_s[kvc, :, hc] + pv

    m_s[...] = jnp.full(m_s.shape, NEG_BIG, F32)
    acc_s[...] = jnp.zeros(acc_s.shape, F32)
    logits_to(lga_s, 0)

    def run_pairs(t0, npairs):
        for p in range(npairs):
            logits_to(lgb_s, t0 + 2 * p + 1)
            consume(lga_s, t0 + 2 * p)
            logits_to(lga_s, t0 + 2 * p + 2)
            consume(lgb_s, t0 + 2 * p + 1)

    def quad_body(i4, carry):
        run_pairs(4 * i4, 2)
        return carry

    lax.fori_loop(0, nsteps // 4, quad_body, 0)

    @pl.when(nsteps % 4 == 2)
    def _():
        run_pairs(nsteps - 2, 1)

    for kvc in range(n_kvc):
        acc = acc_s[kvc]
        for j in range(per_kv_chunk):
            a = acc[:, cols(j)]
            b = acc[:, cols(per_kv_chunk + j)]
            chunk_t = jnp.concatenate([a[0:HEAD, :] / a[HEAD:HEAD + 1, :], b[0:HEAD, :] / b[HEAD:HEAD + 1, :]],
                                      axis=0)
            cj = kvc * per_kv_chunk + j
            o_ref[:, cj * LANES:(cj + 1) * LANES] = chunk_t.T.astype(BF16)


def _dsa_prompt_t(q, qi, sm, ki2, k, vt, nb, topk):
    m, aw = q.shape
    t = m // nb
    nq = t // Q_BLOCK
    tpad = -(-t // (2 * KEY_BLOCK)) * 2 * KEY_BLOCK
    srows = aw // HEAD // (ATTN_KV_HEADS // 2) * Q_BLOCK
    idx_scale = IDX_HEADS ** -0.5 * HEAD ** -0.5
    row = lambda n: pl.BlockSpec((Q_BLOCK, n), lambda b, i: (b * nq + i, 0))
    per_b = lambda n: pl.BlockSpec((t, n), lambda b, i: (b, 0))
    return pl.pallas_call(
        functools.partial(_dsa_t_body, topk, idx_scale), grid=(nb, nq),
        in_specs=[row(aw), row(qi.shape[1]), row(LANES), per_b(LANES), per_b(k.shape[1]),
                  pl.BlockSpec((None,) + vt.shape[1:], lambda b, i: (b, 0, 0))],
        out_specs=row(aw),
        out_shape=jax.ShapeDtypeStruct((m, aw), BF16),
        scratch_shapes=[pltpu.VMEM((tpad, Q_BLOCK), F32), pltpu.VMEM((tpad, Q_BLOCK), F32),
                        pltpu.VMEM((tpad, Q_BLOCK), BF16), pltpu.VMEM((tpad, Q_BLOCK), F32),
                        pltpu.VMEM((2 * srows, LANES), BF16),
                        pltpu.VMEM((KEY_BLOCK, srows), F32), pltpu.VMEM((KEY_BLOCK, srows), F32),
                        pltpu.VMEM((ATTN_KV_HEADS // 2, 1, srows), F32),
                        pltpu.VMEM((ATTN_KV_HEADS // 2, HEAD + ONES_ROWS, srows), F32)],
        compiler_params=_cparams(("parallel", "arbitrary"), 52), name="dsa_prompt",
    )(q, qi, sm, ki2, k, vt)


def _page_specs(block, npages):
    def make(u):
        return pl.BlockSpec((None,) + block, lambda b, pt: (pt[b, u],) + (0,) * len(block))
    return [make(u) for u in range(npages)]


def _idx_score_body(pt_ref, qi_ref, w_ref, *refs):
    pages, o_ref = refs[:-1], refs[-1]
    qi = qi_ref[...]
    w = w_ref[...]
    for u, page in enumerate(pages):
        s = jnp.maximum(_bdot(qi, page[...]), 0.0)
        o_ref[:, u * PAGE_SIZE:(u + 1) * PAGE_SIZE] = jnp.sum(w * s, axis=0, keepdims=True)


def _idx_scores_sample(page_table, qi8, w8, kidx_t):
    db, npages = page_table.shape
    grid_spec = pltpu.PrefetchScalarGridSpec(
        num_scalar_prefetch=1, grid=(db,),
        in_specs=[pl.BlockSpec((None,) + qi8.shape[1:], lambda b, pt: (b, 0, 0)),
                  pl.BlockSpec((None,) + w8.shape[1:], lambda b, pt: (b, 0, 0))]
                 + _page_specs(kidx_t.shape[1:], npages),
        out_specs=pl.BlockSpec((None, 1, npages * PAGE_SIZE), lambda b, pt: (b, 0, 0)))
    return pl.pallas_call(
        _idx_score_body, grid_spec=grid_spec,
        out_shape=jax.ShapeDtypeStruct((db, 1, npages * PAGE_SIZE), F32),
        compiler_params=_cparams(("parallel",), 32), name="idx_scores_sample",
    )(page_table, qi8, w8, *([kidx_t] * npages))


def _select_sample_body(topk, idx_scale, past, sc_ref, qi_ref, sm_ref, bias_ref, score_s, sort_s, img_s, bias_s):
    rows = sc_ref.shape[0]
    sm = sm_ref[...]
    qi = qi_ref[...]
    ki = sm[:, 0:HEAD]
    new = jnp.zeros((rows, 1), F32)
    for h in range(IDX_HEADS):
        d = jnp.sum(qi[:, h * HEAD:(h + 1) * HEAD] * ki, axis=-1, keepdims=True)
        new = new + (sm[:, HEAD + h:HEAD + h + 1] * idx_scale) * jnp.maximum(d, 0.0)
    nblk = (past + LANES) // LANES
    lane = lax.broadcasted_iota(I32, (rows, LANES), 1)
    for j in range(nblk):
        js = slice(j * LANES, (j + 1) * LANES)
        sc = sc_ref[:, js] if j < nblk - 1 else jnp.where(lane == 0, new, MASKED_SCORE)
        sc = jnp.maximum(sc, MASKED_SCORE).T
        score_s[js, :] = sc
        sc = _sort_groups(sc)
        sort_s[js, :] = sc
        img_s[js, :] = sc.astype(BF16)
    qpos = jnp.full((1, rows), past, I32)
    _topk_bias(score_s, sort_s, img_s, bias_s, nblk, LANES, topk, qpos)
    for j in range(nblk):
        js = slice(j * LANES, (j + 1) * LANES)
        bias_ref[:, js] = bias_s[js, :].T


def _select_sample(scores, qi, sm, topk, idx_scale):
    db, past = scores.shape
    full = lambda a: pl.BlockSpec(a.shape, lambda i: (0,) * a.ndim)
    keys = past + LANES
    return pl.pallas_call(
        functools.partial(_select_sample_body, topk, idx_scale, past), grid=(1,),
        in_specs=[full(scores), full(qi), full(sm)],
        out_specs=pl.BlockSpec((db, keys), lambda i: (0, 0)),
        out_shape=jax.ShapeDtypeStruct((db, keys), F32),
        scratch_shapes=[pltpu.VMEM((keys, db), F32), pltpu.VMEM((keys, db), F32), pltpu.VMEM((keys, db), BF16),
                        pltpu.VMEM((keys, db), F32)],
        compiler_params=_cparams(("arbitrary",), 32), name="select_sample")(scores, qi, sm)


def _attend_sample_body(npages, pt_ref, q_ref, bias_ref, knew_ref, vnew_ref, *refs):
    kpages, vpages, o_ref = refs[:npages], refs[npages:2 * npages], refs[2 * npages]
    G = ATTN_KV_HEADS
    q = q_ref[...]
    q_bf = q.astype(BF16)
    nh = q.shape[0]
    past = npages * PAGE_SIZE
    group = lax.broadcasted_iota(I32, (nh, 1), 0) // (nh // G)

    def by_group(parts):
        out = parts[G - 1]
        for g in range(G - 2, -1, -1):
            out = jnp.where(group == g, parts[g], out)
        return out

    lg = jnp.concatenate(
        [by_group([_bdot(q_bf, kpages[u][g]) for g in range(G)]) for u in range(npages)], axis=1)
    lg = lg + bias_ref[:, 0:past]
    lg_new = by_group([jnp.sum(q * knew_ref[g:g + 1, :], axis=-1, keepdims=True) for g in range(G)])
    lg_new = lg_new + bias_ref[:, past:past + 1]
    m = jnp.maximum(jnp.max(lg, axis=-1, keepdims=True), lg_new)
    p = jnp.exp2(lg - m)
    p_new = jnp.exp2(lg_new - m)
    denom = jnp.sum(p, axis=-1, keepdims=True) + p_new
    p_bf = p.astype(BF16)
    accs = [p_new * vnew_ref[g:g + 1, :] for g in range(G)]
    for u in range(npages):
        pu = p_bf[:, u * PAGE_SIZE:(u + 1) * PAGE_SIZE]
        for g in range(G):
            accs[g] = accs[g] + _bdot_nt(pu, vpages[u][g])
    o_ref[...] = by_group(accs) / denom


def _attend_sample(page_table, q, bias, knew, vnew, k_t, v_t):
    db, npages = page_table.shape
    per_b = lambda a: pl.BlockSpec((None,) + a.shape[1:], lambda b, pt: (b,) + (0,) * (a.ndim - 1))
    grid_spec = pltpu.PrefetchScalarGridSpec(
        num_scalar_prefetch=1, grid=(db,),
        in_specs=[per_b(q), per_b(bias), per_b(knew), per_b(vnew)]
                 + _page_specs(k_t.shape[1:], npages) + _page_specs(v_t.shape[1:], npages),
        out_specs=per_b(q))
    return pl.pallas_call(
        functools.partial(_attend_sample_body, npages), grid_spec=grid_spec,
        out_shape=jax.ShapeDtypeStruct(q.shape, F32),
        compiler_params=_cparams(("parallel",), 48), name="attend_sample",
    )(page_table, q, bias, knew, vnew, *([k_t] * npages), *([v_t] * npages))


def _merge_body(x_ref, mod_ref, ys_ref, ya_ref, gs_ref, ga_ref, wps_ref, wpa_ref, wo_ref, o_ref):
    d = x_ref.shape[1]
    merged = (jax.nn.sigmoid(gs_ref[...]) * _bdot(ys_ref[...], wps_ref[...])
              + jax.nn.sigmoid(ga_ref[...]) * _bdot(ya_ref[...], wpa_ref[...]))
    o_ref[...] = x_ref[...] + mod_ref[:, 2 * d:3 * d] * _bdot(merged, wo_ref[...])


def _merge(x, mod3, ys, ya, gs, ga, wps, wpa, wo, tm, tpb):
    m, d = x.shape
    x_spec, mod_spec = _row_specs(tm, tpb, d, mod3.shape[1])
    row = lambda n: pl.BlockSpec((tm, n), lambda i: (i, 0))
    return pl.pallas_call(
        _merge_body, grid=(m // tm,),
        in_specs=[x_spec, mod_spec, row(ys.shape[1]), row(ya.shape[1]), row(d), row(d),
                  _const_spec(wps.shape), _const_spec(wpa.shape), _const_spec(wo.shape)],
        out_specs=row(d), out_shape=jax.ShapeDtypeStruct((m, d), F32),
        compiler_params=_cparams(("parallel",), 48), name="merge")(x, mod3, ys, ya, gs, ga, wps, wpa, wo)


def _ffn_body(last_layer, x_ref, mod_ref, g_ref, fg_ref, wg_ref, wu_ref, wo_ref, o_ref):
    d = x_ref.shape[1]
    x = x_ref[...]
    h = _norm_mod(x, g_ref[...], mod_ref[:, 4 * d:5 * d], mod_ref[:, 3 * d:4 * d]).astype(BF16)
    gate = jnp.dot(h, wg_ref[...], preferred_element_type=F32)
    up = jnp.dot(h, wu_ref[...], preferred_element_type=F32)
    x2 = x + mod_ref[:, 5 * d:6 * d] * _bdot(_silu(gate) * up, wo_ref[...])
    if last_layer:
        ms = jnp.mean(x2 * x2, axis=-1, keepdims=True)
        x2 = x2 * lax.rsqrt(ms + NORM_EPS) * fg_ref[...]
    o_ref[...] = x2


def _ffn(x, mod3, g, fg, wg, wu, wo, tm, tpb, last_layer):
    m, d = x.shape
    x_spec, mod_spec = _row_specs(tm, tpb, d, mod3.shape[1])
    return pl.pallas_call(
        functools.partial(_ffn_body, last_layer), grid=(m // tm,),
        in_specs=[x_spec, mod_spec, _const_spec((1, d)), _const_spec((1, d)),
                  _const_spec(wg.shape), _const_spec(wu.shape), _const_spec(wo.shape)],
        out_specs=pl.BlockSpec((tm, d), lambda i: (i, 0)), out_shape=jax.ShapeDtypeStruct((m, d), F32),
        compiler_params=_cparams(("parallel",), 56), name="ffn")(x, mod3, g, fg, wg, wu, wo)


def _rope_tables(pos):
    half = HEAD // 2
    inv = ROPE_THETA ** (-jnp.arange(half, dtype=F32) / half)
    ang = pos.astype(F32)[:, None] * inv[None, :]
    cos = jnp.tile(jnp.cos(ang), (1, LANES // half))
    sin = jnp.tile(jnp.sin(ang), (1, LANES // half))
    first = (jnp.arange(LANES) % HEAD) < half
    return cos, jnp.where(first, -sin, 0.0), jnp.where(first, 0.0, sin)


def _q_head_order(n_heads):
    rep = n_heads // ATTN_KV_HEADS
    order = []
    for c in range(ATTN_KV_HEADS // 2):
        for j in range(rep):
            order += [2 * c * rep + j, (2 * c + 1) * rep + j]
    return np.asarray(order)


def kernel(x_prompt, x_sample, cache_k, cache_v, cache_kidx, state_conv, state_ssm, page_table, c_prompt, c_sample, w_ada, b_ada, norm1_g, w_in, conv_w, conv_b, dt_bias, a_log, d_skip, ssm_norm_g, w_proj_ssm, w_proj_attn, w_out, norm2_g, w_ffn_in, w_ffn_out, final_g):
    nb, t, d = x_prompt.shape
    db, ds, _ = x_sample.shape
    depth = w_in.shape[0]
    assert ds == 1 and t % Q_BLOCK == 0 and t % SSM_CHUNK == 0
    n_heads_ssm = dt_bias.shape[1]
    inner = n_heads_ssm * HEAD
    gn = SSM_GROUPS * SSM_STATE
    cdim = inner + 2 * gn
    kvw = ATTN_KV_HEADS * HEAD
    aw = w_proj_attn.shape[1]
    n_heads = aw // HEAD
    iw = IDX_HEADS * HEAD
    ffn_hidden = w_ffn_out.shape[1]
    npages = page_table.shape[1]
    past = npages * PAGE_SIZE
    topk_p = min(TOPK_MAX, t // 4)
    topk_s = min(TOPK_MAX, (past + ds) // 4)
    assert past + ds >= topk_s
    idx_scale = IDX_HEADS ** -0.5 * HEAD ** -0.5
    tm_p = 256 if t % 256 == 0 else 128
    tpb_p = t // tm_p

    splits = np.cumsum([inner, inner, gn, gn, n_heads_ssm, aw, kvw, kvw, iw, HEAD, IDX_HEADS, d])
    order = _q_head_order(n_heads)
    inv_order = np.argsort(order)

    cos_p, slo_p, shi_p = _rope_tables(jnp.arange(t, dtype=I32))
    tabs_p = (cos_p, slo_p, shi_p)
    tabs_s = tuple(jnp.broadcast_to(a, (db, LANES)) for a in _rope_tables(past + jnp.arange(ds, dtype=I32)))

    rows_c = nb + db
    c_all = jnp.concatenate([c_prompt, c_sample, jnp.zeros((-rows_c % 8, d), F32)], axis=0)

    yp = x_prompt.reshape(nb * t, d)
    ys = x_sample.reshape(db, d)
    outs_p, outs_s = [], []
    for l in range(depth):
        (wz, wxs, wbm, wcm, wdt, wq, wk, wv, wqi, wki, wwi, wgs, wga) = jnp.split(w_in[l], splits, axis=1)
        w_ssm = jnp.concatenate([wz, wxs, wbm, wcm, wdt, jnp.zeros((d, LANES - n_heads_ssm), F32)],
                                axis=1).astype(BF16)
        wq_perm = wq.reshape(d, n_heads, HEAD)[:, order].reshape(d, aw)
        w_attn = jnp.concatenate([wq_perm, wk, wv, wqi, wgs, wga, wki, wwi,
                                  jnp.zeros((d, LANES - HEAD - IDX_HEADS), F32)], axis=1).astype(BF16)
        wps = w_proj_ssm[l].astype(BF16)
        wpa = w_proj_attn[l].reshape(n_heads, HEAD, d)[order].reshape(aw, d).astype(BF16)
        wo = w_out[l].astype(BF16)
        wg = w_ffn_in[l][:, :ffn_hidden].astype(BF16)
        wu = w_ffn_in[l][:, ffn_hidden:].astype(BF16)
        wfo = w_ffn_out[l].astype(BF16)
        g1 = norm1_g[l][None, :]
        g2 = norm2_g[l][None, :]
        a_neg = -jnp.exp(a_log[l])
        pad_h = LANES - n_heads_ssm
        dtb_row = jnp.pad(dt_bias[l], (0, pad_h))[None, :]
        a_row = jnp.pad(a_neg, (0, pad_h))[None, :]
        dsk_row = jnp.repeat(d_skip[l], HEAD)[None, :]
        ng_row = ssm_norm_g[l][None, :]
        cw = conv_w[l]
        cb = conv_b[l][None, :]

        mod = _ada(c_all, w_ada[l].astype(BF16), b_ada[l][None, :])
        mod_p = mod[:nb][:, None, :]
        mod_s = mod[nb:nb + db][None]

        z, xbc, dtr = _inproj_ssm(yp, mod_p, g1, w_ssm, tm_p, tpb_p, inner, cdim)
        (q_bf, k_bf, kt, vt, vt_bf, qi_bf, gs, ga, sm, ki2, kit) = _inproj_attn(
            yp, mod_p, g1, w_attn, tabs_p, tm_p, tpb_p, aw, kvw, iw)
        y_ssm, st = _ssd_prompt(z, xbc, dtr, cw, cb, dtb_row, a_row, dsk_row, ng_row, nb)
        y_attn = _dsa_prompt_t(q_bf, qi_bf, sm, ki2, k_bf, vt_bf, nb, topk_p)
        x1 = _merge(yp, mod_p, y_ssm, y_attn, gs, ga, wps, wpa, wo, tm_p, tpb_p)
        yp_next = _ffn(x1, mod_p, g2, final_g[None, :], wg, wu, wfo, tm_p, tpb_p, l == depth - 1)
        heads_last = lambda a: jnp.transpose(a.reshape(a.shape[0], ATTN_KV_HEADS, HEAD, a.shape[2]), (0, 3, 1, 2))
        outs_p.append((heads_last(kt), heads_last(vt), jnp.transpose(kit, (0, 2, 1)),
                       xbc.reshape(nb, t, cdim)[:, t - (SSM_CONV - 1):],
                       st.reshape(nb, n_heads_ssm, HEAD, SSM_STATE)))

        z_s, xbc_s, dtr_s = _inproj_ssm(ys, mod_s, g1, w_ssm, db, 1, inner, cdim)
        (q_s, _, kt_s, vt_s, _, qi_s, gs_s, ga_s, sm_s, _, kit_s) = _inproj_attn(
            ys, mod_s, g1, w_attn, tabs_s, db, 1, aw, kvw, iw)
        k_s, v_s = heads_last(kt_s)[0], heads_last(vt_s)[0]
        npair = inner // LANES
        nbc = 2 * gn // LANES
        sc = state_conv[l]
        y_ssm_s, st_s = _ssd_sample(
            z_s.reshape(db, npair, LANES), xbc_s[:, :inner].reshape(db, npair, LANES),
            xbc_s[:, inner:].reshape(db, nbc, LANES),
            sc[:, :, :inner].reshape(db, SSM_CONV - 1, npair, LANES),
            sc[:, :, inner:].reshape(db, SSM_CONV - 1, nbc, LANES),
            jnp.repeat(dtr_s[:, :n_heads_ssm], HEAD, axis=1).reshape(db, npair, LANES),
            state_ssm[l].reshape(db, inner, SSM_STATE),
            cw[:, :inner].reshape(SSM_CONV, npair, LANES), cw[:, inner:].reshape(SSM_CONV, nbc, LANES),
            cb[:, :inner].reshape(npair, LANES), cb[:, inner:].reshape(nbc, LANES),
            jnp.repeat(dt_bias[l], HEAD).reshape(npair, LANES), jnp.repeat(a_neg, HEAD).reshape(npair, LANES),
            dsk_row.reshape(npair, LANES), ng_row.reshape(npair, LANES))

        qi_f = qi_s.astype(F32)
        qi8 = jnp.pad(qi_f.reshape(db, IDX_HEADS, HEAD), ((0, 0), (0, 8 - IDX_HEADS), (0, 0)))
        w8 = jnp.broadcast_to(jnp.pad(sm_s[:, HEAD:HEAD + IDX_HEADS] * idx_scale,
                                      ((0, 0), (0, 8 - IDX_HEADS)))[:, :, None], (db, 8, LANES))
        scores = _idx_scores_sample(page_table, qi8, w8, jnp.transpose(cache_kidx[l], (0, 2, 1)))
        bias = _select_sample(scores.reshape(db, past), qi_f, sm_s, topk_s, idx_scale)
        q_orig = q_s.astype(F32).reshape(db, n_heads, HEAD)[:, inv_order]
        att = _attend_sample(page_table, q_orig, bias[:, None, :],
                             k_s, v_s,
                             jnp.transpose(cache_k[l], (0, 2, 3, 1)), jnp.transpose(cache_v[l], (0, 2, 3, 1)))
        y_attn_s = att[:, order].reshape(db, aw)
        x1_s = _merge(ys, mod_s, y_ssm_s.reshape(db, inner), y_attn_s, gs_s, ga_s, wps, wpa, wo, db, 1)
        ys_next = _ffn(x1_s, mod_s, g2, final_g[None, :], wg, wu, wfo, db, 1, l == depth - 1)
        outs_s.append((k_s[:, None], v_s[:, None], jnp.transpose(kit_s, (2, 0, 1)),
                       jnp.concatenate([sc[:, 1:], xbc_s[:, None, :]], axis=1),
                       st_s.reshape(db, n_heads_ssm, HEAD, SSM_STATE)))
        yp, ys = yp_next, ys_next

    stack = lambda outs, i: jnp.stack([o[i] for o in outs], axis=0)
    return (yp.reshape(nb, t, d), ys.reshape(db, ds, d),
            stack(outs_p, 0), stack(outs_p, 1), stack(outs_p, 2), stack(outs_p, 3), stack(outs_p, 4),
            stack(outs_s, 0), stack(outs_s, 1), stack(outs_s, 2), stack(outs_s, 3), stack(outs_s, 4))
```

```python
import functools

import jax
import jax.numpy as jnp
import numpy as np
from jax import lax
from jax.experimental import pallas as pl
from jax.experimental.pallas import tpu as pltpu

F32, BF16, I32, I16 = jnp.float32, jnp.bfloat16, jnp.int32, jnp.int16
HIGHEST = lax.Precision.HIGHEST

LANES = 128
HEAD = 64
SSM_STATE = 128
SSM_GROUPS = 4
SSM_CONV = 4
SSM_CHUNK = 128
ATTN_KV_HEADS = 4
IDX_HEADS = 4
TOPK_MAX = 256
Q_BLOCK = 128
PAGE_SIZE = 128
ROPE_THETA = 10000.0
NORM_EPS = 1e-6
KEY_BLOCK = 512
ONES_ROWS = 16
Q_SCALE = HEAD ** -0.5 * 1.4426950408889634
MASKED_SCORE = -3.3895313892515355e38
NEG_BIG = -1e30


def _cparams(sem, vmem_mb):
    return pltpu.CompilerParams(dimension_semantics=sem, vmem_limit_bytes=vmem_mb << 20)


def _bdot(a, b):
    return jnp.dot(a.astype(BF16), b.astype(BF16), preferred_element_type=F32)


def _bdot_nt(a, b):
    return lax.dot_general(a.astype(BF16), b.astype(BF16), (((1,), (1,)), ((), ())),
                           preferred_element_type=F32)


def _silu(x):
    h = 0.5 * x
    return h + h * jnp.tanh(h)


def _softplus(x):
    return jnp.maximum(x, 0.0) + jnp.log(1.0 + jnp.exp(-jnp.abs(x)))


def _norm_mod(x, g, scale, shift):
    ms = jnp.mean(x * x, axis=-1, keepdims=True)
    return (x * lax.rsqrt(ms + NORM_EPS) * g) * (1.0 + scale) + shift


def _rope128(x, cos, sin_lo, sin_hi):
    return x * cos + pltpu.roll(x, 96, 1) * sin_lo + pltpu.roll(x, 32, 1) * sin_hi


def _rope_wide(x, cos, sin_lo, sin_hi):
    parts = [_rope128(x[:, j:j + LANES], cos, sin_lo, sin_hi) for j in range(0, x.shape[1], LANES)]
    return parts[0] if len(parts) == 1 else jnp.concatenate(parts, axis=1)


def _ada_body(c_ref, w_ref, b_ref, o_ref):
    o_ref[...] = _bdot(_silu(c_ref[...]), w_ref[...]) + b_ref[...]


def _ada(c_all, w_bf, b):
    mp, d = c_all.shape
    n = w_bf.shape[1]
    tn = n // 4
    return pl.pallas_call(
        _ada_body, grid=(n // tn,),
        in_specs=[pl.BlockSpec((mp, d), lambda j: (0, 0)),
                  pl.BlockSpec((d, tn), lambda j: (0, j)),
                  pl.BlockSpec((1, tn), lambda j: (0, j))],
        out_specs=pl.BlockSpec((mp, tn), lambda j: (0, j)),
        out_shape=jax.ShapeDtypeStruct((mp, n), F32),
        compiler_params=_cparams(("arbitrary",), 32), name="ada")(c_all, w_bf, b)


def _inproj_ssm_body(inner, cdim, x_ref, mod_ref, g_ref, w_ref, z_ref, xbc_ref, dt_ref):
    d = x_ref.shape[1]
    h = _norm_mod(x_ref[...], g_ref[...], mod_ref[:, d:2 * d], mod_ref[:, 0:d]).astype(BF16)
    z_ref[...] = jnp.dot(h, w_ref[:, 0:inner], preferred_element_type=F32)
    xbc_ref[...] = jnp.dot(h, w_ref[:, inner:inner + cdim], preferred_element_type=F32)
    dt_ref[...] = jnp.dot(h, w_ref[:, inner + cdim:inner + cdim + LANES], preferred_element_type=F32)


def _row_specs(tm, tpb, d, mod_rows):
    x_spec = pl.BlockSpec((tm, d), lambda m: (m, 0))
    mod_spec = pl.BlockSpec((None, mod_rows, 6 * d), lambda m: (m // tpb, 0, 0))
    return x_spec, mod_spec


def _const_spec(shape):
    return pl.BlockSpec(shape, lambda m: (0,) * len(shape))


def _inproj_ssm(x, mod3, g, w_bf, tm, tpb, inner, cdim):
    m, d = x.shape
    x_spec, mod_spec = _row_specs(tm, tpb, d, mod3.shape[1])
    row = lambda n: pl.BlockSpec((tm, n), lambda i: (i, 0))
    return pl.pallas_call(
        functools.partial(_inproj_ssm_body, inner, cdim), grid=(m // tm,),
        in_specs=[x_spec, mod_spec, _const_spec((1, d)), _const_spec(w_bf.shape)],
        out_specs=[row(inner), row(cdim), row(LANES)],
        out_shape=[jax.ShapeDtypeStruct((m, inner), F32), jax.ShapeDtypeStruct((m, cdim), F32),
                   jax.ShapeDtypeStruct((m, LANES), F32)],
        compiler_params=_cparams(("parallel",), 52), name="inproj_ssm")(x, mod3, g, w_bf)


def _inproj_attn_body(aw, kvw, iw, x_ref, mod_ref, g_ref, w_ref, cos_ref, slo_ref, shi_ref,
                      q_ref, kb_ref, kt_ref, vt_ref, vtb_ref, qi_ref, gs_ref, ga_ref, sm_ref, ki2_ref, kit_ref):
    d = x_ref.shape[1]
    h = _norm_mod(x_ref[...], g_ref[...], mod_ref[:, d:2 * d], mod_ref[:, 0:d]).astype(BF16)
    cos, slo, shi = cos_ref[...], slo_ref[...], shi_ref[...]

    def proj(a, b):
        return jnp.dot(h, w_ref[:, a:b], preferred_element_type=F32)

    o = 0
    q_ref[...] = (_rope_wide(proj(o, o + aw), cos, slo, shi) * Q_SCALE).astype(BF16)
    o += aw
    k = _rope_wide(proj(o, o + kvw), cos, slo, shi)
    kb_ref[...] = k.astype(BF16)
    kt_ref[...] = k.T
    o += kvw
    vt = proj(o, o + kvw).T
    vt_ref[...] = vt
    vtb_ref[...] = vt.astype(BF16)
    o += kvw
    qi_ref[...] = _rope_wide(proj(o, o + iw), cos, slo, shi).astype(BF16)
    o += iw
    gs_ref[...] = proj(o, o + d)
    o += d
    ga_ref[...] = proj(o, o + d)
    o += d
    s = proj(o, o + LANES)
    lane = lax.broadcasted_iota(I32, s.shape, 1)
    sm = jnp.where(lane < HEAD, _rope128(s, cos, slo, shi), s)
    sm_ref[...] = sm
    ki2_ref[...] = jnp.where(lane < HEAD, sm, pltpu.roll(sm, HEAD, 1)).astype(BF16)
    kit_ref[...] = sm.T[0:HEAD, :]


def _inproj_attn(x, mod3, g, w_bf, tabs, tm, tpb, aw, kvw, iw):
    m, d = x.shape
    x_spec, mod_spec = _row_specs(tm, tpb, d, mod3.shape[1])
    ntab = tabs[0].shape[0] // tm
    nb = m // (tm * tpb)
    tab_spec = pl.BlockSpec((tm, LANES), lambda i: (i % ntab, 0))
    row = lambda n, dt: (pl.BlockSpec((tm, n), lambda i: (i, 0)), jax.ShapeDtypeStruct((m, n), dt))
    tmin = lambda n, dt: (pl.BlockSpec((None, n, tm), lambda i: (i // tpb, 0, i % tpb)),
                          jax.ShapeDtypeStruct((nb, n, tm * tpb), dt))
    outs = [row(aw, BF16), row(kvw, BF16), tmin(kvw, F32), tmin(kvw, F32), tmin(kvw, BF16), row(iw, BF16),
            row(d, F32), row(d, F32), row(LANES, F32), row(LANES, BF16), tmin(HEAD, F32)]
    return pl.pallas_call(
        functools.partial(_inproj_attn_body, aw, kvw, iw), grid=(m // tm,),
        in_specs=[x_spec, mod_spec, _const_spec((1, d)), _const_spec(w_bf.shape),
                  tab_spec, tab_spec, tab_spec],
        out_specs=[spec for spec, _ in outs],
        out_shape=[shape for _, shape in outs],
        compiler_params=_cparams(("parallel",), 52), name="inproj_attn")(x, mod3, g, w_bf, *tabs)


def _ssd_body(inner, z_ref, xbc_ref, dtr_ref, cw_ref, cb_ref, dtb_ref, a_ref, dsk_ref, ng_ref,
              y_ref, st_ref, full_s, act_s, st_s, y_s):
    c = pl.program_id(1)
    Q, N = SSM_CHUNK, SSM_STATE
    cdim = xbc_ref.shape[1]
    heads_per_group = inner // HEAD // SSM_GROUPS
    gw = inner // SSM_GROUPS

    @pl.when(c == 0)
    def _():
        full_s[0:8, :] = jnp.zeros((8, cdim), F32)
        st_s[...] = jnp.zeros(st_s.shape, F32)

    full_s[8:8 + Q, :] = xbc_ref[...]
    for j in range(0, cdim, 512):
        acc = cb_ref[:, j:j + 512] + full_s[8:8 + Q, j:j + 512] * cw_ref[3:4, j:j + 512]
        for i in range(SSM_CONV - 1):
            acc = acc + full_s[5 + i:5 + i + Q, j:j + 512] * cw_ref[i:i + 1, j:j + 512]
        act_s[:, j:j + 512] = _silu(acc)
    full_s[0:8, :] = full_s[Q:Q + 8, :]

    dt = _softplus(dtr_ref[...] + dtb_ref[...])
    row = lax.broadcasted_iota(I32, (Q, Q), 0)
    col = lax.broadcasted_iota(I32, (Q, Q), 1)
    tri = row >= col
    acs = jnp.dot(tri.astype(F32), dt * a_ref[...], precision=HIGHEST, preferred_element_type=F32)
    acs_t, dt_t = acs.T, dt.T
    last = acs[Q - 1:Q, :]
    wdt = jnp.exp(last - acs) * dt
    eacs = jnp.exp(acs)
    cdec = jnp.exp(last)
    low = lax.broadcasted_iota(I32, (Q, LANES), 1) < HEAD
    low1 = low[0:1, :]

    for g in range(SSM_GROUPS):
        bg = act_s[:, inner + g * N:inner + (g + 1) * N]
        cg = act_s[:, inner + SSM_GROUPS * N + g * N:inner + SSM_GROUPS * N + (g + 1) * N]
        cb = _bdot_nt(cg, bg)
        bg_t = bg.T.astype(BF16)
        for p in range(heads_per_group // 2):
            h0 = g * heads_per_group + 2 * p
            js = slice(h0 * HEAD, h0 * HEAD + LANES)
            xp = act_s[:, js]
            xp_bf = xp.astype(BF16)
            stp = st_s[:, js]
            stp_bf = stp.astype(BF16)
            ys = []
            for h in (h0, h0 + 1):
                seg = acs[:, h:h + 1] - acs_t[h:h + 1, :]
                decay = jnp.exp(jnp.where(tri, seg, -jnp.inf))
                m = (cb * decay) * dt_t[h:h + 1, :]
                ce = cg * eacs[:, h:h + 1]
                ys.append(_bdot(m, xp_bf) + _bdot(ce, stp_bf))
            y_s[:, js] = jnp.where(low, ys[0], ys[1])
            wcol = jnp.where(low, wdt[:, h0:h0 + 1], wdt[:, h0 + 1:h0 + 2])
            dst = jnp.dot(bg_t, (xp * wcol).astype(BF16), preferred_element_type=F32)
            cd = jnp.where(low1, cdec[:, h0:h0 + 1], cdec[:, h0 + 1:h0 + 2])
            st_s[:, js] = stp * cd + dst

    for g in range(SSM_GROUPS):
        gs = slice(g * gw, (g + 1) * gw)
        y = y_s[:, gs] + dsk_ref[:, gs] * act_s[:, gs]
        y = y * _silu(z_ref[:, gs])
        ms = jnp.mean(y * y, axis=-1, keepdims=True)
        y_ref[:, gs] = (y * lax.rsqrt(ms + NORM_EPS) * ng_ref[:, gs]).astype(BF16)

    @pl.when(c == pl.num_programs(1) - 1)
    def _():
        st_ref[...] = st_s[...].T


def _ssd_prompt(z, xbc, dtr, cw, cb, dtb, a, dsk, ng, nb):
    m, inner = z.shape
    cdim = xbc.shape[1]
    nc = m // nb // SSM_CHUNK
    row = lambda n: pl.BlockSpec((SSM_CHUNK, n), lambda b, c: (b * nc + c, 0))
    const = lambda shape: pl.BlockSpec(shape, lambda b, c: (0,) * len(shape))
    return pl.pallas_call(
        functools.partial(_ssd_body, inner), grid=(nb, nc),
        in_specs=[row(inner), row(cdim), row(LANES), const(cw.shape), const(cb.shape),
                  const(dtb.shape), const(a.shape), const(dsk.shape), const(ng.shape)],
        out_specs=[row(inner), pl.BlockSpec((None, inner, SSM_STATE), lambda b, c: (b, 0, 0))],
        out_shape=[jax.ShapeDtypeStruct((m, inner), BF16),
                   jax.ShapeDtypeStruct((nb, inner, SSM_STATE), F32)],
        scratch_shapes=[pltpu.VMEM((SSM_CHUNK + 8, cdim), F32), pltpu.VMEM((SSM_CHUNK, cdim), F32),
                        pltpu.VMEM((SSM_STATE, inner), F32), pltpu.VMEM((SSM_CHUNK, inner), F32)],
        compiler_params=_cparams(("parallel", "arbitrary"), 40), name="ssd_prompt",
    )(z, xbc, dtr, cw, cb, dtb, a, dsk, ng)


def _ssd_step_body(z_ref, xs_ref, bc_ref, cxs_ref, cbc_ref, dtr_ref, st_ref,
                   wxs_ref, wbc_ref, bxs_ref, bbc_ref, dtb_ref, a_ref, dsk_ref, ng_ref,
                   y_ref, sto_ref):
    G = SSM_GROUPS
    last = SSM_CONV - 1
    xs = bxs_ref[...] + xs_ref[...] * wxs_ref[last]
    bc = bbc_ref[...] + bc_ref[...] * wbc_ref[last]
    for i in range(last):
        xs = xs + cxs_ref[i] * wxs_ref[i]
        bc = bc + cbc_ref[i] * wbc_ref[i]
    xs, bc = _silu(xs), _silu(bc)
    dt = _softplus(dtr_ref[...] + dtb_ref[...])
    dec = jnp.exp(dt * a_ref[...])
    xdt = xs * dt
    npair = xs.shape[0]
    pairs_per_group = npair // G
    r = lax.broadcasted_iota(I32, (LANES, LANES), 0)
    cidx = lax.broadcasted_iota(I32, (LANES, LANES), 1)
    eye = (r == cidx).astype(F32)
    nt = (((1,), (1,)), ((), ()))
    dec_t = lax.dot_general(eye, dec, nt, precision=HIGHEST, preferred_element_type=F32)
    xdt_t = lax.dot_general(eye, xdt, nt, precision=HIGHEST, preferred_element_type=F32)
    rows = lax.broadcasted_iota(I32, (npair, 1), 0)
    cbv = jnp.sum(bc[0:G, :] * bc[G:2 * G, :], axis=-1, keepdims=True)
    cbx = jnp.zeros((npair, 1), F32)
    for g in range(G):
        cbx = cbx + jnp.where(rows // pairs_per_group == g, cbv[g:g + 1, :], 0.0)
    c_bf = bc.astype(BF16)
    yoff = jnp.zeros(xs.shape, F32)
    for j in range(npair):
        g = j // pairs_per_group
        s = st_ref[j * LANES:(j + 1) * LANES, :]
        sto_ref[j * LANES:(j + 1) * LANES, :] = s * dec_t[:, j:j + 1] + xdt_t[:, j:j + 1] * bc[g:g + 1, :]
        rj = _bdot_nt(c_bf, s)
        yoff = yoff + jnp.where(rows == j, rj[G + g:G + g + 1, :], 0.0)
    y = yoff * dec + cbx * dt * xs + dsk_ref[...] * xs
    y = y * _silu(z_ref[...])
    ssq = jnp.sum(y * y, axis=-1, keepdims=True)
    msx = jnp.zeros((npair, 1), F32)
    for g in range(G):
        ing = rows // pairs_per_group == g
        tot = jnp.sum(jnp.where(ing, ssq, 0.0), axis=0, keepdims=True)
        msx = msx + jnp.where(ing, tot, 0.0)
    msx = msx / (pairs_per_group * LANES)
    y_ref[...] = y * lax.rsqrt(msx + NORM_EPS) * ng_ref[...]


def _ssd_sample(z, xs, bc, cxs, cbc, dtr, st, wxs, wbc, bxs, bbc, dtb, a, dsk, ng):
    db, npair, _ = z.shape
    per_b = lambda shape: pl.BlockSpec((None,) + shape, lambda b: (b,) + (0,) * len(shape))
    const = lambda arr: pl.BlockSpec(arr.shape, lambda b: (0,) * arr.ndim)
    return pl.pallas_call(
        _ssd_step_body, grid=(db,),
        in_specs=[per_b(z.shape[1:]), per_b(xs.shape[1:]), per_b(bc.shape[1:]), per_b(cxs.shape[1:]),
                  per_b(cbc.shape[1:]), per_b(dtr.shape[1:]), per_b(st.shape[1:]),
                  const(wxs), const(wbc), const(bxs), const(bbc), const(dtb), const(a), const(dsk), const(ng)],
        out_specs=[per_b(z.shape[1:]), per_b(st.shape[1:])],
        out_shape=[jax.ShapeDtypeStruct(z.shape, F32), jax.ShapeDtypeStruct(st.shape, F32)],
        compiler_params=_cparams(("parallel",), 32), name="ssd_sample",
    )(z, xs, bc, cxs, cbc, dtr, st, wxs, wbc, bxs, bbc, dtb, a, dsk, ng)


def _row_fold(x, h, op=jnp.add):
    parts = [x[j:j + h, :] for j in range(0, x.shape[0], h)]
    while len(parts) > 1:
        parts = [op(a, b) for a, b in zip(parts[0::2], parts[1::2])] + (parts[-1:] if len(parts) % 2 else [])
    return parts[0]


IMG_ROWS = 16


def _sort_network(n):
    def merge(lo, hi, r):
        step = 2 * r
        if step < hi - lo:
            yield from merge(lo, hi, step)
            yield from merge(lo + r, hi, step)
            yield from ((i, i + r) for i in range(lo + r, hi - r, step))
        else:
            yield (lo, lo + r)

    def sort(lo, hi):
        if hi - lo >= 1:
            mid = lo + (hi - lo) // 2
            yield from sort(lo, mid)
            yield from sort(mid + 1, hi)
            yield from merge(lo, hi, 1)
    return tuple(sort(0, n - 1))


def _group_members(kb):
    tiles = kb // IMG_ROWS
    size = 16 if tiles % 16 == 0 else 8
    n = tiles // size
    return [[g + n * k for k in range(size)] for g in range(n)]


def _sort_groups(x):
    kb = x.shape[0]
    half = IMG_ROWS // 2
    rows = [x[r:r + half, :] for r in range(0, kb, half)]
    for members in _group_members(kb):
        network = _sort_network(len(members))
        for b in range(2):
            idx = [2 * p + b for p in members]
            vals = [rows[i] for i in idx]
            for i, j in network:
                vals[i], vals[j] = jnp.maximum(vals[i], vals[j]), jnp.minimum(vals[i], vals[j])
            for i, v in zip(idx, vals):
                rows[i] = v
    return jnp.concatenate(rows, axis=0)


def _topk_bias(score_s, sort_s, img_s, bias_s, nblk, kb, topk, qpos, unroll=1):
    nq = score_s.shape[1]
    blk = lambda s: pl.ds(pl.multiple_of(s * kb, kb), kb)
    one16, zero16 = jnp.int16(1), jnp.int16(0)
    work_s = bias_s
    groups = _group_members(kb)

    def over_blocks(body, init):
        def trip(s2, carry):
            for k in range(unroll):
                carry = body(s2 * unroll + k, carry)
            return carry
        return lax.fori_loop(0, nblk // unroll, trip, init)

    def count(c):
        def body(s, acc):
            img = img_s[blk(s), :]
            for members in groups:
                p = [img[m * IMG_ROWS:(m + 1) * IMG_ROWS, :] for m in members]
                masks = []

                def comparand(lo, hi, level):
                    mid = (lo + hi) // 2
                    if level == len(masks):
                        return p[mid]
                    return jnp.where(masks[level], comparand(mid + 1, hi, level + 1), comparand(lo, mid, level + 1))

                depth = len(p).bit_length() - 1
                terms = [jnp.where(p[-1] >= c, one16, zero16)]
                for level in range(depth):
                    masks.append(comparand(0, len(p) - 1, 0) >= c)
                    terms.append(jnp.where(masks[-1], jnp.int16(1 << (depth - 1 - level)), zero16))
                while len(terms) > 1:
                    terms = [a + b for a, b in zip(terms[0::2], terms[1::2])] + (terms[-1:] if len(terms) % 2 else [])
                acc = acc + terms[0]
            return acc
        acc = over_blocks(body, jnp.zeros((IMG_ROWS, nq), I16))
        return jnp.sum(acc.astype(F32), axis=0, keepdims=True)

    def search(nbits, value_of):
        def step(t, u):
            code = u | lax.shift_left(jnp.int32(1), jnp.asarray(nbits - 1 - t, I32))
            c = value_of(code).astype(BF16)
            return jnp.where(count(c) >= topk, code, u)
        return lax.fori_loop(0, nbits, step, jnp.zeros((1, nq), I32))

    def bf16_value(code):
        pattern = jnp.where(code >= 32768, code - 32768, 65535 - code)
        return pltpu.bitcast(pattern << 16, jnp.float32).astype(F32)

    def set_image(fn):
        def body(s, carry):
            img_s[blk(s), :] = fn(s).astype(BF16)
            return carry
        over_blocks(body, 0)

    t1 = bf16_value(search(16, bf16_value))

    e1 = jnp.clip((pltpu.bitcast(t1.astype(jnp.float32), I32) >> 23) & 0xFF, 25, 254)
    unit = pltpu.bitcast((e1 - 24) << 23, jnp.float32).astype(F32)
    inv_unit = pltpu.bitcast((278 - e1) << 23, jnp.float32).astype(F32)
    B2, B1 = 65536.0, 256.0

    def digit2(s):
        y = (sort_s[blk(s), :] - t1) * inv_unit
        work_s[blk(s), :] = y
        return jnp.floor(y * (1.0 / B2))
    set_image(digit2)
    t2 = (search(2, lambda code: (code - 1).astype(F32)) - 1).astype(F32)
    set_image(lambda s: jnp.floor((work_s[blk(s), :] - t2 * B2) * (1.0 / B1)))
    t3 = search(8, lambda code: code.astype(F32)).astype(F32)
    set_image(lambda s: jnp.floor(work_s[blk(s), :] - (t2 * B2 + t3 * B1)))
    t4 = search(8, lambda code: code.astype(F32)).astype(F32)
    v0 = t1 + (t2 * B2 + t3 * B1 + t4) * unit

    def smallest(keep):
        def body(s, acc):
            x = score_s[blk(s), :]
            return jnp.minimum(acc, _row_fold(jnp.where(keep(x), x, jnp.inf), 8, jnp.minimum))
        acc = over_blocks(body, jnp.full((8, nq), jnp.inf, F32))
        return jnp.min(acc, axis=0, keepdims=True)

    def count_above(v):
        def body(s, acc):
            return acc + _row_fold(jnp.where(score_s[blk(s), :] > v, 1.0, 0.0), 8)
        return jnp.sum(over_blocks(body, jnp.zeros((8, nq), F32)), axis=0, keepdims=True)

    def refine(carry):
        v, above = carry
        v = jnp.where(above >= topk, smallest(lambda x: x > v), v)
        return v, count_above(v)

    v = smallest(lambda x: x >= v0)
    v, above = lax.while_loop(lambda c: jnp.max(c[1]) >= topk, refine, (v, count_above(v)))
    need = topk - above

    r_i = lax.broadcasted_iota(I32, (LANES, LANES), 0)
    c_i = lax.broadcasted_iota(I32, (LANES, LANES), 1)
    lower = jnp.where(r_i >= c_i, 1.0, 0.0).astype(BF16)
    sub_iota = lax.broadcasted_iota(I32, (LANES, 1), 0)

    def bias_body(s, carry):
        for j in range(0, kb, LANES):
            off = pl.multiple_of(s * kb + j, LANES)
            x = score_s[pl.ds(off, LANES), :]
            eq = x == v
            eqf = jnp.where(eq, 1.0, 0.0)
            incl = jnp.dot(lower, eqf.astype(BF16), preferred_element_type=F32)
            tie = jnp.where(carry + incl - eqf < need, 0.0, -jnp.inf)
            b = jnp.where(x > v, 0.0, jnp.where(eq, tie, -jnp.inf))
            bias_s[pl.ds(off, LANES), :] = jnp.where((off + sub_iota) <= qpos, b, -jnp.inf)
            carry = carry + incl[LANES - 1:LANES, :]
        return carry

    over_blocks(bias_body, jnp.zeros((1, nq), F32))


def _dsa_t_body(topk, idx_scale, q_ref, qi_ref, sm_ref, ki2_ref, k_ref, vt_ref, o_ref,
                score_s, sort_s, img_s, bias_s, qs_s, lga_s, lgb_s, m_s, acc_s):
    i = pl.program_id(1)
    QB, KB = Q_BLOCK, KEY_BLOCK
    nkb = (i * QB + QB + KB - 1) // KB
    qpos = i * QB + lax.broadcasted_iota(I32, (1, QB), 1)
    low = lax.broadcasted_iota(I32, (QB, LANES), 1) < HEAD
    zero_bf = jnp.zeros((QB, LANES), BF16)
    blk = lambda s: pl.ds(pl.multiple_of(s * KB, KB), KB)

    qi = qi_ref[...]
    sm_t = sm_ref[...].T
    qh, wh = [], []
    for h in range(IDX_HEADS):
        chunk = qi[:, (h // 2) * LANES:(h // 2 + 1) * LANES]
        qh.append(jnp.where(low if h % 2 == 0 else ~low, chunk, zero_bf))
        wh.append(sm_t[HEAD + h:HEAD + h + 1, :] * idx_scale)

    q_stack = jnp.concatenate(qh, axis=0)

    last_key_block = ki2_ref.shape[0] // KB - 1

    def score_pair(s2, carry):
        blocks = (2 * s2, 2 * s2 + 1)
        dots = [_bdot_nt(ki2_ref[blk(jnp.minimum(s, last_key_block)), :], q_stack) for s in blocks]
        for s, sc in zip(blocks, dots):
            acc = jnp.zeros((KB, QB), F32)
            for h in range(IDX_HEADS):
                acc = acc + wh[h] * jnp.maximum(sc[:, h * QB:(h + 1) * QB], 0.0)
            kpos = s * KB + lax.broadcasted_iota(I32, (KB, 1), 0)
            sc = jnp.where(kpos <= qpos, jnp.maximum(acc, MASKED_SCORE), MASKED_SCORE)
            score_s[blk(s), :] = sc
            sc = _sort_groups(sc)
            sort_s[blk(s), :] = sc
            img_s[blk(s), :] = sc.astype(BF16)
        return carry

    nkb_even = 2 * ((nkb + 1) // 2)
    lax.fori_loop(0, nkb_even // 2, score_pair, 0)
    _topk_bias(score_s, sort_s, img_s, bias_s, nkb_even, KB, topk, qpos, unroll=2)

    q = q_ref[...]
    nchunk = q.shape[1] // LANES
    per_kv_chunk = nchunk // (ATTN_KV_HEADS // 2)
    nstack = 2 * per_kv_chunk
    srows = nstack * QB
    for cj in range(nchunk):
        kvc, j = divmod(cj, per_kv_chunk)
        qc = q[:, cj * LANES:(cj + 1) * LANES]
        for half in range(2):
            r0 = (kvc * nstack + half * per_kv_chunk + j) * QB
            qs_s[r0:r0 + QB, :] = jnp.where(low if half == 0 else ~low, qc, zero_bf)

    ones_rows = jnp.ones((ONES_ROWS, KB), BF16)
    cols = lambda r: slice(r * QB, (r + 1) * QB)
    n_kvc = ATTN_KV_HEADS // 2
    nsteps = n_kvc * nkb

    def step_of(t):
        t = jnp.minimum(t, nsteps - 1)
        kvc = (t >= nkb).astype(I32)
        return kvc, t - kvc * nkb

    def logits_to(dst, t):
        kvc, s = step_of(t)
        kb = k_ref[blk(s), pl.ds(pl.multiple_of(kvc * LANES, LANES), LANES)]
        bias = bias_s[blk(s), :]
        lg = _bdot_nt(kb, qs_s[pl.ds(pl.multiple_of(kvc * srows, srows), srows), :])
        for r in range(nstack):
            dst[:, cols(r)] = lg[:, cols(r)] + bias

    def consume(src, t):
        kvc, s = step_of(t)
        vt = vt_ref[pl.ds(pl.multiple_of(kvc * LANES, LANES), LANES), blk(s)]
        ps, alphas = [], []
        for r in range(nstack):
            lg = src[:, cols(r)]
            m = m_s[kvc, :, cols(r)]
            mn = jnp.maximum(m, jnp.max(lg, axis=0, keepdims=True))
            m_s[kvc, :, cols(r)] = mn
            ps.append(jnp.exp2(lg - mn).astype(BF16))
            alphas.append(jnp.exp2(m - mn))
        for half in range(2):
            hs = slice(half * per_kv_chunk, (half + 1) * per_kv_chunk)
            hc = slice(half * per_kv_chunk * QB, (half + 1) * per_kv_chunk * QB)
            v_aug = jnp.concatenate([vt[half * HEAD:(half + 1) * HEAD, :], ones_rows], axis=0)
            pv = jnp.dot(v_aug, jnp.concatenate(ps[hs], axis=1), preferred_element_type=F32)
            acc_s[kvc, :, hc] = jnp.concatenate(alphas[hs], axis=1) * acc_s[kvc, :, hc] + pv

    m_s[...] = jnp.full(m_s.shape, NEG_BIG, F32)
    acc_s[...] = jnp.zeros(acc_s.shape, F32)
    logits_to(lga_s, 0)

    def run_pairs(t0, npairs):
        for p in range(npairs):
            logits_to(lgb_s, t0 + 2 * p + 1)
            consume(lga_s, t0 + 2 * p)
            logits_to(lga_s, t0 + 2 * p + 2)
            consume(lgb_s, t0 + 2 * p + 1)

    def quad_body(i4, carry):
        run_pairs(4 * i4, 2)
        return carry

    lax.fori_loop(0, nsteps // 4, quad_body, 0)

    @pl.when(nsteps % 4 == 2)
    def _():
        run_pairs(nsteps - 2, 1)

    for kvc in range(n_kvc):
        acc = acc_s[kvc]
        for j in range(per_kv_chunk):
            a = acc[:, cols(j)]
            b = acc[:, cols(per_kv_chunk + j)]
            chunk_t = jnp.concatenate([a[0:HEAD, :] / a[HEAD:HEAD + 1, :], b[0:HEAD, :] / b[HEAD:HEAD + 1, :]],
                                      axis=0)
            cj = kvc * per_kv_chunk + j
            o_ref[:, cj * LANES:(cj + 1) * LANES] = chunk_t.T.astype(BF16)


def _dsa_prompt_t(q, qi, sm, ki2, k, vt, nb, topk):
    m, aw = q.shape
    t = m // nb
    nq = t // Q_BLOCK
    tpad = -(-t // (2 * KEY_BLOCK)) * 2 * KEY_BLOCK
    srows = aw // HEAD // (ATTN_KV_HEADS // 2) * Q_BLOCK
    idx_scale = IDX_HEADS ** -0.5 * HEAD ** -0.5
    row = lambda n: pl.BlockSpec((Q_BLOCK, n), lambda b, i: (b * nq + i, 0))
    per_b = lambda n: pl.BlockSpec((t, n), lambda b, i: (b, 0))
    return pl.pallas_call(
        functools.partial(_dsa_t_body, topk, idx_scale), grid=(nb, nq),
        in_specs=[row(aw), row(qi.shape[1]), row(LANES), per_b(LANES), per_b(k.shape[1]),
                  pl.BlockSpec((None,) + vt.shape[1:], lambda b, i: (b, 0, 0))],
        out_specs=row(aw),
        out_shape=jax.ShapeDtypeStruct((m, aw), BF16),
        scratch_shapes=[pltpu.VMEM((tpad, Q_BLOCK), F32), pltpu.VMEM((tpad, Q_BLOCK), F32),
                        pltpu.VMEM((tpad, Q_BLOCK), BF16), pltpu.VMEM((tpad, Q_BLOCK), F32),
                        pltpu.VMEM((2 * srows, LANES), BF16),
                        pltpu.VMEM((KEY_BLOCK, srows), F32), pltpu.VMEM((KEY_BLOCK, srows), F32),
                        pltpu.VMEM((ATTN_KV_HEADS // 2, 1, srows), F32),
                        pltpu.VMEM((ATTN_KV_HEADS // 2, HEAD + ONES_ROWS, srows), F32)],
        compiler_params=_cparams(("parallel", "arbitrary"), 52), name="dsa_prompt",
    )(q, qi, sm, ki2, k, vt)


def _page_specs(block, npages):
    def make(u):
        return pl.BlockSpec((None,) + block, lambda b, pt: (pt[b, u],) + (0,) * len(block))
    return [make(u) for u in range(npages)]


def _idx_score_body(pt_ref, qi_ref, w_ref, *refs):
    pages, o_ref = refs[:-1], refs[-1]
    qi = qi_ref[...]
    w = w_ref[...]
    for u, page in enumerate(pages):
        s = jnp.maximum(_bdot(qi, page[...]), 0.0)
        o_ref[:, u * PAGE_SIZE:(u + 1) * PAGE_SIZE] = jnp.sum(w * s, axis=0, keepdims=True)


def _idx_scores_sample(page_table, qi8, w8, kidx_t):
    db, npages = page_table.shape
    grid_spec = pltpu.PrefetchScalarGridSpec(
        num_scalar_prefetch=1, grid=(db,),
        in_specs=[pl.BlockSpec((None,) + qi8.shape[1:], lambda b, pt: (b, 0, 0)),
                  pl.BlockSpec((None,) + w8.shape[1:], lambda b, pt: (b, 0, 0))]
                 + _page_specs(kidx_t.shape[1:], npages),
        out_specs=pl.BlockSpec((None, 1, npages * PAGE_SIZE), lambda b, pt: (b, 0, 0)))
    return pl.pallas_call(
        _idx_score_body, grid_spec=grid_spec,
        out_shape=jax.ShapeDtypeStruct((db, 1, npages * PAGE_SIZE), F32),
        compiler_params=_cparams(("parallel",), 32), name="idx_scores_sample",
    )(page_table, qi8, w8, *([kidx_t] * npages))


def _select_sample_body(topk, idx_scale, past, sc_ref, qi_ref, sm_ref, bias_ref, score_s, sort_s, img_s, bias_s):
    rows = sc_ref.shape[0]
    sm = sm_ref[...]
    qi = qi_ref[...]
    ki = sm[:, 0:HEAD]
    new = jnp.zeros((rows, 1), F32)
    for h in range(IDX_HEADS):
        d = jnp.sum(qi[:, h * HEAD:(h + 1) * HEAD] * ki, axis=-1, keepdims=True)
        new = new + (sm[:, HEAD + h:HEAD + h + 1] * idx_scale) * jnp.maximum(d, 0.0)
    nblk = (past + LANES) // LANES
    lane = lax.broadcasted_iota(I32, (rows, LANES), 1)
    for j in range(nblk):
        js = slice(j * LANES, (j + 1) * LANES)
        sc = sc_ref[:, js] if j < nblk - 1 else jnp.where(lane == 0, new, MASKED_SCORE)
        sc = jnp.maximum(sc, MASKED_SCORE).T
        score_s[js, :] = sc
        sc = _sort_groups(sc)
        sort_s[js, :] = sc
        img_s[js, :] = sc.astype(BF16)
    qpos = jnp.full((1, rows), past, I32)
    _topk_bias(score_s, sort_s, img_s, bias_s, nblk, LANES, topk, qpos)
    for j in range(nblk):
        js = slice(j * LANES, (j + 1) * LANES)
        bias_ref[:, js] = bias_s[js, :].T


def _select_sample(scores, qi, sm, topk, idx_scale):
    db, past = scores.shape
    full = lambda a: pl.BlockSpec(a.shape, lambda i: (0,) * a.ndim)
    keys = past + LANES
    return pl.pallas_call(
        functools.partial(_select_sample_body, topk, idx_scale, past), grid=(1,),
        in_specs=[full(scores), full(qi), full(sm)],
        out_specs=pl.BlockSpec((db, keys), lambda i: (0, 0)),
        out_shape=jax.ShapeDtypeStruct((db, keys), F32),
        scratch_shapes=[pltpu.VMEM((keys, db), F32), pltpu.VMEM((keys, db), F32), pltpu.VMEM((keys, db), BF16),
                        pltpu.VMEM((keys, db), F32)],
        compiler_params=_cparams(("arbitrary",), 32), name="select_sample")(scores, qi, sm)


def _attend_sample_body(npages, pt_ref, q_ref, bias_ref, knew_ref, vnew_ref, *refs):
    kpages, vpages, o_ref = refs[:npages], refs[npages:2 * npages], refs[2 * npages]
    G = ATTN_KV_HEADS
    q = q_ref[...]
    q_bf = q.astype(BF16)
    nh = q.shape[0]
    past = npages * PAGE_SIZE
    group = lax.broadcasted_iota(I32, (nh, 1), 0) // (nh // G)

    def by_group(parts):
        out = parts[G - 1]
        for g in range(G - 2, -1, -1):
            out = jnp.where(group == g, parts[g], out)
        return out

    lg = jnp.concatenate(
        [by_group([_bdot(q_bf, kpages[u][g]) for g in range(G)]) for u in range(npages)], axis=1)
    lg = lg + bias_ref[:, 0:past]
    lg_new = by_group([jnp.sum(q * knew_ref[g:g + 1, :], axis=-1, keepdims=True) for g in range(G)])
    lg_new = lg_new + bias_ref[:, past:past + 1]
    m = jnp.maximum(jnp.max(lg, axis=-1, keepdims=True), lg_new)
    p = jnp.exp2(lg - m)
    p_new = jnp.exp2(lg_new - m)
    denom = jnp.sum(p, axis=-1, keepdims=True) + p_new
    p_bf = p.astype(BF16)
    accs = [p_new * vnew_ref[g:g + 1, :] for g in range(G)]
    for u in range(npages):
        pu = p_bf[:, u * PAGE_SIZE:(u + 1) * PAGE_SIZE]
        for g in range(G):
            accs[g] = accs[g] + _bdot_nt(pu, vpages[u][g])
    o_ref[...] = by_group(accs) / denom


def _attend_sample(page_table, q, bias, knew, vnew, k_t, v_t):
    db, npages = page_table.shape
    per_b = lambda a: pl.BlockSpec((None,) + a.shape[1:], lambda b, pt: (b,) + (0,) * (a.ndim - 1))
    grid_spec = pltpu.PrefetchScalarGridSpec(
        num_scalar_prefetch=1, grid=(db,),
        in_specs=[per_b(q), per_b(bias), per_b(knew), per_b(vnew)]
                 + _page_specs(k_t.shape[1:], npages) + _page_specs(v_t.shape[1:], npages),
        out_specs=per_b(q))
    return pl.pallas_call(
        functools.partial(_attend_sample_body, npages), grid_spec=grid_spec,
        out_shape=jax.ShapeDtypeStruct(q.shape, F32),
        compiler_params=_cparams(("parallel",), 48), name="attend_sample",
    )(page_table, q, bias, knew, vnew, *([k_t] * npages), *([v_t] * npages))


def _merge_body(x_ref, mod_ref, ys_ref, ya_ref, gs_ref, ga_ref, wps_ref, wpa_ref, wo_ref, o_ref):
    d = x_ref.shape[1]
    merged = (jax.nn.sigmoid(gs_ref[...]) * _bdot(ys_ref[...], wps_ref[...])
              + jax.nn.sigmoid(ga_ref[...]) * _bdot(ya_ref[...], wpa_ref[...]))
    o_ref[...] = x_ref[...] + mod_ref[:, 2 * d:3 * d] * _bdot(merged, wo_ref[...])


def _merge(x, mod3, ys, ya, gs, ga, wps, wpa, wo, tm, tpb):
    m, d = x.shape
    x_spec, mod_spec = _row_specs(tm, tpb, d, mod3.shape[1])
    row = lambda n: pl.BlockSpec((tm, n), lambda i: (i, 0))
    return pl.pallas_call(
        _merge_body, grid=(m // tm,),
        in_specs=[x_spec, mod_spec, row(ys.shape[1]), row(ya.shape[1]), row(d), row(d),
                  _const_spec(wps.shape), _const_spec(wpa.shape), _const_spec(wo.shape)],
        out_specs=row(d), out_shape=jax.ShapeDtypeStruct((m, d), F32),
        compiler_params=_cparams(("parallel",), 48), name="merge")(x, mod3, ys, ya, gs, ga, wps, wpa, wo)


def _ffn_body(last_layer, x_ref, mod_ref, g_ref, fg_ref, wg_ref, wu_ref, wo_ref, o_ref):
    d = x_ref.shape[1]
    x = x_ref[...]
    h = _norm_mod(x, g_ref[...], mod_ref[:, 4 * d:5 * d], mod_ref[:, 3 * d:4 * d]).astype(BF16)
    gate = jnp.dot(h, wg_ref[...], preferred_element_type=F32)
    up = jnp.dot(h, wu_ref[...], preferred_element_type=F32)
    x2 = x + mod_ref[:, 5 * d:6 * d] * _bdot(_silu(gate) * up, wo_ref[...])
    if last_layer:
        ms = jnp.mean(x2 * x2, axis=-1, keepdims=True)
        x2 = x2 * lax.rsqrt(ms + NORM_EPS) * fg_ref[...]
    o_ref[...] = x2


def _ffn(x, mod3, g, fg, wg, wu, wo, tm, tpb, last_layer):
    m, d = x.shape
    x_spec, mod_spec = _row_specs(tm, tpb, d, mod3.shape[1])
    return pl.pallas_call(
        functools.partial(_ffn_body, last_layer), grid=(m // tm,),
        in_specs=[x_spec, mod_spec, _const_spec((1, d)), _const_spec((1, d)),
                  _const_spec(wg.shape), _const_spec(wu.shape), _const_spec(wo.shape)],
        out_specs=pl.BlockSpec((tm, d), lambda i: (i, 0)), out_shape=jax.ShapeDtypeStruct((m, d), F32),
        compiler_params=_cparams(("parallel",), 56), name="ffn")(x, mod3, g, fg, wg, wu, wo)


def _rope_tables(pos):
    half = HEAD // 2
    inv = ROPE_THETA ** (-jnp.arange(half, dtype=F32) / half)
    ang = pos.astype(F32)[:, None] * inv[None, :]
    cos = jnp.tile(jnp.cos(ang), (1, LANES // half))
    sin = jnp.tile(jnp.sin(ang), (1, LANES // half))
    first = (jnp.arange(LANES) % HEAD) < half
    return cos, jnp.where(first, -sin, 0.0), jnp.where(first, 0.0, sin)


def _q_head_order(n_heads):
    rep = n_heads // ATTN_KV_HEADS
    order = []
    for c in range(ATTN_KV_HEADS // 2):
        for j in range(rep):
            order += [2 * c * rep + j, (2 * c + 1) * rep + j]
    return np.asarray(order)


def kernel(x_prompt, x_sample, cache_k, cache_v, cache_kidx, state_conv, state_ssm, page_table, c_prompt, c_sample, w_ada, b_ada, norm1_g, w_in, conv_w, conv_b, dt_bias, a_log, d_skip, ssm_norm_g, w_proj_ssm, w_proj_attn, w_out, norm2_g, w_ffn_in, w_ffn_out, final_g):
    nb, t, d = x_prompt.shape
    db, ds, _ = x_sample.shape
    depth = w_in.shape[0]
    assert ds == 1 and t % Q_BLOCK == 0 and t % SSM_CHUNK == 0
    n_heads_ssm = dt_bias.shape[1]
    inner = n_heads_ssm * HEAD
    gn = SSM_GROUPS * SSM_STATE
    cdim = inner + 2 * gn
    kvw = ATTN_KV_HEADS * HEAD
    aw = w_proj_attn.shape[1]
    n_heads = aw // HEAD
    iw = IDX_HEADS * HEAD
    ffn_hidden = w_ffn_out.shape[1]
    npages = page_table.shape[1]
    past = npages * PAGE_SIZE
    topk_p = min(TOPK_MAX, t // 4)
    topk_s = min(TOPK_MAX, (past + ds) // 4)
    assert past + ds >= topk_s
    idx_scale = IDX_HEADS ** -0.5 * HEAD ** -0.5
    tm_p = 256 if t % 256 == 0 else 128
    tpb_p = t // tm_p

    splits = np.cumsum([inner, inner, gn, gn, n_heads_ssm, aw, kvw, kvw, iw, HEAD, IDX_HEADS, d])
    order = _q_head_order(n_heads)
    inv_order = np.argsort(order)

    cos_p, slo_p, shi_p = _rope_tables(jnp.arange(t, dtype=I32))
    tabs_p = (cos_p, slo_p, shi_p)
    tabs_s = tuple(jnp.broadcast_to(a, (db, LANES)) for a in _rope_tables(past + jnp.arange(ds, dtype=I32)))

    rows_c = nb + db
    c_all = jnp.concatenate([c_prompt, c_sample, jnp.zeros((-rows_c % 8, d), F32)], axis=0)

    yp = x_prompt.reshape(nb * t, d)
    ys = x_sample.reshape(db, d)
    outs_p, outs_s = [], []
    for l in range(depth):
        (wz, wxs, wbm, wcm, wdt, wq, wk, wv, wqi, wki, wwi, wgs, wga) = jnp.split(w_in[l], splits, axis=1)
        w_ssm = jnp.concatenate([wz, wxs, wbm, wcm, wdt, jnp.zeros((d, LANES - n_heads_ssm), F32)],
                                axis=1).astype(BF16)
        wq_perm = wq.reshape(d, n_heads, HEAD)[:, order].reshape(d, aw)
        w_attn = jnp.concatenate([wq_perm, wk, wv, wqi, wgs, wga, wki, wwi,
                                  jnp.zeros((d, LANES - HEAD - IDX_HEADS), F32)], axis=1).astype(BF16)
        wps = w_proj_ssm[l].astype(BF16)
        wpa = w_proj_attn[l].reshape(n_heads, HEAD, d)[order].reshape(aw, d).astype(BF16)
        wo = w_out[l].astype(BF16)
        wg = w_ffn_in[l][:, :ffn_hidden].astype(BF16)
        wu = w_ffn_in[l][:, ffn_hidden:].astype(BF16)
        wfo = w_ffn_out[l].astype(BF16)
        g1 = norm1_g[l][None, :]
        g2 = norm2_g[l][None, :]
        a_neg = -jnp.exp(a_log[l])
        pad_h = LANES - n_heads_ssm
        dtb_row = jnp.pad(dt_bias[l], (0, pad_h))[None, :]
        a_row = jnp.pad(a_neg, (0, pad_h))[None, :]
        dsk_row = jnp.repeat(d_skip[l], HEAD)[None, :]
        ng_row = ssm_norm_g[l][None, :]
        cw = conv_w[l]
        cb = conv_b[l][None, :]

        mod = _ada(c_all, w_ada[l].astype(BF16), b_ada[l][None, :])
        mod_p = mod[:nb][:, None, :]
        mod_s = mod[nb:nb + db][None]

        z, xbc, dtr = _inproj_ssm(yp, mod_p, g1, w_ssm, tm_p, tpb_p, inner, cdim)
        (q_bf, k_bf, kt, vt, vt_bf, qi_bf, gs, ga, sm, ki2, kit) = _inproj_attn(
            yp, mod_p, g1, w_attn, tabs_p, tm_p, tpb_p, aw, kvw, iw)
        y_ssm, st = _ssd_prompt(z, xbc, dtr, cw, cb, dtb_row, a_row, dsk_row, ng_row, nb)
        y_attn = _dsa_prompt_t(q_bf, qi_bf, sm, ki2, k_bf, vt_bf, nb, topk_p)
        x1 = _merge(yp, mod_p, y_ssm, y_attn, gs, ga, wps, wpa, wo, tm_p, tpb_p)
        yp_next = _ffn(x1, mod_p, g2, final_g[None, :], wg, wu, wfo, tm_p, tpb_p, l == depth - 1)
        heads_last = lambda a: jnp.transpose(a.reshape(a.shape[0], ATTN_KV_HEADS, HEAD, a.shape[2]), (0, 3, 1, 2))
        outs_p.append((heads_last(kt), heads_last(vt), jnp.transpose(kit, (0, 2, 1)),
                       xbc.reshape(nb, t, cdim)[:, t - (SSM_CONV - 1):],
                       st.reshape(nb, n_heads_ssm, HEAD, SSM_STATE)))

        z_s, xbc_s, dtr_s = _inproj_ssm(ys, mod_s, g1, w_ssm, db, 1, inner, cdim)
        (q_s, _, kt_s, vt_s, _, qi_s, gs_s, ga_s, sm_s, _, kit_s) = _inproj_attn(
            ys, mod_s, g1, w_attn, tabs_s, db, 1, aw, kvw, iw)
        k_s, v_s = heads_last(kt_s)[0], heads_last(vt_s)[0]
        npair = inner // LANES
        nbc = 2 * gn // LANES
        sc = state_conv[l]
        y_ssm_s, st_s = _ssd_sample(
            z_s.reshape(db, npair, LANES), xbc_s[:, :inner].reshape(db, npair, LANES),
            xbc_s[:, inner:].reshape(db, nbc, LANES),
            sc[:, :, :inner].reshape(db, SSM_CONV - 1, npair, LANES),
            sc[:, :, inner:].reshape(db, SSM_CONV - 1, nbc, LANES),
            jnp.repeat(dtr_s[:, :n_heads_ssm], HEAD, axis=1).reshape(db, npair, LANES),
            state_ssm[l].reshape(db, inner, SSM_STATE),
            cw[:, :inner].reshape(SSM_CONV, npair, LANES), cw[:, inner:].reshape(SSM_CONV, nbc, LANES),
            cb[:, :inner].reshape(npair, LANES), cb[:, inner:].reshape(nbc, LANES),
            jnp.repeat(dt_bias[l], HEAD).reshape(npair, LANES), jnp.repeat(a_neg, HEAD).reshape(npair, LANES),
            dsk_row.reshape(npair, LANES), ng_row.reshape(npair, LANES))

        qi_f = qi_s.astype(F32)
        qi8 = jnp.pad(qi_f.reshape(db, IDX_HEADS, HEAD), ((0, 0), (0, 8 - IDX_HEADS), (0, 0)))
        w8 = jnp.broadcast_to(jnp.pad(sm_s[:, HEAD:HEAD + IDX_HEADS] * idx_scale,
                                      ((0, 0), (0, 8 - IDX_HEADS)))[:, :, None], (db, 8, LANES))
        scores = _idx_scores_sample(page_table, qi8, w8, jnp.transpose(cache_kidx[l], (0, 2, 1)))
        bias = _select_sample(scores.reshape(db, past), qi_f, sm_s, topk_s, idx_scale)
        q_orig = q_s.astype(F32).reshape(db, n_heads, HEAD)[:, inv_order]
        att = _attend_sample(page_table, q_orig, bias[:, None, :],
                             k_s, v_s,
                             jnp.transpose(cache_k[l], (0, 2, 3, 1)), jnp.transpose(cache_v[l], (0, 2, 3, 1)))
        y_attn_s = att[:, order].reshape(db, aw)
        x1_s = _merge(ys, mod_s, y_ssm_s.reshape(db, inner), y_attn_s, gs_s, ga_s, wps, wpa, wo, db, 1)
        ys_next = _ffn(x1_s, mod_s, g2, final_g[None, :], wg, wu, wfo, db, 1, l == depth - 1)
        outs_s.append((k_s[:, None], v_s[:, None], jnp.transpose(kit_s, (2, 0, 1)),
                       jnp.concatenate([sc[:, 1:], xbc_s[:, None, :]], axis=1),
                       st_s.reshape(db, n_heads_ssm, HEAD, SSM_STATE)))
        yp, ys = yp_next, ys_next

    stack = lambda outs, i: jnp.stack([o[i] for o in outs], axis=0)
    return (yp.reshape(nb, t, d), ys.reshape(db, ds, d),
            stack(outs_p, 0), stack(outs_p, 1), stack(outs_p, 2), stack(outs_p, 3), stack(outs_p, 4),
            stack(outs_s, 0), stack(outs_s, 1), stack(outs_s, 2), stack(outs_s, 3), stack(outs_s, 4))
```

```python
import functools

import jax
import jax.numpy as jnp
import numpy as np
from jax import lax
from jax.experimental import pallas as pl
from jax.experimental.pallas import tpu as pltpu

F32, BF16, I32, I16 = jnp.float32, jnp.bfloat16, jnp.int32, jnp.int16
HIGHEST = lax.Precision.HIGHEST

LANES = 128
HEAD = 64
SSM_STATE = 128
SSM_GROUPS = 4
SSM_CONV = 4
SSM_CHUNK = 128
ATTN_KV_HEADS = 4
IDX_HEADS = 4
TOPK_MAX = 256
Q_BLOCK = 128
PAGE_SIZE = 128
ROPE_THETA = 10000.0
NORM_EPS = 1e-6
KEY_BLOCK = 512
ONES_ROWS = 16
Q_SCALE = HEAD ** -0.5 * 1.4426950408889634
MASKED_SCORE = -3.3895313892515355e38
NEG_BIG = -1e30


VMEM_LIMIT_MB = {"ada": 32, "inproj_ssm": 52, "inproj_attn": 52, "ssd_prompt": 40, "ssd_sample": 32,
                 "dsa_prompt": 52, "idx_scores_sample": 32, "select_sample": 32, "attend_sample": 48,
                 "merge": 48, "ffn": 56}


def _call_opts(name, sem):
    return dict(name=name, compiler_params=pltpu.CompilerParams(
        dimension_semantics=sem, vmem_limit_bytes=VMEM_LIMIT_MB[name] << 20))


def _bdot(a, b):
    return jnp.dot(a.astype(BF16), b.astype(BF16), preferred_element_type=F32)


def _bdot_nt(a, b):
    return lax.dot_general(a.astype(BF16), b.astype(BF16), (((1,), (1,)), ((), ())),
                           preferred_element_type=F32)


def _silu(x):
    h = 0.5 * x
    return h + h * jnp.tanh(h)


def _softplus(x):
    return jnp.maximum(x, 0.0) + jnp.log(1.0 + jnp.exp(-jnp.abs(x)))


def _norm_mod(x, g, scale, shift):
    ms = jnp.mean(x * x, axis=-1, keepdims=True)
    return (x * lax.rsqrt(ms + NORM_EPS) * g) * (1.0 + scale) + shift


def _rope128(x, cos, sin_lo, sin_hi):
    return x * cos + pltpu.roll(x, 96, 1) * sin_lo + pltpu.roll(x, 32, 1) * sin_hi


def _rope_wide(x, cos, sin_lo, sin_hi):
    parts = [_rope128(x[:, j:j + LANES], cos, sin_lo, sin_hi) for j in range(0, x.shape[1], LANES)]
    return parts[0] if len(parts) == 1 else jnp.concatenate(parts, axis=1)


def _ada_body(c_ref, w_ref, b_ref, o_ref):
    o_ref[...] = _bdot(_silu(c_ref[...]), w_ref[...]) + b_ref[...]


def _ada(c_all, w_bf, b):
    mp, d = c_all.shape
    n = w_bf.shape[1]
    tn = n // 4
    return pl.pallas_call(
        _ada_body, grid=(n // tn,),
        in_specs=[pl.BlockSpec((mp, d), lambda j: (0, 0)),
                  pl.BlockSpec((d, tn), lambda j: (0, j)),
                  pl.BlockSpec((1, tn), lambda j: (0, j))],
        out_specs=pl.BlockSpec((mp, tn), lambda j: (0, j)),
        out_shape=jax.ShapeDtypeStruct((mp, n), F32),
        **_call_opts("ada", ("arbitrary",)))(c_all, w_bf, b)


def _inproj_ssm_body(inner, cdim, x_ref, mod_ref, g_ref, w_ref, z_ref, xbc_ref, dt_ref):
    d = x_ref.shape[1]
    h = _norm_mod(x_ref[...], g_ref[...], mod_ref[:, d:2 * d], mod_ref[:, 0:d]).astype(BF16)
    z_ref[...] = jnp.dot(h, w_ref[:, 0:inner], preferred_element_type=F32)
    xbc_ref[...] = jnp.dot(h, w_ref[:, inner:inner + cdim], preferred_element_type=F32)
    dt_ref[...] = jnp.dot(h, w_ref[:, inner + cdim:inner + cdim + LANES], preferred_element_type=F32)


def _row_specs(tm, tpb, d, mod_rows):
    x_spec = pl.BlockSpec((tm, d), lambda m: (m, 0))
    mod_spec = pl.BlockSpec((None, mod_rows, 6 * d), lambda m: (m // tpb, 0, 0))
    return x_spec, mod_spec


def _const_spec(shape):
    return pl.BlockSpec(shape, lambda m: (0,) * len(shape))


def _inproj_ssm(x, mod3, g, w_bf, tm, tpb, inner, cdim):
    m, d = x.shape
    x_spec, mod_spec = _row_specs(tm, tpb, d, mod3.shape[1])
    row = lambda n: pl.BlockSpec((tm, n), lambda i: (i, 0))
    return pl.pallas_call(
        functools.partial(_inproj_ssm_body, inner, cdim), grid=(m // tm,),
        in_specs=[x_spec, mod_spec, _const_spec((1, d)), _const_spec(w_bf.shape)],
        out_specs=[row(inner), row(cdim), row(LANES)],
        out_shape=[jax.ShapeDtypeStruct((m, inner), F32), jax.ShapeDtypeStruct((m, cdim), F32),
                   jax.ShapeDtypeStruct((m, LANES), F32)],
        **_call_opts("inproj_ssm", ("parallel",)))(x, mod3, g, w_bf)


def _inproj_attn_body(aw, kvw, iw, x_ref, mod_ref, g_ref, w_ref, cos_ref, slo_ref, shi_ref,
                      q_ref, kb_ref, kt_ref, vt_ref, vtb_ref, qi_ref, gs_ref, ga_ref, sm_ref, ki2_ref, kit_ref):
    d = x_ref.shape[1]
    h = _norm_mod(x_ref[...], g_ref[...], mod_ref[:, d:2 * d], mod_ref[:, 0:d]).astype(BF16)
    cos, slo, shi = cos_ref[...], slo_ref[...], shi_ref[...]

    def proj(a, b):
        return jnp.dot(h, w_ref[:, a:b], preferred_element_type=F32)

    o = 0
    q_ref[...] = (_rope_wide(proj(o, o + aw), cos, slo, shi) * Q_SCALE).astype(BF16)
    o += aw
    k = _rope_wide(proj(o, o + kvw), cos, slo, shi)
    kb_ref[...] = k.astype(BF16)
    kt_ref[...] = k.T
    o += kvw
    vt = proj(o, o + kvw).T
    vt_ref[...] = vt
    vtb_ref[...] = vt.astype(BF16)
    o += kvw
    qi_ref[...] = _rope_wide(proj(o, o + iw), cos, slo, shi).astype(BF16)
    o += iw
    gs_ref[...] = proj(o, o + d)
    o += d
    ga_ref[...] = proj(o, o + d)
    o += d
    s = proj(o, o + LANES)
    lane = lax.broadcasted_iota(I32, s.shape, 1)
    sm = jnp.where(lane < HEAD, _rope128(s, cos, slo, shi), s)
    sm_ref[...] = sm
    ki2_ref[...] = jnp.where(lane < HEAD, sm, pltpu.roll(sm, HEAD, 1)).astype(BF16)
    kit_ref[...] = sm.T[0:HEAD, :]


def _inproj_attn(x, mod3, g, w_bf, tabs, tm, tpb, aw, kvw, iw):
    m, d = x.shape
    x_spec, mod_spec = _row_specs(tm, tpb, d, mod3.shape[1])
    ntab = tabs[0].shape[0] // tm
    nb = m // (tm * tpb)
    tab_spec = pl.BlockSpec((tm, LANES), lambda i: (i % ntab, 0))
    row = lambda n, dt: (pl.BlockSpec((tm, n), lambda i: (i, 0)), jax.ShapeDtypeStruct((m, n), dt))
    tmin = lambda n, dt: (pl.BlockSpec((None, n, tm), lambda i: (i // tpb, 0, i % tpb)),
                          jax.ShapeDtypeStruct((nb, n, tm * tpb), dt))
    outs = [row(aw, BF16), row(kvw, BF16), tmin(kvw, F32), tmin(kvw, F32), tmin(kvw, BF16), row(iw, BF16),
            row(d, F32), row(d, F32), row(LANES, F32), row(LANES, BF16), tmin(HEAD, F32)]
    return pl.pallas_call(
        functools.partial(_inproj_attn_body, aw, kvw, iw), grid=(m // tm,),
        in_specs=[x_spec, mod_spec, _const_spec((1, d)), _const_spec(w_bf.shape),
                  tab_spec, tab_spec, tab_spec],
        out_specs=[spec for spec, _ in outs],
        out_shape=[shape for _, shape in outs],
        **_call_opts("inproj_attn", ("parallel",)))(x, mod3, g, w_bf, *tabs)


def _ssd_body(inner, z_ref, xbc_ref, dtr_ref, cw_ref, cb_ref, dtb_ref, a_ref, dsk_ref, ng_ref,
              y_ref, st_ref, full_s, act_s, st_s, y_s):
    c = pl.program_id(1)
    Q, N = SSM_CHUNK, SSM_STATE
    cdim = xbc_ref.shape[1]
    heads_per_group = inner // HEAD // SSM_GROUPS
    gw = inner // SSM_GROUPS

    @pl.when(c == 0)
    def _():
        full_s[0:8, :] = jnp.zeros((8, cdim), F32)
        st_s[...] = jnp.zeros(st_s.shape, F32)

    full_s[8:8 + Q, :] = xbc_ref[...]
    for j in range(0, cdim, 512):
        acc = cb_ref[:, j:j + 512] + full_s[8:8 + Q, j:j + 512] * cw_ref[3:4, j:j + 512]
        for i in range(SSM_CONV - 1):
            acc = acc + full_s[5 + i:5 + i + Q, j:j + 512] * cw_ref[i:i + 1, j:j + 512]
        act_s[:, j:j + 512] = _silu(acc)
    full_s[0:8, :] = full_s[Q:Q + 8, :]

    dt = _softplus(dtr_ref[...] + dtb_ref[...])
    row = lax.broadcasted_iota(I32, (Q, Q), 0)
    col = lax.broadcasted_iota(I32, (Q, Q), 1)
    tri = row >= col
    acs = jnp.dot(tri.astype(F32), dt * a_ref[...], precision=HIGHEST, preferred_element_type=F32)
    acs_t, dt_t = acs.T, dt.T
    last = acs[Q - 1:Q, :]
    wdt = jnp.exp(last - acs) * dt
    eacs = jnp.exp(acs)
    cdec = jnp.exp(last)
    low = lax.broadcasted_iota(I32, (Q, LANES), 1) < HEAD
    low1 = low[0:1, :]

    for g in range(SSM_GROUPS):
        bg = act_s[:, inner + g * N:inner + (g + 1) * N]
        cg = act_s[:, inner + SSM_GROUPS * N + g * N:inner + SSM_GROUPS * N + (g + 1) * N]
        cb = _bdot_nt(cg, bg)
        bg_t = bg.T.astype(BF16)
        for p in range(heads_per_group // 2):
            h0 = g * heads_per_group + 2 * p
            js = slice(h0 * HEAD, h0 * HEAD + LANES)
            xp = act_s[:, js]
            xp_bf = xp.astype(BF16)
            stp = st_s[:, js]
            stp_bf = stp.astype(BF16)
            ys = []
            for h in (h0, h0 + 1):
                seg = acs[:, h:h + 1] - acs_t[h:h + 1, :]
                decay = jnp.exp(jnp.where(tri, seg, -jnp.inf))
                m = (cb * decay) * dt_t[h:h + 1, :]
                ce = cg * eacs[:, h:h + 1]
                ys.append(_bdot(m, xp_bf) + _bdot(ce, stp_bf))
            y_s[:, js] = jnp.where(low, ys[0], ys[1])
            wcol = jnp.where(low, wdt[:, h0:h0 + 1], wdt[:, h0 + 1:h0 + 2])
            dst = jnp.dot(bg_t, (xp * wcol).astype(BF16), preferred_element_type=F32)
            cd = jnp.where(low1, cdec[:, h0:h0 + 1], cdec[:, h0 + 1:h0 + 2])
            st_s[:, js] = stp * cd + dst

    for g in range(SSM_GROUPS):
        gs = slice(g * gw, (g + 1) * gw)
        y = y_s[:, gs] + dsk_ref[:, gs] * act_s[:, gs]
        y = y * _silu(z_ref[:, gs])
        ms = jnp.mean(y * y, axis=-1, keepdims=True)
        y_ref[:, gs] = (y * lax.rsqrt(ms + NORM_EPS) * ng_ref[:, gs]).astype(BF16)

    @pl.when(c == pl.num_programs(1) - 1)
    def _():
        st_ref[...] = st_s[...].T


def _ssd_prompt(z, xbc, dtr, cw, cb, dtb, a, dsk, ng, nb):
    m, inner = z.shape
    cdim = xbc.shape[1]
    nc = m // nb // SSM_CHUNK
    row = lambda n: pl.BlockSpec((SSM_CHUNK, n), lambda b, c: (b * nc + c, 0))
    const = lambda shape: pl.BlockSpec(shape, lambda b, c: (0,) * len(shape))
    return pl.pallas_call(
        functools.partial(_ssd_body, inner), grid=(nb, nc),
        in_specs=[row(inner), row(cdim), row(LANES), const(cw.shape), const(cb.shape),
                  const(dtb.shape), const(a.shape), const(dsk.shape), const(ng.shape)],
        out_specs=[row(inner), pl.BlockSpec((None, inner, SSM_STATE), lambda b, c: (b, 0, 0))],
        out_shape=[jax.ShapeDtypeStruct((m, inner), BF16),
                   jax.ShapeDtypeStruct((nb, inner, SSM_STATE), F32)],
        scratch_shapes=[pltpu.VMEM((SSM_CHUNK + 8, cdim), F32), pltpu.VMEM((SSM_CHUNK, cdim), F32),
                        pltpu.VMEM((SSM_STATE, inner), F32), pltpu.VMEM((SSM_CHUNK, inner), F32)],
        **_call_opts("ssd_prompt", ("parallel", "arbitrary")),
    )(z, xbc, dtr, cw, cb, dtb, a, dsk, ng)


def _ssd_step_body(z_ref, xs_ref, bc_ref, cxs_ref, cbc_ref, dtr_ref, st_ref,
                   wxs_ref, wbc_ref, bxs_ref, bbc_ref, dtb_ref, a_ref, dsk_ref, ng_ref,
                   y_ref, sto_ref):
    G = SSM_GROUPS
    last = SSM_CONV - 1
    xs = bxs_ref[...] + xs_ref[...] * wxs_ref[last]
    bc = bbc_ref[...] + bc_ref[...] * wbc_ref[last]
    for i in range(last):
        xs = xs + cxs_ref[i] * wxs_ref[i]
        bc = bc + cbc_ref[i] * wbc_ref[i]
    xs, bc = _silu(xs), _silu(bc)
    dt = _softplus(dtr_ref[...] + dtb_ref[...])
    dec = jnp.exp(dt * a_ref[...])
    xdt = xs * dt
    npair = xs.shape[0]
    pairs_per_group = npair // G
    r = lax.broadcasted_iota(I32, (LANES, LANES), 0)
    cidx = lax.broadcasted_iota(I32, (LANES, LANES), 1)
    eye = (r == cidx).astype(F32)
    nt = (((1,), (1,)), ((), ()))
    dec_t = lax.dot_general(eye, dec, nt, precision=HIGHEST, preferred_element_type=F32)
    xdt_t = lax.dot_general(eye, xdt, nt, precision=HIGHEST, preferred_element_type=F32)
    rows = lax.broadcasted_iota(I32, (npair, 1), 0)
    cbv = jnp.sum(bc[0:G, :] * bc[G:2 * G, :], axis=-1, keepdims=True)
    cbx = jnp.zeros((npair, 1), F32)
    for g in range(G):
        cbx = cbx + jnp.where(rows // pairs_per_group == g, cbv[g:g + 1, :], 0.0)
    c_bf = bc.astype(BF16)
    yoff = jnp.zeros(xs.shape, F32)
    for j in range(npair):
        g = j // pairs_per_group
        s = st_ref[j * LANES:(j + 1) * LANES, :]
        sto_ref[j * LANES:(j + 1) * LANES, :] = s * dec_t[:, j:j + 1] + xdt_t[:, j:j + 1] * bc[g:g + 1, :]
        rj = _bdot_nt(c_bf, s)
        yoff = yoff + jnp.where(rows == j, rj[G + g:G + g + 1, :], 0.0)
    y = yoff * dec + cbx * dt * xs + dsk_ref[...] * xs
    y = y * _silu(z_ref[...])
    ssq = jnp.sum(y * y, axis=-1, keepdims=True)
    msx = jnp.zeros((npair, 1), F32)
    for g in range(G):
        ing = rows // pairs_per_group == g
        tot = jnp.sum(jnp.where(ing, ssq, 0.0), axis=0, keepdims=True)
        msx = msx + jnp.where(ing, tot, 0.0)
    msx = msx / (pairs_per_group * LANES)
    y_ref[...] = y * lax.rsqrt(msx + NORM_EPS) * ng_ref[...]


def _ssd_sample(z, xs, bc, cxs, cbc, dtr, st, wxs, wbc, bxs, bbc, dtb, a, dsk, ng):
    db, npair, _ = z.shape
    per_b = lambda shape: pl.BlockSpec((None,) + shape, lambda b: (b,) + (0,) * len(shape))
    const = lambda arr: pl.BlockSpec(arr.shape, lambda b: (0,) * arr.ndim)
    return pl.pallas_call(
        _ssd_step_body, grid=(db,),
        in_specs=[per_b(z.shape[1:]), per_b(xs.shape[1:]), per_b(bc.shape[1:]), per_b(cxs.shape[1:]),
                  per_b(cbc.shape[1:]), per_b(dtr.shape[1:]), per_b(st.shape[1:]),
                  const(wxs), const(wbc), const(bxs), const(bbc), const(dtb), const(a), const(dsk), const(ng)],
        out_specs=[per_b(z.shape[1:]), per_b(st.shape[1:])],
        out_shape=[jax.ShapeDtypeStruct(z.shape, F32), jax.ShapeDtypeStruct(st.shape, F32)],
        **_call_opts("ssd_sample", ("parallel",)),
    )(z, xs, bc, cxs, cbc, dtr, st, wxs, wbc, bxs, bbc, dtb, a, dsk, ng)


def _row_fold(x, h, op=jnp.add):
    parts = [x[j:j + h, :] for j in range(0, x.shape[0], h)]
    while len(parts) > 1:
        parts = [op(a, b) for a, b in zip(parts[0::2], parts[1::2])] + (parts[-1:] if len(parts) % 2 else [])
    return parts[0]


IMG_ROWS = 16


def _sort_network(n):
    def merge(lo, hi, r):
        step = 2 * r
        if step < hi - lo:
            yield from merge(lo, hi, step)
            yield from merge(lo + r, hi, step)
            yield from ((i, i + r) for i in range(lo + r, hi - r, step))
        else:
            yield (lo, lo + r)

    def sort(lo, hi):
        if hi - lo >= 1:
            mid = lo + (hi - lo) // 2
            yield from sort(lo, mid)
            yield from sort(mid + 1, hi)
            yield from merge(lo, hi, 1)
    return tuple(sort(0, n - 1))


def _group_members(kb):
    tiles = kb // IMG_ROWS
    size = 16 if tiles % 16 == 0 else 8
    n = tiles // size
    return [[g + n * k for k in range(size)] for g in range(n)]


def _sort_groups(x):
    kb = x.shape[0]
    half = IMG_ROWS // 2
    rows = [x[r:r + half, :] for r in range(0, kb, half)]
    for members in _group_members(kb):
        network = _sort_network(len(members))
        for b in range(2):
            idx = [2 * p + b for p in members]
            vals = [rows[i] for i in idx]
            for i, j in network:
                vals[i], vals[j] = jnp.maximum(vals[i], vals[j]), jnp.minimum(vals[i], vals[j])
            for i, v in zip(idx, vals):
                rows[i] = v
    return jnp.concatenate(rows, axis=0)


def _topk_bias(score_s, sort_s, img_s, bias_s, nblk, kb, topk, qpos, unroll=1):
    nq = score_s.shape[1]
    blk = lambda s: pl.ds(pl.multiple_of(s * kb, kb), kb)
    one16, zero16 = jnp.int16(1), jnp.int16(0)
    work_s = bias_s
    groups = _group_members(kb)

    def over_blocks(body, init):
        def trip(s2, carry):
            for k in range(unroll):
                carry = body(s2 * unroll + k, carry)
            return carry
        return lax.fori_loop(0, nblk // unroll, trip, init)

    def count(c):
        def body(s, acc):
            img = img_s[blk(s), :]
            for members in groups:
                p = [img[m * IMG_ROWS:(m + 1) * IMG_ROWS, :] for m in members]
                masks = []

                def comparand(lo, hi, level):
                    mid = (lo + hi) // 2
                    if level == len(masks):
                        return p[mid]
                    return jnp.where(masks[level], comparand(mid + 1, hi, level + 1), comparand(lo, mid, level + 1))

                depth = len(p).bit_length() - 1
                terms = [jnp.where(p[-1] >= c, one16, zero16)]
                for level in range(depth):
                    masks.append(comparand(0, len(p) - 1, 0) >= c)
                    terms.append(jnp.where(masks[-1], jnp.int16(1 << (depth - 1 - level)), zero16))
                while len(terms) > 1:
                    terms = [a + b for a, b in zip(terms[0::2], terms[1::2])] + (terms[-1:] if len(terms) % 2 else [])
                acc = acc + terms[0]
            return acc
        acc = over_blocks(body, jnp.zeros((IMG_ROWS, nq), I16))
        return jnp.sum(acc.astype(F32), axis=0, keepdims=True)

    def search(nbits, value_of):
        def step(t, u):
            code = u | lax.shift_left(jnp.int32(1), jnp.asarray(nbits - 1 - t, I32))
            c = value_of(code).astype(BF16)
            return jnp.where(count(c) >= topk, code, u)
        return lax.fori_loop(0, nbits, step, jnp.zeros((1, nq), I32))

    def bf16_value(code):
        pattern = jnp.where(code >= 32768, code - 32768, 65535 - code)
        return pltpu.bitcast(pattern << 16, jnp.float32).astype(F32)

    def set_image(fn):
        def body(s, carry):
            img_s[blk(s), :] = fn(s).astype(BF16)
            return carry
        over_blocks(body, 0)

    t1 = bf16_value(search(16, bf16_value))

    e1 = jnp.clip((pltpu.bitcast(t1.astype(jnp.float32), I32) >> 23) & 0xFF, 25, 254)
    unit = pltpu.bitcast((e1 - 24) << 23, jnp.float32).astype(F32)
    inv_unit = pltpu.bitcast((278 - e1) << 23, jnp.float32).astype(F32)
    B2, B1 = 65536.0, 256.0

    def digit2(s):
        y = (sort_s[blk(s), :] - t1) * inv_unit
        work_s[blk(s), :] = y
        return jnp.floor(y * (1.0 / B2))
    set_image(digit2)
    t2 = (search(2, lambda code: (code - 1).astype(F32)) - 1).astype(F32)
    set_image(lambda s: jnp.floor((work_s[blk(s), :] - t2 * B2) * (1.0 / B1)))
    t3 = search(8, lambda code: code.astype(F32)).astype(F32)
    set_image(lambda s: jnp.floor(work_s[blk(s), :] - (t2 * B2 + t3 * B1)))
    t4 = search(8, lambda code: code.astype(F32)).astype(F32)
    v0 = t1 + (t2 * B2 + t3 * B1 + t4) * unit

    def smallest(keep):
        def body(s, acc):
            x = score_s[blk(s), :]
            return jnp.minimum(acc, _row_fold(jnp.where(keep(x), x, jnp.inf), 8, jnp.minimum))
        acc = over_blocks(body, jnp.full((8, nq), jnp.inf, F32))
        return jnp.min(acc, axis=0, keepdims=True)

    def count_above(v):
        def body(s, acc):
            return acc + _row_fold(jnp.where(score_s[blk(s), :] > v, 1.0, 0.0), 8)
        return jnp.sum(over_blocks(body, jnp.zeros((8, nq), F32)), axis=0, keepdims=True)

    def refine(carry):
        v, above = carry
        v = jnp.where(above >= topk, smallest(lambda x: x > v), v)
        return v, count_above(v)

    v = smallest(lambda x: x >= v0)
    v, above = lax.while_loop(lambda c: jnp.max(c[1]) >= topk, refine, (v, count_above(v)))
    need = topk - above

    r_i = lax.broadcasted_iota(I32, (LANES, LANES), 0)
    c_i = lax.broadcasted_iota(I32, (LANES, LANES), 1)
    lower = jnp.where(r_i >= c_i, 1.0, 0.0).astype(BF16)
    sub_iota = lax.broadcasted_iota(I32, (LANES, 1), 0)

    def bias_body(s, carry):
        for j in range(0, kb, LANES):
            off = pl.multiple_of(s * kb + j, LANES)
            x = score_s[pl.ds(off, LANES), :]
            eq = x == v
            eqf = jnp.where(eq, 1.0, 0.0)
            incl = jnp.dot(lower, eqf.astype(BF16), preferred_element_type=F32)
            tie = jnp.where(carry + incl - eqf < need, 0.0, -jnp.inf)
            b = jnp.where(x > v, 0.0, jnp.where(eq, tie, -jnp.inf))
            bias_s[pl.ds(off, LANES), :] = jnp.where((off + sub_iota) <= qpos, b, -jnp.inf)
            carry = carry + incl[LANES - 1:LANES, :]
        return carry

    over_blocks(bias_body, jnp.zeros((1, nq), F32))


def _dsa_t_body(topk, idx_scale, q_ref, qi_ref, sm_ref, ki2_ref, k_ref, vt_ref, o_ref,
                score_s, sort_s, img_s, bias_s, qs_s, lga_s, lgb_s, m_s, acc_s):
    i = pl.program_id(1)
    QB, KB = Q_BLOCK, KEY_BLOCK
    nkb = (i * QB + QB + KB - 1) // KB
    qpos = i * QB + lax.broadcasted_iota(I32, (1, QB), 1)
    low = lax.broadcasted_iota(I32, (QB, LANES), 1) < HEAD
    zero_bf = jnp.zeros((QB, LANES), BF16)
    blk = lambda s: pl.ds(pl.multiple_of(s * KB, KB), KB)

    qi = qi_ref[...]
    sm_t = sm_ref[...].T
    qh, wh = [], []
    for h in range(IDX_HEADS):
        chunk = qi[:, (h // 2) * LANES:(h // 2 + 1) * LANES]
        qh.append(jnp.where(low if h % 2 == 0 else ~low, chunk, zero_bf))
        wh.append(sm_t[HEAD + h:HEAD + h + 1, :] * idx_scale)

    q_stack = jnp.concatenate(qh, axis=0)

    last_key_block = ki2_ref.shape[0] // KB - 1

    def score_pair(s2, carry):
        blocks = (2 * s2, 2 * s2 + 1)
        dots = [_bdot_nt(ki2_ref[blk(jnp.minimum(s, last_key_block)), :], q_stack) for s in blocks]
        for s, sc in zip(blocks, dots):
            acc = jnp.zeros((KB, QB), F32)
            for h in range(IDX_HEADS):
                acc = acc + wh[h] * jnp.maximum(sc[:, h * QB:(h + 1) * QB], 0.0)
            kpos = s * KB + lax.broadcasted_iota(I32, (KB, 1), 0)
            sc = jnp.where(kpos <= qpos, jnp.maximum(acc, MASKED_SCORE), MASKED_SCORE)
            score_s[blk(s), :] = sc
            sc = _sort_groups(sc)
            sort_s[blk(s), :] = sc
            img_s[blk(s), :] = sc.astype(BF16)
        return carry

    nkb_even = 2 * ((nkb + 1) // 2)
    lax.fori_loop(0, nkb_even // 2, score_pair, 0)
    _topk_bias(score_s, sort_s, img_s, bias_s, nkb_even, KB, topk, qpos, unroll=2)

    q = q_ref[...]
    nchunk = q.shape[1] // LANES
    per_kv_chunk = nchunk // (ATTN_KV_HEADS // 2)
    nstack = 2 * per_kv_chunk
    srows = nstack * QB
    for cj in range(nchunk):
        kvc, j = divmod(cj, per_kv_chunk)
        qc = q[:, cj * LANES:(cj + 1) * LANES]
        for half in range(2):
            r0 = (kvc * nstack + half * per_kv_chunk + j) * QB
            qs_s[r0:r0 + QB, :] = jnp.where(low if half == 0 else ~low, qc, zero_bf)

    ones_rows = jnp.ones((ONES_ROWS, KB), BF16)
    cols = lambda r: slice(r * QB, (r + 1) * QB)
    n_kvc = ATTN_KV_HEADS // 2
    nsteps = n_kvc * nkb

    def step_of(t):
        t = jnp.minimum(t, nsteps - 1)
        kvc = (t >= nkb).astype(I32)
        return kvc, t - kvc * nkb

    def logits_to(dst, t):
        kvc, s = step_of(t)
        kb = k_ref[blk(s), pl.ds(pl.multiple_of(kvc * LANES, LANES), LANES)]
        bias = bias_s[blk(s), :]
        lg = _bdot_nt(kb, qs_s[pl.ds(pl.multiple_of(kvc * srows, srows), srows), :])
        for r in range(nstack):
            dst[:, cols(r)] = lg[:, cols(r)] + bias

    def consume(src, t):
        kvc, s = step_of(t)
        vt = vt_ref[pl.ds(pl.multiple_of(kvc * LANES, LANES), LANES), blk(s)]
        ps, alphas = [], []
        for r in range(nstack):
            lg = src[:, cols(r)]
            m = m_s[kvc, :, cols(r)]
            mn = jnp.maximum(m, jnp.max(lg, axis=0, keepdims=True))
            m_s[kvc, :, cols(r)] = mn
            ps.append(jnp.exp2(lg - mn).astype(BF16))
            alphas.append(jnp.exp2(m - mn))
        for half in range(2):
            hs = slice(half * per_kv_chunk, (half + 1) * per_kv_chunk)
            hc = slice(half * per_kv_chunk * QB, (half + 1) * per_kv_chunk * QB)
            v_aug = jnp.concatenate([vt[half * HEAD:(half + 1) * HEAD, :], ones_rows], axis=0)
            pv = jnp.dot(v_aug, jnp.concatenate(ps[hs], axis=1), preferred_element_type=F32)
            acc_s[kvc, :, hc] = jnp.concatenate(alphas[hs], axis=1) * acc_s[kvc, :, hc] + pv

    m_s[...] = jnp.full(m_s.shape, NEG_BIG, F32)
    acc_s[...] = jnp.zeros(acc_s.shape, F32)
    logits_to(lga_s, 0)

    def run_pairs(t0, npairs):
        for p in range(npairs):
            logits_to(lgb_s, t0 + 2 * p + 1)
            consume(lga_s, t0 + 2 * p)
            logits_to(lga_s, t0 + 2 * p + 2)
            consume(lgb_s, t0 + 2 * p + 1)

    def quad_body(i4, carry):
        run_pairs(4 * i4, 2)
        return carry

    lax.fori_loop(0, nsteps // 4, quad_body, 0)

    @pl.when(nsteps % 4 == 2)
    def _():
        run_pairs(nsteps - 2, 1)

    for kvc in range(n_kvc):
        acc = acc_s[kvc]
        for j in range(per_kv_chunk):
            a = acc[:, cols(j)]
            b = acc[:, cols(per_kv_chunk + j)]
            chunk_t = jnp.concatenate([a[0:HEAD, :] / a[HEAD:HEAD + 1, :], b[0:HEAD, :] / b[HEAD:HEAD + 1, :]],
                                      axis=0)
            cj = kvc * per_kv_chunk + j
            o_ref[:, cj * LANES:(cj + 1) * LANES] = chunk_t.T.astype(BF16)


def _dsa_prompt_t(q, qi, sm, ki2, k, vt, nb, topk):
    m, aw = q.shape
    t = m // nb
    nq = t // Q_BLOCK
    tpad = -(-t // (2 * KEY_BLOCK)) * 2 * KEY_BLOCK
    srows = aw // HEAD // (ATTN_KV_HEADS // 2) * Q_BLOCK
    idx_scale = IDX_HEADS ** -0.5 * HEAD ** -0.5
    row = lambda n: pl.BlockSpec((Q_BLOCK, n), lambda b, i: (b * nq + i, 0))
    per_b = lambda n: pl.BlockSpec((t, n), lambda b, i: (b, 0))
    return pl.pallas_call(
        functools.partial(_dsa_t_body, topk, idx_scale), grid=(nb, nq),
        in_specs=[row(aw), row(qi.shape[1]), row(LANES), per_b(LANES), per_b(k.shape[1]),
                  pl.BlockSpec((None,) + vt.shape[1:], lambda b, i: (b, 0, 0))],
        out_specs=row(aw),
        out_shape=jax.ShapeDtypeStruct((m, aw), BF16),
        scratch_shapes=[pltpu.VMEM((tpad, Q_BLOCK), F32), pltpu.VMEM((tpad, Q_BLOCK), F32),
                        pltpu.VMEM((tpad, Q_BLOCK), BF16), pltpu.VMEM((tpad, Q_BLOCK), F32),
                        pltpu.VMEM((2 * srows, LANES), BF16),
                        pltpu.VMEM((KEY_BLOCK, srows), F32), pltpu.VMEM((KEY_BLOCK, srows), F32),
                        pltpu.VMEM((ATTN_KV_HEADS // 2, 1, srows), F32),
                        pltpu.VMEM((ATTN_KV_HEADS // 2, HEAD + ONES_ROWS, srows), F32)],
        **_call_opts("dsa_prompt", ("parallel", "arbitrary")),
    )(q, qi, sm, ki2, k, vt)


def _page_specs(block, npages):
    def make(u):
        return pl.BlockSpec((None,) + block, lambda b, pt: (pt[b, u],) + (0,) * len(block))
    return [make(u) for u in range(npages)]


def _idx_score_body(pt_ref, qi_ref, w_ref, *refs):
    pages, o_ref = refs[:-1], refs[-1]
    qi = qi_ref[...]
    w = w_ref[...]
    for u, page in enumerate(pages):
        s = jnp.maximum(_bdot(qi, page[...]), 0.0)
        o_ref[:, u * PAGE_SIZE:(u + 1) * PAGE_SIZE] = jnp.sum(w * s, axis=0, keepdims=True)


def _idx_scores_sample(page_table, qi8, w8, kidx_t):
    db, npages = page_table.shape
    grid_spec = pltpu.PrefetchScalarGridSpec(
        num_scalar_prefetch=1, grid=(db,),
        in_specs=[pl.BlockSpec((None,) + qi8.shape[1:], lambda b, pt: (b, 0, 0)),
                  pl.BlockSpec((None,) + w8.shape[1:], lambda b, pt: (b, 0, 0))]
                 + _page_specs(kidx_t.shape[1:], npages),
        out_specs=pl.BlockSpec((None, 1, npages * PAGE_SIZE), lambda b, pt: (b, 0, 0)))
    return pl.pallas_call(
        _idx_score_body, grid_spec=grid_spec,
        out_shape=jax.ShapeDtypeStruct((db, 1, npages * PAGE_SIZE), F32),
        **_call_opts("idx_scores_sample", ("parallel",)),
    )(page_table, qi8, w8, *([kidx_t] * npages))


def _select_sample_body(topk, idx_scale, past, sc_ref, qi_ref, sm_ref, bias_ref, score_s, sort_s, img_s, bias_s):
    rows = sc_ref.shape[0]
    sm = sm_ref[...]
    qi = qi_ref[...]
    ki = sm[:, 0:HEAD]
    new = jnp.zeros((rows, 1), F32)
    for h in range(IDX_HEADS):
        d = jnp.sum(qi[:, h * HEAD:(h + 1) * HEAD] * ki, axis=-1, keepdims=True)
        new = new + (sm[:, HEAD + h:HEAD + h + 1] * idx_scale) * jnp.maximum(d, 0.0)
    nblk = (past + LANES) // LANES
    lane = lax.broadcasted_iota(I32, (rows, LANES), 1)
    for j in range(nblk):
        js = slice(j * LANES, (j + 1) * LANES)
        sc = sc_ref[:, js] if j < nblk - 1 else jnp.where(lane == 0, new, MASKED_SCORE)
        sc = jnp.maximum(sc, MASKED_SCORE).T
        score_s[js, :] = sc
        sc = _sort_groups(sc)
        sort_s[js, :] = sc
        img_s[js, :] = sc.astype(BF16)
    qpos = jnp.full((1, rows), past, I32)
    _topk_bias(score_s, sort_s, img_s, bias_s, nblk, LANES, topk, qpos)
    for j in range(nblk):
        js = slice(j * LANES, (j + 1) * LANES)
        bias_ref[:, js] = bias_s[js, :].T


def _select_sample(scores, qi, sm, topk, idx_scale):
    db, past = scores.shape
    full = lambda a: pl.BlockSpec(a.shape, lambda i: (0,) * a.ndim)
    keys = past + LANES
    return pl.pallas_call(
        functools.partial(_select_sample_body, topk, idx_scale, past), grid=(1,),
        in_specs=[full(scores), full(qi), full(sm)],
        out_specs=pl.BlockSpec((db, keys), lambda i: (0, 0)),
        out_shape=jax.ShapeDtypeStruct((db, keys), F32),
        scratch_shapes=[pltpu.VMEM((keys, db), F32), pltpu.VMEM((keys, db), F32), pltpu.VMEM((keys, db), BF16),
                        pltpu.VMEM((keys, db), F32)],
        **_call_opts("select_sample", ("arbitrary",)))(scores, qi, sm)


def _attend_sample_body(npages, pt_ref, q_ref, bias_ref, knew_ref, vnew_ref, *refs):
    kpages, vpages, o_ref = refs[:npages], refs[npages:2 * npages], refs[2 * npages]
    G = ATTN_KV_HEADS
    q = q_ref[...]
    q_bf = q.astype(BF16)
    nh = q.shape[0]
    past = npages * PAGE_SIZE
    group = lax.broadcasted_iota(I32, (nh, 1), 0) // (nh // G)

    def by_group(parts):
        out = parts[G - 1]
        for g in range(G - 2, -1, -1):
            out = jnp.where(group == g, parts[g], out)
        return out

    lg = jnp.concatenate(
        [by_group([_bdot(q_bf, kpages[u][g]) for g in range(G)]) for u in range(npages)], axis=1)
    lg = lg + bias_ref[:, 0:past]
    lg_new = by_group([jnp.sum(q * knew_ref[g:g + 1, :], axis=-1, keepdims=True) for g in range(G)])
    lg_new = lg_new + bias_ref[:, past:past + 1]
    m = jnp.maximum(jnp.max(lg, axis=-1, keepdims=True), lg_new)
    p = jnp.exp2(lg - m)
    p_new = jnp.exp2(lg_new - m)
    denom = jnp.sum(p, axis=-1, keepdims=True) + p_new
    p_bf = p.astype(BF16)
    accs = [p_new * vnew_ref[g:g + 1, :] for g in range(G)]
    for u in range(npages):
        pu = p_bf[:, u * PAGE_SIZE:(u + 1) * PAGE_SIZE]
        for g in range(G):
            accs[g] = accs[g] + _bdot_nt(pu, vpages[u][g])
    o_ref[...] = by_group(accs) / denom


def _attend_sample(page_table, q, bias, knew, vnew, k_t, v_t):
    db, npages = page_table.shape
    per_b = lambda a: pl.BlockSpec((None,) + a.shape[1:], lambda b, pt: (b,) + (0,) * (a.ndim - 1))
    grid_spec = pltpu.PrefetchScalarGridSpec(
        num_scalar_prefetch=1, grid=(db,),
        in_specs=[per_b(q), per_b(bias), per_b(knew), per_b(vnew)]
                 + _page_specs(k_t.shape[1:], npages) + _page_specs(v_t.shape[1:], npages),
        out_specs=per_b(q))
    return pl.pallas_call(
        functools.partial(_attend_sample_body, npages), grid_spec=grid_spec,
        out_shape=jax.ShapeDtypeStruct(q.shape, F32),
        **_call_opts("attend_sample", ("parallel",)),
    )(page_table, q, bias, knew, vnew, *([k_t] * npages), *([v_t] * npages))


def _merge_body(x_ref, mod_ref, ys_ref, ya_ref, gs_ref, ga_ref, wps_ref, wpa_ref, wo_ref, o_ref):
    d = x_ref.shape[1]
    merged = (jax.nn.sigmoid(gs_ref[...]) * _bdot(ys_ref[...], wps_ref[...])
              + jax.nn.sigmoid(ga_ref[...]) * _bdot(ya_ref[...], wpa_ref[...]))
    o_ref[...] = x_ref[...] + mod_ref[:, 2 * d:3 * d] * _bdot(merged, wo_ref[...])


def _merge(x, mod3, ys, ya, gs, ga, wps, wpa, wo, tm, tpb):
    m, d = x.shape
    x_spec, mod_spec = _row_specs(tm, tpb, d, mod3.shape[1])
    row = lambda n: pl.BlockSpec((tm, n), lambda i: (i, 0))
    return pl.pallas_call(
        _merge_body, grid=(m // tm,),
        in_specs=[x_spec, mod_spec, row(ys.shape[1]), row(ya.shape[1]), row(d), row(d),
                  _const_spec(wps.shape), _const_spec(wpa.shape), _const_spec(wo.shape)],
        out_specs=row(d), out_shape=jax.ShapeDtypeStruct((m, d), F32),
        **_call_opts("merge", ("parallel",)))(x, mod3, ys, ya, gs, ga, wps, wpa, wo)


def _ffn_body(last_layer, x_ref, mod_ref, g_ref, fg_ref, wg_ref, wu_ref, wo_ref, o_ref):
    d = x_ref.shape[1]
    x = x_ref[...]
    h = _norm_mod(x, g_ref[...], mod_ref[:, 4 * d:5 * d], mod_ref[:, 3 * d:4 * d]).astype(BF16)
    gate = jnp.dot(h, wg_ref[...], preferred_element_type=F32)
    up = jnp.dot(h, wu_ref[...], preferred_element_type=F32)
    x2 = x + mod_ref[:, 5 * d:6 * d] * _bdot(_silu(gate) * up, wo_ref[...])
    if last_layer:
        ms = jnp.mean(x2 * x2, axis=-1, keepdims=True)
        x2 = x2 * lax.rsqrt(ms + NORM_EPS) * fg_ref[...]
    o_ref[...] = x2


def _ffn(x, mod3, g, fg, wg, wu, wo, tm, tpb, last_layer):
    m, d = x.shape
    x_spec, mod_spec = _row_specs(tm, tpb, d, mod3.shape[1])
    return pl.pallas_call(
        functools.partial(_ffn_body, last_layer), grid=(m // tm,),
        in_specs=[x_spec, mod_spec, _const_spec((1, d)), _const_spec((1, d)),
                  _const_spec(wg.shape), _const_spec(wu.shape), _const_spec(wo.shape)],
        out_specs=pl.BlockSpec((tm, d), lambda i: (i, 0)), out_shape=jax.ShapeDtypeStruct((m, d), F32),
        **_call_opts("ffn", ("parallel",)))(x, mod3, g, fg, wg, wu, wo)


def _rope_tables(pos):
    half = HEAD // 2
    inv = ROPE_THETA ** (-jnp.arange(half, dtype=F32) / half)
    ang = pos.astype(F32)[:, None] * inv[None, :]
    cos = jnp.tile(jnp.cos(ang), (1, LANES // half))
    sin = jnp.tile(jnp.sin(ang), (1, LANES // half))
    first = (jnp.arange(LANES) % HEAD) < half
    return cos, jnp.where(first, -sin, 0.0), jnp.where(first, 0.0, sin)


def _q_head_order(n_heads):
    rep = n_heads // ATTN_KV_HEADS
    order = []
    for c in range(ATTN_KV_HEADS // 2):
        for j in range(rep):
            order += [2 * c * rep + j, (2 * c + 1) * rep + j]
    return np.asarray(order)


def kernel(x_prompt, x_sample, cache_k, cache_v, cache_kidx, state_conv, state_ssm, page_table, c_prompt, c_sample, w_ada, b_ada, norm1_g, w_in, conv_w, conv_b, dt_bias, a_log, d_skip, ssm_norm_g, w_proj_ssm, w_proj_attn, w_out, norm2_g, w_ffn_in, w_ffn_out, final_g):
    nb, t, d = x_prompt.shape
    db, ds, _ = x_sample.shape
    depth = w_in.shape[0]
    assert ds == 1 and t % Q_BLOCK == 0 and t % SSM_CHUNK == 0
    n_heads_ssm = dt_bias.shape[1]
    inner = n_heads_ssm * HEAD
    gn = SSM_GROUPS * SSM_STATE
    cdim = inner + 2 * gn
    kvw = ATTN_KV_HEADS * HEAD
    aw = w_proj_attn.shape[1]
    n_heads = aw // HEAD
    iw = IDX_HEADS * HEAD
    ffn_hidden = w_ffn_out.shape[1]
    npages = page_table.shape[1]
    past = npages * PAGE_SIZE
    topk_p = min(TOPK_MAX, t // 4)
    topk_s = min(TOPK_MAX, (past + ds) // 4)
    assert past + ds >= topk_s
    idx_scale = IDX_HEADS ** -0.5 * HEAD ** -0.5
    tm_p = 256 if t % 256 == 0 else 128
    tpb_p = t // tm_p

    splits = np.cumsum([inner, inner, gn, gn, n_heads_ssm, aw, kvw, kvw, iw, HEAD, IDX_HEADS, d])
    order = _q_head_order(n_heads)
    inv_order = np.argsort(order)

    cos_p, slo_p, shi_p = _rope_tables(jnp.arange(t, dtype=I32))
    tabs_p = (cos_p, slo_p, shi_p)
    tabs_s = tuple(jnp.broadcast_to(a, (db, LANES)) for a in _rope_tables(past + jnp.arange(ds, dtype=I32)))

    rows_c = nb + db
    c_all = jnp.concatenate([c_prompt, c_sample, jnp.zeros((-rows_c % 8, d), F32)], axis=0)

    yp = x_prompt.reshape(nb * t, d)
    ys = x_sample.reshape(db, d)
    outs_p, outs_s = [], []
    for l in range(depth):
        (wz, wxs, wbm, wcm, wdt, wq, wk, wv, wqi, wki, wwi, wgs, wga) = jnp.split(w_in[l], splits, axis=1)
        w_ssm = jnp.concatenate([wz, wxs, wbm, wcm, wdt, jnp.zeros((d, LANES - n_heads_ssm), F32)],
                                axis=1).astype(BF16)
        wq_perm = wq.reshape(d, n_heads, HEAD)[:, order].reshape(d, aw)
        w_attn = jnp.concatenate([wq_perm, wk, wv, wqi, wgs, wga, wki, wwi,
                                  jnp.zeros((d, LANES - HEAD - IDX_HEADS), F32)], axis=1).astype(BF16)
        wps = w_proj_ssm[l].astype(BF16)
        wpa = w_proj_attn[l].reshape(n_heads, HEAD, d)[order].reshape(aw, d).astype(BF16)
        wo = w_out[l].astype(BF16)
        wg = w_ffn_in[l][:, :ffn_hidden].astype(BF16)
        wu = w_ffn_in[l][:, ffn_hidden:].astype(BF16)
        wfo = w_ffn_out[l].astype(BF16)
        g1 = norm1_g[l][None, :]
        g2 = norm2_g[l][None, :]
        a_neg = -jnp.exp(a_log[l])
        pad_h = LANES - n_heads_ssm
        dtb_row = jnp.pad(dt_bias[l], (0, pad_h))[None, :]
        a_row = jnp.pad(a_neg, (0, pad_h))[None, :]
        dsk_row = jnp.repeat(d_skip[l], HEAD)[None, :]
        ng_row = ssm_norm_g[l][None, :]
        cw = conv_w[l]
        cb = conv_b[l][None, :]

        mod = _ada(c_all, w_ada[l].astype(BF16), b_ada[l][None, :])
        mod_p = mod[:nb][:, None, :]
        mod_s = mod[nb:nb + db][None]

        z, xbc, dtr = _inproj_ssm(yp, mod_p, g1, w_ssm, tm_p, tpb_p, inner, cdim)
        (q_bf, k_bf, kt, vt, vt_bf, qi_bf, gs, ga, sm, ki2, kit) = _inproj_attn(
            yp, mod_p, g1, w_attn, tabs_p, tm_p, tpb_p, aw, kvw, iw)
        y_ssm, st = _ssd_prompt(z, xbc, dtr, cw, cb, dtb_row, a_row, dsk_row, ng_row, nb)
        y_attn = _dsa_prompt_t(q_bf, qi_bf, sm, ki2, k_bf, vt_bf, nb, topk_p)
        x1 = _merge(yp, mod_p, y_ssm, y_attn, gs, ga, wps, wpa, wo, tm_p, tpb_p)
        yp_next = _ffn(x1, mod_p, g2, final_g[None, :], wg, wu, wfo, tm_p, tpb_p, l == depth - 1)
        heads_last = lambda a: jnp.transpose(a.reshape(a.shape[0], ATTN_KV_HEADS, HEAD, a.shape[2]), (0, 3, 1, 2))
        outs_p.append((heads_last(kt), heads_last(vt), jnp.transpose(kit, (0, 2, 1)),
                       xbc.reshape(nb, t, cdim)[:, t - (SSM_CONV - 1):],
                       st.reshape(nb, n_heads_ssm, HEAD, SSM_STATE)))

        z_s, xbc_s, dtr_s = _inproj_ssm(ys, mod_s, g1, w_ssm, db, 1, inner, cdim)
        (q_s, _, kt_s, vt_s, _, qi_s, gs_s, ga_s, sm_s, _, kit_s) = _inproj_attn(
            ys, mod_s, g1, w_attn, tabs_s, db, 1, aw, kvw, iw)
        k_s, v_s = heads_last(kt_s)[0], heads_last(vt_s)[0]
        npair = inner // LANES
        nbc = 2 * gn // LANES
        sc = state_conv[l]
        y_ssm_s, st_s = _ssd_sample(
            z_s.reshape(db, npair, LANES), xbc_s[:, :inner].reshape(db, npair, LANES),
            xbc_s[:, inner:].reshape(db, nbc, LANES),
            sc[:, :, :inner].reshape(db, SSM_CONV - 1, npair, LANES),
            sc[:, :, inner:].reshape(db, SSM_CONV - 1, nbc, LANES),
            jnp.repeat(dtr_s[:, :n_heads_ssm], HEAD, axis=1).reshape(db, npair, LANES),
            state_ssm[l].reshape(db, inner, SSM_STATE),
            cw[:, :inner].reshape(SSM_CONV, npair, LANES), cw[:, inner:].reshape(SSM_CONV, nbc, LANES),
            cb[:, :inner].reshape(npair, LANES), cb[:, inner:].reshape(nbc, LANES),
            jnp.repeat(dt_bias[l], HEAD).reshape(npair, LANES), jnp.repeat(a_neg, HEAD).reshape(npair, LANES),
            dsk_row.reshape(npair, LANES), ng_row.reshape(npair, LANES))

        qi_f = qi_s.astype(F32)
        qi8 = jnp.pad(qi_f.reshape(db, IDX_HEADS, HEAD), ((0, 0), (0, 8 - IDX_HEADS), (0, 0)))
        w8 = jnp.broadcast_to(jnp.pad(sm_s[:, HEAD:HEAD + IDX_HEADS] * idx_scale,
                                      ((0, 0), (0, 8 - IDX_HEADS)))[:, :, None], (db, 8, LANES))
        scores = _idx_scores_sample(page_table, qi8, w8, jnp.transpose(cache_kidx[l], (0, 2, 1)))
        bias = _select_sample(scores.reshape(db, past), qi_f, sm_s, topk_s, idx_scale)
        q_orig = q_s.astype(F32).reshape(db, n_heads, HEAD)[:, inv_order]
        att = _attend_sample(page_table, q_orig, bias[:, None, :],
                             k_s, v_s,
                             jnp.transpose(cache_k[l], (0, 2, 3, 1)), jnp.transpose(cache_v[l], (0, 2, 3, 1)))
        y_attn_s = att[:, order].reshape(db, aw)
        x1_s = _merge(ys, mod_s, y_ssm_s.reshape(db, inner), y_attn_s, gs_s, ga_s, wps, wpa, wo, db, 1)
        ys_next = _ffn(x1_s, mod_s, g2, final_g[None, :], wg, wu, wfo, db, 1, l == depth - 1)
        outs_s.append((k_s[:, None], v_s[:, None], jnp.transpose(kit_s, (2, 0, 1)),
                       jnp.concatenate([sc[:, 1:], xbc_s[:, None, :]], axis=1),
                       st_s.reshape(db, n_heads_ssm, HEAD, SSM_STATE)))
        yp, ys = yp_next, ys_next

    stack = lambda outs, i: jnp.stack([o[i] for o in outs], axis=0)
    return (yp.reshape(nb, t, d), ys.reshape(db, ds, d),
            stack(outs_p, 0), stack(outs_p, 1), stack(outs_p, 2), stack(outs_p, 3), stack(outs_p, 4),
            stack(outs_s, 0), stack(outs_s, 1), stack(outs_s, 2), stack(outs_s, 3), stack(outs_s, 4))
```

```python
import functools

import jax
import jax.numpy as jnp
import numpy as np
from jax import lax
from jax.experimental import pallas as pl
from jax.experimental.pallas import tpu as pltpu

F32, BF16, I32, I16 = jnp.float32, jnp.bfloat16, jnp.int32, jnp.int16
HIGHEST = lax.Precision.HIGHEST

LANES = 128
HEAD = 64
SSM_STATE = 128
SSM_GROUPS = 4
SSM_CONV = 4
SSM_CHUNK = 128
ATTN_KV_HEADS = 4
IDX_HEADS = 4
TOPK_MAX = 256
Q_BLOCK = 128
PAGE_SIZE = 128
ROPE_THETA = 10000.0
NORM_EPS = 1e-6
KEY_BLOCK = 512
ONES_ROWS = 16
Q_SCALE = HEAD ** -0.5 * 1.4426950408889634
MASKED_SCORE = -3.3895313892515355e38
NEG_BIG = -1e30


VMEM_LIMIT_MB = {"ada": 32, "inproj_ssm": 56, "inproj_attn": 52, "ssd_prompt": 40, "ssd_sample": 32,
                 "dsa_prompt": 52, "idx_scores_sample": 32, "select_sample": 32, "attend_sample": 48,
                 "merge": 48, "ffn": 56}


def _call_opts(name, sem):
    return dict(name=name, compiler_params=pltpu.CompilerParams(
        dimension_semantics=sem, vmem_limit_bytes=VMEM_LIMIT_MB[name] << 20))


def _bdot(a, b):
    return jnp.dot(a.astype(BF16), b.astype(BF16), preferred_element_type=F32)


def _bdot_nt(a, b):
    return lax.dot_general(a.astype(BF16), b.astype(BF16), (((1,), (1,)), ((), ())),
                           preferred_element_type=F32)


def _silu(x):
    h = 0.5 * x
    return h + h * jnp.tanh(h)


def _softplus(x):
    return jnp.maximum(x, 0.0) + jnp.log(1.0 + jnp.exp(-jnp.abs(x)))


def _norm_mod(x, g, scale, shift):
    ms = jnp.mean(x * x, axis=-1, keepdims=True)
    return (x * lax.rsqrt(ms + NORM_EPS) * g) * (1.0 + scale) + shift


def _rope128(x, cos, sin_lo, sin_hi):
    return x * cos + pltpu.roll(x, 96, 1) * sin_lo + pltpu.roll(x, 32, 1) * sin_hi


def _rope_wide(x, cos, sin_lo, sin_hi):
    parts = [_rope128(x[:, j:j + LANES], cos, sin_lo, sin_hi) for j in range(0, x.shape[1], LANES)]
    return parts[0] if len(parts) == 1 else jnp.concatenate(parts, axis=1)


def _ada_body(c_ref, w_ref, b_ref, o_ref):
    o_ref[...] = _bdot(_silu(c_ref[...]), w_ref[...]) + b_ref[...]


def _ada(c_all, w_bf, b):
    mp, d = c_all.shape
    n = w_bf.shape[1]
    tn = n // 4
    return pl.pallas_call(
        _ada_body, grid=(n // tn,),
        in_specs=[pl.BlockSpec((mp, d), lambda j: (0, 0)),
                  pl.BlockSpec((d, tn), lambda j: (0, j)),
                  pl.BlockSpec((1, tn), lambda j: (0, j))],
        out_specs=pl.BlockSpec((mp, tn), lambda j: (0, j)),
        out_shape=jax.ShapeDtypeStruct((mp, n), F32),
        **_call_opts("ada", ("arbitrary",)))(c_all, w_bf, b)


def _inproj_ssm_body(inner, cdim, x_ref, mod_ref, g_ref, w_ref, z_ref, xbc_ref, dt_ref):
    d = x_ref.shape[1]
    h = _norm_mod(x_ref[...], g_ref[...], mod_ref[:, d:2 * d], mod_ref[:, 0:d]).astype(BF16)
    z_ref[...] = jnp.dot(h, w_ref[:, 0:inner], preferred_element_type=F32)
    xbc_ref[...] = jnp.dot(h, w_ref[:, inner:inner + cdim], preferred_element_type=F32)
    dt_ref[...] = jnp.dot(h, w_ref[:, inner + cdim:inner + cdim + LANES], preferred_element_type=F32)


def _row_specs(tm, tpb, d, mod_rows):
    x_spec = pl.BlockSpec((tm, d), lambda m: (m, 0))
    mod_spec = pl.BlockSpec((None, mod_rows, 6 * d), lambda m: (m // tpb, 0, 0))
    return x_spec, mod_spec


def _const_spec(shape):
    return pl.BlockSpec(shape, lambda m: (0,) * len(shape))


def _inproj_ssm(x, mod3, g, w_bf, tm, tpb, inner, cdim):
    m, d = x.shape
    x_spec, mod_spec = _row_specs(tm, tpb, d, mod3.shape[1])
    row = lambda n: pl.BlockSpec((tm, n), lambda i: (i, 0))
    return pl.pallas_call(
        functools.partial(_inproj_ssm_body, inner, cdim), grid=(m // tm,),
        in_specs=[x_spec, mod_spec, _const_spec((1, d)), _const_spec(w_bf.shape)],
        out_specs=[row(inner), row(cdim), row(LANES)],
        out_shape=[jax.ShapeDtypeStruct((m, inner), F32), jax.ShapeDtypeStruct((m, cdim), F32),
                   jax.ShapeDtypeStruct((m, LANES), F32)],
        **_call_opts("inproj_ssm", ("parallel",)))(x, mod3, g, w_bf)


def _inproj_attn_body(aw, kvw, iw, x_ref, mod_ref, g_ref, w_ref, cos_ref, slo_ref, shi_ref,
                      q_ref, kb_ref, kt_ref, vt_ref, vtb_ref, qi_ref, gs_ref, ga_ref, sm_ref, ki2_ref, kit_ref):
    d = x_ref.shape[1]
    h = _norm_mod(x_ref[...], g_ref[...], mod_ref[:, d:2 * d], mod_ref[:, 0:d]).astype(BF16)
    cos, slo, shi = cos_ref[...], slo_ref[...], shi_ref[...]

    def proj(a, b):
        return jnp.dot(h, w_ref[:, a:b], preferred_element_type=F32)

    o = 0
    q_ref[...] = (_rope_wide(proj(o, o + aw), cos, slo, shi) * Q_SCALE).astype(BF16)
    o += aw
    k = _rope_wide(proj(o, o + kvw), cos, slo, shi)
    kb_ref[...] = k.astype(BF16)
    kt_ref[...] = k.T
    o += kvw
    vt = proj(o, o + kvw).T
    vt_ref[...] = vt
    vtb_ref[...] = vt.astype(BF16)
    o += kvw
    qi_ref[...] = _rope_wide(proj(o, o + iw), cos, slo, shi).astype(BF16)
    o += iw
    gs_ref[...] = proj(o, o + d)
    o += d
    ga_ref[...] = proj(o, o + d)
    o += d
    s = proj(o, o + LANES)
    lane = lax.broadcasted_iota(I32, s.shape, 1)
    sm = jnp.where(lane < HEAD, _rope128(s, cos, slo, shi), s)
    sm_ref[...] = sm
    ki2_ref[...] = jnp.where(lane < HEAD, sm, pltpu.roll(sm, HEAD, 1)).astype(BF16)
    kit_ref[...] = sm.T[0:HEAD, :]


def _inproj_attn(x, mod3, g, w_bf, tabs, tm, tpb, aw, kvw, iw):
    m, d = x.shape
    x_spec, mod_spec = _row_specs(tm, tpb, d, mod3.shape[1])
    ntab = tabs[0].shape[0] // tm
    nb = m // (tm * tpb)
    tab_spec = pl.BlockSpec((tm, LANES), lambda i: (i % ntab, 0))
    row = lambda n, dt: (pl.BlockSpec((tm, n), lambda i: (i, 0)), jax.ShapeDtypeStruct((m, n), dt))
    tmin = lambda n, dt: (pl.BlockSpec((None, n, tm), lambda i: (i // tpb, 0, i % tpb)),
                          jax.ShapeDtypeStruct((nb, n, tm * tpb), dt))
    outs = [row(aw, BF16), row(kvw, BF16), tmin(kvw, F32), tmin(kvw, F32), tmin(kvw, BF16), row(iw, BF16),
            row(d, F32), row(d, F32), row(LANES, F32), row(LANES, BF16), tmin(HEAD, F32)]
    return pl.pallas_call(
        functools.partial(_inproj_attn_body, aw, kvw, iw), grid=(m // tm,),
        in_specs=[x_spec, mod_spec, _const_spec((1, d)), _const_spec(w_bf.shape),
                  tab_spec, tab_spec, tab_spec],
        out_specs=[spec for spec, _ in outs],
        out_shape=[shape for _, shape in outs],
        **_call_opts("inproj_attn", ("parallel",)))(x, mod3, g, w_bf, *tabs)


def _ssd_body(inner, z_ref, xbc_ref, dtr_ref, cw_ref, cb_ref, dtb_ref, a_ref, dsk_ref, ng_ref,
              y_ref, st_ref, full_s, act_s, st_s, y_s):
    c = pl.program_id(1)
    Q, N = SSM_CHUNK, SSM_STATE
    cdim = xbc_ref.shape[1]
    heads_per_group = inner // HEAD // SSM_GROUPS
    gw = inner // SSM_GROUPS

    @pl.when(c == 0)
    def _():
        full_s[0:8, :] = jnp.zeros((8, cdim), F32)
        st_s[...] = jnp.zeros(st_s.shape, F32)

    full_s[8:8 + Q, :] = xbc_ref[...]
    for j in range(0, cdim, 512):
        acc = cb_ref[:, j:j + 512] + full_s[8:8 + Q, j:j + 512] * cw_ref[3:4, j:j + 512]
        for i in range(SSM_CONV - 1):
            acc = acc + full_s[5 + i:5 + i + Q, j:j + 512] * cw_ref[i:i + 1, j:j + 512]
        act_s[:, j:j + 512] = _silu(acc)
    full_s[0:8, :] = full_s[Q:Q + 8, :]

    dt = _softplus(dtr_ref[...] + dtb_ref[...])
    row = lax.broadcasted_iota(I32, (Q, Q), 0)
    col = lax.broadcasted_iota(I32, (Q, Q), 1)
    tri = row >= col
    acs = jnp.dot(tri.astype(F32), dt * a_ref[...], precision=HIGHEST, preferred_element_type=F32)
    acs_t, dt_t = acs.T, dt.T
    last = acs[Q - 1:Q, :]
    wdt = jnp.exp(last - acs) * dt
    eacs = jnp.exp(acs)
    cdec = jnp.exp(last)
    low = lax.broadcasted_iota(I32, (Q, LANES), 1) < HEAD
    low1 = low[0:1, :]

    for g in range(SSM_GROUPS):
        bg = act_s[:, inner + g * N:inner + (g + 1) * N]
        cg = act_s[:, inner + SSM_GROUPS * N + g * N:inner + SSM_GROUPS * N + (g + 1) * N]
        cb = _bdot_nt(cg, bg)
        bg_t = bg.T.astype(BF16)
        for p in range(heads_per_group // 2):
            h0 = g * heads_per_group + 2 * p
            js = slice(h0 * HEAD, h0 * HEAD + LANES)
            xp = act_s[:, js]
            xp_bf = xp.astype(BF16)
            stp = st_s[:, js]
            stp_bf = stp.astype(BF16)
            ys = []
            for h in (h0, h0 + 1):
                seg = acs[:, h:h + 1] - acs_t[h:h + 1, :]
                decay = jnp.exp(jnp.where(tri, seg, -jnp.inf))
                m = (cb * decay) * dt_t[h:h + 1, :]
                ce = cg * eacs[:, h:h + 1]
                ys.append(_bdot(m, xp_bf) + _bdot(ce, stp_bf))
            y_s[:, js] = jnp.where(low, ys[0], ys[1])
            wcol = jnp.where(low, wdt[:, h0:h0 + 1], wdt[:, h0 + 1:h0 + 2])
            dst = jnp.dot(bg_t, (xp * wcol).astype(BF16), preferred_element_type=F32)
            cd = jnp.where(low1, cdec[:, h0:h0 + 1], cdec[:, h0 + 1:h0 + 2])
            st_s[:, js] = stp * cd + dst

    for g in range(SSM_GROUPS):
        gs = slice(g * gw, (g + 1) * gw)
        y = y_s[:, gs] + dsk_ref[:, gs] * act_s[:, gs]
        y = y * _silu(z_ref[:, gs])
        ms = jnp.mean(y * y, axis=-1, keepdims=True)
        y_ref[:, gs] = (y * lax.rsqrt(ms + NORM_EPS) * ng_ref[:, gs]).astype(BF16)

    @pl.when(c == pl.num_programs(1) - 1)
    def _():
        st_ref[...] = st_s[...].T


def _ssd_prompt(z, xbc, dtr, cw, cb, dtb, a, dsk, ng, nb):
    m, inner = z.shape
    cdim = xbc.shape[1]
    nc = m // nb // SSM_CHUNK
    row = lambda n: pl.BlockSpec((SSM_CHUNK, n), lambda b, c: (b * nc + c, 0))
    const = lambda shape: pl.BlockSpec(shape, lambda b, c: (0,) * len(shape))
    return pl.pallas_call(
        functools.partial(_ssd_body, inner), grid=(nb, nc),
        in_specs=[row(inner), row(cdim), row(LANES), const(cw.shape), const(cb.shape),
                  const(dtb.shape), const(a.shape), const(dsk.shape), const(ng.shape)],
        out_specs=[row(inner), pl.BlockSpec((None, inner, SSM_STATE), lambda b, c: (b, 0, 0))],
        out_shape=[jax.ShapeDtypeStruct((m, inner), BF16),
                   jax.ShapeDtypeStruct((nb, inner, SSM_STATE), F32)],
        scratch_shapes=[pltpu.VMEM((SSM_CHUNK + 8, cdim), F32), pltpu.VMEM((SSM_CHUNK, cdim), F32),
                        pltpu.VMEM((SSM_STATE, inner), F32), pltpu.VMEM((SSM_CHUNK, inner), F32)],
        **_call_opts("ssd_prompt", ("parallel", "arbitrary")),
    )(z, xbc, dtr, cw, cb, dtb, a, dsk, ng)


def _ssd_step_body(z_ref, xs_ref, bc_ref, cxs_ref, cbc_ref, dtr_ref, st_ref,
                   wxs_ref, wbc_ref, bxs_ref, bbc_ref, dtb_ref, a_ref, dsk_ref, ng_ref,
                   y_ref, sto_ref):
    G = SSM_GROUPS
    last = SSM_CONV - 1
    xs = bxs_ref[...] + xs_ref[...] * wxs_ref[last]
    bc = bbc_ref[...] + bc_ref[...] * wbc_ref[last]
    for i in range(last):
        xs = xs + cxs_ref[i] * wxs_ref[i]
        bc = bc + cbc_ref[i] * wbc_ref[i]
    xs, bc = _silu(xs), _silu(bc)
    dt = _softplus(dtr_ref[...] + dtb_ref[...])
    dec = jnp.exp(dt * a_ref[...])
    xdt = xs * dt
    npair = xs.shape[0]
    pairs_per_group = npair // G
    r = lax.broadcasted_iota(I32, (LANES, LANES), 0)
    cidx = lax.broadcasted_iota(I32, (LANES, LANES), 1)
    eye = (r == cidx).astype(F32)
    nt = (((1,), (1,)), ((), ()))
    dec_t = lax.dot_general(eye, dec, nt, precision=HIGHEST, preferred_element_type=F32)
    xdt_t = lax.dot_general(eye, xdt, nt, precision=HIGHEST, preferred_element_type=F32)
    rows = lax.broadcasted_iota(I32, (npair, 1), 0)
    cbv = jnp.sum(bc[0:G, :] * bc[G:2 * G, :], axis=-1, keepdims=True)
    cbx = jnp.zeros((npair, 1), F32)
    for g in range(G):
        cbx = cbx + jnp.where(rows // pairs_per_group == g, cbv[g:g + 1, :], 0.0)
    c_bf = bc.astype(BF16)
    yoff = jnp.zeros(xs.shape, F32)
    for j in range(npair):
        g = j // pairs_per_group
        s = st_ref[j * LANES:(j + 1) * LANES, :]
        sto_ref[j * LANES:(j + 1) * LANES, :] = s * dec_t[:, j:j + 1] + xdt_t[:, j:j + 1] * bc[g:g + 1, :]
        rj = _bdot_nt(c_bf, s)
        yoff = yoff + jnp.where(rows == j, rj[G + g:G + g + 1, :], 0.0)
    y = yoff * dec + cbx * dt * xs + dsk_ref[...] * xs
    y = y * _silu(z_ref[...])
    ssq = jnp.sum(y * y, axis=-1, keepdims=True)
    msx = jnp.zeros((npair, 1), F32)
    for g in range(G):
        ing = rows // pairs_per_group == g
        tot = jnp.sum(jnp.where(ing, ssq, 0.0), axis=0, keepdims=True)
        msx = msx + jnp.where(ing, tot, 0.0)
    msx = msx / (pairs_per_group * LANES)
    y_ref[...] = y * lax.rsqrt(msx + NORM_EPS) * ng_ref[...]


def _ssd_sample(z, xs, bc, cxs, cbc, dtr, st, wxs, wbc, bxs, bbc, dtb, a, dsk, ng):
    db, npair, _ = z.shape
    per_b = lambda shape: pl.BlockSpec((None,) + shape, lambda b: (b,) + (0,) * len(shape))
    const = lambda arr: pl.BlockSpec(arr.shape, lambda b: (0,) * arr.ndim)
    return pl.pallas_call(
        _ssd_step_body, grid=(db,),
        in_specs=[per_b(z.shape[1:]), per_b(xs.shape[1:]), per_b(bc.shape[1:]), per_b(cxs.shape[1:]),
                  per_b(cbc.shape[1:]), per_b(dtr.shape[1:]), per_b(st.shape[1:]),
                  const(wxs), const(wbc), const(bxs), const(bbc), const(dtb), const(a), const(dsk), const(ng)],
        out_specs=[per_b(z.shape[1:]), per_b(st.shape[1:])],
        out_shape=[jax.ShapeDtypeStruct(z.shape, F32), jax.ShapeDtypeStruct(st.shape, F32)],
        **_call_opts("ssd_sample", ("parallel",)),
    )(z, xs, bc, cxs, cbc, dtr, st, wxs, wbc, bxs, bbc, dtb, a, dsk, ng)


def _row_fold(x, h, op=jnp.add):
    parts = [x[j:j + h, :] for j in range(0, x.shape[0], h)]
    while len(parts) > 1:
        parts = [op(a, b) for a, b in zip(parts[0::2], parts[1::2])] + (parts[-1:] if len(parts) % 2 else [])
    return parts[0]


IMG_ROWS = 16


def _sort_network(n):
    def merge(lo, hi, r):
        step = 2 * r
        if step < hi - lo:
            yield from merge(lo, hi, step)
            yield from merge(lo + r, hi, step)
            yield from ((i, i + r) for i in range(lo + r, hi - r, step))
        else:
            yield (lo, lo + r)

    def sort(lo, hi):
        if hi - lo >= 1:
            mid = lo + (hi - lo) // 2
            yield from sort(lo, mid)
            yield from sort(mid + 1, hi)
            yield from merge(lo, hi, 1)
    return tuple(sort(0, n - 1))


def _group_members(kb):
    tiles = kb // IMG_ROWS
    size = 16 if tiles % 16 == 0 else 8
    n = tiles // size
    return [[g + n * k for k in range(size)] for g in range(n)]


def _sort_groups(x):
    kb = x.shape[0]
    half = IMG_ROWS // 2
    rows = [x[r:r + half, :] for r in range(0, kb, half)]
    for members in _group_members(kb):
        network = _sort_network(len(members))
        for b in range(2):
            idx = [2 * p + b for p in members]
            vals = [rows[i] for i in idx]
            for i, j in network:
                vals[i], vals[j] = jnp.maximum(vals[i], vals[j]), jnp.minimum(vals[i], vals[j])
            for i, v in zip(idx, vals):
                rows[i] = v
    return jnp.concatenate(rows, axis=0)


def _topk_bias(score_s, sort_s, img_s, bias_s, nblk, kb, topk, qpos, unroll=1):
    nq = score_s.shape[1]
    blk = lambda s: pl.ds(pl.multiple_of(s * kb, kb), kb)
    one16, zero16 = jnp.int16(1), jnp.int16(0)
    work_s = bias_s
    groups = _group_members(kb)

    def over_blocks(body, init):
        def trip(s2, carry):
            for k in range(unroll):
                carry = body(s2 * unroll + k, carry)
            return carry
        return lax.fori_loop(0, nblk // unroll, trip, init)

    def count(c):
        def body(s, acc):
            img = img_s[blk(s), :]
            for members in groups:
                p = [img[m * IMG_ROWS:(m + 1) * IMG_ROWS, :] for m in members]
                masks = []

                def comparand(lo, hi, level):
                    mid = (lo + hi) // 2
                    if level == len(masks):
                        return p[mid]
                    return jnp.where(masks[level], comparand(mid + 1, hi, level + 1), comparand(lo, mid, level + 1))

                depth = len(p).bit_length() - 1
                terms = [jnp.where(p[-1] >= c, one16, zero16)]
                for level in range(depth):
                    masks.append(comparand(0, len(p) - 1, 0) >= c)
                    terms.append(jnp.where(masks[-1], jnp.int16(1 << (depth - 1 - level)), zero16))
                while len(terms) > 1:
                    terms = [a + b for a, b in zip(terms[0::2], terms[1::2])] + (terms[-1:] if len(terms) % 2 else [])
                acc = acc + terms[0]
            return acc
        acc = over_blocks(body, jnp.zeros((IMG_ROWS, nq), I16))
        return jnp.sum(acc.astype(F32), axis=0, keepdims=True)

    def search(nbits, value_of):
        def step(t, u):
            code = u | lax.shift_left(jnp.int32(1), jnp.asarray(nbits - 1 - t, I32))
            c = value_of(code).astype(BF16)
            return jnp.where(count(c) >= topk, code, u)
        return lax.fori_loop(0, nbits, step, jnp.zeros((1, nq), I32))

    def bf16_value(code):
        pattern = jnp.where(code >= 32768, code - 32768, 65535 - code)
        return pltpu.bitcast(pattern << 16, jnp.float32).astype(F32)

    def set_image(fn):
        def body(s, carry):
            img_s[blk(s), :] = fn(s).astype(BF16)
            return carry
        over_blocks(body, 0)

    t1 = bf16_value(search(16, bf16_value))

    e1 = jnp.clip((pltpu.bitcast(t1.astype(jnp.float32), I32) >> 23) & 0xFF, 25, 254)
    unit = pltpu.bitcast((e1 - 24) << 23, jnp.float32).astype(F32)
    inv_unit = pltpu.bitcast((278 - e1) << 23, jnp.float32).astype(F32)
    B2, B1 = 65536.0, 256.0

    def digit2(s):
        y = (sort_s[blk(s), :] - t1) * inv_unit
        work_s[blk(s), :] = y
        return jnp.floor(y * (1.0 / B2))
    set_image(digit2)
    t2 = (search(2, lambda code: (code - 1).astype(F32)) - 1).astype(F32)
    set_image(lambda s: jnp.floor((work_s[blk(s), :] - t2 * B2) * (1.0 / B1)))
    t3 = search(8, lambda code: code.astype(F32)).astype(F32)
    set_image(lambda s: jnp.floor(work_s[blk(s), :] - (t2 * B2 + t3 * B1)))
    t4 = search(8, lambda code: code.astype(F32)).astype(F32)
    v0 = t1 + (t2 * B2 + t3 * B1 + t4) * unit

    def smallest(keep):
        def body(s, acc):
            x = score_s[blk(s), :]
            return jnp.minimum(acc, _row_fold(jnp.where(keep(x), x, jnp.inf), 8, jnp.minimum))
        acc = over_blocks(body, jnp.full((8, nq), jnp.inf, F32))
        return jnp.min(acc, axis=0, keepdims=True)

    def count_above(v):
        def body(s, acc):
            return acc + _row_fold(jnp.where(score_s[blk(s), :] > v, 1.0, 0.0), 8)
        return jnp.sum(over_blocks(body, jnp.zeros((8, nq), F32)), axis=0, keepdims=True)

    def refine(carry):
        v, above = carry
        v = jnp.where(above >= topk, smallest(lambda x: x > v), v)
        return v, count_above(v)

    v = smallest(lambda x: x >= v0)
    v, above = lax.while_loop(lambda c: jnp.max(c[1]) >= topk, refine, (v, count_above(v)))
    need = topk - above

    r_i = lax.broadcasted_iota(I32, (LANES, LANES), 0)
    c_i = lax.broadcasted_iota(I32, (LANES, LANES), 1)
    lower = jnp.where(r_i >= c_i, 1.0, 0.0).astype(BF16)
    sub_iota = lax.broadcasted_iota(I32, (LANES, 1), 0)

    def bias_body(s, carry):
        for j in range(0, kb, LANES):
            off = pl.multiple_of(s * kb + j, LANES)
            x = score_s[pl.ds(off, LANES), :]
            eq = x == v
            eqf = jnp.where(eq, 1.0, 0.0)
            incl = jnp.dot(lower, eqf.astype(BF16), preferred_element_type=F32)
            tie = jnp.where(carry + incl - eqf < need, 0.0, -jnp.inf)
            b = jnp.where(x > v, 0.0, jnp.where(eq, tie, -jnp.inf))
            bias_s[pl.ds(off, LANES), :] = jnp.where((off + sub_iota) <= qpos, b, -jnp.inf)
            carry = carry + incl[LANES - 1:LANES, :]
        return carry

    over_blocks(bias_body, jnp.zeros((1, nq), F32))


def _dsa_t_body(topk, idx_scale, q_ref, qi_ref, sm_ref, ki2_ref, k_ref, vt_ref, o_ref,
                score_s, sort_s, img_s, bias_s, qs_s, lga_s, lgb_s, m_s, acc_s):
    i = pl.program_id(1)
    QB, KB = Q_BLOCK, KEY_BLOCK
    nkb = (i * QB + QB + KB - 1) // KB
    qpos = i * QB + lax.broadcasted_iota(I32, (1, QB), 1)
    low = lax.broadcasted_iota(I32, (QB, LANES), 1) < HEAD
    zero_bf = jnp.zeros((QB, LANES), BF16)
    blk = lambda s: pl.ds(pl.multiple_of(s * KB, KB), KB)

    qi = qi_ref[...]
    sm_t = sm_ref[...].T
    qh, wh = [], []
    for h in range(IDX_HEADS):
        chunk = qi[:, (h // 2) * LANES:(h // 2 + 1) * LANES]
        qh.append(jnp.where(low if h % 2 == 0 else ~low, chunk, zero_bf))
        wh.append(sm_t[HEAD + h:HEAD + h + 1, :] * idx_scale)

    q_stack = jnp.concatenate(qh, axis=0)

    last_key_block = ki2_ref.shape[0] // KB - 1

    def score_pair(s2, carry):
        blocks = (2 * s2, 2 * s2 + 1)
        dots = [_bdot_nt(ki2_ref[blk(jnp.minimum(s, last_key_block)), :], q_stack) for s in blocks]
        for s, sc in zip(blocks, dots):
            acc = jnp.zeros((KB, QB), F32)
            for h in range(IDX_HEADS):
                acc = acc + wh[h] * jnp.maximum(sc[:, h * QB:(h + 1) * QB], 0.0)
            kpos = s * KB + lax.broadcasted_iota(I32, (KB, 1), 0)
            sc = jnp.where(kpos <= qpos, jnp.maximum(acc, MASKED_SCORE), MASKED_SCORE)
            score_s[blk(s), :] = sc
            sc = _sort_groups(sc)
            sort_s[blk(s), :] = sc
            img_s[blk(s), :] = sc.astype(BF16)
        return carry

    nkb_even = 2 * ((nkb + 1) // 2)
    lax.fori_loop(0, nkb_even // 2, score_pair, 0)
    _topk_bias(score_s, sort_s, img_s, bias_s, nkb_even, KB, topk, qpos, unroll=2)

    q = q_ref[...]
    nchunk = q.shape[1] // LANES
    per_kv_chunk = nchunk // (ATTN_KV_HEADS // 2)
    nstack = 2 * per_kv_chunk
    srows = nstack * QB
    for cj in range(nchunk):
        kvc, j = divmod(cj, per_kv_chunk)
        qc = q[:, cj * LANES:(cj + 1) * LANES]
        for half in range(2):
            r0 = (kvc * nstack + half * per_kv_chunk + j) * QB
            qs_s[r0:r0 + QB, :] = jnp.where(low if half == 0 else ~low, qc, zero_bf)

    ones_rows = jnp.ones((ONES_ROWS, KB), BF16)
    cols = lambda r: slice(r * QB, (r + 1) * QB)
    n_kvc = ATTN_KV_HEADS // 2
    nsteps = n_kvc * nkb

    def step_of(t):
        t = jnp.minimum(t, nsteps - 1)
        kvc = (t >= nkb).astype(I32)
        return kvc, t - kvc * nkb

    def logits_to(dst, t):
        kvc, s = step_of(t)
        kb = k_ref[blk(s), pl.ds(pl.multiple_of(kvc * LANES, LANES), LANES)]
        bias = bias_s[blk(s), :]
        lg = _bdot_nt(kb, qs_s[pl.ds(pl.multiple_of(kvc * srows, srows), srows), :])
        for r in range(nstack):
            dst[:, cols(r)] = lg[:, cols(r)] + bias

    def consume(src, t):
        kvc, s = step_of(t)
        vt = vt_ref[pl.ds(pl.multiple_of(kvc * LANES, LANES), LANES), blk(s)]
        ps, alphas = [], []
        for r in range(nstack):
            lg = src[:, cols(r)]
            m = m_s[kvc, :, cols(r)]
            mn = jnp.maximum(m, jnp.max(lg, axis=0, keepdims=True))
            m_s[kvc, :, cols(r)] = mn
            ps.append(jnp.exp2(lg - mn).astype(BF16))
            alphas.append(jnp.exp2(m - mn))
        for half in range(2):
            hs = slice(half * per_kv_chunk, (half + 1) * per_kv_chunk)
            hc = slice(half * per_kv_chunk * QB, (half + 1) * per_kv_chunk * QB)
            v_aug = jnp.concatenate([vt[half * HEAD:(half + 1) * HEAD, :], ones_rows], axis=0)
            pv = jnp.dot(v_aug, jnp.concatenate(ps[hs], axis=1), preferred_element_type=F32)
            acc_s[kvc, :, hc] = jnp.concatenate(alphas[hs], axis=1) * acc_s[kvc, :, hc] + pv

    m_s[...] = jnp.full(m_s.shape, NEG_BIG, F32)
    acc_s[...] = jnp.zeros(acc_s.shape, F32)
    logits_to(lga_s, 0)

    def run_pairs(t0, npairs):
        for p in range(npairs):
            logits_to(lgb_s, t0 + 2 * p + 1)
            consume(lga_s, t0 + 2 * p)
            logits_to(lga_s, t0 + 2 * p + 2)
            consume(lgb_s, t0 + 2 * p + 1)

    def quad_body(i4, carry):
        run_pairs(4 * i4, 2)
        return carry

    lax.fori_loop(0, nsteps // 4, quad_body, 0)

    @pl.when(nsteps % 4 == 2)
    def _():
        run_pairs(nsteps - 2, 1)

    for kvc in range(n_kvc):
        acc = acc_s[kvc]
        for j in range(per_kv_chunk):
            a = acc[:, cols(j)]
            b = acc[:, cols(per_kv_chunk + j)]
            chunk_t = jnp.concatenate([a[0:HEAD, :] / a[HEAD:HEAD + 1, :], b[0:HEAD, :] / b[HEAD:HEAD + 1, :]],
                                      axis=0)
            cj = kvc * per_kv_chunk + j
            o_ref[:, cj * LANES:(cj + 1) * LANES] = chunk_t.T.astype(BF16)


def _dsa_prompt_t(q, qi, sm, ki2, k, vt, nb, topk):
    m, aw = q.shape
    t = m // nb
    nq = t // Q_BLOCK
    tpad = -(-t // (2 * KEY_BLOCK)) * 2 * KEY_BLOCK
    srows = aw // HEAD // (ATTN_KV_HEADS // 2) * Q_BLOCK
    idx_scale = IDX_HEADS ** -0.5 * HEAD ** -0.5
    row = lambda n: pl.BlockSpec((Q_BLOCK, n), lambda b, i: (b * nq + i, 0))
    per_b = lambda n: pl.BlockSpec((t, n), lambda b, i: (b, 0))
    return pl.pallas_call(
        functools.partial(_dsa_t_body, topk, idx_scale), grid=(nb, nq),
        in_specs=[row(aw), row(qi.shape[1]), row(LANES), per_b(LANES), per_b(k.shape[1]),
                  pl.BlockSpec((None,) + vt.shape[1:], lambda b, i: (b, 0, 0))],
        out_specs=row(aw),
        out_shape=jax.ShapeDtypeStruct((m, aw), BF16),
        scratch_shapes=[pltpu.VMEM((tpad, Q_BLOCK), F32), pltpu.VMEM((tpad, Q_BLOCK), F32),
                        pltpu.VMEM((tpad, Q_BLOCK), BF16), pltpu.VMEM((tpad, Q_BLOCK), F32),
                        pltpu.VMEM((2 * srows, LANES), BF16),
                        pltpu.VMEM((KEY_BLOCK, srows), F32), pltpu.VMEM((KEY_BLOCK, srows), F32),
                        pltpu.VMEM((ATTN_KV_HEADS // 2, 1, srows), F32),
                        pltpu.VMEM((ATTN_KV_HEADS // 2, HEAD + ONES_ROWS, srows), F32)],
        **_call_opts("dsa_prompt", ("parallel", "arbitrary")),
    )(q, qi, sm, ki2, k, vt)


def _page_specs(block, npages):
    def make(u):
        return pl.BlockSpec((None,) + block, lambda b, pt: (pt[b, u],) + (0,) * len(block))
    return [make(u) for u in range(npages)]


def _idx_score_body(pt_ref, qi_ref, w_ref, *refs):
    pages, o_ref = refs[:-1], refs[-1]
    qi = qi_ref[...]
    w = w_ref[...]
    for u, page in enumerate(pages):
        s = jnp.maximum(_bdot(qi, page[...]), 0.0)
        o_ref[:, u * PAGE_SIZE:(u + 1) * PAGE_SIZE] = jnp.sum(w * s, axis=0, keepdims=True)


def _idx_scores_sample(page_table, qi8, w8, kidx_t):
    db, npages = page_table.shape
    grid_spec = pltpu.PrefetchScalarGridSpec(
        num_scalar_prefetch=1, grid=(db,),
        in_specs=[pl.BlockSpec((None,) + qi8.shape[1:], lambda b, pt: (b, 0, 0)),
                  pl.BlockSpec((None,) + w8.shape[1:], lambda b, pt: (b, 0, 0))]
                 + _page_specs(kidx_t.shape[1:], npages),
        out_specs=pl.BlockSpec((None, 1, npages * PAGE_SIZE), lambda b, pt: (b, 0, 0)))
    return pl.pallas_call(
        _idx_score_body, grid_spec=grid_spec,
        out_shape=jax.ShapeDtypeStruct((db, 1, npages * PAGE_SIZE), F32),
        **_call_opts("idx_scores_sample", ("parallel",)),
    )(page_table, qi8, w8, *([kidx_t] * npages))


def _select_sample_body(topk, idx_scale, past, sc_ref, qi_ref, sm_ref, bias_ref, score_s, sort_s, img_s, bias_s):
    rows = sc_ref.shape[0]
    sm = sm_ref[...]
    qi = qi_ref[...]
    ki = sm[:, 0:HEAD]
    new = jnp.zeros((rows, 1), F32)
    for h in range(IDX_HEADS):
        d = jnp.sum(qi[:, h * HEAD:(h + 1) * HEAD] * ki, axis=-1, keepdims=True)
        new = new + (sm[:, HEAD + h:HEAD + h + 1] * idx_scale) * jnp.maximum(d, 0.0)
    nblk = (past + LANES) // LANES
    lane = lax.broadcasted_iota(I32, (rows, LANES), 1)
    for j in range(nblk):
        js = slice(j * LANES, (j + 1) * LANES)
        sc = sc_ref[:, js] if j < nblk - 1 else jnp.where(lane == 0, new, MASKED_SCORE)
        sc = jnp.maximum(sc, MASKED_SCORE).T
        score_s[js, :] = sc
        sc = _sort_groups(sc)
        sort_s[js, :] = sc
        img_s[js, :] = sc.astype(BF16)
    qpos = jnp.full((1, rows), past, I32)
    _topk_bias(score_s, sort_s, img_s, bias_s, nblk, LANES, topk, qpos)
    for j in range(nblk):
        js = slice(j * LANES, (j + 1) * LANES)
        bias_ref[:, js] = bias_s[js, :].T


def _select_sample(scores, qi, sm, topk, idx_scale):
    db, past = scores.shape
    full = lambda a: pl.BlockSpec(a.shape, lambda i: (0,) * a.ndim)
    keys = past + LANES
    return pl.pallas_call(
        functools.partial(_select_sample_body, topk, idx_scale, past), grid=(1,),
        in_specs=[full(scores), full(qi), full(sm)],
        out_specs=pl.BlockSpec((db, keys), lambda i: (0, 0)),
        out_shape=jax.ShapeDtypeStruct((db, keys), F32),
        scratch_shapes=[pltpu.VMEM((keys, db), F32), pltpu.VMEM((keys, db), F32), pltpu.VMEM((keys, db), BF16),
                        pltpu.VMEM((keys, db), F32)],
        **_call_opts("select_sample", ("arbitrary",)))(scores, qi, sm)


def _attend_sample_body(npages, pt_ref, q_ref, bias_ref, knew_ref, vnew_ref, *refs):
    kpages, vpages, o_ref = refs[:npages], refs[npages:2 * npages], refs[2 * npages]
    G = ATTN_KV_HEADS
    q = q_ref[...]
    q_bf = q.astype(BF16)
    nh = q.shape[0]
    past = npages * PAGE_SIZE
    group = lax.broadcasted_iota(I32, (nh, 1), 0) // (nh // G)

    def by_group(parts):
        out = parts[G - 1]
        for g in range(G - 2, -1, -1):
            out = jnp.where(group == g, parts[g], out)
        return out

    lg = jnp.concatenate(
        [by_group([_bdot(q_bf, kpages[u][g]) for g in range(G)]) for u in range(npages)], axis=1)
    lg = lg + bias_ref[:, 0:past]
    lg_new = by_group([jnp.sum(q * knew_ref[g:g + 1, :], axis=-1, keepdims=True) for g in range(G)])
    lg_new = lg_new + bias_ref[:, past:past + 1]
    m = jnp.maximum(jnp.max(lg, axis=-1, keepdims=True), lg_new)
    p = jnp.exp2(lg - m)
    p_new = jnp.exp2(lg_new - m)
    denom = jnp.sum(p, axis=-1, keepdims=True) + p_new
    p_bf = p.astype(BF16)
    accs = [p_new * vnew_ref[g:g + 1, :] for g in range(G)]
    for u in range(npages):
        pu = p_bf[:, u * PAGE_SIZE:(u + 1) * PAGE_SIZE]
        for g in range(G):
            accs[g] = accs[g] + _bdot_nt(pu, vpages[u][g])
    o_ref[...] = by_group(accs) / denom


def _attend_sample(page_table, q, bias, knew, vnew, k_t, v_t):
    db, npages = page_table.shape
    per_b = lambda a: pl.BlockSpec((None,) + a.shape[1:], lambda b, pt: (b,) + (0,) * (a.ndim - 1))
    grid_spec = pltpu.PrefetchScalarGridSpec(
        num_scalar_prefetch=1, grid=(db,),
        in_specs=[per_b(q), per_b(bias), per_b(knew), per_b(vnew)]
                 + _page_specs(k_t.shape[1:], npages) + _page_specs(v_t.shape[1:], npages),
        out_specs=per_b(q))
    return pl.pallas_call(
        functools.partial(_attend_sample_body, npages), grid_spec=grid_spec,
        out_shape=jax.ShapeDtypeStruct(q.shape, F32),
        **_call_opts("attend_sample", ("parallel",)),
    )(page_table, q, bias, knew, vnew, *([k_t] * npages), *([v_t] * npages))


def _merge_body(x_ref, mod_ref, ys_ref, ya_ref, gs_ref, ga_ref, wps_ref, wpa_ref, wo_ref, o_ref):
    d = x_ref.shape[1]
    merged = (jax.nn.sigmoid(gs_ref[...]) * _bdot(ys_ref[...], wps_ref[...])
              + jax.nn.sigmoid(ga_ref[...]) * _bdot(ya_ref[...], wpa_ref[...]))
    o_ref[...] = x_ref[...] + mod_ref[:, 2 * d:3 * d] * _bdot(merged, wo_ref[...])


def _merge(x, mod3, ys, ya, gs, ga, wps, wpa, wo, tm, tpb):
    m, d = x.shape
    x_spec, mod_spec = _row_specs(tm, tpb, d, mod3.shape[1])
    row = lambda n: pl.BlockSpec((tm, n), lambda i: (i, 0))
    return pl.pallas_call(
        _merge_body, grid=(m // tm,),
        in_specs=[x_spec, mod_spec, row(ys.shape[1]), row(ya.shape[1]), row(d), row(d),
                  _const_spec(wps.shape), _const_spec(wpa.shape), _const_spec(wo.shape)],
        out_specs=row(d), out_shape=jax.ShapeDtypeStruct((m, d), F32),
        **_call_opts("merge", ("parallel",)))(x, mod3, ys, ya, gs, ga, wps, wpa, wo)


def _ffn_body(last_layer, x_ref, mod_ref, g_ref, fg_ref, wg_ref, wu_ref, wo_ref, o_ref):
    d = x_ref.shape[1]
    x = x_ref[...]
    h = _norm_mod(x, g_ref[...], mod_ref[:, 4 * d:5 * d], mod_ref[:, 3 * d:4 * d]).astype(BF16)
    gate = jnp.dot(h, wg_ref[...], preferred_element_type=F32)
    up = jnp.dot(h, wu_ref[...], preferred_element_type=F32)
    x2 = x + mod_ref[:, 5 * d:6 * d] * _bdot(_silu(gate) * up, wo_ref[...])
    if last_layer:
        ms = jnp.mean(x2 * x2, axis=-1, keepdims=True)
        x2 = x2 * lax.rsqrt(ms + NORM_EPS) * fg_ref[...]
    o_ref[...] = x2


def _ffn(x, mod3, g, fg, wg, wu, wo, tm, tpb, last_layer):
    m, d = x.shape
    x_spec, mod_spec = _row_specs(tm, tpb, d, mod3.shape[1])
    return pl.pallas_call(
        functools.partial(_ffn_body, last_layer), grid=(m // tm,),
        in_specs=[x_spec, mod_spec, _const_spec((1, d)), _const_spec((1, d)),
                  _const_spec(wg.shape), _const_spec(wu.shape), _const_spec(wo.shape)],
        out_specs=pl.BlockSpec((tm, d), lambda i: (i, 0)), out_shape=jax.ShapeDtypeStruct((m, d), F32),
        **_call_opts("ffn", ("parallel",)))(x, mod3, g, fg, wg, wu, wo)


def _rope_tables(pos):
    half = HEAD // 2
    inv = ROPE_THETA ** (-jnp.arange(half, dtype=F32) / half)
    ang = pos.astype(F32)[:, None] * inv[None, :]
    cos = jnp.tile(jnp.cos(ang), (1, LANES // half))
    sin = jnp.tile(jnp.sin(ang), (1, LANES // half))
    first = (jnp.arange(LANES) % HEAD) < half
    return cos, jnp.where(first, -sin, 0.0), jnp.where(first, 0.0, sin)


def _q_head_order(n_heads):
    rep = n_heads // ATTN_KV_HEADS
    order = []
    for c in range(ATTN_KV_HEADS // 2):
        for j in range(rep):
            order += [2 * c * rep + j, (2 * c + 1) * rep + j]
    return np.asarray(order)


def kernel(x_prompt, x_sample, cache_k, cache_v, cache_kidx, state_conv, state_ssm, page_table, c_prompt, c_sample, w_ada, b_ada, norm1_g, w_in, conv_w, conv_b, dt_bias, a_log, d_skip, ssm_norm_g, w_proj_ssm, w_proj_attn, w_out, norm2_g, w_ffn_in, w_ffn_out, final_g):
    nb, t, d = x_prompt.shape
    db, ds, _ = x_sample.shape
    depth = w_in.shape[0]
    assert ds == 1 and t % Q_BLOCK == 0 and t % SSM_CHUNK == 0
    n_heads_ssm = dt_bias.shape[1]
    inner = n_heads_ssm * HEAD
    gn = SSM_GROUPS * SSM_STATE
    cdim = inner + 2 * gn
    kvw = ATTN_KV_HEADS * HEAD
    aw = w_proj_attn.shape[1]
    n_heads = aw // HEAD
    iw = IDX_HEADS * HEAD
    ffn_hidden = w_ffn_out.shape[1]
    npages = page_table.shape[1]
    past = npages * PAGE_SIZE
    topk_p = min(TOPK_MAX, t // 4)
    topk_s = min(TOPK_MAX, (past + ds) // 4)
    assert past + ds >= topk_s
    idx_scale = IDX_HEADS ** -0.5 * HEAD ** -0.5
    tm_p = 256 if t % 256 == 0 else 128
    tpb_p = t // tm_p
    tm_tall = 512 if t % 512 == 0 else tm_p
    tpb_tall = t // tm_tall

    splits = np.cumsum([inner, inner, gn, gn, n_heads_ssm, aw, kvw, kvw, iw, HEAD, IDX_HEADS, d])
    order = _q_head_order(n_heads)
    inv_order = np.argsort(order)

    cos_p, slo_p, shi_p = _rope_tables(jnp.arange(t, dtype=I32))
    tabs_p = (cos_p, slo_p, shi_p)
    tabs_s = tuple(jnp.broadcast_to(a, (db, LANES)) for a in _rope_tables(past + jnp.arange(ds, dtype=I32)))

    rows_c = nb + db
    c_all = jnp.concatenate([c_prompt, c_sample, jnp.zeros((-rows_c % 8, d), F32)], axis=0)

    yp = x_prompt.reshape(nb * t, d)
    ys = x_sample.reshape(db, d)
    outs_p, outs_s = [], []
    for l in range(depth):
        (wz, wxs, wbm, wcm, wdt, wq, wk, wv, wqi, wki, wwi, wgs, wga) = jnp.split(w_in[l], splits, axis=1)
        w_ssm = jnp.concatenate([wz, wxs, wbm, wcm, wdt, jnp.zeros((d, LANES - n_heads_ssm), F32)],
                                axis=1).astype(BF16)
        wq_perm = wq.reshape(d, n_heads, HEAD)[:, order].reshape(d, aw)
        w_attn = jnp.concatenate([wq_perm, wk, wv, wqi, wgs, wga, wki, wwi,
                                  jnp.zeros((d, LANES - HEAD - IDX_HEADS), F32)], axis=1).astype(BF16)
        wps = w_proj_ssm[l].astype(BF16)
        wpa = w_proj_attn[l].reshape(n_heads, HEAD, d)[order].reshape(aw, d).astype(BF16)
        wo = w_out[l].astype(BF16)
        wg = w_ffn_in[l][:, :ffn_hidden].astype(BF16)
        wu = w_ffn_in[l][:, ffn_hidden:].astype(BF16)
        wfo = w_ffn_out[l].astype(BF16)
        g1 = norm1_g[l][None, :]
        g2 = norm2_g[l][None, :]
        a_neg = -jnp.exp(a_log[l])
        pad_h = LANES - n_heads_ssm
        dtb_row = jnp.pad(dt_bias[l], (0, pad_h))[None, :]
        a_row = jnp.pad(a_neg, (0, pad_h))[None, :]
        dsk_row = jnp.repeat(d_skip[l], HEAD)[None, :]
        ng_row = ssm_norm_g[l][None, :]
        cw = conv_w[l]
        cb = conv_b[l][None, :]

        mod = _ada(c_all, w_ada[l].astype(BF16), b_ada[l][None, :])
        mod_p = mod[:nb][:, None, :]
        mod_s = mod[nb:nb + db][None]

        z, xbc, dtr = _inproj_ssm(yp, mod_p, g1, w_ssm, tm_tall, tpb_tall, inner, cdim)
        (q_bf, k_bf, kt, vt, vt_bf, qi_bf, gs, ga, sm, ki2, kit) = _inproj_attn(
            yp, mod_p, g1, w_attn, tabs_p, tm_tall, tpb_tall, aw, kvw, iw)
        y_ssm, st = _ssd_prompt(z, xbc, dtr, cw, cb, dtb_row, a_row, dsk_row, ng_row, nb)
        y_attn = _dsa_prompt_t(q_bf, qi_bf, sm, ki2, k_bf, vt_bf, nb, topk_p)
        x1 = _merge(yp, mod_p, y_ssm, y_attn, gs, ga, wps, wpa, wo, tm_tall, tpb_tall)
        yp_next = _ffn(x1, mod_p, g2, final_g[None, :], wg, wu, wfo, tm_tall, tpb_tall, l == depth - 1)
        heads_last = lambda a: jnp.transpose(a.reshape(a.shape[0], ATTN_KV_HEADS, HEAD, a.shape[2]), (0, 3, 1, 2))
        outs_p.append((heads_last(kt), heads_last(vt), jnp.transpose(kit, (0, 2, 1)),
                       xbc.reshape(nb, t, cdim)[:, t - (SSM_CONV - 1):],
                       st.reshape(nb, n_heads_ssm, HEAD, SSM_STATE)))

        z_s, xbc_s, dtr_s = _inproj_ssm(ys, mod_s, g1, w_ssm, db, 1, inner, cdim)
        (q_s, _, kt_s, vt_s, _, qi_s, gs_s, ga_s, sm_s, _, kit_s) = _inproj_attn(
            ys, mod_s, g1, w_attn, tabs_s, db, 1, aw, kvw, iw)
        k_s, v_s = heads_last(kt_s)[0], heads_last(vt_s)[0]
        npair = inner // LANES
        nbc = 2 * gn // LANES
        sc = state_conv[l]
        y_ssm_s, st_s = _ssd_sample(
            z_s.reshape(db, npair, LANES), xbc_s[:, :inner].reshape(db, npair, LANES),
            xbc_s[:, inner:].reshape(db, nbc, LANES),
            sc[:, :, :inner].reshape(db, SSM_CONV - 1, npair, LANES),
            sc[:, :, inner:].reshape(db, SSM_CONV - 1, nbc, LANES),
            jnp.repeat(dtr_s[:, :n_heads_ssm], HEAD, axis=1).reshape(db, npair, LANES),
            state_ssm[l].reshape(db, inner, SSM_STATE),
            cw[:, :inner].reshape(SSM_CONV, npair, LANES), cw[:, inner:].reshape(SSM_CONV, nbc, LANES),
            cb[:, :inner].reshape(npair, LANES), cb[:, inner:].reshape(nbc, LANES),
            jnp.repeat(dt_bias[l], HEAD).reshape(npair, LANES), jnp.repeat(a_neg, HEAD).reshape(npair, LANES),
            dsk_row.reshape(npair, LANES), ng_row.reshape(npair, LANES))

        qi_f = qi_s.astype(F32)
        qi8 = jnp.pad(qi_f.reshape(db, IDX_HEADS, HEAD), ((0, 0), (0, 8 - IDX_HEADS), (0, 0)))
        w8 = jnp.broadcast_to(jnp.pad(sm_s[:, HEAD:HEAD + IDX_HEADS] * idx_scale,
                                      ((0, 0), (0, 8 - IDX_HEADS)))[:, :, None], (db, 8, LANES))
        scores = _idx_scores_sample(page_table, qi8, w8, jnp.transpose(cache_kidx[l], (0, 2, 1)))
        bias = _select_sample(scores.reshape(db, past), qi_f, sm_s, topk_s, idx_scale)
        q_orig = q_s.astype(F32).reshape(db, n_heads, HEAD)[:, inv_order]
        att = _attend_sample(page_table, q_orig, bias[:, None, :],
                             k_s, v_s,
                             jnp.transpose(cache_k[l], (0, 2, 3, 1)), jnp.transpose(cache_v[l], (0, 2, 3, 1)))
        y_attn_s = att[:, order].reshape(db, aw)
        x1_s = _merge(ys, mod_s, y_ssm_s.reshape(db, inner), y_attn_s, gs_s, ga_s, wps, wpa, wo, db, 1)
        ys_next = _ffn(x1_s, mod_s, g2, final_g[None, :], wg, wu, wfo, db, 1, l == depth - 1)
        outs_s.append((k_s[:, None], v_s[:, None], jnp.transpose(kit_s, (2, 0, 1)),
                       jnp.concatenate([sc[:, 1:], xbc_s[:, None, :]], axis=1),
                       st_s.reshape(db, n_heads_ssm, HEAD, SSM_STATE)))
        yp, ys = yp_next, ys_next

    stack = lambda outs, i: jnp.stack([o[i] for o in outs], axis=0)
    return (yp.reshape(nb, t, d), ys.reshape(db, ds, d),
            stack(outs_p, 0), stack(outs_p, 1), stack(outs_p, 2), stack(outs_p, 3), stack(outs_p, 4),
            stack(outs_s, 0), stack(outs_s, 1), stack(outs_s, 2), stack(outs_s, 3), stack(outs_s, 4))
```

```python
import functools

import jax
import jax.numpy as jnp
import numpy as np
from jax import lax
from jax.experimental import pallas as pl
from jax.experimental.pallas import tpu as pltpu

F32, BF16, I32, I16 = jnp.float32, jnp.bfloat16, jnp.int32, jnp.int16
HIGHEST = lax.Precision.HIGHEST

LANES = 128
HEAD = 64
SSM_STATE = 128
SSM_GROUPS = 4
SSM_CONV = 4
SSM_CHUNK = 128
ATTN_KV_HEADS = 4
IDX_HEADS = 4
TOPK_MAX = 256
Q_BLOCK = 128
PAGE_SIZE = 128
ROPE_THETA = 10000.0
NORM_EPS = 1e-6
KEY_BLOCK = 512
ONES_ROWS = 16
Q_SCALE = HEAD ** -0.5 * 1.4426950408889634
MASKED_SCORE = -3.3895313892515355e38
NEG_BIG = -1e30


VMEM_LIMIT_MB = {"ada": 32, "inproj_ssm": 56, "inproj_attn": 52, "ssd_prompt": 40, "ssd_sample": 32,
                 "dsa_prompt": 52, "idx_scores_sample": 32, "select_sample": 32, "attend_sample": 48,
                 "merge": 48, "ffn": 56}


def _call_opts(name, sem):
    return dict(name=name, compiler_params=pltpu.CompilerParams(
        dimension_semantics=sem, vmem_limit_bytes=VMEM_LIMIT_MB[name] << 20))


def _bdot(a, b):
    return jnp.dot(a.astype(BF16), b.astype(BF16), preferred_element_type=F32)


def _bdot_nt(a, b):
    return lax.dot_general(a.astype(BF16), b.astype(BF16), (((1,), (1,)), ((), ())),
                           preferred_element_type=F32)


def _silu(x):
    h = 0.5 * x
    return h + h * jnp.tanh(h)


def _softplus(x):
    return jnp.maximum(x, 0.0) + jnp.log(1.0 + jnp.exp(-jnp.abs(x)))


def _norm_mod(x, g, scale, shift):
    ms = jnp.mean(x * x, axis=-1, keepdims=True)
    return (x * lax.rsqrt(ms + NORM_EPS) * g) * (1.0 + scale) + shift


def _rope128(x, cos, sin_lo, sin_hi):
    return x * cos + pltpu.roll(x, 96, 1) * sin_lo + pltpu.roll(x, 32, 1) * sin_hi


def _rope_wide(x, cos, sin_lo, sin_hi):
    parts = [_rope128(x[:, j:j + LANES], cos, sin_lo, sin_hi) for j in range(0, x.shape[1], LANES)]
    return parts[0] if len(parts) == 1 else jnp.concatenate(parts, axis=1)


def _ada_body(c_ref, w_ref, b_ref, o_ref):
    o_ref[...] = _bdot(_silu(c_ref[...]), w_ref[...]) + b_ref[...]


def _ada(c_all, w_bf, b):
    mp, d = c_all.shape
    n = w_bf.shape[1]
    tn = n // 4
    return pl.pallas_call(
        _ada_body, grid=(n // tn,),
        in_specs=[pl.BlockSpec((mp, d), lambda j: (0, 0)),
                  pl.BlockSpec((d, tn), lambda j: (0, j)),
                  pl.BlockSpec((1, tn), lambda j: (0, j))],
        out_specs=pl.BlockSpec((mp, tn), lambda j: (0, j)),
        out_shape=jax.ShapeDtypeStruct((mp, n), F32),
        **_call_opts("ada", ("arbitrary",)))(c_all, w_bf, b)


def _inproj_ssm_body(inner, cdim, x_ref, mod_ref, g_ref, w_ref, z_ref, xbc_ref, dt_ref):
    d = x_ref.shape[1]
    h = _norm_mod(x_ref[...], g_ref[...], mod_ref[:, d:2 * d], mod_ref[:, 0:d]).astype(BF16)
    z_ref[...] = _bdot_nt(h, w_ref[0:inner, :])
    xbc_ref[...] = _bdot_nt(h, w_ref[inner:inner + cdim, :])
    dt_ref[...] = _bdot_nt(h, w_ref[inner + cdim:inner + cdim + LANES, :])


def _row_specs(tm, tpb, d, mod_rows):
    x_spec = pl.BlockSpec((tm, d), lambda m: (m, 0))
    mod_spec = pl.BlockSpec((None, mod_rows, 6 * d), lambda m: (m // tpb, 0, 0))
    return x_spec, mod_spec


def _const_spec(shape):
    return pl.BlockSpec(shape, lambda m: (0,) * len(shape))


def _inproj_ssm(x, mod3, g, w_bf, tm, tpb, inner, cdim):
    m, d = x.shape
    x_spec, mod_spec = _row_specs(tm, tpb, d, mod3.shape[1])
    row = lambda n: pl.BlockSpec((tm, n), lambda i: (i, 0))
    return pl.pallas_call(
        functools.partial(_inproj_ssm_body, inner, cdim), grid=(m // tm,),
        in_specs=[x_spec, mod_spec, _const_spec((1, d)), _const_spec(w_bf.shape)],
        out_specs=[row(inner), row(cdim), row(LANES)],
        out_shape=[jax.ShapeDtypeStruct((m, inner), F32), jax.ShapeDtypeStruct((m, cdim), F32),
                   jax.ShapeDtypeStruct((m, LANES), F32)],
        **_call_opts("inproj_ssm", ("parallel",)))(x, mod3, g, w_bf)


def _inproj_attn_body(aw, kvw, iw, x_ref, mod_ref, g_ref, w_ref, cos_ref, slo_ref, shi_ref,
                      q_ref, kb_ref, kt_ref, vt_ref, vtb_ref, qi_ref, gs_ref, ga_ref, sm_ref, ki2_ref, kit_ref):
    d = x_ref.shape[1]
    h = _norm_mod(x_ref[...], g_ref[...], mod_ref[:, d:2 * d], mod_ref[:, 0:d]).astype(BF16)
    cos, slo, shi = cos_ref[...], slo_ref[...], shi_ref[...]

    def proj(a, b):
        return _bdot_nt(h, w_ref[a:b, :])

    o = 0
    q_ref[...] = (_rope_wide(proj(o, o + aw), cos, slo, shi) * Q_SCALE).astype(BF16)
    o += aw
    k = _rope_wide(proj(o, o + kvw), cos, slo, shi)
    kb_ref[...] = k.astype(BF16)
    kt_ref[...] = k.T
    o += kvw
    vt = proj(o, o + kvw).T
    vt_ref[...] = vt
    vtb_ref[...] = vt.astype(BF16)
    o += kvw
    qi_ref[...] = _rope_wide(proj(o, o + iw), cos, slo, shi).astype(BF16)
    o += iw
    gs_ref[...] = proj(o, o + d)
    o += d
    ga_ref[...] = proj(o, o + d)
    o += d
    s = proj(o, o + LANES)
    lane = lax.broadcasted_iota(I32, s.shape, 1)
    sm = jnp.where(lane < HEAD, _rope128(s, cos, slo, shi), s)
    sm_ref[...] = sm
    ki2_ref[...] = jnp.where(lane < HEAD, sm, pltpu.roll(sm, HEAD, 1)).astype(BF16)
    kit_ref[...] = sm.T[0:HEAD, :]


def _inproj_attn(x, mod3, g, w_bf, tabs, tm, tpb, aw, kvw, iw):
    m, d = x.shape
    x_spec, mod_spec = _row_specs(tm, tpb, d, mod3.shape[1])
    ntab = tabs[0].shape[0] // tm
    nb = m // (tm * tpb)
    tab_spec = pl.BlockSpec((tm, LANES), lambda i: (i % ntab, 0))
    row = lambda n, dt: (pl.BlockSpec((tm, n), lambda i: (i, 0)), jax.ShapeDtypeStruct((m, n), dt))
    tmin = lambda n, dt: (pl.BlockSpec((None, n, tm), lambda i: (i // tpb, 0, i % tpb)),
                          jax.ShapeDtypeStruct((nb, n, tm * tpb), dt))
    outs = [row(aw, BF16), row(kvw, BF16), tmin(kvw, F32), tmin(kvw, F32), tmin(kvw, BF16), row(iw, BF16),
            row(d, F32), row(d, F32), row(LANES, F32), row(LANES, BF16), tmin(HEAD, F32)]
    return pl.pallas_call(
        functools.partial(_inproj_attn_body, aw, kvw, iw), grid=(m // tm,),
        in_specs=[x_spec, mod_spec, _const_spec((1, d)), _const_spec(w_bf.shape),
                  tab_spec, tab_spec, tab_spec],
        out_specs=[spec for spec, _ in outs],
        out_shape=[shape for _, shape in outs],
        **_call_opts("inproj_attn", ("parallel",)))(x, mod3, g, w_bf, *tabs)


def _ssd_body(inner, z_ref, xbc_ref, dtr_ref, cw_ref, cb_ref, dtb_ref, a_ref, dsk_ref, ng_ref,
              y_ref, st_ref, full_s, act_s, st_s, y_s):
    c = pl.program_id(1)
    Q, N = SSM_CHUNK, SSM_STATE
    cdim = xbc_ref.shape[1]
    heads_per_group = inner // HEAD // SSM_GROUPS
    gw = inner // SSM_GROUPS

    @pl.when(c == 0)
    def _():
        full_s[0:8, :] = jnp.zeros((8, cdim), F32)
        st_s[...] = jnp.zeros(st_s.shape, F32)

    full_s[8:8 + Q, :] = xbc_ref[...]
    for j in range(0, cdim, 512):
        acc = cb_ref[:, j:j + 512] + full_s[8:8 + Q, j:j + 512] * cw_ref[3:4, j:j + 512]
        for i in range(SSM_CONV - 1):
            acc = acc + full_s[5 + i:5 + i + Q, j:j + 512] * cw_ref[i:i + 1, j:j + 512]
        act_s[:, j:j + 512] = _silu(acc)
    full_s[0:8, :] = full_s[Q:Q + 8, :]

    dt = _softplus(dtr_ref[...] + dtb_ref[...])
    row = lax.broadcasted_iota(I32, (Q, Q), 0)
    col = lax.broadcasted_iota(I32, (Q, Q), 1)
    tri = row >= col
    acs = jnp.dot(tri.astype(F32), dt * a_ref[...], precision=HIGHEST, preferred_element_type=F32)
    acs_t, dt_t = acs.T, dt.T
    last = acs[Q - 1:Q, :]
    wdt = jnp.exp(last - acs) * dt
    eacs = jnp.exp(acs)
    cdec = jnp.exp(last)
    low = lax.broadcasted_iota(I32, (Q, LANES), 1) < HEAD
    low1 = low[0:1, :]

    for g in range(SSM_GROUPS):
        bg = act_s[:, inner + g * N:inner + (g + 1) * N]
        cg = act_s[:, inner + SSM_GROUPS * N + g * N:inner + SSM_GROUPS * N + (g + 1) * N]
        cb = _bdot_nt(cg, bg)
        bg_t = bg.T.astype(BF16)
        for p in range(heads_per_group // 2):
            h0 = g * heads_per_group + 2 * p
            js = slice(h0 * HEAD, h0 * HEAD + LANES)
            xp = act_s[:, js]
            xp_bf = xp.astype(BF16)
            stp = st_s[:, js]
            stp_bf = stp.astype(BF16)
            ys = []
            for h in (h0, h0 + 1):
                seg = acs[:, h:h + 1] - acs_t[h:h + 1, :]
                decay = jnp.exp(jnp.where(tri, seg, -jnp.inf))
                m = (cb * decay) * dt_t[h:h + 1, :]
                ce = cg * eacs[:, h:h + 1]
                ys.append(_bdot(m, xp_bf) + _bdot(ce, stp_bf))
            y_s[:, js] = jnp.where(low, ys[0], ys[1])
            wcol = jnp.where(low, wdt[:, h0:h0 + 1], wdt[:, h0 + 1:h0 + 2])
            dst = jnp.dot(bg_t, (xp * wcol).astype(BF16), preferred_element_type=F32)
            cd = jnp.where(low1, cdec[:, h0:h0 + 1], cdec[:, h0 + 1:h0 + 2])
            st_s[:, js] = stp * cd + dst

    for g in range(SSM_GROUPS):
        gs = slice(g * gw, (g + 1) * gw)
        y = y_s[:, gs] + dsk_ref[:, gs] * act_s[:, gs]
        y = y * _silu(z_ref[:, gs])
        ms = jnp.mean(y * y, axis=-1, keepdims=True)
        y_ref[:, gs] = (y * lax.rsqrt(ms + NORM_EPS) * ng_ref[:, gs]).astype(BF16)

    @pl.when(c == pl.num_programs(1) - 1)
    def _():
        st_ref[...] = st_s[...].T


def _ssd_prompt(z, xbc, dtr, cw, cb, dtb, a, dsk, ng, nb):
    m, inner = z.shape
    cdim = xbc.shape[1]
    nc = m // nb // SSM_CHUNK
    row = lambda n: pl.BlockSpec((SSM_CHUNK, n), lambda b, c: (b * nc + c, 0))
    const = lambda shape: pl.BlockSpec(shape, lambda b, c: (0,) * len(shape))
    return pl.pallas_call(
        functools.partial(_ssd_body, inner), grid=(nb, nc),
        in_specs=[row(inner), row(cdim), row(LANES), const(cw.shape), const(cb.shape),
                  const(dtb.shape), const(a.shape), const(dsk.shape), const(ng.shape)],
        out_specs=[row(inner), pl.BlockSpec((None, inner, SSM_STATE), lambda b, c: (b, 0, 0))],
        out_shape=[jax.ShapeDtypeStruct((m, inner), BF16),
                   jax.ShapeDtypeStruct((nb, inner, SSM_STATE), F32)],
        scratch_shapes=[pltpu.VMEM((SSM_CHUNK + 8, cdim), F32), pltpu.VMEM((SSM_CHUNK, cdim), F32),
                        pltpu.VMEM((SSM_STATE, inner), F32), pltpu.VMEM((SSM_CHUNK, inner), F32)],
        **_call_opts("ssd_prompt", ("parallel", "arbitrary")),
    )(z, xbc, dtr, cw, cb, dtb, a, dsk, ng)


def _ssd_step_body(z_ref, xs_ref, bc_ref, cxs_ref, cbc_ref, dtr_ref, st_ref,
                   wxs_ref, wbc_ref, bxs_ref, bbc_ref, dtb_ref, a_ref, dsk_ref, ng_ref,
                   y_ref, sto_ref):
    G = SSM_GROUPS
    last = SSM_CONV - 1
    xs = bxs_ref[...] + xs_ref[...] * wxs_ref[last]
    bc = bbc_ref[...] + bc_ref[...] * wbc_ref[last]
    for i in range(last):
        xs = xs + cxs_ref[i] * wxs_ref[i]
        bc = bc + cbc_ref[i] * wbc_ref[i]
    xs, bc = _silu(xs), _silu(bc)
    dt = _softplus(dtr_ref[...] + dtb_ref[...])
    dec = jnp.exp(dt * a_ref[...])
    xdt = xs * dt
    npair = xs.shape[0]
    pairs_per_group = npair // G
    r = lax.broadcasted_iota(I32, (LANES, LANES), 0)
    cidx = lax.broadcasted_iota(I32, (LANES, LANES), 1)
    eye = (r == cidx).astype(F32)
    nt = (((1,), (1,)), ((), ()))
    dec_t = lax.dot_general(eye, dec, nt, precision=HIGHEST, preferred_element_type=F32)
    xdt_t = lax.dot_general(eye, xdt, nt, precision=HIGHEST, preferred_element_type=F32)
    rows = lax.broadcasted_iota(I32, (npair, 1), 0)
    cbv = jnp.sum(bc[0:G, :] * bc[G:2 * G, :], axis=-1, keepdims=True)
    cbx = jnp.zeros((npair, 1), F32)
    for g in range(G):
        cbx = cbx + jnp.where(rows // pairs_per_group == g, cbv[g:g + 1, :], 0.0)
    c_bf = bc.astype(BF16)
    yoff = jnp.zeros(xs.shape, F32)
    for j in range(npair):
        g = j // pairs_per_group
        s = st_ref[j * LANES:(j + 1) * LANES, :]
        sto_ref[j * LANES:(j + 1) * LANES, :] = s * dec_t[:, j:j + 1] + xdt_t[:, j:j + 1] * bc[g:g + 1, :]
        rj = _bdot_nt(c_bf, s)
        yoff = yoff + jnp.where(rows == j, rj[G + g:G + g + 1, :], 0.0)
    y = yoff * dec + cbx * dt * xs + dsk_ref[...] * xs
    y = y * _silu(z_ref[...])
    ssq = jnp.sum(y * y, axis=-1, keepdims=True)
    msx = jnp.zeros((npair, 1), F32)
    for g in range(G):
        ing = rows // pairs_per_group == g
        tot = jnp.sum(jnp.where(ing, ssq, 0.0), axis=0, keepdims=True)
        msx = msx + jnp.where(ing, tot, 0.0)
    msx = msx / (pairs_per_group * LANES)
    y_ref[...] = y * lax.rsqrt(msx + NORM_EPS) * ng_ref[...]


def _ssd_sample(z, xs, bc, cxs, cbc, dtr, st, wxs, wbc, bxs, bbc, dtb, a, dsk, ng):
    db, npair, _ = z.shape
    per_b = lambda shape: pl.BlockSpec((None,) + shape, lambda b: (b,) + (0,) * len(shape))
    const = lambda arr: pl.BlockSpec(arr.shape, lambda b: (0,) * arr.ndim)
    return pl.pallas_call(
        _ssd_step_body, grid=(db,),
        in_specs=[per_b(z.shape[1:]), per_b(xs.shape[1:]), per_b(bc.shape[1:]), per_b(cxs.shape[1:]),
                  per_b(cbc.shape[1:]), per_b(dtr.shape[1:]), per_b(st.shape[1:]),
                  const(wxs), const(wbc), const(bxs), const(bbc), const(dtb), const(a), const(dsk), const(ng)],
        out_specs=[per_b(z.shape[1:]), per_b(st.shape[1:])],
        out_shape=[jax.ShapeDtypeStruct(z.shape, F32), jax.ShapeDtypeStruct(st.shape, F32)],
        **_call_opts("ssd_sample", ("parallel",)),
    )(z, xs, bc, cxs, cbc, dtr, st, wxs, wbc, bxs, bbc, dtb, a, dsk, ng)


def _row_fold(x, h, op=jnp.add):
    parts = [x[j:j + h, :] for j in range(0, x.shape[0], h)]
    while len(parts) > 1:
        parts = [op(a, b) for a, b in zip(parts[0::2], parts[1::2])] + (parts[-1:] if len(parts) % 2 else [])
    return parts[0]


IMG_ROWS = 16


def _sort_network(n):
    def merge(lo, hi, r):
        step = 2 * r
        if step < hi - lo:
            yield from merge(lo, hi, step)
            yield from merge(lo + r, hi, step)
            yield from ((i, i + r) for i in range(lo + r, hi - r, step))
        else:
            yield (lo, lo + r)

    def sort(lo, hi):
        if hi - lo >= 1:
            mid = lo + (hi - lo) // 2
            yield from sort(lo, mid)
            yield from sort(mid + 1, hi)
            yield from merge(lo, hi, 1)
    return tuple(sort(0, n - 1))


def _group_members(kb):
    tiles = kb // IMG_ROWS
    size = 16 if tiles % 16 == 0 else 8
    n = tiles // size
    return [[g + n * k for k in range(size)] for g in range(n)]


def _sort_groups(x):
    kb = x.shape[0]
    half = IMG_ROWS // 2
    rows = [x[r:r + half, :] for r in range(0, kb, half)]
    for members in _group_members(kb):
        network = _sort_network(len(members))
        for b in range(2):
            idx = [2 * p + b for p in members]
            vals = [rows[i] for i in idx]
            for i, j in network:
                vals[i], vals[j] = jnp.maximum(vals[i], vals[j]), jnp.minimum(vals[i], vals[j])
            for i, v in zip(idx, vals):
                rows[i] = v
    return jnp.concatenate(rows, axis=0)


def _topk_bias(score_s, sort_s, img_s, bias_s, nblk, kb, topk, qpos, unroll=1):
    nq = score_s.shape[1]
    blk = lambda s: pl.ds(pl.multiple_of(s * kb, kb), kb)
    one16, zero16 = jnp.int16(1), jnp.int16(0)
    work_s = bias_s
    groups = _group_members(kb)

    def over_blocks(body, init):
        def trip(s2, carry):
            for k in range(unroll):
                carry = body(s2 * unroll + k, carry)
            return carry
        return lax.fori_loop(0, nblk // unroll, trip, init)

    def count(c):
        def body(s, acc):
            img = img_s[blk(s), :]
            for members in groups:
                p = [img[m * IMG_ROWS:(m + 1) * IMG_ROWS, :] for m in members]
                masks = []

                def comparand(lo, hi, level):
                    mid = (lo + hi) // 2
                    if level == len(masks):
                        return p[mid]
                    return jnp.where(masks[level], comparand(mid + 1, hi, level + 1), comparand(lo, mid, level + 1))

                depth = len(p).bit_length() - 1
                terms = [jnp.where(p[-1] >= c, one16, zero16)]
                for level in range(depth):
                    masks.append(comparand(0, len(p) - 1, 0) >= c)
                    terms.append(jnp.where(masks[-1], jnp.int16(1 << (depth - 1 - level)), zero16))
                while len(terms) > 1:
                    terms = [a + b for a, b in zip(terms[0::2], terms[1::2])] + (terms[-1:] if len(terms) % 2 else [])
                acc = acc + terms[0]
            return acc
        acc = over_blocks(body, jnp.zeros((IMG_ROWS, nq), I16))
        return jnp.sum(acc.astype(F32), axis=0, keepdims=True)

    def search(nbits, value_of):
        def step(t, u):
            code = u | lax.shift_left(jnp.int32(1), jnp.asarray(nbits - 1 - t, I32))
            c = value_of(code).astype(BF16)
            return jnp.where(count(c) >= topk, code, u)
        return lax.fori_loop(0, nbits, step, jnp.zeros((1, nq), I32))

    def bf16_value(code):
        pattern = jnp.where(code >= 32768, code - 32768, 65535 - code)
        return pltpu.bitcast(pattern << 16, jnp.float32).astype(F32)

    def set_image(fn):
        def body(s, carry):
            img_s[blk(s), :] = fn(s).astype(BF16)
            return carry
        over_blocks(body, 0)

    t1 = bf16_value(search(16, bf16_value))

    e1 = jnp.clip((pltpu.bitcast(t1.astype(jnp.float32), I32) >> 23) & 0xFF, 25, 254)
    unit = pltpu.bitcast((e1 - 24) << 23, jnp.float32).astype(F32)
    inv_unit = pltpu.bitcast((278 - e1) << 23, jnp.float32).astype(F32)
    B2, B1 = 65536.0, 256.0

    def digit2(s):
        y = (sort_s[blk(s), :] - t1) * inv_unit
        work_s[blk(s), :] = y
        return jnp.floor(y * (1.0 / B2))
    set_image(digit2)
    t2 = (search(2, lambda code: (code - 1).astype(F32)) - 1).astype(F32)
    set_image(lambda s: jnp.floor((work_s[blk(s), :] - t2 * B2) * (1.0 / B1)))
    t3 = search(8, lambda code: code.astype(F32)).astype(F32)
    set_image(lambda s: jnp.floor(work_s[blk(s), :] - (t2 * B2 + t3 * B1)))
    t4 = search(8, lambda code: code.astype(F32)).astype(F32)
    v0 = t1 + (t2 * B2 + t3 * B1 + t4) * unit

    def smallest(keep):
        def body(s, acc):
            x = score_s[blk(s), :]
            return jnp.minimum(acc, _row_fold(jnp.where(keep(x), x, jnp.inf), 8, jnp.minimum))
        acc = over_blocks(body, jnp.full((8, nq), jnp.inf, F32))
        return jnp.min(acc, axis=0, keepdims=True)

    def count_above(v):
        def body(s, acc):
            return acc + _row_fold(jnp.where(score_s[blk(s), :] > v, 1.0, 0.0), 8)
        return jnp.sum(over_blocks(body, jnp.zeros((8, nq), F32)), axis=0, keepdims=True)

    def refine(carry):
        v, above = carry
        v = jnp.where(above >= topk, smallest(lambda x: x > v), v)
        return v, count_above(v)

    v = smallest(lambda x: x >= v0)
    v, above = lax.while_loop(lambda c: jnp.max(c[1]) >= topk, refine, (v, count_above(v)))
    need = topk - above

    r_i = lax.broadcasted_iota(I32, (LANES, LANES), 0)
    c_i = lax.broadcasted_iota(I32, (LANES, LANES), 1)
    lower = jnp.where(r_i >= c_i, 1.0, 0.0).astype(BF16)
    sub_iota = lax.broadcasted_iota(I32, (LANES, 1), 0)

    def bias_body(s, carry):
        for j in range(0, kb, LANES):
            off = pl.multiple_of(s * kb + j, LANES)
            x = score_s[pl.ds(off, LANES), :]
            eq = x == v
            eqf = jnp.where(eq, 1.0, 0.0)
            incl = jnp.dot(lower, eqf.astype(BF16), preferred_element_type=F32)
            tie = jnp.where(carry + incl - eqf < need, 0.0, -jnp.inf)
            b = jnp.where(x > v, 0.0, jnp.where(eq, tie, -jnp.inf))
            bias_s[pl.ds(off, LANES), :] = jnp.where((off + sub_iota) <= qpos, b, -jnp.inf)
            carry = carry + incl[LANES - 1:LANES, :]
        return carry

    over_blocks(bias_body, jnp.zeros((1, nq), F32))


def _dsa_t_body(topk, idx_scale, q_ref, qi_ref, sm_ref, ki2_ref, k_ref, vt_ref, o_ref,
                score_s, sort_s, img_s, bias_s, qs_s, lga_s, lgb_s, m_s, acc_s):
    i = pl.program_id(1)
    QB, KB = Q_BLOCK, KEY_BLOCK
    nkb = (i * QB + QB + KB - 1) // KB
    qpos = i * QB + lax.broadcasted_iota(I32, (1, QB), 1)
    low = lax.broadcasted_iota(I32, (QB, LANES), 1) < HEAD
    zero_bf = jnp.zeros((QB, LANES), BF16)
    blk = lambda s: pl.ds(pl.multiple_of(s * KB, KB), KB)

    qi = qi_ref[...]
    sm_t = sm_ref[...].T
    qh, wh = [], []
    for h in range(IDX_HEADS):
        chunk = qi[:, (h // 2) * LANES:(h // 2 + 1) * LANES]
        qh.append(jnp.where(low if h % 2 == 0 else ~low, chunk, zero_bf))
        wh.append(sm_t[HEAD + h:HEAD + h + 1, :] * idx_scale)

    q_stack = jnp.concatenate(qh, axis=0)

    last_key_block = ki2_ref.shape[0] // KB - 1

    def score_pair(s2, carry):
        blocks = (2 * s2, 2 * s2 + 1)
        dots = [_bdot_nt(ki2_ref[blk(jnp.minimum(s, last_key_block)), :], q_stack) for s in blocks]
        for s, sc in zip(blocks, dots):
            acc = jnp.zeros((KB, QB), F32)
            for h in range(IDX_HEADS):
                acc = acc + wh[h] * jnp.maximum(sc[:, h * QB:(h + 1) * QB], 0.0)
            kpos = s * KB + lax.broadcasted_iota(I32, (KB, 1), 0)
            sc = jnp.where(kpos <= qpos, jnp.maximum(acc, MASKED_SCORE), MASKED_SCORE)
            score_s[blk(s), :] = sc
            sc = _sort_groups(sc)
            sort_s[blk(s), :] = sc
            img_s[blk(s), :] = sc.astype(BF16)
        return carry

    nkb_even = 2 * ((nkb + 1) // 2)
    lax.fori_loop(0, nkb_even // 2, score_pair, 0)
    _topk_bias(score_s, sort_s, img_s, bias_s, nkb_even, KB, topk, qpos, unroll=2)

    q = q_ref[...]
    nchunk = q.shape[1] // LANES
    per_kv_chunk = nchunk // (ATTN_KV_HEADS // 2)
    nstack = 2 * per_kv_chunk
    srows = nstack * QB
    for cj in range(nchunk):
        kvc, j = divmod(cj, per_kv_chunk)
        qc = q[:, cj * LANES:(cj + 1) * LANES]
        for half in range(2):
            r0 = (kvc * nstack + half * per_kv_chunk + j) * QB
            qs_s[r0:r0 + QB, :] = jnp.where(low if half == 0 else ~low, qc, zero_bf)

    ones_rows = jnp.ones((ONES_ROWS, KB), BF16)
    cols = lambda r: slice(r * QB, (r + 1) * QB)
    n_kvc = ATTN_KV_HEADS // 2
    nsteps = n_kvc * nkb

    def step_of(t):
        t = jnp.minimum(t, nsteps - 1)
        kvc = (t >= nkb).astype(I32)
        return kvc, t - kvc * nkb

    def logits_to(dst, t):
        kvc, s = step_of(t)
        kb = k_ref[blk(s), pl.ds(pl.multiple_of(kvc * LANES, LANES), LANES)]
        bias = bias_s[blk(s), :]
        lg = _bdot_nt(kb, qs_s[pl.ds(pl.multiple_of(kvc * srows, srows), srows), :])
        for r in range(nstack):
            dst[:, cols(r)] = lg[:, cols(r)] + bias

    def consume(src, t):
        kvc, s = step_of(t)
        vt = vt_ref[pl.ds(pl.multiple_of(kvc * LANES, LANES), LANES), blk(s)]
        ps, alphas = [], []
        for r in range(nstack):
            lg = src[:, cols(r)]
            m = m_s[kvc, :, cols(r)]
            mn = jnp.maximum(m, jnp.max(lg, axis=0, keepdims=True))
            m_s[kvc, :, cols(r)] = mn
            ps.append(jnp.exp2(lg - mn).astype(BF16))
            alphas.append(jnp.exp2(m - mn))
        for half in range(2):
            hs = slice(half * per_kv_chunk, (half + 1) * per_kv_chunk)
            hc = slice(half * per_kv_chunk * QB, (half + 1) * per_kv_chunk * QB)
            v_aug = jnp.concatenate([vt[half * HEAD:(half + 1) * HEAD, :], ones_rows], axis=0)
            pv = jnp.dot(v_aug, jnp.concatenate(ps[hs], axis=1), preferred_element_type=F32)
            acc_s[kvc, :, hc] = jnp.concatenate(alphas[hs], axis=1) * acc_s[kvc, :, hc] + pv

    m_s[...] = jnp.full(m_s.shape, NEG_BIG, F32)
    acc_s[...] = jnp.zeros(acc_s.shape, F32)
    logits_to(lga_s, 0)

    def run_pairs(t0, npairs):
        for p in range(npairs):
            logits_to(lgb_s, t0 + 2 * p + 1)
            consume(lga_s, t0 + 2 * p)
            logits_to(lga_s, t0 + 2 * p + 2)
            consume(lgb_s, t0 + 2 * p + 1)

    def quad_body(i4, carry):
        run_pairs(4 * i4, 2)
        return carry

    lax.fori_loop(0, nsteps // 4, quad_body, 0)

    @pl.when(nsteps % 4 == 2)
    def _():
        run_pairs(nsteps - 2, 1)

    for kvc in range(n_kvc):
        acc = acc_s[kvc]
        for j in range(per_kv_chunk):
            a = acc[:, cols(j)]
            b = acc[:, cols(per_kv_chunk + j)]
            chunk_t = jnp.concatenate([a[0:HEAD, :] / a[HEAD:HEAD + 1, :], b[0:HEAD, :] / b[HEAD:HEAD + 1, :]],
                                      axis=0)
            cj = kvc * per_kv_chunk + j
            o_ref[:, cj * LANES:(cj + 1) * LANES] = chunk_t.T.astype(BF16)


def _dsa_prompt_t(q, qi, sm, ki2, k, vt, nb, topk):
    m, aw = q.shape
    t = m // nb
    nq = t // Q_BLOCK
    tpad = -(-t // (2 * KEY_BLOCK)) * 2 * KEY_BLOCK
    srows = aw // HEAD // (ATTN_KV_HEADS // 2) * Q_BLOCK
    idx_scale = IDX_HEADS ** -0.5 * HEAD ** -0.5
    row = lambda n: pl.BlockSpec((Q_BLOCK, n), lambda b, i: (b * nq + i, 0))
    per_b = lambda n: pl.BlockSpec((t, n), lambda b, i: (b, 0))
    return pl.pallas_call(
        functools.partial(_dsa_t_body, topk, idx_scale), grid=(nb, nq),
        in_specs=[row(aw), row(qi.shape[1]), row(LANES), per_b(LANES), per_b(k.shape[1]),
                  pl.BlockSpec((None,) + vt.shape[1:], lambda b, i: (b, 0, 0))],
        out_specs=row(aw),
        out_shape=jax.ShapeDtypeStruct((m, aw), BF16),
        scratch_shapes=[pltpu.VMEM((tpad, Q_BLOCK), F32), pltpu.VMEM((tpad, Q_BLOCK), F32),
                        pltpu.VMEM((tpad, Q_BLOCK), BF16), pltpu.VMEM((tpad, Q_BLOCK), F32),
                        pltpu.VMEM((2 * srows, LANES), BF16),
                        pltpu.VMEM((KEY_BLOCK, srows), F32), pltpu.VMEM((KEY_BLOCK, srows), F32),
                        pltpu.VMEM((ATTN_KV_HEADS // 2, 1, srows), F32),
                        pltpu.VMEM((ATTN_KV_HEADS // 2, HEAD + ONES_ROWS, srows), F32)],
        **_call_opts("dsa_prompt", ("parallel", "arbitrary")),
    )(q, qi, sm, ki2, k, vt)


def _page_specs(block, npages):
    def make(u):
        return pl.BlockSpec((None,) + block, lambda b, pt: (pt[b, u],) + (0,) * len(block))
    return [make(u) for u in range(npages)]


def _idx_score_body(pt_ref, qi_ref, w_ref, *refs):
    pages, o_ref = refs[:-1], refs[-1]
    qi = qi_ref[...]
    w = w_ref[...]
    for u, page in enumerate(pages):
        s = jnp.maximum(_bdot(qi, page[...]), 0.0)
        o_ref[:, u * PAGE_SIZE:(u + 1) * PAGE_SIZE] = jnp.sum(w * s, axis=0, keepdims=True)


def _idx_scores_sample(page_table, qi8, w8, kidx_t):
    db, npages = page_table.shape
    grid_spec = pltpu.PrefetchScalarGridSpec(
        num_scalar_prefetch=1, grid=(db,),
        in_specs=[pl.BlockSpec((None,) + qi8.shape[1:], lambda b, pt: (b, 0, 0)),
                  pl.BlockSpec((None,) + w8.shape[1:], lambda b, pt: (b, 0, 0))]
                 + _page_specs(kidx_t.shape[1:], npages),
        out_specs=pl.BlockSpec((None, 1, npages * PAGE_SIZE), lambda b, pt: (b, 0, 0)))
    return pl.pallas_call(
        _idx_score_body, grid_spec=grid_spec,
        out_shape=jax.ShapeDtypeStruct((db, 1, npages * PAGE_SIZE), F32),
        **_call_opts("idx_scores_sample", ("parallel",)),
    )(page_table, qi8, w8, *([kidx_t] * npages))


def _select_sample_body(topk, idx_scale, past, sc_ref, qi_ref, sm_ref, bias_ref, score_s, sort_s, img_s, bias_s):
    rows = sc_ref.shape[0]
    sm = sm_ref[...]
    qi = qi_ref[...]
    ki = sm[:, 0:HEAD]
    new = jnp.zeros((rows, 1), F32)
    for h in range(IDX_HEADS):
        d = jnp.sum(qi[:, h * HEAD:(h + 1) * HEAD] * ki, axis=-1, keepdims=True)
        new = new + (sm[:, HEAD + h:HEAD + h + 1] * idx_scale) * jnp.maximum(d, 0.0)
    nblk = (past + LANES) // LANES
    lane = lax.broadcasted_iota(I32, (rows, LANES), 1)
    for j in range(nblk):
        js = slice(j * LANES, (j + 1) * LANES)
        sc = sc_ref[:, js] if j < nblk - 1 else jnp.where(lane == 0, new, MASKED_SCORE)
        sc = jnp.maximum(sc, MASKED_SCORE).T
        score_s[js, :] = sc
        sc = _sort_groups(sc)
        sort_s[js, :] = sc
        img_s[js, :] = sc.astype(BF16)
    qpos = jnp.full((1, rows), past, I32)
    _topk_bias(score_s, sort_s, img_s, bias_s, nblk, LANES, topk, qpos)
    for j in range(nblk):
        js = slice(j * LANES, (j + 1) * LANES)
        bias_ref[:, js] = bias_s[js, :].T


def _select_sample(scores, qi, sm, topk, idx_scale):
    db, past = scores.shape
    full = lambda a: pl.BlockSpec(a.shape, lambda i: (0,) * a.ndim)
    keys = past + LANES
    return pl.pallas_call(
        functools.partial(_select_sample_body, topk, idx_scale, past), grid=(1,),
        in_specs=[full(scores), full(qi), full(sm)],
        out_specs=pl.BlockSpec((db, keys), lambda i: (0, 0)),
        out_shape=jax.ShapeDtypeStruct((db, keys), F32),
        scratch_shapes=[pltpu.VMEM((keys, db), F32), pltpu.VMEM((keys, db), F32), pltpu.VMEM((keys, db), BF16),
                        pltpu.VMEM((keys, db), F32)],
        **_call_opts("select_sample", ("arbitrary",)))(scores, qi, sm)


def _attend_sample_body(npages, pt_ref, q_ref, bias_ref, knew_ref, vnew_ref, *refs):
    kpages, vpages, o_ref = refs[:npages], refs[npages:2 * npages], refs[2 * npages]
    G = ATTN_KV_HEADS
    q = q_ref[...]
    q_bf = q.astype(BF16)
    nh = q.shape[0]
    past = npages * PAGE_SIZE
    group = lax.broadcasted_iota(I32, (nh, 1), 0) // (nh // G)

    def by_group(parts):
        out = parts[G - 1]
        for g in range(G - 2, -1, -1):
            out = jnp.where(group == g, parts[g], out)
        return out

    lg = jnp.concatenate(
        [by_group([_bdot(q_bf, kpages[u][g]) for g in range(G)]) for u in range(npages)], axis=1)
    lg = lg + bias_ref[:, 0:past]
    lg_new = by_group([jnp.sum(q * knew_ref[g:g + 1, :], axis=-1, keepdims=True) for g in range(G)])
    lg_new = lg_new + bias_ref[:, past:past + 1]
    m = jnp.maximum(jnp.max(lg, axis=-1, keepdims=True), lg_new)
    p = jnp.exp2(lg - m)
    p_new = jnp.exp2(lg_new - m)
    denom = jnp.sum(p, axis=-1, keepdims=True) + p_new
    p_bf = p.astype(BF16)
    accs = [p_new * vnew_ref[g:g + 1, :] for g in range(G)]
    for u in range(npages):
        pu = p_bf[:, u * PAGE_SIZE:(u + 1) * PAGE_SIZE]
        for g in range(G):
            accs[g] = accs[g] + _bdot_nt(pu, vpages[u][g])
    o_ref[...] = by_group(accs) / denom


def _attend_sample(page_table, q, bias, knew, vnew, k_t, v_t):
    db, npages = page_table.shape
    per_b = lambda a: pl.BlockSpec((None,) + a.shape[1:], lambda b, pt: (b,) + (0,) * (a.ndim - 1))
    grid_spec = pltpu.PrefetchScalarGridSpec(
        num_scalar_prefetch=1, grid=(db,),
        in_specs=[per_b(q), per_b(bias), per_b(knew), per_b(vnew)]
                 + _page_specs(k_t.shape[1:], npages) + _page_specs(v_t.shape[1:], npages),
        out_specs=per_b(q))
    return pl.pallas_call(
        functools.partial(_attend_sample_body, npages), grid_spec=grid_spec,
        out_shape=jax.ShapeDtypeStruct(q.shape, F32),
        **_call_opts("attend_sample", ("parallel",)),
    )(page_table, q, bias, knew, vnew, *([k_t] * npages), *([v_t] * npages))


def _merge_body(x_ref, mod_ref, ys_ref, ya_ref, gs_ref, ga_ref, wps_ref, wpa_ref, wo_ref, o_ref):
    d = x_ref.shape[1]
    merged = (jax.nn.sigmoid(gs_ref[...]) * _bdot(ys_ref[...], wps_ref[...])
              + jax.nn.sigmoid(ga_ref[...]) * _bdot(ya_ref[...], wpa_ref[...]))
    o_ref[...] = x_ref[...] + mod_ref[:, 2 * d:3 * d] * _bdot(merged, wo_ref[...])


def _merge(x, mod3, ys, ya, gs, ga, wps, wpa, wo, tm, tpb):
    m, d = x.shape
    x_spec, mod_spec = _row_specs(tm, tpb, d, mod3.shape[1])
    row = lambda n: pl.BlockSpec((tm, n), lambda i: (i, 0))
    return pl.pallas_call(
        _merge_body, grid=(m // tm,),
        in_specs=[x_spec, mod_spec, row(ys.shape[1]), row(ya.shape[1]), row(d), row(d),
                  _const_spec(wps.shape), _const_spec(wpa.shape), _const_spec(wo.shape)],
        out_specs=row(d), out_shape=jax.ShapeDtypeStruct((m, d), F32),
        **_call_opts("merge", ("parallel",)))(x, mod3, ys, ya, gs, ga, wps, wpa, wo)


def _ffn_body(last_layer, x_ref, mod_ref, g_ref, fg_ref, wg_ref, wu_ref, wo_ref, o_ref):
    d = x_ref.shape[1]
    x = x_ref[...]
    h = _norm_mod(x, g_ref[...], mod_ref[:, 4 * d:5 * d], mod_ref[:, 3 * d:4 * d]).astype(BF16)
    gate = jnp.dot(h, wg_ref[...], preferred_element_type=F32)
    up = jnp.dot(h, wu_ref[...], preferred_element_type=F32)
    x2 = x + mod_ref[:, 5 * d:6 * d] * _bdot(_silu(gate) * up, wo_ref[...])
    if last_layer:
        ms = jnp.mean(x2 * x2, axis=-1, keepdims=True)
        x2 = x2 * lax.rsqrt(ms + NORM_EPS) * fg_ref[...]
    o_ref[...] = x2


def _ffn(x, mod3, g, fg, wg, wu, wo, tm, tpb, last_layer):
    m, d = x.shape
    x_spec, mod_spec = _row_specs(tm, tpb, d, mod3.shape[1])
    return pl.pallas_call(
        functools.partial(_ffn_body, last_layer), grid=(m // tm,),
        in_specs=[x_spec, mod_spec, _const_spec((1, d)), _const_spec((1, d)),
                  _const_spec(wg.shape), _const_spec(wu.shape), _const_spec(wo.shape)],
        out_specs=pl.BlockSpec((tm, d), lambda i: (i, 0)), out_shape=jax.ShapeDtypeStruct((m, d), F32),
        **_call_opts("ffn", ("parallel",)))(x, mod3, g, fg, wg, wu, wo)


def _rope_tables(pos):
    half = HEAD // 2
    inv = ROPE_THETA ** (-jnp.arange(half, dtype=F32) / half)
    ang = pos.astype(F32)[:, None] * inv[None, :]
    cos = jnp.tile(jnp.cos(ang), (1, LANES // half))
    sin = jnp.tile(jnp.sin(ang), (1, LANES // half))
    first = (jnp.arange(LANES) % HEAD) < half
    return cos, jnp.where(first, -sin, 0.0), jnp.where(first, 0.0, sin)


def _q_head_order(n_heads):
    rep = n_heads // ATTN_KV_HEADS
    order = []
    for c in range(ATTN_KV_HEADS // 2):
        for j in range(rep):
            order += [2 * c * rep + j, (2 * c + 1) * rep + j]
    return np.asarray(order)


def kernel(x_prompt, x_sample, cache_k, cache_v, cache_kidx, state_conv, state_ssm, page_table, c_prompt, c_sample, w_ada, b_ada, norm1_g, w_in, conv_w, conv_b, dt_bias, a_log, d_skip, ssm_norm_g, w_proj_ssm, w_proj_attn, w_out, norm2_g, w_ffn_in, w_ffn_out, final_g):
    nb, t, d = x_prompt.shape
    db, ds, _ = x_sample.shape
    depth = w_in.shape[0]
    assert ds == 1 and t % Q_BLOCK == 0 and t % SSM_CHUNK == 0
    n_heads_ssm = dt_bias.shape[1]
    inner = n_heads_ssm * HEAD
    gn = SSM_GROUPS * SSM_STATE
    cdim = inner + 2 * gn
    kvw = ATTN_KV_HEADS * HEAD
    aw = w_proj_attn.shape[1]
    n_heads = aw // HEAD
    iw = IDX_HEADS * HEAD
    ffn_hidden = w_ffn_out.shape[1]
    npages = page_table.shape[1]
    past = npages * PAGE_SIZE
    topk_p = min(TOPK_MAX, t // 4)
    topk_s = min(TOPK_MAX, (past + ds) // 4)
    assert past + ds >= topk_s
    idx_scale = IDX_HEADS ** -0.5 * HEAD ** -0.5
    tm_p = 256 if t % 256 == 0 else 128
    tpb_p = t // tm_p
    tm_tall = 512 if t % 512 == 0 else tm_p
    tpb_tall = t // tm_tall

    splits = np.cumsum([inner, inner, gn, gn, n_heads_ssm, aw, kvw, kvw, iw, HEAD, IDX_HEADS, d])
    order = _q_head_order(n_heads)
    inv_order = np.argsort(order)

    cos_p, slo_p, shi_p = _rope_tables(jnp.arange(t, dtype=I32))
    tabs_p = (cos_p, slo_p, shi_p)
    tabs_s = tuple(jnp.broadcast_to(a, (db, LANES)) for a in _rope_tables(past + jnp.arange(ds, dtype=I32)))

    rows_c = nb + db
    c_all = jnp.concatenate([c_prompt, c_sample, jnp.zeros((-rows_c % 8, d), F32)], axis=0)

    yp = x_prompt.reshape(nb * t, d)
    ys = x_sample.reshape(db, d)
    outs_p, outs_s = [], []
    for l in range(depth):
        (wz, wxs, wbm, wcm, wdt, wq, wk, wv, wqi, wki, wwi, wgs, wga) = jnp.split(w_in[l].T, splits, axis=0)
        w_ssm = jnp.concatenate([wz, wxs, wbm, wcm, wdt, jnp.zeros((LANES - n_heads_ssm, d), F32)],
                                axis=0).astype(BF16)
        wq_perm = wq.reshape(n_heads, HEAD, d)[order].reshape(aw, d)
        w_attn = jnp.concatenate([wq_perm, wk, wv, wqi, wgs, wga, wki, wwi,
                                  jnp.zeros((LANES - HEAD - IDX_HEADS, d), F32)], axis=0).astype(BF16)
        wps = w_proj_ssm[l].astype(BF16)
        wpa = w_proj_attn[l].reshape(n_heads, HEAD, d)[order].reshape(aw, d).astype(BF16)
        wo = w_out[l].astype(BF16)
        wg = w_ffn_in[l][:, :ffn_hidden].astype(BF16)
        wu = w_ffn_in[l][:, ffn_hidden:].astype(BF16)
        wfo = w_ffn_out[l].astype(BF16)
        g1 = norm1_g[l][None, :]
        g2 = norm2_g[l][None, :]
        a_neg = -jnp.exp(a_log[l])
        pad_h = LANES - n_heads_ssm
        dtb_row = jnp.pad(dt_bias[l], (0, pad_h))[None, :]
        a_row = jnp.pad(a_neg, (0, pad_h))[None, :]
        dsk_row = jnp.repeat(d_skip[l], HEAD)[None, :]
        ng_row = ssm_norm_g[l][None, :]
        cw = conv_w[l]
        cb = conv_b[l][None, :]

        mod = _ada(c_all, w_ada[l].astype(BF16), b_ada[l][None, :])
        mod_p = mod[:nb][:, None, :]
        mod_s = mod[nb:nb + db][None]

        z, xbc, dtr = _inproj_ssm(yp, mod_p, g1, w_ssm, tm_tall, tpb_tall, inner, cdim)
        (q_bf, k_bf, kt, vt, vt_bf, qi_bf, gs, ga, sm, ki2, kit) = _inproj_attn(
            yp, mod_p, g1, w_attn, tabs_p, tm_tall, tpb_tall, aw, kvw, iw)
        y_ssm, st = _ssd_prompt(z, xbc, dtr, cw, cb, dtb_row, a_row, dsk_row, ng_row, nb)
        y_attn = _dsa_prompt_t(q_bf, qi_bf, sm, ki2, k_bf, vt_bf, nb, topk_p)
        x1 = _merge(yp, mod_p, y_ssm, y_attn, gs, ga, wps, wpa, wo, tm_tall, tpb_tall)
        yp_next = _ffn(x1, mod_p, g2, final_g[None, :], wg, wu, wfo, tm_tall, tpb_tall, l == depth - 1)
        heads_last = lambda a: jnp.transpose(a.reshape(a.shape[0], ATTN_KV_HEADS, HEAD, a.shape[2]), (0, 3, 1, 2))
        outs_p.append((heads_last(kt), heads_last(vt), jnp.transpose(kit, (0, 2, 1)),
                       xbc.reshape(nb, t, cdim)[:, t - (SSM_CONV - 1):],
                       st.reshape(nb, n_heads_ssm, HEAD, SSM_STATE)))

        z_s, xbc_s, dtr_s = _inproj_ssm(ys, mod_s, g1, w_ssm, db, 1, inner, cdim)
        (q_s, _, kt_s, vt_s, _, qi_s, gs_s, ga_s, sm_s, _, kit_s) = _inproj_attn(
            ys, mod_s, g1, w_attn, tabs_s, db, 1, aw, kvw, iw)
        k_s, v_s = heads_last(kt_s)[0], heads_last(vt_s)[0]
        npair = inner // LANES
        nbc = 2 * gn // LANES
        sc = state_conv[l]
        y_ssm_s, st_s = _ssd_sample(
            z_s.reshape(db, npair, LANES), xbc_s[:, :inner].reshape(db, npair, LANES),
            xbc_s[:, inner:].reshape(db, nbc, LANES),
            sc[:, :, :inner].reshape(db, SSM_CONV - 1, npair, LANES),
            sc[:, :, inner:].reshape(db, SSM_CONV - 1, nbc, LANES),
            jnp.repeat(dtr_s[:, :n_heads_ssm], HEAD, axis=1).reshape(db, npair, LANES),
            state_ssm[l].reshape(db, inner, SSM_STATE),
            cw[:, :inner].reshape(SSM_CONV, npair, LANES), cw[:, inner:].reshape(SSM_CONV, nbc, LANES),
            cb[:, :inner].reshape(npair, LANES), cb[:, inner:].reshape(nbc, LANES),
            jnp.repeat(dt_bias[l], HEAD).reshape(npair, LANES), jnp.repeat(a_neg, HEAD).reshape(npair, LANES),
            dsk_row.reshape(npair, LANES), ng_row.reshape(npair, LANES))

        qi_f = qi_s.astype(F32)
        qi8 = jnp.pad(qi_f.reshape(db, IDX_HEADS, HEAD), ((0, 0), (0, 8 - IDX_HEADS), (0, 0)))
        w8 = jnp.broadcast_to(jnp.pad(sm_s[:, HEAD:HEAD + IDX_HEADS] * idx_scale,
                                      ((0, 0), (0, 8 - IDX_HEADS)))[:, :, None], (db, 8, LANES))
        scores = _idx_scores_sample(page_table, qi8, w8, jnp.transpose(cache_kidx[l], (0, 2, 1)))
        bias = _select_sample(scores.reshape(db, past), qi_f, sm_s, topk_s, idx_scale)
        q_orig = q_s.astype(F32).reshape(db, n_heads, HEAD)[:, inv_order]
        att = _attend_sample(page_table, q_orig, bias[:, None, :],
                             k_s, v_s,
                             jnp.transpose(cache_k[l], (0, 2, 3, 1)), jnp.transpose(cache_v[l], (0, 2, 3, 1)))
        y_attn_s = att[:, order].reshape(db, aw)
        x1_s = _merge(ys, mod_s, y_ssm_s.reshape(db, inner), y_attn_s, gs_s, ga_s, wps, wpa, wo, db, 1)
        ys_next = _ffn(x1_s, mod_s, g2, final_g[None, :], wg, wu, wfo, db, 1, l == depth - 1)
        outs_s.append((k_s[:, None], v_s[:, None], jnp.transpose(kit_s, (2, 0, 1)),
                       jnp.concatenate([sc[:, 1:], xbc_s[:, None, :]], axis=1),
                       st_s.reshape(db, n_heads_ssm, HEAD, SSM_STATE)))
        yp, ys = yp_next, ys_next

    stack = lambda outs, i: jnp.stack([o[i] for o in outs], axis=0)
    return (yp.reshape(nb, t, d), ys.reshape(db, ds, d),
            stack(outs_p, 0), stack(outs_p, 1), stack(outs_p, 2), stack(outs_p, 3), stack(outs_p, 4),
            stack(outs_s, 0), stack(outs_s, 1), stack(outs_s, 2), stack(outs_s, 3), stack(outs_s, 4))
```

```python
import functools

import jax
import jax.numpy as jnp
import numpy as np
from jax import lax
from jax.experimental import pallas as pl
from jax.experimental.pallas import tpu as pltpu

F32, BF16, I32, I16 = jnp.float32, jnp.bfloat16, jnp.int32, jnp.int16
HIGHEST = lax.Precision.HIGHEST

LANES = 128
HEAD = 64
SSM_STATE = 128
SSM_GROUPS = 4
SSM_CONV = 4
SSM_CHUNK = 128
ATTN_KV_HEADS = 4
IDX_HEADS = 4
TOPK_MAX = 256
Q_BLOCK = 128
PAGE_SIZE = 128
ROPE_THETA = 10000.0
NORM_EPS = 1e-6
KEY_BLOCK = 512
ONES_ROWS = 16
Q_SCALE = HEAD ** -0.5 * 1.4426950408889634
MASKED_SCORE = -3.3895313892515355e38
NEG_BIG = -1e30


VMEM_LIMIT_MB = {"ada": 32, "inproj_ssm": 56, "inproj_attn": 52, "ssd_prompt": 40, "ssd_sample": 32,
                 "dsa_prompt": 52, "idx_scores_sample": 32, "select_sample": 32, "attend_sample": 48,
                 "merge_ffn": 56}


def _call_opts(name, sem):
    return dict(name=name, compiler_params=pltpu.CompilerParams(
        dimension_semantics=sem, vmem_limit_bytes=VMEM_LIMIT_MB[name] << 20))


def _bdot(a, b):
    return jnp.dot(a.astype(BF16), b.astype(BF16), preferred_element_type=F32)


def _bdot_nt(a, b):
    return lax.dot_general(a.astype(BF16), b.astype(BF16), (((1,), (1,)), ((), ())),
                           preferred_element_type=F32)


def _silu(x):
    h = 0.5 * x
    return h + h * jnp.tanh(h)


def _softplus(x):
    return jnp.maximum(x, 0.0) + jnp.log(1.0 + jnp.exp(-jnp.abs(x)))


def _norm_mod(x, g, scale, shift):
    ms = jnp.mean(x * x, axis=-1, keepdims=True)
    return (x * lax.rsqrt(ms + NORM_EPS) * g) * (1.0 + scale) + shift


def _rope128(x, cos, sin_lo, sin_hi):
    return x * cos + pltpu.roll(x, 96, 1) * sin_lo + pltpu.roll(x, 32, 1) * sin_hi


def _rope_wide(x, cos, sin_lo, sin_hi):
    parts = [_rope128(x[:, j:j + LANES], cos, sin_lo, sin_hi) for j in range(0, x.shape[1], LANES)]
    return parts[0] if len(parts) == 1 else jnp.concatenate(parts, axis=1)


def _ada_body(c_ref, w_ref, b_ref, o_ref):
    o_ref[...] = _bdot(_silu(c_ref[...]), w_ref[...]) + b_ref[...]


def _ada(c_all, w_bf, b):
    mp, d = c_all.shape
    n = w_bf.shape[1]
    tn = n // 4
    return pl.pallas_call(
        _ada_body, grid=(n // tn,),
        in_specs=[pl.BlockSpec((mp, d), lambda j: (0, 0)),
                  pl.BlockSpec((d, tn), lambda j: (0, j)),
                  pl.BlockSpec((1, tn), lambda j: (0, j))],
        out_specs=pl.BlockSpec((mp, tn), lambda j: (0, j)),
        out_shape=jax.ShapeDtypeStruct((mp, n), F32),
        **_call_opts("ada", ("arbitrary",)))(c_all, w_bf, b)


def _inproj_ssm_body(inner, cdim, x_ref, mod_ref, g_ref, w_ref, z_ref, xbc_ref, dt_ref):
    d = x_ref.shape[1]
    h = _norm_mod(x_ref[...], g_ref[...], mod_ref[:, d:2 * d], mod_ref[:, 0:d]).astype(BF16)
    z_ref[...] = _bdot_nt(h, w_ref[0:inner, :])
    xbc_ref[...] = _bdot_nt(h, w_ref[inner:inner + cdim, :])
    dt_ref[...] = _bdot_nt(h, w_ref[inner + cdim:inner + cdim + LANES, :])


def _row_specs(tm, tpb, d, mod_rows):
    x_spec = pl.BlockSpec((tm, d), lambda m: (m, 0))
    mod_spec = pl.BlockSpec((None, mod_rows, 6 * d), lambda m: (m // tpb, 0, 0))
    return x_spec, mod_spec


def _const_spec(shape):
    return pl.BlockSpec(shape, lambda m: (0,) * len(shape))


def _inproj_ssm(x, mod3, g, w_bf, tm, tpb, inner, cdim):
    m, d = x.shape
    x_spec, mod_spec = _row_specs(tm, tpb, d, mod3.shape[1])
    row = lambda n: pl.BlockSpec((tm, n), lambda i: (i, 0))
    return pl.pallas_call(
        functools.partial(_inproj_ssm_body, inner, cdim), grid=(m // tm,),
        in_specs=[x_spec, mod_spec, _const_spec((1, d)), _const_spec(w_bf.shape)],
        out_specs=[row(inner), row(cdim), row(LANES)],
        out_shape=[jax.ShapeDtypeStruct((m, inner), F32), jax.ShapeDtypeStruct((m, cdim), F32),
                   jax.ShapeDtypeStruct((m, LANES), F32)],
        **_call_opts("inproj_ssm", ("parallel",)))(x, mod3, g, w_bf)


def _inproj_attn_body(aw, kvw, iw, x_ref, mod_ref, g_ref, w_ref, cos_ref, slo_ref, shi_ref,
                      q_ref, kb_ref, kt_ref, vt_ref, vtb_ref, qi_ref, gs_ref, ga_ref, sm_ref, ki2_ref, kit_ref):
    d = x_ref.shape[1]
    h = _norm_mod(x_ref[...], g_ref[...], mod_ref[:, d:2 * d], mod_ref[:, 0:d]).astype(BF16)
    cos, slo, shi = cos_ref[...], slo_ref[...], shi_ref[...]

    def proj(a, b):
        return _bdot_nt(h, w_ref[a:b, :])

    o = 0
    q_ref[...] = (_rope_wide(proj(o, o + aw), cos, slo, shi) * Q_SCALE).astype(BF16)
    o += aw
    k = _rope_wide(proj(o, o + kvw), cos, slo, shi)
    kb_ref[...] = k.astype(BF16)
    kt_ref[...] = k.T
    o += kvw
    vt = proj(o, o + kvw).T
    vt_ref[...] = vt
    vtb_ref[...] = vt.astype(BF16)
    o += kvw
    qi_ref[...] = _rope_wide(proj(o, o + iw), cos, slo, shi).astype(BF16)
    o += iw
    gs_ref[...] = proj(o, o + d)
    o += d
    ga_ref[...] = proj(o, o + d)
    o += d
    s = proj(o, o + LANES)
    lane = lax.broadcasted_iota(I32, s.shape, 1)
    sm = jnp.where(lane < HEAD, _rope128(s, cos, slo, shi), s)
    sm_ref[...] = sm
    ki2_ref[...] = jnp.where(lane < HEAD, sm, pltpu.roll(sm, HEAD, 1)).astype(BF16)
    kit_ref[...] = sm.T[0:HEAD, :]


def _inproj_attn(x, mod3, g, w_bf, tabs, tm, tpb, aw, kvw, iw):
    m, d = x.shape
    x_spec, mod_spec = _row_specs(tm, tpb, d, mod3.shape[1])
    ntab = tabs[0].shape[0] // tm
    nb = m // (tm * tpb)
    tab_spec = pl.BlockSpec((tm, LANES), lambda i: (i % ntab, 0))
    row = lambda n, dt: (pl.BlockSpec((tm, n), lambda i: (i, 0)), jax.ShapeDtypeStruct((m, n), dt))
    tmin = lambda n, dt: (pl.BlockSpec((None, n, tm), lambda i: (i // tpb, 0, i % tpb)),
                          jax.ShapeDtypeStruct((nb, n, tm * tpb), dt))
    outs = [row(aw, BF16), row(kvw, BF16), tmin(kvw, F32), tmin(kvw, F32), tmin(kvw, BF16), row(iw, BF16),
            row(d, F32), row(d, F32), row(LANES, F32), row(LANES, BF16), tmin(HEAD, F32)]
    return pl.pallas_call(
        functools.partial(_inproj_attn_body, aw, kvw, iw), grid=(m // tm,),
        in_specs=[x_spec, mod_spec, _const_spec((1, d)), _const_spec(w_bf.shape),
                  tab_spec, tab_spec, tab_spec],
        out_specs=[spec for spec, _ in outs],
        out_shape=[shape for _, shape in outs],
        **_call_opts("inproj_attn", ("parallel",)))(x, mod3, g, w_bf, *tabs)


def _ssd_body(inner, z_ref, xbc_ref, dtr_ref, cw_ref, cb_ref, dtb_ref, a_ref, dsk_ref, ng_ref,
              y_ref, st_ref, full_s, act_s, st_s, y_s):
    c = pl.program_id(1)
    Q, N = SSM_CHUNK, SSM_STATE
    cdim = xbc_ref.shape[1]
    heads_per_group = inner // HEAD // SSM_GROUPS
    gw = inner // SSM_GROUPS

    @pl.when(c == 0)
    def _():
        full_s[0:8, :] = jnp.zeros((8, cdim), F32)
        st_s[...] = jnp.zeros(st_s.shape, F32)

    full_s[8:8 + Q, :] = xbc_ref[...]
    for j in range(0, cdim, 512):
        acc = cb_ref[:, j:j + 512] + full_s[8:8 + Q, j:j + 512] * cw_ref[3:4, j:j + 512]
        for i in range(SSM_CONV - 1):
            acc = acc + full_s[5 + i:5 + i + Q, j:j + 512] * cw_ref[i:i + 1, j:j + 512]
        act_s[:, j:j + 512] = _silu(acc)
    full_s[0:8, :] = full_s[Q:Q + 8, :]

    dt = _softplus(dtr_ref[...] + dtb_ref[...])
    row = lax.broadcasted_iota(I32, (Q, Q), 0)
    col = lax.broadcasted_iota(I32, (Q, Q), 1)
    tri = row >= col
    acs = jnp.dot(tri.astype(F32), dt * a_ref[...], precision=HIGHEST, preferred_element_type=F32)
    acs_t, dt_t = acs.T, dt.T
    last = acs[Q - 1:Q, :]
    wdt = jnp.exp(last - acs) * dt
    eacs = jnp.exp(acs)
    cdec = jnp.exp(last)
    low = lax.broadcasted_iota(I32, (Q, LANES), 1) < HEAD
    low1 = low[0:1, :]

    for g in range(SSM_GROUPS):
        bg = act_s[:, inner + g * N:inner + (g + 1) * N]
        cg = act_s[:, inner + SSM_GROUPS * N + g * N:inner + SSM_GROUPS * N + (g + 1) * N]
        cb = _bdot_nt(cg, bg)
        bg_t = bg.T.astype(BF16)
        for p in range(heads_per_group // 2):
            h0 = g * heads_per_group + 2 * p
            js = slice(h0 * HEAD, h0 * HEAD + LANES)
            xp = act_s[:, js]
            xp_bf = xp.astype(BF16)
            stp = st_s[:, js]
            stp_bf = stp.astype(BF16)
            ys = []
            for h in (h0, h0 + 1):
                seg = acs[:, h:h + 1] - acs_t[h:h + 1, :]
                decay = jnp.exp(jnp.where(tri, seg, -jnp.inf))
                m = (cb * decay) * dt_t[h:h + 1, :]
                ce = cg * eacs[:, h:h + 1]
                ys.append(_bdot(m, xp_bf) + _bdot(ce, stp_bf))
            y_s[:, js] = jnp.where(low, ys[0], ys[1])
            wcol = jnp.where(low, wdt[:, h0:h0 + 1], wdt[:, h0 + 1:h0 + 2])
            dst = jnp.dot(bg_t, (xp * wcol).astype(BF16), preferred_element_type=F32)
            cd = jnp.where(low1, cdec[:, h0:h0 + 1], cdec[:, h0 + 1:h0 + 2])
            st_s[:, js] = stp * cd + dst

    for g in range(SSM_GROUPS):
        gs = slice(g * gw, (g + 1) * gw)
        y = y_s[:, gs] + dsk_ref[:, gs] * act_s[:, gs]
        y = y * _silu(z_ref[:, gs])
        ms = jnp.mean(y * y, axis=-1, keepdims=True)
        y_ref[:, gs] = (y * lax.rsqrt(ms + NORM_EPS) * ng_ref[:, gs]).astype(BF16)

    @pl.when(c == pl.num_programs(1) - 1)
    def _():
        st_ref[...] = st_s[...].T


def _ssd_prompt(z, xbc, dtr, cw, cb, dtb, a, dsk, ng, nb):
    m, inner = z.shape
    cdim = xbc.shape[1]
    nc = m // nb // SSM_CHUNK
    row = lambda n: pl.BlockSpec((SSM_CHUNK, n), lambda b, c: (b * nc + c, 0))
    const = lambda shape: pl.BlockSpec(shape, lambda b, c: (0,) * len(shape))
    return pl.pallas_call(
        functools.partial(_ssd_body, inner), grid=(nb, nc),
        in_specs=[row(inner), row(cdim), row(LANES), const(cw.shape), const(cb.shape),
                  const(dtb.shape), const(a.shape), const(dsk.shape), const(ng.shape)],
        out_specs=[row(inner), pl.BlockSpec((None, inner, SSM_STATE), lambda b, c: (b, 0, 0))],
        out_shape=[jax.ShapeDtypeStruct((m, inner), BF16),
                   jax.ShapeDtypeStruct((nb, inner, SSM_STATE), F32)],
        scratch_shapes=[pltpu.VMEM((SSM_CHUNK + 8, cdim), F32), pltpu.VMEM((SSM_CHUNK, cdim), F32),
                        pltpu.VMEM((SSM_STATE, inner), F32), pltpu.VMEM((SSM_CHUNK, inner), F32)],
        **_call_opts("ssd_prompt", ("parallel", "arbitrary")),
    )(z, xbc, dtr, cw, cb, dtb, a, dsk, ng)


def _ssd_step_body(z_ref, xs_ref, bc_ref, cxs_ref, cbc_ref, dtr_ref, st_ref,
                   wxs_ref, wbc_ref, bxs_ref, bbc_ref, dtb_ref, a_ref, dsk_ref, ng_ref,
                   y_ref, sto_ref):
    G = SSM_GROUPS
    last = SSM_CONV - 1
    xs = bxs_ref[...] + xs_ref[...] * wxs_ref[last]
    bc = bbc_ref[...] + bc_ref[...] * wbc_ref[last]
    for i in range(last):
        xs = xs + cxs_ref[i] * wxs_ref[i]
        bc = bc + cbc_ref[i] * wbc_ref[i]
    xs, bc = _silu(xs), _silu(bc)
    dt = _softplus(dtr_ref[...] + dtb_ref[...])
    dec = jnp.exp(dt * a_ref[...])
    xdt = xs * dt
    npair = xs.shape[0]
    pairs_per_group = npair // G
    r = lax.broadcasted_iota(I32, (LANES, LANES), 0)
    cidx = lax.broadcasted_iota(I32, (LANES, LANES), 1)
    eye = (r == cidx).astype(F32)
    nt = (((1,), (1,)), ((), ()))
    dec_t = lax.dot_general(eye, dec, nt, precision=HIGHEST, preferred_element_type=F32)
    xdt_t = lax.dot_general(eye, xdt, nt, precision=HIGHEST, preferred_element_type=F32)
    rows = lax.broadcasted_iota(I32, (npair, 1), 0)
    cbv = jnp.sum(bc[0:G, :] * bc[G:2 * G, :], axis=-1, keepdims=True)
    cbx = jnp.zeros((npair, 1), F32)
    for g in range(G):
        cbx = cbx + jnp.where(rows // pairs_per_group == g, cbv[g:g + 1, :], 0.0)
    c_bf = bc.astype(BF16)
    yoff = jnp.zeros(xs.shape, F32)
    for j in range(npair):
        g = j // pairs_per_group
        s = st_ref[j * LANES:(j + 1) * LANES, :]
        sto_ref[j * LANES:(j + 1) * LANES, :] = s * dec_t[:, j:j + 1] + xdt_t[:, j:j + 1] * bc[g:g + 1, :]
        rj = _bdot_nt(c_bf, s)
        yoff = yoff + jnp.where(rows == j, rj[G + g:G + g + 1, :], 0.0)
    y = yoff * dec + cbx * dt * xs + dsk_ref[...] * xs
    y = y * _silu(z_ref[...])
    ssq = jnp.sum(y * y, axis=-1, keepdims=True)
    msx = jnp.zeros((npair, 1), F32)
    for g in range(G):
        ing = rows // pairs_per_group == g
        tot = jnp.sum(jnp.where(ing, ssq, 0.0), axis=0, keepdims=True)
        msx = msx + jnp.where(ing, tot, 0.0)
    msx = msx / (pairs_per_group * LANES)
    y_ref[...] = y * lax.rsqrt(msx + NORM_EPS) * ng_ref[...]


def _ssd_sample(z, xs, bc, cxs, cbc, dtr, st, wxs, wbc, bxs, bbc, dtb, a, dsk, ng):
    db, npair, _ = z.shape
    per_b = lambda shape: pl.BlockSpec((None,) + shape, lambda b: (b,) + (0,) * len(shape))
    const = lambda arr: pl.BlockSpec(arr.shape, lambda b: (0,) * arr.ndim)
    return pl.pallas_call(
        _ssd_step_body, grid=(db,),
        in_specs=[per_b(z.shape[1:]), per_b(xs.shape[1:]), per_b(bc.shape[1:]), per_b(cxs.shape[1:]),
                  per_b(cbc.shape[1:]), per_b(dtr.shape[1:]), per_b(st.shape[1:]),
                  const(wxs), const(wbc), const(bxs), const(bbc), const(dtb), const(a), const(dsk), const(ng)],
        out_specs=[per_b(z.shape[1:]), per_b(st.shape[1:])],
        out_shape=[jax.ShapeDtypeStruct(z.shape, F32), jax.ShapeDtypeStruct(st.shape, F32)],
        **_call_opts("ssd_sample", ("parallel",)),
    )(z, xs, bc, cxs, cbc, dtr, st, wxs, wbc, bxs, bbc, dtb, a, dsk, ng)


def _row_fold(x, h, op=jnp.add):
    parts = [x[j:j + h, :] for j in range(0, x.shape[0], h)]
    while len(parts) > 1:
        parts = [op(a, b) for a, b in zip(parts[0::2], parts[1::2])] + (parts[-1:] if len(parts) % 2 else [])
    return parts[0]


IMG_ROWS = 16


def _sort_network(n):
    def merge(lo, hi, r):
        step = 2 * r
        if step < hi - lo:
            yield from merge(lo, hi, step)
            yield from merge(lo + r, hi, step)
            yield from ((i, i + r) for i in range(lo + r, hi - r, step))
        else:
            yield (lo, lo + r)

    def sort(lo, hi):
        if hi - lo >= 1:
            mid = lo + (hi - lo) // 2
            yield from sort(lo, mid)
            yield from sort(mid + 1, hi)
            yield from merge(lo, hi, 1)
    return tuple(sort(0, n - 1))


def _group_members(kb):
    tiles = kb // IMG_ROWS
    size = 16 if tiles % 16 == 0 else 8
    n = tiles // size
    return [[g + n * k for k in range(size)] for g in range(n)]


def _sort_groups(x):
    kb = x.shape[0]
    half = IMG_ROWS // 2
    rows = [x[r:r + half, :] for r in range(0, kb, half)]
    for members in _group_members(kb):
        network = _sort_network(len(members))
        for b in range(2):
            idx = [2 * p + b for p in members]
            vals = [rows[i] for i in idx]
            for i, j in network:
                vals[i], vals[j] = jnp.maximum(vals[i], vals[j]), jnp.minimum(vals[i], vals[j])
            for i, v in zip(idx, vals):
                rows[i] = v
    return jnp.concatenate(rows, axis=0)


def _topk_bias(score_s, sort_s, img_s, bias_s, nblk, kb, topk, qpos, unroll=1):
    nq = score_s.shape[1]
    blk = lambda s: pl.ds(pl.multiple_of(s * kb, kb), kb)
    one16, zero16 = jnp.int16(1), jnp.int16(0)
    work_s = bias_s
    groups = _group_members(kb)

    def over_blocks(body, init):
        def trip(s2, carry):
            for k in range(unroll):
                carry = body(s2 * unroll + k, carry)
            return carry
        return lax.fori_loop(0, nblk // unroll, trip, init)

    def count(c):
        def body(s, acc):
            img = img_s[blk(s), :]
            for members in groups:
                p = [img[m * IMG_ROWS:(m + 1) * IMG_ROWS, :] for m in members]
                masks = []

                def comparand(lo, hi, level):
                    mid = (lo + hi) // 2
                    if level == len(masks):
                        return p[mid]
                    return jnp.where(masks[level], comparand(mid + 1, hi, level + 1), comparand(lo, mid, level + 1))

                depth = len(p).bit_length() - 1
                terms = [jnp.where(p[-1] >= c, one16, zero16)]
                for level in range(depth):
                    masks.append(comparand(0, len(p) - 1, 0) >= c)
                    terms.append(jnp.where(masks[-1], jnp.int16(1 << (depth - 1 - level)), zero16))
                while len(terms) > 1:
                    terms = [a + b for a, b in zip(terms[0::2], terms[1::2])] + (terms[-1:] if len(terms) % 2 else [])
                acc = acc + terms[0]
            return acc
        acc = over_blocks(body, jnp.zeros((IMG_ROWS, nq), I16))
        return jnp.sum(acc.astype(F32), axis=0, keepdims=True)

    def search(nbits, value_of):
        def step(t, u):
            code = u | lax.shift_left(jnp.int32(1), jnp.asarray(nbits - 1 - t, I32))
            c = value_of(code).astype(BF16)
            return jnp.where(count(c) >= topk, code, u)
        return lax.fori_loop(0, nbits, step, jnp.zeros((1, nq), I32))

    def bf16_value(code):
        pattern = jnp.where(code >= 32768, code - 32768, 65535 - code)
        return pltpu.bitcast(pattern << 16, jnp.float32).astype(F32)

    def set_image(fn):
        def body(s, carry):
            img_s[blk(s), :] = fn(s).astype(BF16)
            return carry
        over_blocks(body, 0)

    t1 = bf16_value(search(16, bf16_value))

    e1 = jnp.clip((pltpu.bitcast(t1.astype(jnp.float32), I32) >> 23) & 0xFF, 25, 254)
    unit = pltpu.bitcast((e1 - 24) << 23, jnp.float32).astype(F32)
    inv_unit = pltpu.bitcast((278 - e1) << 23, jnp.float32).astype(F32)
    B2, B1 = 65536.0, 256.0

    def digit2(s):
        y = (sort_s[blk(s), :] - t1) * inv_unit
        work_s[blk(s), :] = y
        return jnp.floor(y * (1.0 / B2))
    set_image(digit2)
    t2 = (search(2, lambda code: (code - 1).astype(F32)) - 1).astype(F32)
    set_image(lambda s: jnp.floor((work_s[blk(s), :] - t2 * B2) * (1.0 / B1)))
    t3 = search(8, lambda code: code.astype(F32)).astype(F32)
    set_image(lambda s: jnp.floor(work_s[blk(s), :] - (t2 * B2 + t3 * B1)))
    t4 = search(8, lambda code: code.astype(F32)).astype(F32)
    v0 = t1 + (t2 * B2 + t3 * B1 + t4) * unit

    def smallest(keep):
        def body(s, acc):
            x = score_s[blk(s), :]
            return jnp.minimum(acc, _row_fold(jnp.where(keep(x), x, jnp.inf), 8, jnp.minimum))
        acc = over_blocks(body, jnp.full((8, nq), jnp.inf, F32))
        return jnp.min(acc, axis=0, keepdims=True)

    def count_above(v):
        def body(s, acc):
            return acc + _row_fold(jnp.where(score_s[blk(s), :] > v, 1.0, 0.0), 8)
        return jnp.sum(over_blocks(body, jnp.zeros((8, nq), F32)), axis=0, keepdims=True)

    def refine(carry):
        v, above = carry
        v = jnp.where(above >= topk, smallest(lambda x: x > v), v)
        return v, count_above(v)

    v = smallest(lambda x: x >= v0)
    v, above = lax.while_loop(lambda c: jnp.max(c[1]) >= topk, refine, (v, count_above(v)))
    need = topk - above

    r_i = lax.broadcasted_iota(I32, (LANES, LANES), 0)
    c_i = lax.broadcasted_iota(I32, (LANES, LANES), 1)
    lower = jnp.where(r_i >= c_i, 1.0, 0.0).astype(BF16)
    sub_iota = lax.broadcasted_iota(I32, (LANES, 1), 0)

    def bias_body(s, carry):
        for j in range(0, kb, LANES):
            off = pl.multiple_of(s * kb + j, LANES)
            x = score_s[pl.ds(off, LANES), :]
            eq = x == v
            eqf = jnp.where(eq, 1.0, 0.0)
            incl = jnp.dot(lower, eqf.astype(BF16), preferred_element_type=F32)
            tie = jnp.where(carry + incl - eqf < need, 0.0, -jnp.inf)
            b = jnp.where(x > v, 0.0, jnp.where(eq, tie, -jnp.inf))
            bias_s[pl.ds(off, LANES), :] = jnp.where((off + sub_iota) <= qpos, b, -jnp.inf)
            carry = carry + incl[LANES - 1:LANES, :]
        return carry

    over_blocks(bias_body, jnp.zeros((1, nq), F32))


def _dsa_t_body(topk, idx_scale, q_ref, qi_ref, sm_ref, ki2_ref, k_ref, vt_ref, o_ref,
                score_s, sort_s, img_s, bias_s, qs_s, lga_s, lgb_s, m_s, acc_s):
    i = pl.program_id(1)
    QB, KB = Q_BLOCK, KEY_BLOCK
    nkb = (i * QB + QB + KB - 1) // KB
    qpos = i * QB + lax.broadcasted_iota(I32, (1, QB), 1)
    low = lax.broadcasted_iota(I32, (QB, LANES), 1) < HEAD
    zero_bf = jnp.zeros((QB, LANES), BF16)
    blk = lambda s: pl.ds(pl.multiple_of(s * KB, KB), KB)

    qi = qi_ref[...]
    sm_t = sm_ref[...].T
    qh, wh = [], []
    for h in range(IDX_HEADS):
        chunk = qi[:, (h // 2) * LANES:(h // 2 + 1) * LANES]
        qh.append(jnp.where(low if h % 2 == 0 else ~low, chunk, zero_bf))
        wh.append(sm_t[HEAD + h:HEAD + h + 1, :] * idx_scale)

    q_stack = jnp.concatenate(qh, axis=0)

    last_key_block = ki2_ref.shape[0] // KB - 1

    def score_pair(s2, carry):
        blocks = (2 * s2, 2 * s2 + 1)
        dots = [_bdot_nt(ki2_ref[blk(jnp.minimum(s, last_key_block)), :], q_stack) for s in blocks]
        for s, sc in zip(blocks, dots):
            acc = jnp.zeros((KB, QB), F32)
            for h in range(IDX_HEADS):
                acc = acc + wh[h] * jnp.maximum(sc[:, h * QB:(h + 1) * QB], 0.0)
            kpos = s * KB + lax.broadcasted_iota(I32, (KB, 1), 0)
            sc = jnp.where(kpos <= qpos, jnp.maximum(acc, MASKED_SCORE), MASKED_SCORE)
            score_s[blk(s), :] = sc
            sc = _sort_groups(sc)
            sort_s[blk(s), :] = sc
            img_s[blk(s), :] = sc.astype(BF16)
        return carry

    nkb_even = 2 * ((nkb + 1) // 2)
    lax.fori_loop(0, nkb_even // 2, score_pair, 0)
    _topk_bias(score_s, sort_s, img_s, bias_s, nkb_even, KB, topk, qpos, unroll=2)

    q = q_ref[...]
    nchunk = q.shape[1] // LANES
    per_kv_chunk = nchunk // (ATTN_KV_HEADS // 2)
    nstack = 2 * per_kv_chunk
    srows = nstack * QB
    for cj in range(nchunk):
        kvc, j = divmod(cj, per_kv_chunk)
        qc = q[:, cj * LANES:(cj + 1) * LANES]
        for half in range(2):
            r0 = (kvc * nstack + half * per_kv_chunk + j) * QB
            qs_s[r0:r0 + QB, :] = jnp.where(low if half == 0 else ~low, qc, zero_bf)

    ones_rows = jnp.ones((ONES_ROWS, KB), BF16)
    cols = lambda r: slice(r * QB, (r + 1) * QB)
    n_kvc = ATTN_KV_HEADS // 2
    nsteps = n_kvc * nkb

    def step_of(t):
        t = jnp.minimum(t, nsteps - 1)
        kvc = (t >= nkb).astype(I32)
        return kvc, t - kvc * nkb

    def logits_to(dst, t):
        kvc, s = step_of(t)
        kb = k_ref[blk(s), pl.ds(pl.multiple_of(kvc * LANES, LANES), LANES)]
        bias = bias_s[blk(s), :]
        lg = _bdot_nt(kb, qs_s[pl.ds(pl.multiple_of(kvc * srows, srows), srows), :])
        for r in range(nstack):
            dst[:, cols(r)] = lg[:, cols(r)] + bias

    def consume(src, t):
        kvc, s = step_of(t)
        vt = vt_ref[pl.ds(pl.multiple_of(kvc * LANES, LANES), LANES), blk(s)]
        ps, alphas = [], []
        for r in range(nstack):
            lg = src[:, cols(r)]
            m = m_s[kvc, :, cols(r)]
            mn = jnp.maximum(m, jnp.max(lg, axis=0, keepdims=True))
            m_s[kvc, :, cols(r)] = mn
            ps.append(jnp.exp2(lg - mn).astype(BF16))
            alphas.append(jnp.exp2(m - mn))
        for half in range(2):
            hs = slice(half * per_kv_chunk, (half + 1) * per_kv_chunk)
            hc = slice(half * per_kv_chunk * QB, (half + 1) * per_kv_chunk * QB)
            v_aug = jnp.concatenate([vt[half * HEAD:(half + 1) * HEAD, :], ones_rows], axis=0)
            pv = jnp.dot(v_aug, jnp.concatenate(ps[hs], axis=1), preferred_element_type=F32)
            acc_s[kvc, :, hc] = jnp.concatenate(alphas[hs], axis=1) * acc_s[kvc, :, hc] + pv

    m_s[...] = jnp.full(m_s.shape, NEG_BIG, F32)
    acc_s[...] = jnp.zeros(acc_s.shape, F32)
    logits_to(lga_s, 0)

    def run_pairs(t0, npairs):
        for p in range(npairs):
            logits_to(lgb_s, t0 + 2 * p + 1)
            consume(lga_s, t0 + 2 * p)
            logits_to(lga_s, t0 + 2 * p + 2)
            consume(lgb_s, t0 + 2 * p + 1)

    def quad_body(i4, carry):
        run_pairs(4 * i4, 2)
        return carry

    lax.fori_loop(0, nsteps // 4, quad_body, 0)

    @pl.when(nsteps % 4 == 2)
    def _():
        run_pairs(nsteps - 2, 1)

    for kvc in range(n_kvc):
        acc = acc_s[kvc]
        for j in range(per_kv_chunk):
            a = acc[:, cols(j)]
            b = acc[:, cols(per_kv_chunk + j)]
            chunk_t = jnp.concatenate([a[0:HEAD, :] / a[HEAD:HEAD + 1, :], b[0:HEAD, :] / b[HEAD:HEAD + 1, :]],
                                      axis=0)
            cj = kvc * per_kv_chunk + j
            o_ref[:, cj * LANES:(cj + 1) * LANES] = chunk_t.T.astype(BF16)


def _dsa_prompt_t(q, qi, sm, ki2, k, vt, nb, topk):
    m, aw = q.shape
    t = m // nb
    nq = t // Q_BLOCK
    tpad = -(-t // (2 * KEY_BLOCK)) * 2 * KEY_BLOCK
    srows = aw // HEAD // (ATTN_KV_HEADS // 2) * Q_BLOCK
    idx_scale = IDX_HEADS ** -0.5 * HEAD ** -0.5
    row = lambda n: pl.BlockSpec((Q_BLOCK, n), lambda b, i: (b * nq + i, 0))
    per_b = lambda n: pl.BlockSpec((t, n), lambda b, i: (b, 0))
    return pl.pallas_call(
        functools.partial(_dsa_t_body, topk, idx_scale), grid=(nb, nq),
        in_specs=[row(aw), row(qi.shape[1]), row(LANES), per_b(LANES), per_b(k.shape[1]),
                  pl.BlockSpec((None,) + vt.shape[1:], lambda b, i: (b, 0, 0))],
        out_specs=row(aw),
        out_shape=jax.ShapeDtypeStruct((m, aw), BF16),
        scratch_shapes=[pltpu.VMEM((tpad, Q_BLOCK), F32), pltpu.VMEM((tpad, Q_BLOCK), F32),
                        pltpu.VMEM((tpad, Q_BLOCK), BF16), pltpu.VMEM((tpad, Q_BLOCK), F32),
                        pltpu.VMEM((2 * srows, LANES), BF16),
                        pltpu.VMEM((KEY_BLOCK, srows), F32), pltpu.VMEM((KEY_BLOCK, srows), F32),
                        pltpu.VMEM((ATTN_KV_HEADS // 2, 1, srows), F32),
                        pltpu.VMEM((ATTN_KV_HEADS // 2, HEAD + ONES_ROWS, srows), F32)],
        **_call_opts("dsa_prompt", ("parallel", "arbitrary")),
    )(q, qi, sm, ki2, k, vt)


def _page_specs(block, npages):
    def make(u):
        return pl.BlockSpec((None,) + block, lambda b, pt: (pt[b, u],) + (0,) * len(block))
    return [make(u) for u in range(npages)]


def _idx_score_body(pt_ref, qi_ref, w_ref, *refs):
    pages, o_ref = refs[:-1], refs[-1]
    qi = qi_ref[...]
    w = w_ref[...]
    for u, page in enumerate(pages):
        s = jnp.maximum(_bdot(qi, page[...]), 0.0)
        o_ref[:, u * PAGE_SIZE:(u + 1) * PAGE_SIZE] = jnp.sum(w * s, axis=0, keepdims=True)


def _idx_scores_sample(page_table, qi8, w8, kidx_t):
    db, npages = page_table.shape
    grid_spec = pltpu.PrefetchScalarGridSpec(
        num_scalar_prefetch=1, grid=(db,),
        in_specs=[pl.BlockSpec((None,) + qi8.shape[1:], lambda b, pt: (b, 0, 0)),
                  pl.BlockSpec((None,) + w8.shape[1:], lambda b, pt: (b, 0, 0))]
                 + _page_specs(kidx_t.shape[1:], npages),
        out_specs=pl.BlockSpec((None, 1, npages * PAGE_SIZE), lambda b, pt: (b, 0, 0)))
    return pl.pallas_call(
        _idx_score_body, grid_spec=grid_spec,
        out_shape=jax.ShapeDtypeStruct((db, 1, npages * PAGE_SIZE), F32),
        **_call_opts("idx_scores_sample", ("parallel",)),
    )(page_table, qi8, w8, *([kidx_t] * npages))


def _select_sample_body(topk, idx_scale, past, sc_ref, qi_ref, sm_ref, bias_ref, score_s, sort_s, img_s, bias_s):
    rows = sc_ref.shape[0]
    sm = sm_ref[...]
    qi = qi_ref[...]
    ki = sm[:, 0:HEAD]
    new = jnp.zeros((rows, 1), F32)
    for h in range(IDX_HEADS):
        d = jnp.sum(qi[:, h * HEAD:(h + 1) * HEAD] * ki, axis=-1, keepdims=True)
        new = new + (sm[:, HEAD + h:HEAD + h + 1] * idx_scale) * jnp.maximum(d, 0.0)
    nblk = (past + LANES) // LANES
    lane = lax.broadcasted_iota(I32, (rows, LANES), 1)
    for j in range(nblk):
        js = slice(j * LANES, (j + 1) * LANES)
        sc = sc_ref[:, js] if j < nblk - 1 else jnp.where(lane == 0, new, MASKED_SCORE)
        sc = jnp.maximum(sc, MASKED_SCORE).T
        score_s[js, :] = sc
        sc = _sort_groups(sc)
        sort_s[js, :] = sc
        img_s[js, :] = sc.astype(BF16)
    qpos = jnp.full((1, rows), past, I32)
    _topk_bias(score_s, sort_s, img_s, bias_s, nblk, LANES, topk, qpos)
    for j in range(nblk):
        js = slice(j * LANES, (j + 1) * LANES)
        bias_ref[:, js] = bias_s[js, :].T


def _select_sample(scores, qi, sm, topk, idx_scale):
    db, past = scores.shape
    full = lambda a: pl.BlockSpec(a.shape, lambda i: (0,) * a.ndim)
    keys = past + LANES
    return pl.pallas_call(
        functools.partial(_select_sample_body, topk, idx_scale, past), grid=(1,),
        in_specs=[full(scores), full(qi), full(sm)],
        out_specs=pl.BlockSpec((db, keys), lambda i: (0, 0)),
        out_shape=jax.ShapeDtypeStruct((db, keys), F32),
        scratch_shapes=[pltpu.VMEM((keys, db), F32), pltpu.VMEM((keys, db), F32), pltpu.VMEM((keys, db), BF16),
                        pltpu.VMEM((keys, db), F32)],
        **_call_opts("select_sample", ("arbitrary",)))(scores, qi, sm)


def _attend_sample_body(npages, pt_ref, q_ref, bias_ref, knew_ref, vnew_ref, *refs):
    kpages, vpages, o_ref = refs[:npages], refs[npages:2 * npages], refs[2 * npages]
    G = ATTN_KV_HEADS
    q = q_ref[...]
    q_bf = q.astype(BF16)
    nh = q.shape[0]
    past = npages * PAGE_SIZE
    group = lax.broadcasted_iota(I32, (nh, 1), 0) // (nh // G)

    def by_group(parts):
        out = parts[G - 1]
        for g in range(G - 2, -1, -1):
            out = jnp.where(group == g, parts[g], out)
        return out

    lg = jnp.concatenate(
        [by_group([_bdot(q_bf, kpages[u][g]) for g in range(G)]) for u in range(npages)], axis=1)
    lg = lg + bias_ref[:, 0:past]
    lg_new = by_group([jnp.sum(q * knew_ref[g:g + 1, :], axis=-1, keepdims=True) for g in range(G)])
    lg_new = lg_new + bias_ref[:, past:past + 1]
    m = jnp.maximum(jnp.max(lg, axis=-1, keepdims=True), lg_new)
    p = jnp.exp2(lg - m)
    p_new = jnp.exp2(lg_new - m)
    denom = jnp.sum(p, axis=-1, keepdims=True) + p_new
    p_bf = p.astype(BF16)
    accs = [p_new * vnew_ref[g:g + 1, :] for g in range(G)]
    for u in range(npages):
        pu = p_bf[:, u * PAGE_SIZE:(u + 1) * PAGE_SIZE]
        for g in range(G):
            accs[g] = accs[g] + _bdot_nt(pu, vpages[u][g])
    o_ref[...] = by_group(accs) / denom


def _attend_sample(page_table, q, bias, knew, vnew, k_t, v_t):
    db, npages = page_table.shape
    per_b = lambda a: pl.BlockSpec((None,) + a.shape[1:], lambda b, pt: (b,) + (0,) * (a.ndim - 1))
    grid_spec = pltpu.PrefetchScalarGridSpec(
        num_scalar_prefetch=1, grid=(db,),
        in_specs=[per_b(q), per_b(bias), per_b(knew), per_b(vnew)]
                 + _page_specs(k_t.shape[1:], npages) + _page_specs(v_t.shape[1:], npages),
        out_specs=per_b(q))
    return pl.pallas_call(
        functools.partial(_attend_sample_body, npages), grid_spec=grid_spec,
        out_shape=jax.ShapeDtypeStruct(q.shape, F32),
        **_call_opts("attend_sample", ("parallel",)),
    )(page_table, q, bias, knew, vnew, *([k_t] * npages), *([v_t] * npages))


def _merge_ffn_body(last_layer, x_ref, mod_ref, ys_ref, ya_ref, gs_ref, ga_ref, wps_ref, wpa_ref, wout_ref,
                    g_ref, fg_ref, wg_ref, wu_ref, wo_ref, o_ref):
    d = x_ref.shape[1]
    merged = (jax.nn.sigmoid(gs_ref[...]) * _bdot(ys_ref[...], wps_ref[...])
              + jax.nn.sigmoid(ga_ref[...]) * _bdot(ya_ref[...], wpa_ref[...]))
    x = x_ref[...] + mod_ref[:, 2 * d:3 * d] * _bdot(merged, wout_ref[...])
    h = _norm_mod(x, g_ref[...], mod_ref[:, 4 * d:5 * d], mod_ref[:, 3 * d:4 * d]).astype(BF16)
    gate = jnp.dot(h, wg_ref[...], preferred_element_type=F32)
    up = jnp.dot(h, wu_ref[...], preferred_element_type=F32)
    x2 = x + mod_ref[:, 5 * d:6 * d] * _bdot(_silu(gate) * up, wo_ref[...])
    if last_layer:
        ms = jnp.mean(x2 * x2, axis=-1, keepdims=True)
        x2 = x2 * lax.rsqrt(ms + NORM_EPS) * fg_ref[...]
    o_ref[...] = x2


def _merge_ffn(x, mod3, ys, ya, gs, ga, wps, wpa, wout, g, fg, wg, wu, wo, tm, tpb, last_layer):
    m, d = x.shape
    x_spec, mod_spec = _row_specs(tm, tpb, d, mod3.shape[1])
    row = lambda n: pl.BlockSpec((tm, n), lambda i: (i, 0))
    return pl.pallas_call(
        functools.partial(_merge_ffn_body, last_layer), grid=(m // tm,),
        in_specs=[x_spec, mod_spec, row(ys.shape[1]), row(ya.shape[1]), row(d), row(d),
                  _const_spec(wps.shape), _const_spec(wpa.shape), _const_spec(wout.shape),
                  _const_spec((1, d)), _const_spec((1, d)),
                  _const_spec(wg.shape), _const_spec(wu.shape), _const_spec(wo.shape)],
        out_specs=row(d), out_shape=jax.ShapeDtypeStruct((m, d), F32),
        **_call_opts("merge_ffn", ("parallel",)))(x, mod3, ys, ya, gs, ga, wps, wpa, wout, g, fg, wg, wu, wo)


def _rope_tables(pos):
    half = HEAD // 2
    inv = ROPE_THETA ** (-jnp.arange(half, dtype=F32) / half)
    ang = pos.astype(F32)[:, None] * inv[None, :]
    cos = jnp.tile(jnp.cos(ang), (1, LANES // half))
    sin = jnp.tile(jnp.sin(ang), (1, LANES // half))
    first = (jnp.arange(LANES) % HEAD) < half
    return cos, jnp.where(first, -sin, 0.0), jnp.where(first, 0.0, sin)


def _q_head_order(n_heads):
    rep = n_heads // ATTN_KV_HEADS
    order = []
    for c in range(ATTN_KV_HEADS // 2):
        for j in range(rep):
            order += [2 * c * rep + j, (2 * c + 1) * rep + j]
    return np.asarray(order)


def kernel(x_prompt, x_sample, cache_k, cache_v, cache_kidx, state_conv, state_ssm, page_table, c_prompt, c_sample, w_ada, b_ada, norm1_g, w_in, conv_w, conv_b, dt_bias, a_log, d_skip, ssm_norm_g, w_proj_ssm, w_proj_attn, w_out, norm2_g, w_ffn_in, w_ffn_out, final_g):
    nb, t, d = x_prompt.shape
    db, ds, _ = x_sample.shape
    depth = w_in.shape[0]
    assert ds == 1 and t % Q_BLOCK == 0 and t % SSM_CHUNK == 0
    n_heads_ssm = dt_bias.shape[1]
    inner = n_heads_ssm * HEAD
    gn = SSM_GROUPS * SSM_STATE
    cdim = inner + 2 * gn
    kvw = ATTN_KV_HEADS * HEAD
    aw = w_proj_attn.shape[1]
    n_heads = aw // HEAD
    iw = IDX_HEADS * HEAD
    ffn_hidden = w_ffn_out.shape[1]
    npages = page_table.shape[1]
    past = npages * PAGE_SIZE
    topk_p = min(TOPK_MAX, t // 4)
    topk_s = min(TOPK_MAX, (past + ds) // 4)
    assert past + ds >= topk_s
    idx_scale = IDX_HEADS ** -0.5 * HEAD ** -0.5
    tm_p = 256 if t % 256 == 0 else 128
    tpb_p = t // tm_p
    tm_tall = 512 if t % 512 == 0 else tm_p
    tpb_tall = t // tm_tall

    splits = np.cumsum([inner, inner, gn, gn, n_heads_ssm, aw, kvw, kvw, iw, HEAD, IDX_HEADS, d])
    order = _q_head_order(n_heads)
    inv_order = np.argsort(order)

    cos_p, slo_p, shi_p = _rope_tables(jnp.arange(t, dtype=I32))
    tabs_p = (cos_p, slo_p, shi_p)
    tabs_s = tuple(jnp.broadcast_to(a, (db, LANES)) for a in _rope_tables(past + jnp.arange(ds, dtype=I32)))

    rows_c = nb + db
    c_all = jnp.concatenate([c_prompt, c_sample, jnp.zeros((-rows_c % 8, d), F32)], axis=0)

    yp = x_prompt.reshape(nb * t, d)
    ys = x_sample.reshape(db, d)
    outs_p, outs_s = [], []
    for l in range(depth):
        (wz, wxs, wbm, wcm, wdt, wq, wk, wv, wqi, wki, wwi, wgs, wga) = jnp.split(w_in[l].T, splits, axis=0)
        w_ssm = jnp.concatenate([wz, wxs, wbm, wcm, wdt, jnp.zeros((LANES - n_heads_ssm, d), F32)],
                                axis=0).astype(BF16)
        wq_perm = wq.reshape(n_heads, HEAD, d)[order].reshape(aw, d)
        w_attn = jnp.concatenate([wq_perm, wk, wv, wqi, wgs, wga, wki, wwi,
                                  jnp.zeros((LANES - HEAD - IDX_HEADS, d), F32)], axis=0).astype(BF16)
        wps = w_proj_ssm[l].astype(BF16)
        wpa = w_proj_attn[l].reshape(n_heads, HEAD, d)[order].reshape(aw, d).astype(BF16)
        wo = w_out[l].astype(BF16)
        wg = w_ffn_in[l][:, :ffn_hidden].astype(BF16)
        wu = w_ffn_in[l][:, ffn_hidden:].astype(BF16)
        wfo = w_ffn_out[l].astype(BF16)
        g1 = norm1_g[l][None, :]
        g2 = norm2_g[l][None, :]
        a_neg = -jnp.exp(a_log[l])
        pad_h = LANES - n_heads_ssm
        dtb_row = jnp.pad(dt_bias[l], (0, pad_h))[None, :]
        a_row = jnp.pad(a_neg, (0, pad_h))[None, :]
        dsk_row = jnp.repeat(d_skip[l], HEAD)[None, :]
        ng_row = ssm_norm_g[l][None, :]
        cw = conv_w[l]
        cb = conv_b[l][None, :]

        mod = _ada(c_all, w_ada[l].astype(BF16), b_ada[l][None, :])
        mod_p = mod[:nb][:, None, :]
        mod_s = mod[nb:nb + db][None]

        z, xbc, dtr = _inproj_ssm(yp, mod_p, g1, w_ssm, tm_tall, tpb_tall, inner, cdim)
        (q_bf, k_bf, kt, vt, vt_bf, qi_bf, gs, ga, sm, ki2, kit) = _inproj_attn(
            yp, mod_p, g1, w_attn, tabs_p, tm_tall, tpb_tall, aw, kvw, iw)
        y_ssm, st = _ssd_prompt(z, xbc, dtr, cw, cb, dtb_row, a_row, dsk_row, ng_row, nb)
        y_attn = _dsa_prompt_t(q_bf, qi_bf, sm, ki2, k_bf, vt_bf, nb, topk_p)
        yp_next = _merge_ffn(yp, mod_p, y_ssm, y_attn, gs, ga, wps, wpa, wo, g2, final_g[None, :], wg, wu, wfo,
                             tm_tall, tpb_tall, l == depth - 1)
        heads_last = lambda a: jnp.transpose(a.reshape(a.shape[0], ATTN_KV_HEADS, HEAD, a.shape[2]), (0, 3, 1, 2))
        outs_p.append((heads_last(kt), heads_last(vt), jnp.transpose(kit, (0, 2, 1)),
                       xbc.reshape(nb, t, cdim)[:, t - (SSM_CONV - 1):],
                       st.reshape(nb, n_heads_ssm, HEAD, SSM_STATE)))

        z_s, xbc_s, dtr_s = _inproj_ssm(ys, mod_s, g1, w_ssm, db, 1, inner, cdim)
        (q_s, _, kt_s, vt_s, _, qi_s, gs_s, ga_s, sm_s, _, kit_s) = _inproj_attn(
            ys, mod_s, g1, w_attn, tabs_s, db, 1, aw, kvw, iw)
        k_s, v_s = heads_last(kt_s)[0], heads_last(vt_s)[0]
        npair = inner // LANES
        nbc = 2 * gn // LANES
        sc = state_conv[l]
        y_ssm_s, st_s = _ssd_sample(
            z_s.reshape(db, npair, LANES), xbc_s[:, :inner].reshape(db, npair, LANES),
            xbc_s[:, inner:].reshape(db, nbc, LANES),
            sc[:, :, :inner].reshape(db, SSM_CONV - 1, npair, LANES),
            sc[:, :, inner:].reshape(db, SSM_CONV - 1, nbc, LANES),
            jnp.repeat(dtr_s[:, :n_heads_ssm], HEAD, axis=1).reshape(db, npair, LANES),
            state_ssm[l].reshape(db, inner, SSM_STATE),
            cw[:, :inner].reshape(SSM_CONV, npair, LANES), cw[:, inner:].reshape(SSM_CONV, nbc, LANES),
            cb[:, :inner].reshape(npair, LANES), cb[:, inner:].reshape(nbc, LANES),
            jnp.repeat(dt_bias[l], HEAD).reshape(npair, LANES), jnp.repeat(a_neg, HEAD).reshape(npair, LANES),
            dsk_row.reshape(npair, LANES), ng_row.reshape(npair, LANES))

        qi_f = qi_s.astype(F32)
        qi8 = jnp.pad(qi_f.reshape(db, IDX_HEADS, HEAD), ((0, 0), (0, 8 - IDX_HEADS), (0, 0)))
        w8 = jnp.broadcast_to(jnp.pad(sm_s[:, HEAD:HEAD + IDX_HEADS] * idx_scale,
                                      ((0, 0), (0, 8 - IDX_HEADS)))[:, :, None], (db, 8, LANES))
        scores = _idx_scores_sample(page_table, qi8, w8, jnp.transpose(cache_kidx[l], (0, 2, 1)))
        bias = _select_sample(scores.reshape(db, past), qi_f, sm_s, topk_s, idx_scale)
        q_orig = q_s.astype(F32).reshape(db, n_heads, HEAD)[:, inv_order]
        att = _attend_sample(page_table, q_orig, bias[:, None, :],
                             k_s, v_s,
                             jnp.transpose(cache_k[l], (0, 2, 3, 1)), jnp.transpose(cache_v[l], (0, 2, 3, 1)))
        y_attn_s = att[:, order].reshape(db, aw)
        ys_next = _merge_ffn(ys, mod_s, y_ssm_s.reshape(db, inner), y_attn_s, gs_s, ga_s, wps, wpa, wo,
                             g2, final_g[None, :], wg, wu, wfo, db, 1, l == depth - 1)
        outs_s.append((k_s[:, None], v_s[:, None], jnp.transpose(kit_s, (2, 0, 1)),
                       jnp.concatenate([sc[:, 1:], xbc_s[:, None, :]], axis=1),
                       st_s.reshape(db, n_heads_ssm, HEAD, SSM_STATE)))
        yp, ys = yp_next, ys_next

    stack = lambda outs, i: jnp.stack([o[i] for o in outs], axis=0)
    return (yp.reshape(nb, t, d), ys.reshape(db, ds, d),
            stack(outs_p, 0), stack(outs_p, 1), stack(outs_p, 2), stack(outs_p, 3), stack(outs_p, 4),
            stack(outs_s, 0), stack(outs_s, 1), stack(outs_s, 2), stack(outs_s, 3), stack(outs_s, 4))
```
